```python
import math
import jax, jax.numpy as jnp
from jax import lax
import numpy as np

D_MODEL = 1024
BATCH = 8
SEQ = 2048
DEPTH = 1
DEC_BATCH = 128
DEC_SEQ = 4
PAST_LEN = 16384
PAGE_SIZE = 128

D_MIX = D_MODEL
D_RNN = D_MIX // 2
N_RNN_BLOCKS = 8
RNN_BLOCK = D_RNN // N_RNN_BLOCKS
CONV_W = 4
LRU_C = 8.0
HEAD_DIM = 64
N_HEADS = (D_MIX - D_RNN) // HEAD_DIM
N_KV_HEADS = 2
GQA = N_HEADS // N_KV_HEADS
WINDOW = 128
ATT_BLOCK = WINDOW
N_EXPERTS = 64
TOP_K = 6
N_GROUPS = 8
TOPK_GROUPS = 4
D_EXPERT = D_MODEL // 4
D_SHARED = D_MODEL // 4
ROUTE_SCALE = 2.5
MOE_BLOCK = 128
EPS = 1e-6
NEG = -1e30
SPLITS = (D_RNN, 2 * D_RNN, 2 * D_RNN + N_HEADS * HEAD_DIM,
          2 * D_RNN + N_HEADS * HEAD_DIM + N_KV_HEADS * HEAD_DIM)
D_IN = 2 * D_RNN + (N_HEADS + 2 * N_KV_HEADS) * HEAD_DIM

kernel_name = "hymba_rglru_swa_sink_moe_adaln_step"


def rms_norm(x, g):
    xf = x.astype(jnp.float32)
    y = xf * lax.rsqrt(jnp.mean(xf * xf, axis=-1, keepdims=True) + EPS)
    return (y * g.astype(jnp.float32)).astype(x.dtype)


def adaln(c, w, b):
    mod = jnp.einsum('bd,de->be', jax.nn.silu(c), w) + b
    return jnp.split(mod[:, None, :], 6, axis=-1)


def causal_conv(u, prev, w, b):
    T = u.shape[1]
    up = jnp.concatenate([prev.astype(u.dtype), u], axis=1)
    out = b + up[:, 0:T] * w[0]
    for j in range(1, CONV_W):
        out = out + up[:, j:j + T] * w[j]
    return out, up[:, -(CONV_W - 1):]


def rg_lru(u, h0, wa, ba, wx, bx, lam):
    B, T, _ = u.shape
    ub = u.reshape(B, T, N_RNN_BLOCKS, RNN_BLOCK)
    r = jax.nn.sigmoid(jnp.einsum('btni,nij->btnj', ub, wa).reshape(B, T, D_RNN) + ba)
    i = jax.nn.sigmoid(jnp.einsum('btni,nij->btnj', ub, wx).reshape(B, T, D_RNN) + bx)
    log_a = -LRU_C * r.astype(jnp.float32) * jax.nn.softplus(-lam.astype(jnp.float32))
    a = jnp.exp(log_a)
    b_in = jnp.sqrt(-jnp.expm1(2.0 * log_a)) * (i * u).astype(jnp.float32)

    def step(h, ab):
        h = ab[0] * h + ab[1]
        return h, h

    hT, hs = lax.scan(step, h0.astype(jnp.float32), (jnp.swapaxes(a, 0, 1), jnp.swapaxes(b_in, 0, 1)))
    return jnp.swapaxes(hs, 0, 1).astype(u.dtype), hT.astype(h0.dtype)


def alibi_slopes():
    return jnp.exp2(-8.0 * jnp.arange(1, N_HEADS + 1, dtype=jnp.float32) / N_HEADS)


def attend(q, k, v, q_pos, k_pos, sinks):
    s = jnp.einsum('...qhgd,...khd->...hgqk', q, k).astype(jnp.float32) * (HEAD_DIM ** -0.5)
    dist = q_pos[..., :, None] - k_pos[..., None, :]
    valid = (dist >= 0) & (dist <= WINDOW) & (k_pos[..., None, :] >= 0)
    dist = dist[..., None, None, :, :].astype(jnp.float32)
    valid = valid[..., None, None, :, :]
    s = s - alibi_slopes().reshape(N_KV_HEADS, GQA, 1, 1) * dist
    s = jnp.where(valid, s, NEG)
    sink = jnp.broadcast_to(sinks.astype(jnp.float32).reshape(N_KV_HEADS, GQA, 1, 1), s.shape[:-1] + (1,))
    p = jax.nn.softmax(jnp.concatenate([s, sink], axis=-1), axis=-1)[..., :-1]
    return jnp.einsum('...hgqk,...khd->...qhgd', p.astype(v.dtype), v)


def attn_prompt(q, k, v, sinks):
    B, T = q.shape[:2]
    nb = T // ATT_BLOCK
    qb = q.reshape(B, nb, ATT_BLOCK, N_KV_HEADS, GQA, HEAD_DIM)
    pad = ((0, 0), (ATT_BLOCK, 0), (0, 0), (0, 0))
    kp = jnp.pad(k, pad).reshape(B, nb + 1, ATT_BLOCK, N_KV_HEADS, HEAD_DIM)
    vp = jnp.pad(v, pad).reshape(B, nb + 1, ATT_BLOCK, N_KV_HEADS, HEAD_DIM)
    kb = jnp.concatenate([kp[:, :-1], kp[:, 1:]], axis=2)
    vb = jnp.concatenate([vp[:, :-1], vp[:, 1:]], axis=2)
    q_pos = jnp.arange(T, dtype=jnp.int32).reshape(nb, ATT_BLOCK)
    k_pos = (jnp.arange(nb, dtype=jnp.int32)[:, None] - 1) * ATT_BLOCK + jnp.arange(2 * ATT_BLOCK, dtype=jnp.int32)[None, :]
    o = attend(qb, kb, vb, q_pos, k_pos, sinks)
    return o.reshape(B, T, N_HEADS * HEAD_DIM)


def attn_sample(q, k, v, k_buf, v_buf, sinks):
    B, T = q.shape[:2]
    W = k_buf.shape[1]
    kc = jnp.concatenate([k_buf.astype(k.dtype), k], axis=1)
    vc = jnp.concatenate([v_buf.astype(v.dtype), v], axis=1)
    q_pos = PAST_LEN + jnp.arange(T, dtype=jnp.int32)
    k_pos = PAST_LEN - W + jnp.arange(W + T, dtype=jnp.int32)
    o = attend(q.reshape(B, T, N_KV_HEADS, GQA, HEAD_DIM), kc, vc, q_pos, k_pos, sinks)
    return o.reshape(B, T, N_HEADS * HEAD_DIM), kc[:, -W:], vc[:, -W:]


def route(h, w_r, bias):
    N = h.shape[0]
    s = jax.nn.sigmoid(jnp.einsum('nd,de->ne', h, w_r).astype(jnp.float32))
    sb = s + bias.astype(jnp.float32)
    gscore = lax.top_k(sb.reshape(N, N_GROUPS, N_EXPERTS // N_GROUPS), 2)[0].sum(-1)
    _, gidx = lax.top_k(gscore, TOPK_GROUPS)
    gmask = jnp.any(gidx[:, :, None] == jnp.arange(N_GROUPS)[None, None, :], axis=1)
    emask = jnp.repeat(gmask, N_EXPERTS // N_GROUPS, axis=1)
    _, eidx = lax.top_k(jnp.where(emask, sb, NEG), TOP_K)
    w = jnp.take_along_axis(s, eidx, axis=-1)
    w = w / jnp.sum(w, axis=-1, keepdims=True) * ROUTE_SCALE
    return eidx, w


def routed_experts(h, eidx, gw, w_g, w_u, w_d):
    N = h.shape[0]
    R = N * TOP_K
    e_flat = eidx.reshape(R)
    tok_flat = jnp.repeat(jnp.arange(N, dtype=jnp.int32), TOP_K)
    w_flat = gw.reshape(R)
    counts = jnp.bincount(e_flat, length=N_EXPERTS)
    padded = (counts + MOE_BLOCK - 1) // MOE_BLOCK * MOE_BLOCK
    pad_end = jnp.cumsum(padded)
    pad_start = pad_end - padded
    start = jnp.cumsum(counts) - counts
    order = jnp.argsort(e_flat)
    e_sorted = e_flat[order]
    dest = pad_start[e_sorted] + jnp.arange(R, dtype=jnp.int32) - start[e_sorted]
    n_blocks = -(-R // MOE_BLOCK) + N_EXPERTS
    rows = n_blocks * MOE_BLOCK
    row_tok = jnp.zeros((rows,), jnp.int32).at[dest].set(tok_flat[order])
    row_w = jnp.zeros((rows,), jnp.float32).at[dest].set(w_flat[order])
    blk_e = jnp.minimum(jnp.searchsorted(pad_end, jnp.arange(n_blocks, dtype=jnp.int32) * MOE_BLOCK, side='right'), N_EXPERTS - 1)

    def run_block(args):
        tok, e = args
        xb = h[tok]
        a = xb @ w_g[e]
        b = xb @ w_u[e]
        return (jax.nn.silu(a) * b) @ w_d[e]

    yb = lax.map(run_block, (row_tok.reshape(n_blocks, MOE_BLOCK), blk_e)).reshape(rows, D_MODEL)
    yb = yb * row_w[:, None].astype(yb.dtype)
    return jax.ops.segment_sum(yb, row_tok, num_segments=N)


def block(x, c, P, k_buf, v_buf, conv_prev, h0, prompt):
    B, T, _ = x.shape
    sh1, sc1, g1, sh2, sc2, g2 = adaln(c, P['ada_w'], P['ada_b'])
    h = rms_norm(x, P['norm_mix']) * (1 + sc1) + sh1
    proj = jnp.einsum('btd,de->bte', h, P['w_in'])
    xr, yr, q, k, v = jnp.split(proj, SPLITS, axis=-1)
    u, conv_new = causal_conv(xr, conv_prev, P['conv_w'], P['conv_b'])
    hr, h_new = rg_lru(u, h0, P['gate_a_w'], P['gate_a_b'], P['gate_x_w'], P['gate_x_b'], P['lru_lambda'])
    out_r = hr * jax.nn.gelu(yr)
    q = rms_norm(q.reshape(B, T, N_HEADS, HEAD_DIM), P['q_norm'])
    k = rms_norm(k.reshape(B, T, N_KV_HEADS, HEAD_DIM), P['k_norm'])
    v = v.reshape(B, T, N_KV_HEADS, HEAD_DIM)
    if prompt:
        out_a = attn_prompt(q, k, v, P['attn_sinks'])
        wk = min(WINDOW, T)
        k_new, v_new = k[:, -wk:], v[:, -wk:]
    else:
        out_a, k_new, v_new = attn_sample(q, k, v, k_buf, v_buf, P['attn_sinks'])
    mixed = jnp.einsum('bte,ed->btd', jnp.concatenate([out_r, out_a], axis=-1), P['w_out'])
    x = x + g1 * mixed
    h = (rms_norm(x, P['norm_ffn']) * (1 + sc2) + sh2).reshape(B * T, D_MODEL)
    eidx, gw = route(h, P['router_w'], P['router_bias'])
    routed = routed_experts(h, eidx, gw, P['exp_w_gate'], P['exp_w_up'], P['exp_w_down'])
    shared = (jax.nn.silu(h @ P['sh_w_gate']) * (h @ P['sh_w_up'])) @ P['sh_w_down']
    x = x + g2 * (routed + shared).reshape(B, T, D_MODEL)
    return x, k_new, v_new, conv_new, h_new


def setup_inputs(seed: int = 0) -> dict:
    key = jax.random.key(seed)
    ks = iter(jax.random.split(key, 40))
    f32 = jnp.float32
    nrm = lambda shape, scale: jax.random.normal(next(ks), shape, f32) * scale
    cache_win = min(WINDOW, PAST_LEN)
    u = jax.random.uniform(next(ks), (DEPTH, D_RNN), f32, 0.9, 0.999)
    a0 = u ** (1.0 / LRU_C)
    lam = jnp.log(a0) - jnp.log1p(-a0)
    return {
        'x_prompt': nrm((BATCH, SEQ, D_MODEL), 1.0),
        'x_sample': nrm((DEC_BATCH, DEC_SEQ, D_MODEL), 1.0),
        'c_prompt': nrm((BATCH, D_MODEL), 1.0),
        'c_sample': nrm((DEC_BATCH, D_MODEL), 1.0),
        'cache_k_win': nrm((DEPTH, DEC_BATCH, cache_win, N_KV_HEADS, HEAD_DIM), 1.0),
        'cache_v_win': nrm((DEPTH, DEC_BATCH, cache_win, N_KV_HEADS, HEAD_DIM), 1.0),
        'state_conv': nrm((DEPTH, DEC_BATCH, CONV_W - 1, D_RNN), 1.0),
        'state_rnn': nrm((DEPTH, DEC_BATCH, D_RNN), 0.5),
        'ada_w': nrm((DEPTH, D_MODEL, 6 * D_MODEL), 0.5 * D_MODEL ** -0.5),
        'ada_b': nrm((DEPTH, 6 * D_MODEL), 0.02),
        'norm_mix': 1.0 + nrm((DEPTH, D_MODEL), 0.05),
        'w_in': nrm((DEPTH, D_MODEL, D_IN), D_MODEL ** -0.5),
        'conv_w': nrm((DEPTH, CONV_W, D_RNN), CONV_W ** -0.5),
        'conv_b': nrm((DEPTH, D_RNN), 0.02),
        'gate_a_w': nrm((DEPTH, N_RNN_BLOCKS, RNN_BLOCK, RNN_BLOCK), RNN_BLOCK ** -0.5),
        'gate_a_b': nrm((DEPTH, D_RNN), 0.02),
        'gate_x_w': nrm((DEPTH, N_RNN_BLOCKS, RNN_BLOCK, RNN_BLOCK), RNN_BLOCK ** -0.5),
        'gate_x_b': nrm((DEPTH, D_RNN), 0.02),
        'lru_lambda': lam,
        'q_norm': 1.0 + nrm((DEPTH, HEAD_DIM), 0.05),
        'k_norm': 1.0 + nrm((DEPTH, HEAD_DIM), 0.05),
        'attn_sinks': nrm((DEPTH, N_HEADS), 1.0),
        'w_out': nrm((DEPTH, D_MIX, D_MODEL), D_MIX ** -0.5),
        'norm_ffn': 1.0 + nrm((DEPTH, D_MODEL), 0.05),
        'router_w': nrm((DEPTH, D_MODEL, N_EXPERTS), D_MODEL ** -0.5),
        'router_bias': nrm((DEPTH, N_EXPERTS), 0.01),
        'exp_w_gate': nrm((DEPTH, N_EXPERTS, D_MODEL, D_EXPERT), D_MODEL ** -0.5),
        'exp_w_up': nrm((DEPTH, N_EXPERTS, D_MODEL, D_EXPERT), D_MODEL ** -0.5),
        'exp_w_down': nrm((DEPTH, N_EXPERTS, D_EXPERT, D_MODEL), D_EXPERT ** -0.5),
        'sh_w_gate': nrm((DEPTH, D_MODEL, D_SHARED), D_MODEL ** -0.5),
        'sh_w_up': nrm((DEPTH, D_MODEL, D_SHARED), D_MODEL ** -0.5),
        'sh_w_down': nrm((DEPTH, D_SHARED, D_MODEL), D_SHARED ** -0.5),
    }


def reference(x_prompt, x_sample, c_prompt, c_sample, cache_k_win, cache_v_win, state_conv, state_rnn,
              ada_w, ada_b, norm_mix, w_in, conv_w, conv_b, gate_a_w, gate_a_b, gate_x_w, gate_x_b,
              lru_lambda, q_norm, k_norm, attn_sinks, w_out, norm_ffn, router_w, router_bias,
              exp_w_gate, exp_w_up, exp_w_down, sh_w_gate, sh_w_up, sh_w_down):
    yp, ys = x_prompt, x_sample
    kp_l, vp_l, cp_l, hp_l, ks_l, vs_l, cs_l, hs_l = [], [], [], [], [], [], [], []
    for l in range(DEPTH):
        P = {'ada_w': ada_w[l], 'ada_b': ada_b[l], 'norm_mix': norm_mix[l], 'w_in': w_in[l],
             'conv_w': conv_w[l], 'conv_b': conv_b[l], 'gate_a_w': gate_a_w[l], 'gate_a_b': gate_a_b[l],
             'gate_x_w': gate_x_w[l], 'gate_x_b': gate_x_b[l], 'lru_lambda': lru_lambda[l],
             'q_norm': q_norm[l], 'k_norm': k_norm[l], 'attn_sinks': attn_sinks[l], 'w_out': w_out[l],
             'norm_ffn': norm_ffn[l], 'router_w': router_w[l], 'router_bias': router_bias[l],
             'exp_w_gate': exp_w_gate[l], 'exp_w_up': exp_w_up[l], 'exp_w_down': exp_w_down[l],
             'sh_w_gate': sh_w_gate[l], 'sh_w_up': sh_w_up[l], 'sh_w_down': sh_w_down[l]}
        B = yp.shape[0]
        conv0 = jnp.zeros((B, CONV_W - 1, D_RNN), yp.dtype)
        h0 = jnp.zeros((B, D_RNN), state_rnn.dtype)
        yp, kp, vp, cp, hp = block(yp, c_prompt, P, None, None, conv0, h0, True)
        ys, ks_, vs_, cs_, hs_ = block(ys, c_sample, P, cache_k_win[l], cache_v_win[l], state_conv[l], state_rnn[l], False)
        kp_l.append(kp); vp_l.append(vp); cp_l.append(cp); hp_l.append(hp)
        ks_l.append(ks_); vs_l.append(vs_); cs_l.append(cs_); hs_l.append(hs_)
    return (yp, ys, jnp.stack(kp_l), jnp.stack(vp_l), jnp.stack(cp_l), jnp.stack(hp_l),
            jnp.stack(ks_l), jnp.stack(vs_l), jnp.stack(cs_l), jnp.stack(hs_l))
```

```python
import functools

import jax
import jax.numpy as jnp
from jax import lax
from jax.experimental import pallas as pl
from jax.experimental.pallas import tpu as pltpu

F32 = jnp.float32
BF16 = jnp.bfloat16
I32 = jnp.int32

D_MODEL = 1024
D_RNN = 512
N_RNN_BLOCKS = 8
CONV_W = 4
LRU_C = 8.0
HEAD_DIM = 64
N_HEADS = 8
N_KV_HEADS = 2
GQA = N_HEADS // N_KV_HEADS
D_ATT = N_HEADS * HEAD_DIM
D_KV = N_KV_HEADS * HEAD_DIM
WINDOW = 128
N_EXPERTS = 64
TOP_K = 6
N_GROUPS = 8
GROUP_SIZE = N_EXPERTS // N_GROUPS
TOPK_GROUPS = 4
D_EXPERT = 256
D_SHARED = 256
ROUTE_SCALE = 2.5
EPS = 1e-6
NEG = -1e30
D_IN = 2 * D_RNN + D_ATT + 2 * D_KV

SUBLANES = 8
LANES = 128
TM_PROMPT = 256
ATT_BLOCK = WINDOW
EXPERT_BLOCK = 256
SAMPLE_CHUNK = 8
WAIT_UNROLL = 16
VMEM_LIMIT = 48 * 1024 * 1024


def _sigmoid(x):
    return 1.0 / (1.0 + jnp.exp(-x))


def _silu(x):
    return x * _sigmoid(x)


def _gelu_tanh(x):
    c = 0.7978845608028654
    return x * (0.5 * (1.0 + jnp.tanh(c * (x + 0.044715 * (x * x * x)))))


def _log1p(x):
    u = 1.0 + x
    return jnp.where(u == 1.0, x, jnp.log(u) * x / jnp.where(u == 1.0, 1.0, u - 1.0))


def _neg_expm1(x):
    return -jnp.tanh(0.5 * x) * (jnp.exp(x) + 1.0)


def _softplus(z):
    return jnp.maximum(z, 0.0) + _log1p(jnp.exp(-jnp.abs(z)))


def _div_pow2(x, d):
    assert d & (d - 1) == 0
    return lax.shift_right_logical(x, d.bit_length() - 1)


def _mod_pow2(x, d):
    assert d & (d - 1) == 0
    return x & (d - 1)


def _norm_mod(x, g, sc, sh):
    ms = jnp.mean(x * x, axis=-1, keepdims=True)
    return (x * lax.rsqrt(ms + EPS)) * g * (1.0 + sc) + sh


def _bdot(a, b):
    return jnp.dot(a.astype(BF16), b.astype(BF16), preferred_element_type=F32)


def _bdot_nt(a, b):
    return lax.dot_general(a.astype(BF16), b.astype(BF16), (((1,), (1,)), ((), ())),
                           preferred_element_type=F32)


def _knorm(k, gk2):
    lane = lax.broadcasted_iota(I32, k.shape, 1)
    first = lane < HEAD_DIM
    k2 = k * k
    s0 = jnp.sum(jnp.where(first, k2, 0.0), axis=-1, keepdims=True)
    s1 = jnp.sum(jnp.where(first, 0.0, k2), axis=-1, keepdims=True)
    ms = jnp.where(first, s0, s1) * (1.0 / HEAD_DIM)
    return k * lax.rsqrt(ms + EPS) * gk2


def _lru_coeffs(u, wg, gb, lam):
    g = _bdot(u, wg) + gb
    r = _sigmoid(g[:, :D_RNN])
    i = _sigmoid(g[:, D_RNN:])
    log_a = (-LRU_C * r) * _softplus(-lam)
    a = jnp.exp(log_a)
    b = jnp.sqrt(_neg_expm1(2.0 * log_a)) * (i * u)
    return a, b


def _adaln_kernel(c_ref, w_ref, b_ref, o_ref):
    o_ref[...] = _bdot(_silu(c_ref[...]), w_ref[...]) + b_ref[...]


def _adaln(c_all, ada_w, ada_b):
    n = c_all.shape[0]
    return pl.pallas_call(
        _adaln_kernel,
        grid=(6,),
        in_specs=[pl.BlockSpec((n, D_MODEL), lambda j: (0, 0)),
                  pl.BlockSpec((D_MODEL, D_MODEL), lambda j: (0, j)),
                  pl.BlockSpec((1, D_MODEL), lambda j: (0, j))],
        out_specs=pl.BlockSpec((n, D_MODEL), lambda j: (0, j)),
        out_shape=jax.ShapeDtypeStruct((n, 6 * D_MODEL), F32),
        compiler_params=pltpu.CompilerParams(dimension_semantics=("arbitrary",), vmem_limit_bytes=VMEM_LIMIT),
        name="adaln",
    )(c_all, ada_w, ada_b.reshape(1, -1))


def _scan_rows(a, b, h_in):
    n = a.shape[0]
    row = lax.broadcasted_iota(I32, a.shape, 0)
    s = 1
    while s < n:
        m = row >= s
        a_sh = jnp.where(m, pltpu.roll(a, s, 0), 1.0)
        b_sh = jnp.where(m, pltpu.roll(b, s, 0), 0.0)
        b = a * b_sh + b
        a = a * a_sh
        s *= 2
    return a * h_in + b


def _front_prompt_kernel(x_ref, sc_ref, sh_ref, g_ref, win_ref, cw_ref, cb_ref, wg_ref, gb_ref, lam_ref, gk_ref,
                         prev_ref, h0_ref, r_ref, q_ref, k_ref, v_ref, cs_ref, hs_ref, tail_ref, hc_ref):
    j = pl.program_id(1)
    tm = x_ref.shape[0]

    @pl.when(j == 0)
    def _():
        tail_ref[...] = prev_ref[...]
        hc_ref[...] = h0_ref[...]

    h = _norm_mod(x_ref[...], g_ref[...], sc_ref[...], sh_ref[...])
    proj = jnp.dot(h.astype(BF16), win_ref[...], preferred_element_type=F32)
    xr = proj[:, 0:D_RNN]
    yr = proj[:, D_RNN:2 * D_RNN]
    q_ref[...] = proj[:, 2 * D_RNN:2 * D_RNN + D_ATT]
    k_ref[...] = _knorm(proj[:, 2 * D_RNN + D_ATT:2 * D_RNN + D_ATT + D_KV], gk_ref[...])
    v_ref[...] = proj[:, 2 * D_RNN + D_ATT + D_KV:D_IN]

    tail = tail_ref[...]
    row8 = lax.broadcasted_iota(I32, tail.shape, 0)

    def shifted(s):
        rolled = pltpu.roll(xr, s, 0)
        top = jnp.where(row8 < s, pltpu.roll(tail, s, 0), rolled[0:SUBLANES])
        return jnp.concatenate([top, rolled[SUBLANES:]], axis=0)

    cw = cw_ref[...]
    u = cb_ref[...] + shifted(3) * cw[0:1]
    u = u + shifted(2) * cw[1:2]
    u = u + shifted(1) * cw[2:3]
    u = u + xr * cw[3:4]
    tail_ref[...] = xr[tm - SUBLANES:tm]

    a, b = _lru_coeffs(u, wg_ref[...], gb_ref[...], lam_ref[...])
    hs = _scan_rows(a, b, hc_ref[...])
    hc_ref[...] = hs[tm - 1:tm]
    r_ref[...] = hs * _gelu_tanh(yr)

    @pl.when(j == pl.num_programs(1) - 1)
    def _():
        cs_ref[...] = xr[tm - SUBLANES:tm]
        hs_ref[...] = hs[tm - 1:tm]


def _front_prompt(x, sc, sh, g, win, cw, cb, wg, gb, lam, gk, prev, h0):
    bsz, t, _ = x.shape
    tm = TM_PROMPT
    full = lambda a: pl.BlockSpec(a.shape, lambda b, j: (0,) * a.ndim)
    per_b = lambda a: pl.BlockSpec((None,) + a.shape[1:], lambda b, j: (b,) + (0,) * (a.ndim - 1))
    tile = lambda w: pl.BlockSpec((None, tm, w), lambda b, j: (b, j, 0))
    return pl.pallas_call(
        _front_prompt_kernel,
        grid=(bsz, t // tm),
        in_specs=[tile(D_MODEL), per_b(sc), per_b(sh), full(g), full(win), full(cw), full(cb), full(wg), full(gb),
                  full(lam), full(gk), per_b(prev), per_b(h0)],
        out_specs=[tile(D_RNN), tile(D_ATT), tile(D_KV), tile(D_KV),
                   pl.BlockSpec((None, SUBLANES, D_RNN), lambda b, j: (b, 0, 0)),
                   pl.BlockSpec((None, 1, D_RNN), lambda b, j: (b, 0, 0))],
        out_shape=[jax.ShapeDtypeStruct((bsz, t, D_RNN), F32), jax.ShapeDtypeStruct((bsz, t, D_ATT), F32),
                   jax.ShapeDtypeStruct((bsz, t, D_KV), F32), jax.ShapeDtypeStruct((bsz, t, D_KV), F32),
                   jax.ShapeDtypeStruct((bsz, SUBLANES, D_RNN), F32), jax.ShapeDtypeStruct((bsz, 1, D_RNN), F32)],
        scratch_shapes=[pltpu.VMEM((SUBLANES, D_RNN), F32), pltpu.VMEM((1, D_RNN), F32)],
        compiler_params=pltpu.CompilerParams(dimension_semantics=("arbitrary", "arbitrary"),
                                             vmem_limit_bytes=VMEM_LIMIT),
        name="front_prompt",
    )(x, sc, sh, g, win, cw, cb, wg, gb, lam, gk, prev, h0)


def _front_sample_kernel(x_ref, sc_ref, sh_ref, g_ref, win_ref, cw_ref, cb_ref, wg_ref, gb_ref, lam_ref, gk_ref,
                         prev_ref, h0_ref, r_ref, q_ref, k_ref, v_ref, cs_ref, hs_ref):
    t_len, bsz, _ = x_ref.shape
    x = x_ref[...]
    ms = jnp.mean(x * x, axis=-1, keepdims=True)
    h = (x * lax.rsqrt(ms + EPS)) * g_ref[...] * (1.0 + sc_ref[...]) + sh_ref[...]
    proj = jnp.dot(h.reshape(t_len * bsz, D_MODEL).astype(BF16), win_ref[...], preferred_element_type=F32)
    xr = proj[:, 0:D_RNN]
    yr = proj[:, D_RNN:2 * D_RNN]
    q_ref[...] = proj[:, 2 * D_RNN:2 * D_RNN + D_ATT].reshape(t_len, bsz, D_ATT)
    k_ref[...] = _knorm(proj[:, 2 * D_RNN + D_ATT:2 * D_RNN + D_ATT + D_KV], gk_ref[...]).reshape(t_len, bsz, D_KV)
    v_ref[...] = proj[:, 2 * D_RNN + D_ATT + D_KV:D_IN].reshape(t_len, bsz, D_KV)

    def at_time(t):
        if t >= 0:
            return xr[t * bsz:(t + 1) * bsz]
        return prev_ref[CONV_W - 1 + t]

    cw = cw_ref[...]
    us = []
    for t in range(t_len):
        u = cb_ref[...] + at_time(t - 3) * cw[0:1]
        u = u + at_time(t - 2) * cw[1:2]
        u = u + at_time(t - 1) * cw[2:3]
        u = u + at_time(t) * cw[3:4]
        us.append(u)
    a, b = _lru_coeffs(jnp.concatenate(us, axis=0), wg_ref[...], gb_ref[...], lam_ref[...])
    hcur = h0_ref[...]
    for t in range(t_len):
        hcur = a[t * bsz:(t + 1) * bsz] * hcur + b[t * bsz:(t + 1) * bsz]
        r_ref[t] = hcur * _gelu_tanh(yr[t * bsz:(t + 1) * bsz])
    hs_ref[...] = hcur
    for s in range(CONV_W - 1):
        cs_ref[s] = at_time(t_len - (CONV_W - 1) + s)


def _front_sample(x_t, sc, sh, g, win, cw, cb, wg, gb, lam, gk, prev_t, h0):
    t_len, bsz, _ = x_t.shape
    return pl.pallas_call(
        _front_sample_kernel,
        out_shape=[jax.ShapeDtypeStruct((t_len, bsz, D_RNN), F32), jax.ShapeDtypeStruct((t_len, bsz, D_ATT), F32),
                   jax.ShapeDtypeStruct((t_len, bsz, D_KV), F32), jax.ShapeDtypeStruct((t_len, bsz, D_KV), F32),
                   jax.ShapeDtypeStruct((CONV_W - 1, bsz, D_RNN), F32), jax.ShapeDtypeStruct((bsz, D_RNN), F32)],
        compiler_params=pltpu.CompilerParams(vmem_limit_bytes=VMEM_LIMIT),
        name="front_sample",
    )(x_t, sc, sh, g, win, cw, cb, wg, gb, lam, gk, prev_t, h0)


def _qnorm(q, gq):
    ms = jnp.mean(q * q, axis=-1, keepdims=True)
    return q * lax.rsqrt(ms + EPS) * gq


def _attn_prompt_kernel(sink_ref, q_ref, kp_ref, kc_ref, vp_ref, vc_ref, gq_ref, o_ref):
    j = pl.program_id(1)
    blk = q_ref.shape[0]
    q = q_ref[...]
    kk = jnp.concatenate([kp_ref[...], kc_ref[...]], axis=0)
    vv = jnp.concatenate([vp_ref[...], vc_ref[...]], axis=0)
    qi = lax.broadcasted_iota(I32, (blk, 2 * blk), 0)
    kj = lax.broadcasted_iota(I32, (blk, 2 * blk), 1)
    dist = blk + qi - kj
    valid = (dist >= 0) & (dist <= WINDOW) & ((kj >= blk) | (j > 0))
    distf = dist.astype(F32)
    outs = []
    for h in range(N_HEADS):
        g = h // GQA
        qh = _qnorm(q[:, h * HEAD_DIM:(h + 1) * HEAD_DIM], gq_ref[...])
        s = _bdot_nt(qh, kk[:, g * HEAD_DIM:(g + 1) * HEAD_DIM]) * (HEAD_DIM ** -0.5)
        s = s - (2.0 ** -(h + 1)) * distf
        s = jnp.where(valid, s, NEG)
        sink = sink_ref[h]
        m = jnp.maximum(jnp.max(s, axis=-1, keepdims=True), sink)
        p = jnp.exp(s - m)
        l = jnp.sum(p, axis=-1, keepdims=True) + jnp.exp(sink - m)
        outs.append(_bdot(p, vv[:, g * HEAD_DIM:(g + 1) * HEAD_DIM]) / l)
    o_ref[...] = jnp.concatenate(outs, axis=1)


def _attn_prompt(q, k, v, gq, sinks):
    bsz, t, _ = q.shape
    blk = ATT_BLOCK
    cur = lambda w: pl.BlockSpec((None, blk, w), lambda b, j: (b, j, 0))
    prv = lambda w: pl.BlockSpec((None, blk, w), lambda b, j: (b, jnp.maximum(j - 1, 0), 0))
    return pl.pallas_call(
        _attn_prompt_kernel,
        grid=(bsz, t // blk),
        in_specs=[pl.BlockSpec(memory_space=pltpu.SMEM), cur(D_ATT), prv(D_KV), cur(D_KV), prv(D_KV), cur(D_KV),
                  pl.BlockSpec(gq.shape, lambda b, j: (0, 0))],
        out_specs=cur(D_ATT),
        out_shape=jax.ShapeDtypeStruct((bsz, t, D_ATT), F32),
        compiler_params=pltpu.CompilerParams(dimension_semantics=("arbitrary", "arbitrary"),
                                             vmem_limit_bytes=VMEM_LIMIT),
        name="attn_prompt",
    )(sinks, q, k, k, v, v, gq)


def _attn_sample_kernel(sink_ref, q_ref, kn_ref, vn_ref, kc_ref, vc_ref, gq_ref, o_ref):
    t_len, cb, _ = q_ref.shape
    win = kc_ref.shape[1]
    rows = GQA * t_len * cb
    kc = kc_ref[...].reshape(cb * win, D_KV)
    vc = vc_ref[...].reshape(cb * win, D_KV)
    kn = kn_ref[...].reshape(t_len * cb, D_KV)
    vn = vn_ref[...].reshape(t_len * cb, D_KV)

    r_c = lax.broadcasted_iota(I32, (rows, cb * win), 0)
    c_c = lax.broadcasted_iota(I32, (rows, cb * win), 1)
    t_c = _div_pow2(_mod_pow2(r_c, t_len * cb), cb)
    valid_c = (_mod_pow2(r_c, cb) == _div_pow2(c_c, win)) & (_mod_pow2(c_c, win) >= t_c)
    dist_c = (win + t_c - _mod_pow2(c_c, win)).astype(F32)
    r_n = lax.broadcasted_iota(I32, (rows, t_len * cb), 0)
    c_n = lax.broadcasted_iota(I32, (rows, t_len * cb), 1)
    t_n = _div_pow2(_mod_pow2(r_n, t_len * cb), cb)
    valid_n = (_mod_pow2(r_n, cb) == _mod_pow2(c_n, cb)) & (_div_pow2(c_n, cb) <= t_n)
    dist_n = (t_n - _div_pow2(c_n, cb)).astype(F32)
    hl = _div_pow2(lax.broadcasted_iota(I32, (rows, 1), 0), t_len * cb)

    per_group = []
    for g in range(N_KV_HEADS):
        slabs = [q_ref[t][:, (g * GQA + i) * HEAD_DIM:(g * GQA + i + 1) * HEAD_DIM]
                 for i in range(GQA) for t in range(t_len)]
        qg = _qnorm(jnp.concatenate(slabs, axis=0), gq_ref[...])
        slope = jnp.zeros((rows, 1), F32)
        sink = jnp.zeros((rows, 1), F32)
        for i in range(GQA):
            slope = jnp.where(hl == i, 2.0 ** -(g * GQA + i + 1), slope)
            sink = jnp.where(hl == i, sink_ref[g * GQA + i], sink)
        lo, hi = g * HEAD_DIM, (g + 1) * HEAD_DIM
        s_c = _bdot_nt(qg, kc[:, lo:hi]) * (HEAD_DIM ** -0.5) - slope * dist_c
        s_n = _bdot_nt(qg, kn[:, lo:hi]) * (HEAD_DIM ** -0.5) - slope * dist_n
        s_c = jnp.where(valid_c, s_c, NEG)
        s_n = jnp.where(valid_n, s_n, NEG)
        m = jnp.maximum(jnp.maximum(jnp.max(s_c, axis=-1, keepdims=True), jnp.max(s_n, axis=-1, keepdims=True)), sink)
        p_c = jnp.exp(s_c - m)
        p_n = jnp.exp(s_n - m)
        l = jnp.sum(p_c, axis=-1, keepdims=True) + jnp.sum(p_n, axis=-1, keepdims=True) + jnp.exp(sink - m)
        per_group.append((_bdot(p_c, vc[:, lo:hi]) + _bdot(p_n, vn[:, lo:hi])) / l)
    for t in range(t_len):
        o_ref[t] = jnp.concatenate(
            [per_group[g][(i * t_len + t) * cb:(i * t_len + t + 1) * cb] for g in range(N_KV_HEADS) for i in range(GQA)],
            axis=1)


def _attn_sample(q_t, k_t, v_t, cache_k, cache_v, gq, sinks):
    t_len, bsz, _ = q_t.shape
    cb = SAMPLE_CHUNK
    win = cache_k.shape[1]
    new = lambda w: pl.BlockSpec((t_len, cb, w), lambda c: (0, c, 0))
    old = pl.BlockSpec((cb, win, D_KV), lambda c: (c, 0, 0))
    return pl.pallas_call(
        _attn_sample_kernel,
        grid=(bsz // cb,),
        in_specs=[pl.BlockSpec(memory_space=pltpu.SMEM), new(D_ATT), new(D_KV), new(D_KV), old, old,
                  pl.BlockSpec(gq.shape, lambda c: (0, 0))],
        out_specs=new(D_ATT),
        out_shape=jax.ShapeDtypeStruct((t_len, bsz, D_ATT), F32),
        compiler_params=pltpu.CompilerParams(dimension_semantics=("arbitrary",), vmem_limit_bytes=VMEM_LIMIT),
        name="attn_sample",
    )(sinks, q_t, k_t, v_t, cache_k, cache_v, gq)


def _route(s_t, sb_t):
    tm = s_t.shape[1]
    i8 = lax.broadcasted_iota(I32, (GROUP_SIZE, tm), 0)
    ninf = -jnp.inf
    sg = [sb_t[GROUP_SIZE * g:GROUP_SIZE * (g + 1)] for g in range(N_GROUPS)]
    gscore = []
    for g in range(N_GROUPS):
        m1 = jnp.max(sg[g], axis=0, keepdims=True)
        i1 = jnp.min(jnp.where(sg[g] == m1, i8, GROUP_SIZE), axis=0, keepdims=True)
        m2 = jnp.max(jnp.where(i8 == i1, ninf, sg[g]), axis=0, keepdims=True)
        gscore.append(m1 + m2)
    gs = jnp.concatenate(gscore, axis=0)
    gsel = jnp.zeros((N_GROUPS, tm), I32)
    for _ in range(TOPK_GROUPS):
        m = jnp.max(gs, axis=0, keepdims=True)
        idx = jnp.min(jnp.where(gs == m, i8, N_GROUPS), axis=0, keepdims=True)
        hit = i8 == idx
        gsel = jnp.where(hit, 1, gsel)
        gs = jnp.where(hit, ninf, gs)
    sm = [jnp.where(gsel[g:g + 1] > 0, sg[g], NEG) for g in range(N_GROUPS)]
    eid = [i8 + GROUP_SIZE * g for g in range(N_GROUPS)]
    sel = [jnp.zeros((GROUP_SIZE, tm), F32) for _ in range(N_GROUPS)]
    idxs, ws = [], []
    for _ in range(TOP_K):
        cm = functools.reduce(jnp.maximum, sm)
        m = jnp.max(cm, axis=0, keepdims=True)
        cand = functools.reduce(jnp.minimum, [jnp.where(sm[g] == m, eid[g], N_EXPERTS) for g in range(N_GROUPS)])
        idx = jnp.min(cand, axis=0, keepdims=True)
        wk = jnp.zeros((GROUP_SIZE, tm), F32)
        for g in range(N_GROUPS):
            hit = eid[g] == idx
            wk = wk + jnp.where(hit, s_t[GROUP_SIZE * g:GROUP_SIZE * (g + 1)], 0.0)
            sel[g] = jnp.where(hit, 1.0, sel[g])
            sm[g] = jnp.where(hit, ninf, sm[g])
        idxs.append(idx)
        ws.append(jnp.sum(wk, axis=0, keepdims=True))
    return idxs, ws, jnp.concatenate(sel, axis=0), eid


def _post_kernel(x_ref, r_ref, a_ref, g1_ref, sc2_ref, sh2_ref, nf_ref, wot_ref, wob_ref, wrt_ref, rb_ref, cin_ref,
                 x1_ref, h2_ref, eidx_ref, gw_ref, rank_ref, cnt_ref, carry_ref):
    tm = x_ref.shape[0]

    @pl.when(pl.program_id(0) == 0)
    def _():
        carry_ref[...] = cin_ref[...]

    mixed = (jnp.dot(r_ref[...].astype(BF16), wot_ref[...], preferred_element_type=F32)
             + jnp.dot(a_ref[...].astype(BF16), wob_ref[...], preferred_element_type=F32))
    x1 = x_ref[...] + g1_ref[...] * mixed
    x1_ref[...] = x1
    h2 = _norm_mod(x1, nf_ref[...], sc2_ref[...], sh2_ref[...])
    h2_ref[...] = h2

    wr = wrt_ref[...]
    wr_hi = wr.astype(BF16)
    wr_lo = (wr - wr_hi.astype(F32)).astype(BF16)
    h_hi = h2.astype(BF16)
    h_lo = (h2 - h_hi.astype(F32)).astype(BF16)
    logits = _bdot_nt(wr_hi, h_hi) + _bdot_nt(wr_hi, h_lo) + _bdot_nt(wr_lo, h_hi)
    s_t = _sigmoid(logits)
    idxs, ws, sel, eid = _route(s_t, s_t + rb_ref[...])

    rr = lax.broadcasted_iota(I32, (tm, tm), 0)
    cc = lax.broadcasted_iota(I32, (tm, tm), 1)
    before = jnp.where(rr < cc, 1.0, 0.0).astype(BF16)
    carry = carry_ref[...]
    tot = jnp.dot(sel.astype(BF16), before, preferred_element_type=F32) + carry[:, 0:1]
    ranks = []
    for k in range(TOP_K):
        acc = jnp.zeros((GROUP_SIZE, tm), F32)
        for g in range(N_GROUPS):
            acc = acc + jnp.where(eid[g] == idxs[k], tot[GROUP_SIZE * g:GROUP_SIZE * (g + 1)], 0.0)
        ranks.append(jnp.sum(acc, axis=0, keepdims=True))
    carry = carry + jnp.sum(sel, axis=1, keepdims=True)
    carry_ref[...] = carry
    cnt_ref[...] = carry

    wsum = functools.reduce(lambda p, q: p + q, ws)
    pad_i = jnp.zeros((SUBLANES - TOP_K, tm), I32)
    pad_f = jnp.zeros((SUBLANES - TOP_K, tm), F32)
    eidx_ref[...] = jnp.concatenate(idxs + [pad_i], axis=0)
    rank_ref[...] = jnp.concatenate([r.astype(I32) for r in ranks] + [pad_i], axis=0)
    gw_ref[...] = jnp.concatenate([w / wsum * ROUTE_SCALE for w in ws] + [pad_f], axis=0)


def _post(x, r, a, g1, sc2, sh2, nf, wo_top, wo_bot, wr_t, rb, cnt_in, tm, mod_spec):
    n = x.shape[0]
    tile = lambda w: pl.BlockSpec((tm, w), lambda i: (i, 0))
    full = lambda arr: pl.BlockSpec(arr.shape, lambda i: (0,) * arr.ndim)
    slot = pl.BlockSpec((SUBLANES, tm), lambda i: (0, i))
    return pl.pallas_call(
        _post_kernel,
        grid=(n // tm,),
        in_specs=[tile(D_MODEL), tile(D_RNN), tile(D_ATT), mod_spec, mod_spec, mod_spec, full(nf), full(wo_top),
                  full(wo_bot), full(wr_t), full(rb), full(cnt_in)],
        out_specs=[tile(D_MODEL), tile(D_MODEL), slot, slot, slot, full(cnt_in)],
        out_shape=[jax.ShapeDtypeStruct((n, D_MODEL), F32), jax.ShapeDtypeStruct((n, D_MODEL), F32),
                   jax.ShapeDtypeStruct((SUBLANES, n), I32), jax.ShapeDtypeStruct((SUBLANES, n), F32),
                   jax.ShapeDtypeStruct((SUBLANES, n), I32), jax.ShapeDtypeStruct(cnt_in.shape, F32)],
        scratch_shapes=[pltpu.VMEM(cnt_in.shape, F32)],
        compiler_params=pltpu.CompilerParams(dimension_semantics=("arbitrary",), vmem_limit_bytes=VMEM_LIMIT),
        name="post_mix",
    )(x, r, a, g1, sc2, sh2, nf, wo_top, wo_bot, wr_t, rb, cnt_in)


def _row_copy(src_ref, src_row, dst_ref, dst_row, sem):
    return pltpu.make_async_copy(src_ref.at[pl.ds(src_row, 1), :], dst_ref.at[pl.ds(dst_row, 1), :], sem)


def _wait_rows(copy, count):
    def body(_, c):
        for _ in range(WAIT_UNROLL):
            copy.wait()
        return c
    lax.fori_loop(0, count // WAIT_UNROLL, body, 0)


def _dispatch_kernel(pstart_ref, pend_ref, nbu_ref, eidx_a, rank_a, h2_a, eidx_b, rank_b, h2_b, xs_ref, zero_ref, sem,
                     *, tiles_a):
    i = pl.program_id(0)
    blk = zero_ref.shape[0]
    n_blocks = xs_ref.shape[0] // blk

    @pl.when(i == 0)
    def _():
        zero_ref[...] = jnp.zeros_like(zero_ref)

        def zcopy(lo):
            return pltpu.make_async_copy(zero_ref, xs_ref.at[pl.ds(pl.multiple_of(lo, blk), blk), :], sem)

        def start_expert(e, c):
            zcopy(jnp.maximum(pend_ref[e] - blk, 0)).start()
            return c

        def start_block(b, c):
            zcopy(b * blk).start()
            return c

        def wait(_, c):
            zcopy(0).wait()
            return c

        lax.fori_loop(0, N_EXPERTS, start_expert, 0)
        lax.fori_loop(nbu_ref[0], n_blocks, start_block, 0)
        lax.fori_loop(0, N_EXPERTS + n_blocks - nbu_ref[0], wait, 0)

    def scatter_rows(eidx_ref, rank_ref, h2_ref):
        tm = h2_ref.shape[0]

        def issue(n, c):
            for k in range(TOP_K):
                dst = pstart_ref[eidx_ref[k, n]] + rank_ref[k, n]
                _row_copy(h2_ref, n, xs_ref, dst, sem).start()
            return c

        lax.fori_loop(0, tm, issue, 0)
        _wait_rows(_row_copy(h2_ref, 0, xs_ref, 0, sem), tm * TOP_K)

    @pl.when(i < tiles_a)
    def _():
        scatter_rows(eidx_a, rank_a, h2_a)

    @pl.when(i >= tiles_a)
    def _():
        scatter_rows(eidx_b, rank_b, h2_b)


def _dispatch(pstart, pend, nb_used, seg_a, seg_b, total_rows):
    (eidx_a, rank_a, h2_a, tm_a), (eidx_b, rank_b, h2_b, tm_b) = seg_a, seg_b
    tiles_a, tiles_b = h2_a.shape[0] // tm_a, h2_b.shape[0] // tm_b
    ia = lambda i: jnp.minimum(i, tiles_a - 1)
    ib = lambda i: jnp.maximum(i - tiles_a, 0)

    def seg_specs(tm, idx):
        slot = pl.BlockSpec((SUBLANES, tm), lambda i, *s: (0, idx(i)), memory_space=pltpu.SMEM)
        return [slot, slot, pl.BlockSpec((tm, D_MODEL), lambda i, *s: (idx(i), 0))]

    grid_spec = pltpu.PrefetchScalarGridSpec(
        num_scalar_prefetch=3,
        grid=(tiles_a + tiles_b,),
        in_specs=seg_specs(tm_a, ia) + seg_specs(tm_b, ib),
        out_specs=pl.BlockSpec(memory_space=pl.ANY),
        scratch_shapes=[pltpu.VMEM((EXPERT_BLOCK, D_MODEL), F32), pltpu.SemaphoreType.DMA(())],
    )
    return pl.pallas_call(
        functools.partial(_dispatch_kernel, tiles_a=tiles_a),
        grid_spec=grid_spec,
        out_shape=jax.ShapeDtypeStruct((total_rows, D_MODEL), F32),
        compiler_params=pltpu.CompilerParams(dimension_semantics=("arbitrary",), vmem_limit_bytes=VMEM_LIMIT),
        name="moe_dispatch",
    )(pstart, pend, nb_used, eidx_a, rank_a, h2_a, eidx_b, rank_b, h2_b)


def _experts_kernel(be_ref, xs_ref, wg_ref, wu_ref, wd_ref, y_ref):
    del be_ref
    x = xs_ref[...].astype(BF16)
    a = jnp.dot(x, wg_ref[...].astype(BF16), preferred_element_type=F32)
    b = jnp.dot(x, wu_ref[...].astype(BF16), preferred_element_type=F32)
    y_ref[...] = _bdot(_silu(a) * b, wd_ref[...])


def _experts(blk_e, xs, wg, wu, wd):
    rows = xs.shape[0]
    bm = EXPERT_BLOCK
    row_blk = lambda i, be: (i, 0)
    w_blk = lambda i, be: (be[i], 0, 0)
    grid_spec = pltpu.PrefetchScalarGridSpec(
        num_scalar_prefetch=1,
        grid=(rows // bm,),
        in_specs=[pl.BlockSpec((bm, D_MODEL), row_blk),
                  pl.BlockSpec((None, D_MODEL, D_EXPERT), w_blk),
                  pl.BlockSpec((None, D_MODEL, D_EXPERT), w_blk),
                  pl.BlockSpec((None, D_EXPERT, D_MODEL), w_blk)],
        out_specs=pl.BlockSpec((bm, D_MODEL), row_blk),
    )
    return pl.pallas_call(
        _experts_kernel,
        grid_spec=grid_spec,
        out_shape=jax.ShapeDtypeStruct((rows, D_MODEL), F32),
        compiler_params=pltpu.CompilerParams(dimension_semantics=("arbitrary",), vmem_limit_bytes=VMEM_LIMIT),
        name="moe_experts",
    )(blk_e, xs, wg, wu, wd)


def _combine_kernel(pstart_ref, eidx_ref, rank_ref, gw_ref, x1_ref, h2_ref, g2_ref, wsg_ref, wsu_ref, wsd_ref, y_ref,
                    o_ref, ybuf_ref, sem):
    tm = x1_ref.shape[0]

    def issue(n, c):
        for k in range(TOP_K):
            src = pstart_ref[eidx_ref[k, n]] + rank_ref[k, n]
            _row_copy(y_ref, src, ybuf_ref.at[k], n, sem).start()
        return c

    lax.fori_loop(0, tm, issue, 0)

    h2 = h2_ref[...].astype(BF16)
    sa = jnp.dot(h2, wsg_ref[...], preferred_element_type=F32)
    sb = jnp.dot(h2, wsu_ref[...], preferred_element_type=F32)
    acc = jnp.dot((_silu(sa) * sb).astype(BF16), wsd_ref[...], preferred_element_type=F32)

    gw = jnp.concatenate([gw_ref[...], jnp.zeros((LANES - SUBLANES, tm), F32)], axis=0).T

    _wait_rows(_row_copy(y_ref, 0, ybuf_ref.at[0], 0, sem), tm * TOP_K)
    for k in range(TOP_K):
        acc = acc + ybuf_ref[k] * gw[:, k:k + 1]
    o_ref[...] = x1_ref[...] + g2_ref[...] * acc


def _combine(pstart, eidx, rank, gw, x1, h2, g2, wsg, wsu, wsd, y, tm, mod_spec_fn):
    n = x1.shape[0]
    slot_smem = pl.BlockSpec((SUBLANES, tm), lambda i, ps: (0, i), memory_space=pltpu.SMEM)
    slot = pl.BlockSpec((SUBLANES, tm), lambda i, ps: (0, i))
    tile = pl.BlockSpec((tm, D_MODEL), lambda i, ps: (i, 0))
    full = lambda arr: pl.BlockSpec(arr.shape, lambda i, ps: (0,) * arr.ndim)
    grid_spec = pltpu.PrefetchScalarGridSpec(
        num_scalar_prefetch=1,
        grid=(n // tm,),
        in_specs=[slot_smem, slot_smem, slot, tile, tile, mod_spec_fn, full(wsg), full(wsu), full(wsd),
                  pl.BlockSpec(memory_space=pl.ANY)],
        out_specs=tile,
        scratch_shapes=[pltpu.VMEM((TOP_K, tm, D_MODEL), F32), pltpu.SemaphoreType.DMA(())],
    )
    return pl.pallas_call(
        _combine_kernel,
        grid_spec=grid_spec,
        out_shape=jax.ShapeDtypeStruct((n, D_MODEL), F32),
        compiler_params=pltpu.CompilerParams(dimension_semantics=("arbitrary",), vmem_limit_bytes=VMEM_LIMIT),
        name="moe_combine",
    )(pstart, eidx, rank, gw, x1, h2, g2, wsg, wsu, wsd, y)


def _block_diag(w):
    nb, bi, bj = w.shape
    return jnp.einsum('nij,nm->nimj', w, jnp.eye(nb, dtype=w.dtype)).reshape(nb * bi, nb * bj)


def kernel(x_prompt, x_sample, c_prompt, c_sample, cache_k_win, cache_v_win, state_conv, state_rnn, ada_w, ada_b, norm_mix, w_in, conv_w, conv_b, gate_a_w, gate_a_b, gate_x_w, gate_x_b, lru_lambda, q_norm, k_norm, attn_sinks, w_out, norm_ffn, router_w, router_bias, exp_w_gate, exp_w_up, exp_w_down, sh_w_gate, sh_w_up, sh_w_down):
    bp, tp, _ = x_prompt.shape
    bs, ts, _ = x_sample.shape
    win = cache_k_win.shape[2]
    n_p, n_s = bp * tp, bs * ts
    row = lambda v: v.reshape(1, -1)

    g_mix, g_ffn = row(norm_mix[0]), row(norm_ffn[0])
    win_bf = w_in[0].astype(BF16)
    cw, cb = conv_w[0], row(conv_b[0])
    wg = jnp.concatenate([_block_diag(gate_a_w[0]), _block_diag(gate_x_w[0])], axis=1).astype(BF16)
    gb = row(jnp.concatenate([gate_a_b[0], gate_x_b[0]]))
    lam = row(lru_lambda[0])
    gq = row(q_norm[0])
    gk2 = row(jnp.tile(k_norm[0], N_KV_HEADS))
    sinks = attn_sinks[0]
    wo_top, wo_bot = w_out[0, :D_RNN].astype(BF16), w_out[0, D_RNN:].astype(BF16)
    wr_t = router_w[0].T
    rb = router_bias[0].reshape(N_EXPERTS, 1)
    wsg, wsu, wsd = sh_w_gate[0].astype(BF16), sh_w_up[0].astype(BF16), sh_w_down[0].astype(BF16)

    mod = _adaln(jnp.concatenate([c_prompt, c_sample], axis=0), ada_w[0], ada_b[0])
    chunks = [mod[:, i * D_MODEL:(i + 1) * D_MODEL] for i in range(6)]
    sh1p, sc1p, g1p, sh2p, sc2p, g2p = [c[:bp].reshape(bp, 1, D_MODEL) for c in chunks]
    sh1s, sc1s, g1s, sh2s, sc2s, g2s = [c[bp:] for c in chunks]

    conv0 = jnp.zeros((bp, SUBLANES, D_RNN), F32)
    h0 = jnp.zeros((bp, 1, D_RNN), F32)
    r_p, q_p, k_p, v_p, cs_p, hs_p = _front_prompt(x_prompt, sc1p, sh1p, g_mix, win_bf, cw, cb, wg, gb, lam, gk2,
                                                   conv0, h0)
    a_p = _attn_prompt(q_p, k_p, v_p, gq, sinks)

    x_s_t = jnp.swapaxes(x_sample, 0, 1)
    r_s, q_s, k_s, v_s, cs_s, hs_s = _front_sample(x_s_t, sc1s, sh1s, g_mix, win_bf, cw, cb, wg, gb, lam, gk2,
                                                   jnp.swapaxes(state_conv[0], 0, 1), state_rnn[0])
    cache_k = cache_k_win[0].reshape(bs, win, D_KV)
    cache_v = cache_v_win[0].reshape(bs, win, D_KV)
    a_s = _attn_sample(q_s, k_s, v_s, cache_k, cache_v, gq, sinks)

    tiles_per_seq = tp // TM_PROMPT
    mod_p = pl.BlockSpec((None, 1, D_MODEL), lambda i, *_: (i // tiles_per_seq, 0, 0))
    mod_s = pl.BlockSpec((bs, D_MODEL), lambda i, *_: (0, 0))
    cnt0 = jnp.zeros((N_EXPERTS, LANES), F32)
    x1_p, h2_p, eidx_p, gw_p, rank_p, cnt_p = _post(
        x_prompt.reshape(n_p, D_MODEL), r_p.reshape(n_p, D_RNN), a_p.reshape(n_p, D_ATT), g1p, sc2p, sh2p, g_ffn,
        wo_top, wo_bot, wr_t, rb, cnt0, TM_PROMPT, mod_p)
    x1_s, h2_s, eidx_s, gw_s, rank_s, cnt_all = _post(
        x_s_t.reshape(n_s, D_MODEL), r_s.reshape(n_s, D_RNN), a_s.reshape(n_s, D_ATT), g1s, sc2s, sh2s, g_ffn,
        wo_top, wo_bot, wr_t, rb, cnt_p, bs, mod_s)

    bm = EXPERT_BLOCK
    n_blocks = -(-((n_p + n_s) * TOP_K) // bm) + N_EXPERTS
    counts = cnt_all[:, 0].astype(I32)
    padded = (counts + bm - 1) // bm * bm
    pend = jnp.cumsum(padded).astype(I32)
    pstart = pend - padded
    nb_used = jnp.maximum(pend[-1:] // bm, 1)
    blk_e = jnp.minimum(jnp.searchsorted(pend, jnp.arange(n_blocks, dtype=I32) * bm, side='right'),
                        N_EXPERTS - 1).astype(I32)

    xs = _dispatch(pstart, pend, nb_used, (eidx_p, rank_p, h2_p, TM_PROMPT), (eidx_s, rank_s, h2_s, bs),
                   n_blocks * bm)
    y = _experts(blk_e, xs, exp_w_gate[0], exp_w_up[0], exp_w_down[0])

    y_p = _combine(pstart, eidx_p, rank_p, gw_p, x1_p, h2_p, g2p, wsg, wsu, wsd, y, TM_PROMPT, mod_p)
    y_s = _combine(pstart, eidx_s, rank_s, gw_s, x1_s, h2_s, g2s, wsg, wsu, wsd, y, bs, mod_s)

    y_prompt = y_p.reshape(bp, tp, D_MODEL)
    y_sample = jnp.swapaxes(y_s.reshape(ts, bs, D_MODEL), 0, 1)
    wk = min(WINDOW, tp)
    k_win_p = k_p[:, tp - wk:].reshape(1, bp, wk, N_KV_HEADS, HEAD_DIM)
    v_win_p = v_p[:, tp - wk:].reshape(1, bp, wk, N_KV_HEADS, HEAD_DIM)
    k_new = jnp.swapaxes(k_s, 0, 1)
    v_new = jnp.swapaxes(v_s, 0, 1)
    k_win_s = jnp.concatenate([cache_k, k_new], axis=1)[:, ts:].reshape(1, bs, win, N_KV_HEADS, HEAD_DIM)
    v_win_s = jnp.concatenate([cache_v, v_new], axis=1)[:, ts:].reshape(1, bs, win, N_KV_HEADS, HEAD_DIM)
    return (y_prompt, y_sample, k_win_p, v_win_p, cs_p[None, :, SUBLANES - (CONV_W - 1):], hs_p.reshape(1, bp, D_RNN),
            k_win_s, v_win_s, jnp.swapaxes(cs_s, 0, 1)[None], hs_s[None])
```

```python
import functools

import jax
import jax.numpy as jnp
from jax import lax
from jax.experimental import pallas as pl
from jax.experimental.pallas import tpu as pltpu
from jax.experimental.pallas import tpu_sc as plsc

F32 = jnp.float32
BF16 = jnp.bfloat16
I32 = jnp.int32

D_MODEL = 1024
D_RNN = 512
N_RNN_BLOCKS = 8
CONV_W = 4
LRU_C = 8.0
HEAD_DIM = 64
N_HEADS = 8
N_KV_HEADS = 2
GQA = N_HEADS // N_KV_HEADS
D_ATT = N_HEADS * HEAD_DIM
D_KV = N_KV_HEADS * HEAD_DIM
WINDOW = 128
N_EXPERTS = 64
TOP_K = 6
N_GROUPS = 8
GROUP_SIZE = N_EXPERTS // N_GROUPS
TOPK_GROUPS = 4
D_EXPERT = 256
D_SHARED = 256
ROUTE_SCALE = 2.5
EPS = 1e-6
NEG = -1e30
D_IN = 2 * D_RNN + D_ATT + 2 * D_KV

SUBLANES = 8
LANES = 128
TM_PROMPT = 256
ATT_BLOCK = WINDOW
EXPERT_BLOCK = 256
SAMPLE_CHUNK = 8
POS_TILE = 2048
VMEM_LIMIT = 48 * 1024 * 1024

SC_CORES = 2
SC_SUBCORES = 16
SC_WORKERS = SC_CORES * SC_SUBCORES
SC_LANES = 16
DISPATCH_CHUNK = 64
COMBINE_CHUNK = 8


def _sigmoid(x):
    return 1.0 / (1.0 + jnp.exp(-x))


def _silu(x):
    return x * _sigmoid(x)


def _gelu_tanh(x):
    c = 0.7978845608028654
    return x * (0.5 * (1.0 + jnp.tanh(c * (x + 0.044715 * (x * x * x)))))


def _log1p(x):
    u = 1.0 + x
    return jnp.where(u == 1.0, x, jnp.log(u) * x / jnp.where(u == 1.0, 1.0, u - 1.0))


def _neg_expm1(x):
    return -jnp.tanh(0.5 * x) * (jnp.exp(x) + 1.0)


def _softplus(z):
    return jnp.maximum(z, 0.0) + _log1p(jnp.exp(-jnp.abs(z)))


def _div_pow2(x, d):
    assert d & (d - 1) == 0
    return lax.shift_right_logical(x, d.bit_length() - 1)


def _mod_pow2(x, d):
    assert d & (d - 1) == 0
    return x & (d - 1)


def _norm_mod(x, g, sc, sh):
    ms = jnp.mean(x * x, axis=-1, keepdims=True)
    return (x * lax.rsqrt(ms + EPS)) * g * (1.0 + sc) + sh


def _bdot(a, b):
    return jnp.dot(a.astype(BF16), b.astype(BF16), preferred_element_type=F32)


def _bdot_nt(a, b):
    return lax.dot_general(a.astype(BF16), b.astype(BF16), (((1,), (1,)), ((), ())),
                           preferred_element_type=F32)


def _pack_bf16_pairs(x):
    w = x.shape[1] // 2

    def rne_bits(v):
        b = pltpu.bitcast(v, I32)
        return b + 0x7FFF + (lax.shift_right_logical(b, 16) & 1)

    lo = lax.shift_right_logical(rne_bits(x[:, :w]), 16)
    hi = rne_bits(x[:, w:]) & jnp.int32(-65536)
    return lo | hi


def _unpack_bf16_pairs(p):
    lo = pltpu.bitcast(lax.shift_left(p, 16), F32)
    hi = pltpu.bitcast(p & jnp.int32(-65536), F32)
    return lo.astype(BF16), hi.astype(BF16)


def _knorm(k, gk2):
    lane = lax.broadcasted_iota(I32, k.shape, 1)
    first = lane < HEAD_DIM
    k2 = k * k
    s0 = jnp.sum(jnp.where(first, k2, 0.0), axis=-1, keepdims=True)
    s1 = jnp.sum(jnp.where(first, 0.0, k2), axis=-1, keepdims=True)
    ms = jnp.where(first, s0, s1) * (1.0 / HEAD_DIM)
    return k * lax.rsqrt(ms + EPS) * gk2


def _lru_coeffs(u, wg, gb, lam):
    g = _bdot(u, wg) + gb
    r = _sigmoid(g[:, :D_RNN])
    i = _sigmoid(g[:, D_RNN:])
    log_a = (-LRU_C * r) * _softplus(-lam)
    a = jnp.exp(log_a)
    b = jnp.sqrt(_neg_expm1(2.0 * log_a)) * (i * u)
    return a, b


def _adaln_kernel(c_ref, w_ref, b_ref, o_ref):
    o_ref[...] = _bdot(_silu(c_ref[...]), w_ref[...]) + b_ref[...]


def _adaln(c_all, ada_w, ada_b):
    n = c_all.shape[0]
    return pl.pallas_call(
        _adaln_kernel,
        grid=(6,),
        in_specs=[pl.BlockSpec((n, D_MODEL), lambda j: (0, 0)),
                  pl.BlockSpec((D_MODEL, D_MODEL), lambda j: (0, j)),
                  pl.BlockSpec((1, D_MODEL), lambda j: (0, j))],
        out_specs=pl.BlockSpec((n, D_MODEL), lambda j: (0, j)),
        out_shape=jax.ShapeDtypeStruct((n, 6 * D_MODEL), F32),
        compiler_params=pltpu.CompilerParams(dimension_semantics=("arbitrary",), vmem_limit_bytes=VMEM_LIMIT),
        name="adaln",
    )(c_all, ada_w, ada_b.reshape(1, -1))


def _scan_rows(a, b, h_in):
    n = a.shape[0]
    row = lax.broadcasted_iota(I32, a.shape, 0)
    s = 1
    while s < n:
        m = row >= s
        a_sh = jnp.where(m, pltpu.roll(a, s, 0), 1.0)
        b_sh = jnp.where(m, pltpu.roll(b, s, 0), 0.0)
        b = a * b_sh + b
        a = a * a_sh
        s *= 2
    return a * h_in + b


def _front_prompt_kernel(x_ref, sc_ref, sh_ref, g_ref, win_ref, cw_ref, cb_ref, wg_ref, gb_ref, lam_ref, gk_ref,
                         prev_ref, h0_ref, r_ref, q_ref, k_ref, v_ref, cs_ref, hs_ref, tail_ref, hc_ref):
    j = pl.program_id(1)
    tm = x_ref.shape[0]

    @pl.when(j == 0)
    def _():
        tail_ref[...] = prev_ref[...]
        hc_ref[...] = h0_ref[...]

    h = _norm_mod(x_ref[...], g_ref[...], sc_ref[...], sh_ref[...])
    proj = jnp.dot(h.astype(BF16), win_ref[...], preferred_element_type=F32)
    xr = proj[:, 0:D_RNN]
    yr = proj[:, D_RNN:2 * D_RNN]
    q_ref[...] = proj[:, 2 * D_RNN:2 * D_RNN + D_ATT]
    k_ref[...] = _knorm(proj[:, 2 * D_RNN + D_ATT:2 * D_RNN + D_ATT + D_KV], gk_ref[...])
    v_ref[...] = proj[:, 2 * D_RNN + D_ATT + D_KV:D_IN]

    tail = tail_ref[...]
    row8 = lax.broadcasted_iota(I32, tail.shape, 0)

    def shifted(s):
        rolled = pltpu.roll(xr, s, 0)
        top = jnp.where(row8 < s, pltpu.roll(tail, s, 0), rolled[0:SUBLANES])
        return jnp.concatenate([top, rolled[SUBLANES:]], axis=0)

    cw = cw_ref[...]
    u = cb_ref[...] + shifted(3) * cw[0:1]
    u = u + shifted(2) * cw[1:2]
    u = u + shifted(1) * cw[2:3]
    u = u + xr * cw[3:4]
    tail_ref[...] = xr[tm - SUBLANES:tm]

    a, b = _lru_coeffs(u, wg_ref[...], gb_ref[...], lam_ref[...])
    hs = _scan_rows(a, b, hc_ref[...])
    hc_ref[...] = hs[tm - 1:tm]
    r_ref[...] = hs * _gelu_tanh(yr)

    @pl.when(j == pl.num_programs(1) - 1)
    def _():
        cs_ref[...] = xr[tm - SUBLANES:tm]
        hs_ref[...] = hs[tm - 1:tm]


def _front_prompt(x, sc, sh, g, win, cw, cb, wg, gb, lam, gk, prev, h0):
    bsz, t, _ = x.shape
    tm = TM_PROMPT
    full = lambda a: pl.BlockSpec(a.shape, lambda b, j: (0,) * a.ndim)
    per_b = lambda a: pl.BlockSpec((None,) + a.shape[1:], lambda b, j: (b,) + (0,) * (a.ndim - 1))
    tile = lambda w: pl.BlockSpec((None, tm, w), lambda b, j: (b, j, 0))
    return pl.pallas_call(
        _front_prompt_kernel,
        grid=(bsz, t // tm),
        in_specs=[tile(D_MODEL), per_b(sc), per_b(sh), full(g), full(win), full(cw), full(cb), full(wg), full(gb),
                  full(lam), full(gk), per_b(prev), per_b(h0)],
        out_specs=[tile(D_RNN), tile(D_ATT), tile(D_KV), tile(D_KV),
                   pl.BlockSpec((None, SUBLANES, D_RNN), lambda b, j: (b, 0, 0)),
                   pl.BlockSpec((None, 1, D_RNN), lambda b, j: (b, 0, 0))],
        out_shape=[jax.ShapeDtypeStruct((bsz, t, D_RNN), F32), jax.ShapeDtypeStruct((bsz, t, D_ATT), F32),
                   jax.ShapeDtypeStruct((bsz, t, D_KV), F32), jax.ShapeDtypeStruct((bsz, t, D_KV), F32),
                   jax.ShapeDtypeStruct((bsz, SUBLANES, D_RNN), F32), jax.ShapeDtypeStruct((bsz, 1, D_RNN), F32)],
        scratch_shapes=[pltpu.VMEM((SUBLANES, D_RNN), F32), pltpu.VMEM((1, D_RNN), F32)],
        compiler_params=pltpu.CompilerParams(dimension_semantics=("arbitrary", "arbitrary"),
                                             vmem_limit_bytes=VMEM_LIMIT),
        name="front_prompt",
    )(x, sc, sh, g, win, cw, cb, wg, gb, lam, gk, prev, h0)


def _front_sample_kernel(x_ref, sc_ref, sh_ref, g_ref, win_ref, cw_ref, cb_ref, wg_ref, gb_ref, lam_ref, gk_ref,
                         prev_ref, h0_ref, r_ref, q_ref, k_ref, v_ref, cs_ref, hs_ref):
    t_len, bsz, _ = x_ref.shape
    x = x_ref[...]
    ms = jnp.mean(x * x, axis=-1, keepdims=True)
    h = (x * lax.rsqrt(ms + EPS)) * g_ref[...] * (1.0 + sc_ref[...]) + sh_ref[...]
    proj = jnp.dot(h.reshape(t_len * bsz, D_MODEL).astype(BF16), win_ref[...], preferred_element_type=F32)
    xr = proj[:, 0:D_RNN]
    yr = proj[:, D_RNN:2 * D_RNN]
    q_ref[...] = proj[:, 2 * D_RNN:2 * D_RNN + D_ATT].reshape(t_len, bsz, D_ATT)
    k_ref[...] = _knorm(proj[:, 2 * D_RNN + D_ATT:2 * D_RNN + D_ATT + D_KV], gk_ref[...]).reshape(t_len, bsz, D_KV)
    v_ref[...] = proj[:, 2 * D_RNN + D_ATT + D_KV:D_IN].reshape(t_len, bsz, D_KV)

    def at_time(t):
        if t >= 0:
            return xr[t * bsz:(t + 1) * bsz]
        return prev_ref[CONV_W - 1 + t]

    cw = cw_ref[...]
    us = []
    for t in range(t_len):
        u = cb_ref[...] + at_time(t - 3) * cw[0:1]
        u = u + at_time(t - 2) * cw[1:2]
        u = u + at_time(t - 1) * cw[2:3]
        u = u + at_time(t) * cw[3:4]
        us.append(u)
    a, b = _lru_coeffs(jnp.concatenate(us, axis=0), wg_ref[...], gb_ref[...], lam_ref[...])
    hcur = h0_ref[...]
    for t in range(t_len):
        hcur = a[t * bsz:(t + 1) * bsz] * hcur + b[t * bsz:(t + 1) * bsz]
        r_ref[t] = hcur * _gelu_tanh(yr[t * bsz:(t + 1) * bsz])
    hs_ref[...] = hcur
    for s in range(CONV_W - 1):
        cs_ref[s] = at_time(t_len - (CONV_W - 1) + s)


def _front_sample(x_t, sc, sh, g, win, cw, cb, wg, gb, lam, gk, prev_t, h0):
    t_len, bsz, _ = x_t.shape
    return pl.pallas_call(
        _front_sample_kernel,
        out_shape=[jax.ShapeDtypeStruct((t_len, bsz, D_RNN), F32), jax.ShapeDtypeStruct((t_len, bsz, D_ATT), F32),
                   jax.ShapeDtypeStruct((t_len, bsz, D_KV), F32), jax.ShapeDtypeStruct((t_len, bsz, D_KV), F32),
                   jax.ShapeDtypeStruct((CONV_W - 1, bsz, D_RNN), F32), jax.ShapeDtypeStruct((bsz, D_RNN), F32)],
        compiler_params=pltpu.CompilerParams(vmem_limit_bytes=VMEM_LIMIT),
        name="front_sample",
    )(x_t, sc, sh, g, win, cw, cb, wg, gb, lam, gk, prev_t, h0)


def _qnorm(q, gq):
    ms = jnp.mean(q * q, axis=-1, keepdims=True)
    return q * lax.rsqrt(ms + EPS) * gq


def _attn_prompt_kernel(sink_ref, q_ref, kp_ref, kc_ref, vp_ref, vc_ref, gq_ref, o_ref):
    j = pl.program_id(1)
    blk = q_ref.shape[0]
    q = q_ref[...]
    kk = jnp.concatenate([kp_ref[...], kc_ref[...]], axis=0)
    vv = jnp.concatenate([vp_ref[...], vc_ref[...]], axis=0)
    qi = lax.broadcasted_iota(I32, (blk, 2 * blk), 0)
    kj = lax.broadcasted_iota(I32, (blk, 2 * blk), 1)
    dist = blk + qi - kj
    valid = (dist >= 0) & (dist <= WINDOW) & ((kj >= blk) | (j > 0))
    distf = dist.astype(F32)
    outs = []
    for h in range(N_HEADS):
        g = h // GQA
        qh = _qnorm(q[:, h * HEAD_DIM:(h + 1) * HEAD_DIM], gq_ref[...])
        s = _bdot_nt(qh, kk[:, g * HEAD_DIM:(g + 1) * HEAD_DIM]) * (HEAD_DIM ** -0.5)
        s = s - (2.0 ** -(h + 1)) * distf
        s = jnp.where(valid, s, NEG)
        sink = sink_ref[h]
        m = jnp.maximum(jnp.max(s, axis=-1, keepdims=True), sink)
        p = jnp.exp(s - m)
        l = jnp.sum(p, axis=-1, keepdims=True) + jnp.exp(sink - m)
        outs.append(_bdot(p, vv[:, g * HEAD_DIM:(g + 1) * HEAD_DIM]) / l)
    o_ref[...] = jnp.concatenate(outs, axis=1)


def _attn_prompt(q, k, v, gq, sinks):
    bsz, t, _ = q.shape
    blk = ATT_BLOCK
    cur = lambda w: pl.BlockSpec((None, blk, w), lambda b, j: (b, j, 0))
    prv = lambda w: pl.BlockSpec((None, blk, w), lambda b, j: (b, jnp.maximum(j - 1, 0), 0))
    return pl.pallas_call(
        _attn_prompt_kernel,
        grid=(bsz, t // blk),
        in_specs=[pl.BlockSpec(memory_space=pltpu.SMEM), cur(D_ATT), prv(D_KV), cur(D_KV), prv(D_KV), cur(D_KV),
                  pl.BlockSpec(gq.shape, lambda b, j: (0, 0))],
        out_specs=cur(D_ATT),
        out_shape=jax.ShapeDtypeStruct((bsz, t, D_ATT), F32),
        compiler_params=pltpu.CompilerParams(dimension_semantics=("arbitrary", "arbitrary"),
                                             vmem_limit_bytes=VMEM_LIMIT),
        name="attn_prompt",
    )(sinks, q, k, k, v, v, gq)


def _attn_sample_kernel(sink_ref, q_ref, kn_ref, vn_ref, kc_ref, vc_ref, gq_ref, o_ref):
    t_len, cb, _ = q_ref.shape
    win = kc_ref.shape[1]
    rows = GQA * t_len * cb
    kc = kc_ref[...].reshape(cb * win, D_KV)
    vc = vc_ref[...].reshape(cb * win, D_KV)
    kn = kn_ref[...].reshape(t_len * cb, D_KV)
    vn = vn_ref[...].reshape(t_len * cb, D_KV)

    r_c = lax.broadcasted_iota(I32, (rows, cb * win), 0)
    c_c = lax.broadcasted_iota(I32, (rows, cb * win), 1)
    t_c = _div_pow2(_mod_pow2(r_c, t_len * cb), cb)
    valid_c = (_mod_pow2(r_c, cb) == _div_pow2(c_c, win)) & (_mod_pow2(c_c, win) >= t_c)
    dist_c = (win + t_c - _mod_pow2(c_c, win)).astype(F32)
    r_n = lax.broadcasted_iota(I32, (rows, t_len * cb), 0)
    c_n = lax.broadcasted_iota(I32, (rows, t_len * cb), 1)
    t_n = _div_pow2(_mod_pow2(r_n, t_len * cb), cb)
    valid_n = (_mod_pow2(r_n, cb) == _mod_pow2(c_n, cb)) & (_div_pow2(c_n, cb) <= t_n)
    dist_n = (t_n - _div_pow2(c_n, cb)).astype(F32)
    hl = _div_pow2(lax.broadcasted_iota(I32, (rows, 1), 0), t_len * cb)

    per_group = []
    for g in range(N_KV_HEADS):
        slabs = [q_ref[t][:, (g * GQA + i) * HEAD_DIM:(g * GQA + i + 1) * HEAD_DIM]
                 for i in range(GQA) for t in range(t_len)]
        qg = _qnorm(jnp.concatenate(slabs, axis=0), gq_ref[...])
        slope = jnp.zeros((rows, 1), F32)
        sink = jnp.zeros((rows, 1), F32)
        for i in range(GQA):
            slope = jnp.where(hl == i, 2.0 ** -(g * GQA + i + 1), slope)
            sink = jnp.where(hl == i, sink_ref[g * GQA + i], sink)
        lo, hi = g * HEAD_DIM, (g + 1) * HEAD_DIM
        s_c = _bdot_nt(qg, kc[:, lo:hi]) * (HEAD_DIM ** -0.5) - slope * dist_c
        s_n = _bdot_nt(qg, kn[:, lo:hi]) * (HEAD_DIM ** -0.5) - slope * dist_n
        s_c = jnp.where(valid_c, s_c, NEG)
        s_n = jnp.where(valid_n, s_n, NEG)
        m = jnp.maximum(jnp.maximum(jnp.max(s_c, axis=-1, keepdims=True), jnp.max(s_n, axis=-1, keepdims=True)), sink)
        p_c = jnp.exp(s_c - m)
        p_n = jnp.exp(s_n - m)
        l = jnp.sum(p_c, axis=-1, keepdims=True) + jnp.sum(p_n, axis=-1, keepdims=True) + jnp.exp(sink - m)
        per_group.append((_bdot(p_c, vc[:, lo:hi]) + _bdot(p_n, vn[:, lo:hi])) / l)
    for t in range(t_len):
        o_ref[t] = jnp.concatenate(
            [per_group[g][(i * t_len + t) * cb:(i * t_len + t + 1) * cb] for g in range(N_KV_HEADS) for i in range(GQA)],
            axis=1)


def _attn_sample(q_t, k_t, v_t, cache_k, cache_v, gq, sinks):
    t_len, bsz, _ = q_t.shape
    cb = SAMPLE_CHUNK
    win = cache_k.shape[1]
    new = lambda w: pl.BlockSpec((t_len, cb, w), lambda c: (0, c, 0))
    old = pl.BlockSpec((cb, win, D_KV), lambda c: (c, 0, 0))
    return pl.pallas_call(
        _attn_sample_kernel,
        grid=(bsz // cb,),
        in_specs=[pl.BlockSpec(memory_space=pltpu.SMEM), new(D_ATT), new(D_KV), new(D_KV), old, old,
                  pl.BlockSpec(gq.shape, lambda c: (0, 0))],
        out_specs=new(D_ATT),
        out_shape=jax.ShapeDtypeStruct((t_len, bsz, D_ATT), F32),
        compiler_params=pltpu.CompilerParams(dimension_semantics=("arbitrary",), vmem_limit_bytes=VMEM_LIMIT),
        name="attn_sample",
    )(sinks, q_t, k_t, v_t, cache_k, cache_v, gq)


def _route(s_t, sb_t):
    tm = s_t.shape[1]
    i8 = lax.broadcasted_iota(I32, (GROUP_SIZE, tm), 0)
    ninf = -jnp.inf
    sg = [sb_t[GROUP_SIZE * g:GROUP_SIZE * (g + 1)] for g in range(N_GROUPS)]
    gscore = []
    for g in range(N_GROUPS):
        m1 = jnp.max(sg[g], axis=0, keepdims=True)
        i1 = jnp.min(jnp.where(sg[g] == m1, i8, GROUP_SIZE), axis=0, keepdims=True)
        m2 = jnp.max(jnp.where(i8 == i1, ninf, sg[g]), axis=0, keepdims=True)
        gscore.append(m1 + m2)
    gs = jnp.concatenate(gscore, axis=0)
    gsel = jnp.zeros((N_GROUPS, tm), I32)
    for _ in range(TOPK_GROUPS):
        m = jnp.max(gs, axis=0, keepdims=True)
        idx = jnp.min(jnp.where(gs == m, i8, N_GROUPS), axis=0, keepdims=True)
        hit = i8 == idx
        gsel = jnp.where(hit, 1, gsel)
        gs = jnp.where(hit, ninf, gs)
    sm = [jnp.where(gsel[g:g + 1] > 0, sg[g], NEG) for g in range(N_GROUPS)]
    eid = [i8 + GROUP_SIZE * g for g in range(N_GROUPS)]
    sel = [jnp.zeros((GROUP_SIZE, tm), F32) for _ in range(N_GROUPS)]
    idxs, ws = [], []
    for _ in range(TOP_K):
        cm = functools.reduce(jnp.maximum, sm)
        m = jnp.max(cm, axis=0, keepdims=True)
        cand = functools.reduce(jnp.minimum, [jnp.where(sm[g] == m, eid[g], N_EXPERTS) for g in range(N_GROUPS)])
        idx = jnp.min(cand, axis=0, keepdims=True)
        wk = jnp.zeros((GROUP_SIZE, tm), F32)
        for g in range(N_GROUPS):
            hit = eid[g] == idx
            wk = wk + jnp.where(hit, s_t[GROUP_SIZE * g:GROUP_SIZE * (g + 1)], 0.0)
            sel[g] = jnp.where(hit, 1.0, sel[g])
            sm[g] = jnp.where(hit, ninf, sm[g])
        idxs.append(idx)
        ws.append(jnp.sum(wk, axis=0, keepdims=True))
    return idxs, ws, jnp.concatenate(sel, axis=0), eid


def _post_kernel(x_ref, r_ref, a_ref, g1_ref, sc2_ref, sh2_ref, nf_ref, wot_ref, wob_ref, wrt_ref, rb_ref, cin_ref,
                 x1_ref, h2_ref, eidx_ref, gw_ref, rank_ref, cnt_ref, carry_ref):
    tm = x_ref.shape[0]

    @pl.when(pl.program_id(0) == 0)
    def _():
        carry_ref[...] = cin_ref[...]

    mixed = (jnp.dot(r_ref[...].astype(BF16), wot_ref[...], preferred_element_type=F32)
             + jnp.dot(a_ref[...].astype(BF16), wob_ref[...], preferred_element_type=F32))
    x1 = x_ref[...] + g1_ref[...] * mixed
    x1_ref[...] = x1
    h2 = _norm_mod(x1, nf_ref[...], sc2_ref[...], sh2_ref[...])
    h2_ref[...] = _pack_bf16_pairs(h2)

    wr = wrt_ref[...]
    wr_hi = wr.astype(BF16)
    wr_lo = (wr - wr_hi.astype(F32)).astype(BF16)
    h_hi = h2.astype(BF16)
    h_lo = (h2 - h_hi.astype(F32)).astype(BF16)
    logits = _bdot_nt(wr_hi, h_hi) + _bdot_nt(wr_hi, h_lo) + _bdot_nt(wr_lo, h_hi)
    s_t = _sigmoid(logits)
    idxs, ws, sel, eid = _route(s_t, s_t + rb_ref[...])

    rr = lax.broadcasted_iota(I32, (tm, tm), 0)
    cc = lax.broadcasted_iota(I32, (tm, tm), 1)
    before = jnp.where(rr < cc, 1.0, 0.0).astype(BF16)
    carry = carry_ref[...]
    tot = jnp.dot(sel.astype(BF16), before, preferred_element_type=F32) + carry[:, 0:1]
    ranks = []
    for k in range(TOP_K):
        acc = jnp.zeros((GROUP_SIZE, tm), F32)
        for g in range(N_GROUPS):
            acc = acc + jnp.where(eid[g] == idxs[k], tot[GROUP_SIZE * g:GROUP_SIZE * (g + 1)], 0.0)
        ranks.append(jnp.sum(acc, axis=0, keepdims=True))
    carry = carry + jnp.sum(sel, axis=1, keepdims=True)
    carry_ref[...] = carry
    cnt_ref[...] = carry

    wsum = functools.reduce(lambda p, q: p + q, ws)
    pad_i = jnp.zeros((SUBLANES - TOP_K, tm), I32)
    pad_f = jnp.zeros((SUBLANES - TOP_K, tm), F32)
    eidx_ref[...] = jnp.concatenate(idxs + [pad_i], axis=0)
    rank_ref[...] = jnp.concatenate([r.astype(I32) for r in ranks] + [pad_i], axis=0)
    gw_ref[...] = jnp.concatenate([w / wsum * ROUTE_SCALE for w in ws] + [pad_f], axis=0)


def _post(x, r, a, g1, sc2, sh2, nf, wo_top, wo_bot, wr_t, rb, cnt_in, tm, mod_spec):
    n = x.shape[0]
    tile = lambda w: pl.BlockSpec((tm, w), lambda i: (i, 0))
    full = lambda arr: pl.BlockSpec(arr.shape, lambda i: (0,) * arr.ndim)
    slot = pl.BlockSpec((SUBLANES, tm), lambda i: (0, i))
    return pl.pallas_call(
        _post_kernel,
        grid=(n // tm,),
        in_specs=[tile(D_MODEL), tile(D_RNN), tile(D_ATT), mod_spec, mod_spec, mod_spec, full(nf), full(wo_top),
                  full(wo_bot), full(wr_t), full(rb), full(cnt_in)],
        out_specs=[tile(D_MODEL), tile(D_MODEL // 2), slot, slot, slot, full(cnt_in)],
        out_shape=[jax.ShapeDtypeStruct((n, D_MODEL), F32), jax.ShapeDtypeStruct((n, D_MODEL // 2), I32),
                   jax.ShapeDtypeStruct((SUBLANES, n), I32), jax.ShapeDtypeStruct((SUBLANES, n), F32),
                   jax.ShapeDtypeStruct((SUBLANES, n), I32), jax.ShapeDtypeStruct(cnt_in.shape, F32)],
        scratch_shapes=[pltpu.VMEM(cnt_in.shape, F32)],
        compiler_params=pltpu.CompilerParams(dimension_semantics=("arbitrary",), vmem_limit_bytes=VMEM_LIMIT),
        name="post_mix",
    )(x, r, a, g1, sc2, sh2, nf, wo_top, wo_bot, wr_t, rb, cnt_in)


def _pos_kernel(ps_ref, eidx_ref, rank_ref, pos_ref):
    tn = eidx_ref.shape[1]
    e_iota = lax.broadcasted_iota(I32, (N_EXPERTS, tn), 0)
    ps = ps_ref[...]
    rows = []
    for k in range(TOP_K):
        hit = e_iota == eidx_ref[k:k + 1, :]
        base = jnp.sum(jnp.where(hit, ps, 0.0), axis=0, keepdims=True)
        rows.append(base.astype(I32) + rank_ref[k:k + 1, :])
    rows.append(jnp.zeros((SUBLANES - TOP_K, tn), I32))
    pos_ref[...] = jnp.concatenate(rows, axis=0)


def _positions(pstart_col, eidx, rank):
    n = eidx.shape[1]
    tn = min(n, POS_TILE)
    slot = pl.BlockSpec((SUBLANES, tn), lambda i: (0, i))
    return pl.pallas_call(
        _pos_kernel,
        grid=(n // tn,),
        in_specs=[pl.BlockSpec(pstart_col.shape, lambda i: (0, 0)), slot, slot],
        out_specs=slot,
        out_shape=jax.ShapeDtypeStruct((SUBLANES, n), I32),
        compiler_params=pltpu.CompilerParams(dimension_semantics=("arbitrary",)),
        name="moe_positions",
    )(pstart_col, eidx, rank)


def _chunked(slots, c):
    return slots.reshape(SUBLANES, slots.shape[1] // c, c).transpose(1, 0, 2)


def _sc_worker_id():
    return lax.axis_index("s") * SC_CORES + lax.axis_index("c")


def _sc_mesh():
    return plsc.VectorSubcoreMesh(core_axis_name="c", subcore_axis_name="s")


def _sc_dispatch(seg_a, seg_b, total_rows):
    (h_a, pos_a, c_a), (h_b, pos_b, c_b) = seg_a, seg_b
    width = h_a.shape[1]

    @functools.partial(
        pl.kernel, mesh=_sc_mesh(), out_type=jax.ShapeDtypeStruct((total_rows, width), I32),
        scratch_types=[pltpu.VMEM((SUBLANES, c_a), I32), pltpu.VMEM((c_a, width), I32),
                       pltpu.VMEM((SUBLANES, c_b), I32), pltpu.VMEM((c_b, width), I32), pltpu.SemaphoreType.DMA])
    def run(ha_hbm, pa_hbm, hb_hbm, pb_hbm, xs_hbm, idx_a, rows_a, idx_b, rows_b, sem):
        wid = _sc_worker_id()

        def segment(h_hbm, p_hbm, idx_v, rows_v, c):
            nch = h_hbm.shape[0] // (SC_WORKERS * c)

            @pl.loop(0, nch)
            def _(ci):
                chunk = wid * nch + ci
                pltpu.sync_copy(p_hbm.at[chunk], idx_v)
                pltpu.sync_copy(h_hbm.at[pl.ds(chunk * c, c)], rows_v)
                copies = [pltpu.async_copy(rows_v, xs_hbm.at[idx_v.at[k]], sem) for k in range(TOP_K)]
                for cp in copies:
                    cp.wait()

        segment(ha_hbm, pa_hbm, idx_a, rows_a, c_a)
        segment(hb_hbm, pb_hbm, idx_b, rows_b, c_b)

    return run(h_a, pos_a, h_b, pos_b)


def _experts_kernel(be_ref, nv_ref, xs_ref, wg_ref, wu_ref, wd_ref, y_ref):
    del be_ref
    half = xs_ref.shape[1]
    row = lax.broadcasted_iota(I32, xs_ref.shape, 0)
    x_lo, x_hi = _unpack_bf16_pairs(jnp.where(row < nv_ref[pl.program_id(0)], xs_ref[...], 0))
    wg = wg_ref[...].astype(BF16)
    wu = wu_ref[...].astype(BF16)
    a = (jnp.dot(x_lo, wg[:half], preferred_element_type=F32) + jnp.dot(x_hi, wg[half:], preferred_element_type=F32))
    b = (jnp.dot(x_lo, wu[:half], preferred_element_type=F32) + jnp.dot(x_hi, wu[half:], preferred_element_type=F32))
    y_ref[...] = _bdot(_silu(a) * b, wd_ref[...])


def _experts(blk_e, n_valid, xs, wg, wu, wd):
    rows = xs.shape[0]
    bm = EXPERT_BLOCK
    row_blk = lambda i, be, nv: (i, 0)
    w_blk = lambda i, be, nv: (be[i], 0, 0)
    grid_spec = pltpu.PrefetchScalarGridSpec(
        num_scalar_prefetch=2,
        grid=(rows // bm,),
        in_specs=[pl.BlockSpec((bm, D_MODEL // 2), row_blk),
                  pl.BlockSpec((None, D_MODEL, D_EXPERT), w_blk),
                  pl.BlockSpec((None, D_MODEL, D_EXPERT), w_blk),
                  pl.BlockSpec((None, D_EXPERT, D_MODEL), w_blk)],
        out_specs=pl.BlockSpec((bm, D_MODEL), row_blk),
    )
    return pl.pallas_call(
        _experts_kernel,
        grid_spec=grid_spec,
        out_shape=jax.ShapeDtypeStruct((rows, D_MODEL), F32),
        compiler_params=pltpu.CompilerParams(dimension_semantics=("arbitrary",), vmem_limit_bytes=VMEM_LIMIT),
        name="moe_experts",
    )(blk_e, n_valid, xs, wg, wu, wd)


def _sc_combine(y, seg_a, seg_b):
    (pos_a, w_a), (pos_b, w_b) = seg_a, seg_b
    c = COMBINE_CHUNK
    d = y.shape[1]
    n_a, n_b = w_a.shape[0], w_b.shape[0]

    @functools.partial(
        pl.kernel, mesh=_sc_mesh(),
        out_type=[jax.ShapeDtypeStruct((n_a, d), F32), jax.ShapeDtypeStruct((n_b, d), F32)],
        scratch_types=[pltpu.VMEM((2, SUBLANES, c), I32), pltpu.VMEM((2, c, SUBLANES * SC_LANES), F32),
                       pltpu.VMEM((2, TOP_K, c, d), F32), pltpu.VMEM((c, d), F32), pltpu.SemaphoreType.DMA((2,))])
    def run(y_hbm, pa_hbm, wa_hbm, pb_hbm, wb_hbm, oa_hbm, ob_hbm, idx_v, w_v, buf_v, out_v, sems):
        wid = _sc_worker_id()

        def segment(p_hbm, w_hbm, o_hbm):
            nch = w_hbm.shape[0] // (SC_WORKERS * c)
            chunk0 = wid * nch

            def gathers(b):
                return [pltpu.make_async_copy(y_hbm.at[idx_v.at[b].at[k]], buf_v.at[b].at[k], sems.at[b])
                        for k in range(TOP_K)]

            def fetch(ci, b):
                pltpu.sync_copy(p_hbm.at[chunk0 + ci], idx_v.at[b])
                pltpu.sync_copy(w_hbm.at[pl.ds((chunk0 + ci) * c, c)], w_v.at[b])
                for cp in gathers(b):
                    cp.start()

            def reduce_rows(ci, b):
                @pl.loop(0, c)
                def _(t):
                    ws = [w_v[b, t, pl.ds(k * SC_LANES, SC_LANES)] for k in range(TOP_K)]
                    for j in range(d // SC_LANES):
                        lanes = pl.ds(j * SC_LANES, SC_LANES)
                        acc = buf_v[b, 0, t, lanes] * ws[0]
                        for k in range(1, TOP_K):
                            acc = acc + buf_v[b, k, t, lanes] * ws[k]
                        out_v[t, lanes] = acc

                pltpu.sync_copy(out_v, o_hbm.at[pl.ds((chunk0 + ci) * c, c)])

            fetch(0, 0)

            @pl.loop(0, nch, step=2)
            def _(ci):
                for b in range(2):
                    @pl.when(ci + b + 1 < nch)
                    def _():
                        fetch(ci + b + 1, 1 - b)

                    for cp in gathers(b):
                        cp.wait()
                    reduce_rows(ci + b, b)

        segment(pa_hbm, wa_hbm, oa_hbm)
        segment(pb_hbm, wb_hbm, ob_hbm)

    return run(y, pos_a, w_a, pos_b, w_b)


def _shared_kernel(h2_ref, wsg_ref, wsu_ref, wsd_ref, o_ref):
    half = h2_ref.shape[1]
    x_lo, x_hi = _unpack_bf16_pairs(h2_ref[...])
    wsg, wsu = wsg_ref[...], wsu_ref[...]
    a = (jnp.dot(x_lo, wsg[:half], preferred_element_type=F32) + jnp.dot(x_hi, wsg[half:], preferred_element_type=F32))
    b = (jnp.dot(x_lo, wsu[:half], preferred_element_type=F32) + jnp.dot(x_hi, wsu[half:], preferred_element_type=F32))
    o_ref[...] = jnp.dot((_silu(a) * b).astype(BF16), wsd_ref[...], preferred_element_type=F32)


def _shared(h2, wsg, wsu, wsd, tm):
    n = h2.shape[0]
    full = lambda arr: pl.BlockSpec(arr.shape, lambda i: (0,) * arr.ndim)
    return pl.pallas_call(
        _shared_kernel,
        grid=(n // tm,),
        in_specs=[pl.BlockSpec((tm, D_MODEL // 2), lambda i: (i, 0)), full(wsg), full(wsu), full(wsd)],
        out_specs=pl.BlockSpec((tm, D_MODEL), lambda i: (i, 0)),
        out_shape=jax.ShapeDtypeStruct((n, D_MODEL), F32),
        compiler_params=pltpu.CompilerParams(dimension_semantics=("arbitrary",), vmem_limit_bytes=VMEM_LIMIT),
        name="shared_expert",
    )(h2, wsg, wsu, wsd)


def _final_kernel(x1_ref, routed_ref, shared_ref, g2_ref, o_ref):
    o_ref[...] = x1_ref[...] + g2_ref[...] * (routed_ref[...] + shared_ref[...])


def _final(x1, routed, shared, g2, tm, mod_spec):
    n = x1.shape[0]
    tile = pl.BlockSpec((tm, D_MODEL), lambda i: (i, 0))
    return pl.pallas_call(
        _final_kernel,
        grid=(n // tm,),
        in_specs=[tile, tile, tile, mod_spec],
        out_specs=tile,
        out_shape=jax.ShapeDtypeStruct((n, D_MODEL), F32),
        compiler_params=pltpu.CompilerParams(dimension_semantics=("arbitrary",), vmem_limit_bytes=VMEM_LIMIT),
        name="ffn_residual",
    )(x1, routed, shared, g2)


def _block_diag(w):
    nb, bi, bj = w.shape
    return jnp.einsum('nij,nm->nimj', w, jnp.eye(nb, dtype=w.dtype)).reshape(nb * bi, nb * bj)


def kernel(x_prompt, x_sample, c_prompt, c_sample, cache_k_win, cache_v_win, state_conv, state_rnn, ada_w, ada_b, norm_mix, w_in, conv_w, conv_b, gate_a_w, gate_a_b, gate_x_w, gate_x_b, lru_lambda, q_norm, k_norm, attn_sinks, w_out, norm_ffn, router_w, router_bias, exp_w_gate, exp_w_up, exp_w_down, sh_w_gate, sh_w_up, sh_w_down):
    bp, tp, _ = x_prompt.shape
    bs, ts, _ = x_sample.shape
    win = cache_k_win.shape[2]
    n_p, n_s = bp * tp, bs * ts
    row = lambda v: v.reshape(1, -1)

    g_mix, g_ffn = row(norm_mix[0]), row(norm_ffn[0])
    win_bf = w_in[0].astype(BF16)
    cw, cb = conv_w[0], row(conv_b[0])
    wg = jnp.concatenate([_block_diag(gate_a_w[0]), _block_diag(gate_x_w[0])], axis=1).astype(BF16)
    gb = row(jnp.concatenate([gate_a_b[0], gate_x_b[0]]))
    lam = row(lru_lambda[0])
    gq = row(q_norm[0])
    gk2 = row(jnp.tile(k_norm[0], N_KV_HEADS))
    sinks = attn_sinks[0]
    wo_top, wo_bot = w_out[0, :D_RNN].astype(BF16), w_out[0, D_RNN:].astype(BF16)
    wr_t = router_w[0].T
    rb = router_bias[0].reshape(N_EXPERTS, 1)
    wsg, wsu, wsd = sh_w_gate[0].astype(BF16), sh_w_up[0].astype(BF16), sh_w_down[0].astype(BF16)

    mod = _adaln(jnp.concatenate([c_prompt, c_sample], axis=0), ada_w[0], ada_b[0])
    chunks = [mod[:, i * D_MODEL:(i + 1) * D_MODEL] for i in range(6)]
    sh1p, sc1p, g1p, sh2p, sc2p, g2p = [c[:bp].reshape(bp, 1, D_MODEL) for c in chunks]
    sh1s, sc1s, g1s, sh2s, sc2s, g2s = [c[bp:] for c in chunks]

    conv0 = jnp.zeros((bp, SUBLANES, D_RNN), F32)
    h0 = jnp.zeros((bp, 1, D_RNN), F32)
    r_p, q_p, k_p, v_p, cs_p, hs_p = _front_prompt(x_prompt, sc1p, sh1p, g_mix, win_bf, cw, cb, wg, gb, lam, gk2,
                                                   conv0, h0)
    a_p = _attn_prompt(q_p, k_p, v_p, gq, sinks)

    x_s_t = jnp.swapaxes(x_sample, 0, 1)
    r_s, q_s, k_s, v_s, cs_s, hs_s = _front_sample(x_s_t, sc1s, sh1s, g_mix, win_bf, cw, cb, wg, gb, lam, gk2,
                                                   jnp.swapaxes(state_conv[0], 0, 1), state_rnn[0])
    cache_k = cache_k_win[0].reshape(bs, win, D_KV)
    cache_v = cache_v_win[0].reshape(bs, win, D_KV)
    a_s = _attn_sample(q_s, k_s, v_s, cache_k, cache_v, gq, sinks)

    tiles_per_seq = tp // TM_PROMPT
    mod_p = pl.BlockSpec((None, 1, D_MODEL), lambda i, *_: (i // tiles_per_seq, 0, 0))
    mod_s = pl.BlockSpec((bs, D_MODEL), lambda i, *_: (0, 0))
    cnt0 = jnp.zeros((N_EXPERTS, LANES), F32)
    x1_p, h2_p, eidx_p, gw_p, rank_p, cnt_p = _post(
        x_prompt.reshape(n_p, D_MODEL), r_p.reshape(n_p, D_RNN), a_p.reshape(n_p, D_ATT), g1p, sc2p, sh2p, g_ffn,
        wo_top, wo_bot, wr_t, rb, cnt0, TM_PROMPT, mod_p)
    x1_s, h2_s, eidx_s, gw_s, rank_s, cnt_all = _post(
        x_s_t.reshape(n_s, D_MODEL), r_s.reshape(n_s, D_RNN), a_s.reshape(n_s, D_ATT), g1s, sc2s, sh2s, g_ffn,
        wo_top, wo_bot, wr_t, rb, cnt_p, bs, mod_s)

    bm = EXPERT_BLOCK
    n_blocks = -(-((n_p + n_s) * TOP_K) // bm) + N_EXPERTS
    counts = cnt_all[:, 0].astype(I32)
    padded = (counts + bm - 1) // bm * bm
    pend = jnp.cumsum(padded).astype(I32)
    pstart = pend - padded
    blk_start = jnp.arange(n_blocks, dtype=I32) * bm
    blk_e = jnp.minimum(jnp.sum((pend[None, :] <= blk_start[:, None]).astype(I32), axis=1), N_EXPERTS - 1)
    n_valid = jnp.clip((pstart + counts)[blk_e] - blk_start, 0, bm).astype(I32)

    pstart_col = pstart.astype(F32).reshape(N_EXPERTS, 1)
    pos_p = _positions(pstart_col, eidx_p, rank_p)
    pos_s = _positions(pstart_col, eidx_s, rank_s)
    c_p, c_s = min(DISPATCH_CHUNK, n_p // SC_WORKERS), min(DISPATCH_CHUNK, n_s // SC_WORKERS)
    xs = _sc_dispatch((h2_p, _chunked(pos_p, c_p), c_p), (h2_s, _chunked(pos_s, c_s), c_s), n_blocks * bm)
    y = _experts(blk_e, n_valid, xs, exp_w_gate[0], exp_w_up[0], exp_w_down[0])

    lane_expand = lambda gw: jnp.repeat(gw.T, SC_LANES, axis=1)
    routed_p, routed_s = _sc_combine(y, (_chunked(pos_p, COMBINE_CHUNK), lane_expand(gw_p)),
                                     (_chunked(pos_s, COMBINE_CHUNK), lane_expand(gw_s)))
    y_p = _final(x1_p, routed_p, _shared(h2_p, wsg, wsu, wsd, TM_PROMPT), g2p, TM_PROMPT, mod_p)
    y_s = _final(x1_s, routed_s, _shared(h2_s, wsg, wsu, wsd, bs), g2s, bs, mod_s)

    y_prompt = y_p.reshape(bp, tp, D_MODEL)
    y_sample = jnp.swapaxes(y_s.reshape(ts, bs, D_MODEL), 0, 1)
    wk = min(WINDOW, tp)
    k_win_p = k_p[:, tp - wk:].reshape(1, bp, wk, N_KV_HEADS, HEAD_DIM)
    v_win_p = v_p[:, tp - wk:].reshape(1, bp, wk, N_KV_HEADS, HEAD_DIM)
    k_new = jnp.swapaxes(k_s, 0, 1)
    v_new = jnp.swapaxes(v_s, 0, 1)
    k_win_s = jnp.concatenate([cache_k, k_new], axis=1)[:, ts:].reshape(1, bs, win, N_KV_HEADS, HEAD_DIM)
    v_win_s = jnp.concatenate([cache_v, v_new], axis=1)[:, ts:].reshape(1, bs, win, N_KV_HEADS, HEAD_DIM)
    return (y_prompt, y_sample, k_win_p, v_win_p, cs_p[None, :, SUBLANES - (CONV_W - 1):], hs_p.reshape(1, bp, D_RNN),
            k_win_s, v_win_s, jnp.swapaxes(cs_s, 0, 1)[None], hs_s[None])
```

```python
import functools

import jax
import jax.numpy as jnp
from jax import lax
from jax.experimental import pallas as pl
from jax.experimental.pallas import tpu as pltpu
from jax.experimental.pallas import tpu_sc as plsc

F32 = jnp.float32
BF16 = jnp.bfloat16
I32 = jnp.int32

D_MODEL = 1024
D_RNN = 512
N_RNN_BLOCKS = 8
CONV_W = 4
LRU_C = 8.0
HEAD_DIM = 64
N_HEADS = 8
N_KV_HEADS = 2
GQA = N_HEADS // N_KV_HEADS
D_ATT = N_HEADS * HEAD_DIM
D_KV = N_KV_HEADS * HEAD_DIM
WINDOW = 128
N_EXPERTS = 64
TOP_K = 6
N_GROUPS = 8
GROUP_SIZE = N_EXPERTS // N_GROUPS
TOPK_GROUPS = 4
D_EXPERT = 256
D_SHARED = 256
ROUTE_SCALE = 2.5
EPS = 1e-6
NEG = -1e30
D_IN = 2 * D_RNN + D_ATT + 2 * D_KV

SUBLANES = 8
LANES = 128
TM_PROMPT = 256
ATT_BLOCK = WINDOW
EXPERT_BLOCK = 512
SAMPLE_CHUNK = 8
POS_TILE = 2048
VMEM_LIMIT = 48 * 1024 * 1024

SC_CORES = 2
SC_SUBCORES = 16
SC_WORKERS = SC_CORES * SC_SUBCORES
SC_LANES = 16
DISPATCH_CHUNK = 64
COMBINE_CHUNK = 8


def _sigmoid(x):
    return 1.0 / (1.0 + jnp.exp(-x))


def _silu(x):
    return x * _sigmoid(x)


def _gelu_tanh(x):
    c = 0.7978845608028654
    return x * (0.5 * (1.0 + jnp.tanh(c * (x + 0.044715 * (x * x * x)))))


def _log1p(x):
    u = 1.0 + x
    return jnp.where(u == 1.0, x, jnp.log(u) * x / jnp.where(u == 1.0, 1.0, u - 1.0))


def _neg_expm1(x):
    return -jnp.tanh(0.5 * x) * (jnp.exp(x) + 1.0)


def _softplus(z):
    return jnp.maximum(z, 0.0) + _log1p(jnp.exp(-jnp.abs(z)))


def _div_pow2(x, d):
    assert d & (d - 1) == 0
    return lax.shift_right_logical(x, d.bit_length() - 1)


def _mod_pow2(x, d):
    assert d & (d - 1) == 0
    return x & (d - 1)


def _norm_mod(x, g, sc, sh):
    ms = jnp.mean(x * x, axis=-1, keepdims=True)
    return (x * lax.rsqrt(ms + EPS)) * g * (1.0 + sc) + sh


def _bdot(a, b):
    return jnp.dot(a.astype(BF16), b.astype(BF16), preferred_element_type=F32)


def _bdot_nt(a, b):
    return lax.dot_general(a.astype(BF16), b.astype(BF16), (((1,), (1,)), ((), ())),
                           preferred_element_type=F32)


def _pack_bf16_pairs(x):
    w = x.shape[1] // 2

    def rne_bits(v):
        b = pltpu.bitcast(v, I32)
        return b + 0x7FFF + (lax.shift_right_logical(b, 16) & 1)

    lo = lax.shift_right_logical(rne_bits(x[:, :w]), 16)
    hi = rne_bits(x[:, w:]) & jnp.int32(-65536)
    return lo | hi


def _unpack_bf16_pairs(p):
    lo = pltpu.bitcast(lax.shift_left(p, 16), F32)
    hi = pltpu.bitcast(p & jnp.int32(-65536), F32)
    return lo.astype(BF16), hi.astype(BF16)


def _knorm(k, gk2):
    lane = lax.broadcasted_iota(I32, k.shape, 1)
    first = lane < HEAD_DIM
    k2 = k * k
    s0 = jnp.sum(jnp.where(first, k2, 0.0), axis=-1, keepdims=True)
    s1 = jnp.sum(jnp.where(first, 0.0, k2), axis=-1, keepdims=True)
    ms = jnp.where(first, s0, s1) * (1.0 / HEAD_DIM)
    return k * lax.rsqrt(ms + EPS) * gk2


def _lru_coeffs(u, wg, gb, lam):
    g = _bdot(u, wg) + gb
    r = _sigmoid(g[:, :D_RNN])
    i = _sigmoid(g[:, D_RNN:])
    log_a = (-LRU_C * r) * _softplus(-lam)
    a = jnp.exp(log_a)
    b = jnp.sqrt(_neg_expm1(2.0 * log_a)) * (i * u)
    return a, b


def _adaln_kernel(c_ref, w_ref, b_ref, o_ref):
    o_ref[...] = _bdot(_silu(c_ref[...]), w_ref[...]) + b_ref[...]


def _adaln(c_all, ada_w, ada_b):
    n = c_all.shape[0]
    return pl.pallas_call(
        _adaln_kernel,
        grid=(6,),
        in_specs=[pl.BlockSpec((n, D_MODEL), lambda j: (0, 0)),
                  pl.BlockSpec((D_MODEL, D_MODEL), lambda j: (0, j)),
                  pl.BlockSpec((1, D_MODEL), lambda j: (0, j))],
        out_specs=pl.BlockSpec((n, D_MODEL), lambda j: (0, j)),
        out_shape=jax.ShapeDtypeStruct((n, 6 * D_MODEL), F32),
        compiler_params=pltpu.CompilerParams(dimension_semantics=("arbitrary",), vmem_limit_bytes=VMEM_LIMIT),
        name="adaln",
    )(c_all, ada_w, ada_b.reshape(1, -1))


def _scan_rows(a, b, h_in):
    n = a.shape[0]
    row = lax.broadcasted_iota(I32, a.shape, 0)
    s = 1
    while s < n:
        m = row >= s
        a_sh = jnp.where(m, pltpu.roll(a, s, 0), 1.0)
        b_sh = jnp.where(m, pltpu.roll(b, s, 0), 0.0)
        b = a * b_sh + b
        a = a * a_sh
        s *= 2
    return a * h_in + b


def _front_prompt_kernel(x_ref, sc_ref, sh_ref, g_ref, win_ref, cw_ref, cb_ref, wg_ref, gb_ref, lam_ref, gk_ref,
                         prev_ref, h0_ref, r_ref, q_ref, k_ref, v_ref, cs_ref, hs_ref, tail_ref, hc_ref):
    j = pl.program_id(1)
    tm = x_ref.shape[0]

    @pl.when(j == 0)
    def _():
        tail_ref[...] = prev_ref[...]
        hc_ref[...] = h0_ref[...]

    h = _norm_mod(x_ref[...], g_ref[...], sc_ref[...], sh_ref[...])
    proj = jnp.dot(h.astype(BF16), win_ref[...], preferred_element_type=F32)
    xr = proj[:, 0:D_RNN]
    yr = proj[:, D_RNN:2 * D_RNN]
    q_ref[...] = proj[:, 2 * D_RNN:2 * D_RNN + D_ATT]
    k_ref[...] = _knorm(proj[:, 2 * D_RNN + D_ATT:2 * D_RNN + D_ATT + D_KV], gk_ref[...])
    v_ref[...] = proj[:, 2 * D_RNN + D_ATT + D_KV:D_IN]

    tail = tail_ref[...]
    row8 = lax.broadcasted_iota(I32, tail.shape, 0)

    def shifted(s):
        rolled = pltpu.roll(xr, s, 0)
        top = jnp.where(row8 < s, pltpu.roll(tail, s, 0), rolled[0:SUBLANES])
        return jnp.concatenate([top, rolled[SUBLANES:]], axis=0)

    cw = cw_ref[...]
    u = cb_ref[...] + shifted(3) * cw[0:1]
    u = u + shifted(2) * cw[1:2]
    u = u + shifted(1) * cw[2:3]
    u = u + xr * cw[3:4]
    tail_ref[...] = xr[tm - SUBLANES:tm]

    a, b = _lru_coeffs(u, wg_ref[...], gb_ref[...], lam_ref[...])
    hs = _scan_rows(a, b, hc_ref[...])
    hc_ref[...] = hs[tm - 1:tm]
    r_ref[...] = hs * _gelu_tanh(yr)

    @pl.when(j == pl.num_programs(1) - 1)
    def _():
        cs_ref[...] = xr[tm - SUBLANES:tm]
        hs_ref[...] = hs[tm - 1:tm]


def _front_prompt(x, sc, sh, g, win, cw, cb, wg, gb, lam, gk, prev, h0):
    bsz, t, _ = x.shape
    tm = TM_PROMPT
    full = lambda a: pl.BlockSpec(a.shape, lambda b, j: (0,) * a.ndim)
    per_b = lambda a: pl.BlockSpec((None,) + a.shape[1:], lambda b, j: (b,) + (0,) * (a.ndim - 1))
    tile = lambda w: pl.BlockSpec((None, tm, w), lambda b, j: (b, j, 0))
    return pl.pallas_call(
        _front_prompt_kernel,
        grid=(bsz, t // tm),
        in_specs=[tile(D_MODEL), per_b(sc), per_b(sh), full(g), full(win), full(cw), full(cb), full(wg), full(gb),
                  full(lam), full(gk), per_b(prev), per_b(h0)],
        out_specs=[tile(D_RNN), tile(D_ATT), tile(D_KV), tile(D_KV),
                   pl.BlockSpec((None, SUBLANES, D_RNN), lambda b, j: (b, 0, 0)),
                   pl.BlockSpec((None, 1, D_RNN), lambda b, j: (b, 0, 0))],
        out_shape=[jax.ShapeDtypeStruct((bsz, t, D_RNN), F32), jax.ShapeDtypeStruct((bsz, t, D_ATT), F32),
                   jax.ShapeDtypeStruct((bsz, t, D_KV), F32), jax.ShapeDtypeStruct((bsz, t, D_KV), F32),
                   jax.ShapeDtypeStruct((bsz, SUBLANES, D_RNN), F32), jax.ShapeDtypeStruct((bsz, 1, D_RNN), F32)],
        scratch_shapes=[pltpu.VMEM((SUBLANES, D_RNN), F32), pltpu.VMEM((1, D_RNN), F32)],
        compiler_params=pltpu.CompilerParams(dimension_semantics=("arbitrary", "arbitrary"),
                                             vmem_limit_bytes=VMEM_LIMIT),
        name="front_prompt",
    )(x, sc, sh, g, win, cw, cb, wg, gb, lam, gk, prev, h0)


def _front_sample_kernel(x_ref, sc_ref, sh_ref, g_ref, win_ref, cw_ref, cb_ref, wg_ref, gb_ref, lam_ref, gk_ref,
                         prev_ref, h0_ref, r_ref, q_ref, k_ref, v_ref, cs_ref, hs_ref):
    t_len, bsz, _ = x_ref.shape
    x = x_ref[...]
    ms = jnp.mean(x * x, axis=-1, keepdims=True)
    h = (x * lax.rsqrt(ms + EPS)) * g_ref[...] * (1.0 + sc_ref[...]) + sh_ref[...]
    proj = jnp.dot(h.reshape(t_len * bsz, D_MODEL).astype(BF16), win_ref[...], preferred_element_type=F32)
    xr = proj[:, 0:D_RNN]
    yr = proj[:, D_RNN:2 * D_RNN]
    q_ref[...] = proj[:, 2 * D_RNN:2 * D_RNN + D_ATT].reshape(t_len, bsz, D_ATT)
    k_ref[...] = _knorm(proj[:, 2 * D_RNN + D_ATT:2 * D_RNN + D_ATT + D_KV], gk_ref[...]).reshape(t_len, bsz, D_KV)
    v_ref[...] = proj[:, 2 * D_RNN + D_ATT + D_KV:D_IN].reshape(t_len, bsz, D_KV)

    def at_time(t):
        if t >= 0:
            return xr[t * bsz:(t + 1) * bsz]
        return prev_ref[CONV_W - 1 + t]

    cw = cw_ref[...]
    us = []
    for t in range(t_len):
        u = cb_ref[...] + at_time(t - 3) * cw[0:1]
        u = u + at_time(t - 2) * cw[1:2]
        u = u + at_time(t - 1) * cw[2:3]
        u = u + at_time(t) * cw[3:4]
        us.append(u)
    a, b = _lru_coeffs(jnp.concatenate(us, axis=0), wg_ref[...], gb_ref[...], lam_ref[...])
    hcur = h0_ref[...]
    for t in range(t_len):
        hcur = a[t * bsz:(t + 1) * bsz] * hcur + b[t * bsz:(t + 1) * bsz]
        r_ref[t] = hcur * _gelu_tanh(yr[t * bsz:(t + 1) * bsz])
    hs_ref[...] = hcur
    for s in range(CONV_W - 1):
        cs_ref[s] = at_time(t_len - (CONV_W - 1) + s)


def _front_sample(x_t, sc, sh, g, win, cw, cb, wg, gb, lam, gk, prev_t, h0):
    t_len, bsz, _ = x_t.shape
    return pl.pallas_call(
        _front_sample_kernel,
        out_shape=[jax.ShapeDtypeStruct((t_len, bsz, D_RNN), F32), jax.ShapeDtypeStruct((t_len, bsz, D_ATT), F32),
                   jax.ShapeDtypeStruct((t_len, bsz, D_KV), F32), jax.ShapeDtypeStruct((t_len, bsz, D_KV), F32),
                   jax.ShapeDtypeStruct((CONV_W - 1, bsz, D_RNN), F32), jax.ShapeDtypeStruct((bsz, D_RNN), F32)],
        compiler_params=pltpu.CompilerParams(vmem_limit_bytes=VMEM_LIMIT),
        name="front_sample",
    )(x_t, sc, sh, g, win, cw, cb, wg, gb, lam, gk, prev_t, h0)


def _qnorm(q, gq):
    ms = jnp.mean(q * q, axis=-1, keepdims=True)
    return q * lax.rsqrt(ms + EPS) * gq


def _attn_prompt_kernel(sink_ref, q_ref, kp_ref, kc_ref, vp_ref, vc_ref, gq_ref, o_ref):
    j = pl.program_id(1)
    blk = q_ref.shape[0]
    q = q_ref[...]
    kk = jnp.concatenate([kp_ref[...], kc_ref[...]], axis=0)
    vv = jnp.concatenate([vp_ref[...], vc_ref[...]], axis=0)
    qi = lax.broadcasted_iota(I32, (blk, 2 * blk), 0)
    kj = lax.broadcasted_iota(I32, (blk, 2 * blk), 1)
    dist = blk + qi - kj
    valid = (dist >= 0) & (dist <= WINDOW) & ((kj >= blk) | (j > 0))
    distf = dist.astype(F32)
    outs = []
    for h in range(N_HEADS):
        g = h // GQA
        qh = _qnorm(q[:, h * HEAD_DIM:(h + 1) * HEAD_DIM], gq_ref[...])
        s = _bdot_nt(qh, kk[:, g * HEAD_DIM:(g + 1) * HEAD_DIM]) * (HEAD_DIM ** -0.5)
        s = s - (2.0 ** -(h + 1)) * distf
        s = jnp.where(valid, s, NEG)
        sink = sink_ref[h]
        m = jnp.maximum(jnp.max(s, axis=-1, keepdims=True), sink)
        p = jnp.exp(s - m)
        l = jnp.sum(p, axis=-1, keepdims=True) + jnp.exp(sink - m)
        outs.append(_bdot(p, vv[:, g * HEAD_DIM:(g + 1) * HEAD_DIM]) / l)
    o_ref[...] = jnp.concatenate(outs, axis=1)


def _attn_prompt(q, k, v, gq, sinks):
    bsz, t, _ = q.shape
    blk = ATT_BLOCK
    cur = lambda w: pl.BlockSpec((None, blk, w), lambda b, j: (b, j, 0))
    prv = lambda w: pl.BlockSpec((None, blk, w), lambda b, j: (b, jnp.maximum(j - 1, 0), 0))
    return pl.pallas_call(
        _attn_prompt_kernel,
        grid=(bsz, t // blk),
        in_specs=[pl.BlockSpec(memory_space=pltpu.SMEM), cur(D_ATT), prv(D_KV), cur(D_KV), prv(D_KV), cur(D_KV),
                  pl.BlockSpec(gq.shape, lambda b, j: (0, 0))],
        out_specs=cur(D_ATT),
        out_shape=jax.ShapeDtypeStruct((bsz, t, D_ATT), F32),
        compiler_params=pltpu.CompilerParams(dimension_semantics=("arbitrary", "arbitrary"),
                                             vmem_limit_bytes=VMEM_LIMIT),
        name="attn_prompt",
    )(sinks, q, k, k, v, v, gq)


def _attn_sample_kernel(sink_ref, q_ref, kn_ref, vn_ref, kc_ref, vc_ref, gq_ref, o_ref):
    t_len, cb, _ = q_ref.shape
    win = kc_ref.shape[1]
    rows = GQA * t_len * cb
    kc = kc_ref[...].reshape(cb * win, D_KV)
    vc = vc_ref[...].reshape(cb * win, D_KV)
    kn = kn_ref[...].reshape(t_len * cb, D_KV)
    vn = vn_ref[...].reshape(t_len * cb, D_KV)

    r_c = lax.broadcasted_iota(I32, (rows, cb * win), 0)
    c_c = lax.broadcasted_iota(I32, (rows, cb * win), 1)
    t_c = _div_pow2(_mod_pow2(r_c, t_len * cb), cb)
    valid_c = (_mod_pow2(r_c, cb) == _div_pow2(c_c, win)) & (_mod_pow2(c_c, win) >= t_c)
    dist_c = (win + t_c - _mod_pow2(c_c, win)).astype(F32)
    r_n = lax.broadcasted_iota(I32, (rows, t_len * cb), 0)
    c_n = lax.broadcasted_iota(I32, (rows, t_len * cb), 1)
    t_n = _div_pow2(_mod_pow2(r_n, t_len * cb), cb)
    valid_n = (_mod_pow2(r_n, cb) == _mod_pow2(c_n, cb)) & (_div_pow2(c_n, cb) <= t_n)
    dist_n = (t_n - _div_pow2(c_n, cb)).astype(F32)
    hl = _div_pow2(lax.broadcasted_iota(I32, (rows, 1), 0), t_len * cb)

    per_group = []
    for g in range(N_KV_HEADS):
        slabs = [q_ref[t][:, (g * GQA + i) * HEAD_DIM:(g * GQA + i + 1) * HEAD_DIM]
                 for i in range(GQA) for t in range(t_len)]
        qg = _qnorm(jnp.concatenate(slabs, axis=0), gq_ref[...])
        slope = jnp.zeros((rows, 1), F32)
        sink = jnp.zeros((rows, 1), F32)
        for i in range(GQA):
            slope = jnp.where(hl == i, 2.0 ** -(g * GQA + i + 1), slope)
            sink = jnp.where(hl == i, sink_ref[g * GQA + i], sink)
        lo, hi = g * HEAD_DIM, (g + 1) * HEAD_DIM
        s_c = _bdot_nt(qg, kc[:, lo:hi]) * (HEAD_DIM ** -0.5) - slope * dist_c
        s_n = _bdot_nt(qg, kn[:, lo:hi]) * (HEAD_DIM ** -0.5) - slope * dist_n
        s_c = jnp.where(valid_c, s_c, NEG)
        s_n = jnp.where(valid_n, s_n, NEG)
        m = jnp.maximum(jnp.maximum(jnp.max(s_c, axis=-1, keepdims=True), jnp.max(s_n, axis=-1, keepdims=True)), sink)
        p_c = jnp.exp(s_c - m)
        p_n = jnp.exp(s_n - m)
        l = jnp.sum(p_c, axis=-1, keepdims=True) + jnp.sum(p_n, axis=-1, keepdims=True) + jnp.exp(sink - m)
        per_group.append((_bdot(p_c, vc[:, lo:hi]) + _bdot(p_n, vn[:, lo:hi])) / l)
    for t in range(t_len):
        o_ref[t] = jnp.concatenate(
            [per_group[g][(i * t_len + t) * cb:(i * t_len + t + 1) * cb] for g in range(N_KV_HEADS) for i in range(GQA)],
            axis=1)


def _attn_sample(q_t, k_t, v_t, cache_k, cache_v, gq, sinks):
    t_len, bsz, _ = q_t.shape
    cb = SAMPLE_CHUNK
    win = cache_k.shape[1]
    new = lambda w: pl.BlockSpec((t_len, cb, w), lambda c: (0, c, 0))
    old = pl.BlockSpec((cb, win, D_KV), lambda c: (c, 0, 0))
    return pl.pallas_call(
        _attn_sample_kernel,
        grid=(bsz // cb,),
        in_specs=[pl.BlockSpec(memory_space=pltpu.SMEM), new(D_ATT), new(D_KV), new(D_KV), old, old,
                  pl.BlockSpec(gq.shape, lambda c: (0, 0))],
        out_specs=new(D_ATT),
        out_shape=jax.ShapeDtypeStruct((t_len, bsz, D_ATT), F32),
        compiler_params=pltpu.CompilerParams(dimension_semantics=("arbitrary",), vmem_limit_bytes=VMEM_LIMIT),
        name="attn_sample",
    )(sinks, q_t, k_t, v_t, cache_k, cache_v, gq)


def _route(s_t, sb_t):
    tm = s_t.shape[1]
    i8 = lax.broadcasted_iota(I32, (GROUP_SIZE, tm), 0)
    ninf = -jnp.inf
    sg = [sb_t[GROUP_SIZE * g:GROUP_SIZE * (g + 1)] for g in range(N_GROUPS)]
    gscore = []
    for g in range(N_GROUPS):
        m1 = jnp.max(sg[g], axis=0, keepdims=True)
        i1 = jnp.min(jnp.where(sg[g] == m1, i8, GROUP_SIZE), axis=0, keepdims=True)
        m2 = jnp.max(jnp.where(i8 == i1, ninf, sg[g]), axis=0, keepdims=True)
        gscore.append(m1 + m2)
    gs = jnp.concatenate(gscore, axis=0)
    gsel = jnp.zeros((N_GROUPS, tm), I32)
    for _ in range(TOPK_GROUPS):
        m = jnp.max(gs, axis=0, keepdims=True)
        idx = jnp.min(jnp.where(gs == m, i8, N_GROUPS), axis=0, keepdims=True)
        hit = i8 == idx
        gsel = jnp.where(hit, 1, gsel)
        gs = jnp.where(hit, ninf, gs)
    sm = [jnp.where(gsel[g:g + 1] > 0, sg[g], NEG) for g in range(N_GROUPS)]
    eid = [i8 + GROUP_SIZE * g for g in range(N_GROUPS)]
    sel = [jnp.zeros((GROUP_SIZE, tm), F32) for _ in range(N_GROUPS)]
    idxs, ws = [], []
    for _ in range(TOP_K):
        cm = functools.reduce(jnp.maximum, sm)
        m = jnp.max(cm, axis=0, keepdims=True)
        cand = functools.reduce(jnp.minimum, [jnp.where(sm[g] == m, eid[g], N_EXPERTS) for g in range(N_GROUPS)])
        idx = jnp.min(cand, axis=0, keepdims=True)
        wk = jnp.zeros((GROUP_SIZE, tm), F32)
        for g in range(N_GROUPS):
            hit = eid[g] == idx
            wk = wk + jnp.where(hit, s_t[GROUP_SIZE * g:GROUP_SIZE * (g + 1)], 0.0)
            sel[g] = jnp.where(hit, 1.0, sel[g])
            sm[g] = jnp.where(hit, ninf, sm[g])
        idxs.append(idx)
        ws.append(jnp.sum(wk, axis=0, keepdims=True))
    return idxs, ws, jnp.concatenate(sel, axis=0), eid


def _post_kernel(x_ref, r_ref, a_ref, g1_ref, sc2_ref, sh2_ref, nf_ref, wot_ref, wob_ref, wrt_ref, rb_ref, cin_ref,
                 x1_ref, h2_ref, eidx_ref, gw_ref, rank_ref, cnt_ref, carry_ref):
    tm = x_ref.shape[0]

    @pl.when(pl.program_id(0) == 0)
    def _():
        carry_ref[...] = cin_ref[...]

    mixed = (jnp.dot(r_ref[...].astype(BF16), wot_ref[...], preferred_element_type=F32)
             + jnp.dot(a_ref[...].astype(BF16), wob_ref[...], preferred_element_type=F32))
    x1 = x_ref[...] + g1_ref[...] * mixed
    x1_ref[...] = x1
    h2 = _norm_mod(x1, nf_ref[...], sc2_ref[...], sh2_ref[...])
    h2_ref[...] = _pack_bf16_pairs(h2)

    wr = wrt_ref[...]
    wr_hi = wr.astype(BF16)
    wr_lo = (wr - wr_hi.astype(F32)).astype(BF16)
    h_hi = h2.astype(BF16)
    h_lo = (h2 - h_hi.astype(F32)).astype(BF16)
    logits = _bdot_nt(wr_hi, h_hi) + _bdot_nt(wr_hi, h_lo) + _bdot_nt(wr_lo, h_hi)
    s_t = _sigmoid(logits)
    idxs, ws, sel, eid = _route(s_t, s_t + rb_ref[...])

    rr = lax.broadcasted_iota(I32, (tm, tm), 0)
    cc = lax.broadcasted_iota(I32, (tm, tm), 1)
    before = jnp.where(rr < cc, 1.0, 0.0).astype(BF16)
    carry = carry_ref[...]
    tot = jnp.dot(sel.astype(BF16), before, preferred_element_type=F32) + carry[:, 0:1]
    ranks = []
    for k in range(TOP_K):
        acc = jnp.zeros((GROUP_SIZE, tm), F32)
        for g in range(N_GROUPS):
            acc = acc + jnp.where(eid[g] == idxs[k], tot[GROUP_SIZE * g:GROUP_SIZE * (g + 1)], 0.0)
        ranks.append(jnp.sum(acc, axis=0, keepdims=True))
    carry = carry + jnp.sum(sel, axis=1, keepdims=True)
    carry_ref[...] = carry
    cnt_ref[...] = carry

    wsum = functools.reduce(lambda p, q: p + q, ws)
    pad_i = jnp.zeros((SUBLANES - TOP_K, tm), I32)
    pad_f = jnp.zeros((SUBLANES - TOP_K, tm), F32)
    eidx_ref[...] = jnp.concatenate(idxs + [pad_i], axis=0)
    rank_ref[...] = jnp.concatenate([r.astype(I32) for r in ranks] + [pad_i], axis=0)
    gw_ref[...] = jnp.concatenate([w / wsum * ROUTE_SCALE for w in ws] + [pad_f], axis=0)


def _post(x, r, a, g1, sc2, sh2, nf, wo_top, wo_bot, wr_t, rb, cnt_in, tm, mod_spec):
    n = x.shape[0]
    tile = lambda w: pl.BlockSpec((tm, w), lambda i: (i, 0))
    full = lambda arr: pl.BlockSpec(arr.shape, lambda i: (0,) * arr.ndim)
    slot = pl.BlockSpec((SUBLANES, tm), lambda i: (0, i))
    return pl.pallas_call(
        _post_kernel,
        grid=(n // tm,),
        in_specs=[tile(D_MODEL), tile(D_RNN), tile(D_ATT), mod_spec, mod_spec, mod_spec, full(nf), full(wo_top),
                  full(wo_bot), full(wr_t), full(rb), full(cnt_in)],
        out_specs=[tile(D_MODEL), tile(D_MODEL // 2), slot, slot, slot, full(cnt_in)],
        out_shape=[jax.ShapeDtypeStruct((n, D_MODEL), F32), jax.ShapeDtypeStruct((n, D_MODEL // 2), I32),
                   jax.ShapeDtypeStruct((SUBLANES, n), I32), jax.ShapeDtypeStruct((SUBLANES, n), F32),
                   jax.ShapeDtypeStruct((SUBLANES, n), I32), jax.ShapeDtypeStruct(cnt_in.shape, F32)],
        scratch_shapes=[pltpu.VMEM(cnt_in.shape, F32)],
        compiler_params=pltpu.CompilerParams(dimension_semantics=("arbitrary",), vmem_limit_bytes=VMEM_LIMIT),
        name="post_mix",
    )(x, r, a, g1, sc2, sh2, nf, wo_top, wo_bot, wr_t, rb, cnt_in)


def _plan_kernel(cnt_ref, ps_ref, meta_ref, *, bm):
    cnt = cnt_ref[...]
    padded = jnp.ceil(cnt * (1.0 / bm)) * bm
    row = lax.broadcasted_iota(I32, cnt.shape, 0)
    pend = padded
    s = 1
    while s < N_EXPERTS:
        pend = pend + jnp.where(row >= s, pltpu.roll(pend, s, 0), 0.0)
        s *= 2
    pstart = pend - padded
    ps_ref[...] = pstart
    nb = meta_ref.shape[1]
    start = lax.broadcasted_iota(I32, (N_EXPERTS, nb), 1).astype(F32) * bm
    blk_e = jnp.minimum(jnp.sum(jnp.where(pend[:, 0:1] <= start, 1.0, 0.0), axis=0, keepdims=True), N_EXPERTS - 1.0)
    e_iota = lax.broadcasted_iota(I32, (N_EXPERTS, nb), 0).astype(F32)
    end_b = jnp.sum(jnp.where(e_iota == blk_e, (pstart + cnt)[:, 0:1], 0.0), axis=0, keepdims=True)
    n_valid = jnp.clip(end_b - start[0:1], 0.0, bm)
    meta_ref[...] = jnp.concatenate([blk_e.astype(I32), n_valid.astype(I32), jnp.zeros((SUBLANES - 2, nb), I32)],
                                    axis=0)


def _plan(cnt, bm, n_blocks):
    assert bm & (bm - 1) == 0
    nb = -(-n_blocks // LANES) * LANES
    return pl.pallas_call(
        functools.partial(_plan_kernel, bm=bm),
        out_shape=[jax.ShapeDtypeStruct(cnt.shape, F32), jax.ShapeDtypeStruct((SUBLANES, nb), I32)],
        name="moe_plan",
    )(cnt)


def _layout_kernel(ps_ref, eidx_ref, rank_ref, gw_ref, *out_refs, chunks):
    *pos_refs, w_ref = out_refs
    tn = eidx_ref.shape[1]
    e_iota = lax.broadcasted_iota(I32, (N_EXPERTS, tn), 0)
    ps = ps_ref[...][:, 0:1]
    rows = []
    for k in range(TOP_K):
        hit = e_iota == eidx_ref[k:k + 1, :]
        base = jnp.sum(jnp.where(hit, ps, 0.0), axis=0, keepdims=True)
        rows.append(base.astype(I32) + rank_ref[k:k + 1, :])
    rows.append(jnp.zeros((SUBLANES - TOP_K, tn), I32))
    pos = jnp.concatenate(rows, axis=0)
    for pos_ref, c in zip(pos_refs, chunks):
        for q in range(tn // c):
            pos_ref[q] = pos[:, q * c:(q + 1) * c]
    gw = gw_ref[...]
    rep = jnp.concatenate([jnp.broadcast_to(gw[k:k + 1], (SC_LANES, tn)) for k in range(SUBLANES)], axis=0)
    w_ref[...] = rep.T


def _layouts(pstart, eidx, rank, gw, chunks):
    n = eidx.shape[1]
    tn = min(n, POS_TILE)
    slot = pl.BlockSpec((SUBLANES, tn), lambda i: (0, i))
    return pl.pallas_call(
        functools.partial(_layout_kernel, chunks=chunks),
        grid=(n // tn,),
        in_specs=[pl.BlockSpec(pstart.shape, lambda i: (0, 0)), slot, slot, slot],
        out_specs=[pl.BlockSpec((tn // c, SUBLANES, c), lambda i: (i, 0, 0)) for c in chunks]
        + [pl.BlockSpec((tn, SUBLANES * SC_LANES), lambda i: (i, 0))],
        out_shape=[jax.ShapeDtypeStruct((n // c, SUBLANES, c), I32) for c in chunks]
        + [jax.ShapeDtypeStruct((n, SUBLANES * SC_LANES), F32)],
        compiler_params=pltpu.CompilerParams(dimension_semantics=("arbitrary",), vmem_limit_bytes=VMEM_LIMIT),
        name="moe_layout",
    )(pstart, eidx, rank, gw)


def _sc_worker_id():
    return lax.axis_index("s") * SC_CORES + lax.axis_index("c")


def _sc_mesh():
    return plsc.VectorSubcoreMesh(core_axis_name="c", subcore_axis_name="s")


def _sc_dispatch(seg_a, seg_b, total_rows):
    (h_a, pos_a, c_a), (h_b, pos_b, c_b) = seg_a, seg_b
    width = h_a.shape[1]

    @functools.partial(
        pl.kernel, mesh=_sc_mesh(), out_type=jax.ShapeDtypeStruct((total_rows, width), I32),
        scratch_types=[pltpu.VMEM((SUBLANES, c_a), I32), pltpu.VMEM((c_a, width), I32),
                       pltpu.VMEM((SUBLANES, c_b), I32), pltpu.VMEM((c_b, width), I32), pltpu.SemaphoreType.DMA])
    def run(ha_hbm, pa_hbm, hb_hbm, pb_hbm, xs_hbm, idx_a, rows_a, idx_b, rows_b, sem):
        wid = _sc_worker_id()

        def segment(h_hbm, p_hbm, idx_v, rows_v, c):
            nch = h_hbm.shape[0] // (SC_WORKERS * c)

            @pl.loop(0, nch)
            def _(ci):
                chunk = wid * nch + ci
                pltpu.sync_copy(p_hbm.at[chunk], idx_v)
                pltpu.sync_copy(h_hbm.at[pl.ds(chunk * c, c)], rows_v)
                copies = [pltpu.async_copy(rows_v, xs_hbm.at[idx_v.at[k]], sem) for k in range(TOP_K)]
                for cp in copies:
                    cp.wait()

        segment(ha_hbm, pa_hbm, idx_a, rows_a, c_a)
        segment(hb_hbm, pb_hbm, idx_b, rows_b, c_b)

    return run(h_a, pos_a, h_b, pos_b)


def _experts_kernel(be_ref, nv_ref, xs_ref, wg_ref, wu_ref, wd_ref, y_ref):
    del be_ref
    half = xs_ref.shape[1]
    row = lax.broadcasted_iota(I32, xs_ref.shape, 0)
    x_lo, x_hi = _unpack_bf16_pairs(jnp.where(row < nv_ref[pl.program_id(0)], xs_ref[...], 0))
    wg = wg_ref[...].astype(BF16)
    wu = wu_ref[...].astype(BF16)
    a = (jnp.dot(x_lo, wg[:half], preferred_element_type=F32) + jnp.dot(x_hi, wg[half:], preferred_element_type=F32))
    b = (jnp.dot(x_lo, wu[:half], preferred_element_type=F32) + jnp.dot(x_hi, wu[half:], preferred_element_type=F32))
    y_ref[...] = _bdot(_silu(a) * b, wd_ref[...])


def _experts(blk_e, n_valid, xs, wg, wu, wd):
    rows = xs.shape[0]
    bm = EXPERT_BLOCK
    row_blk = lambda i, be, nv: (i, 0)
    w_blk = lambda i, be, nv: (be[i], 0, 0)
    grid_spec = pltpu.PrefetchScalarGridSpec(
        num_scalar_prefetch=2,
        grid=(rows // bm,),
        in_specs=[pl.BlockSpec((bm, D_MODEL // 2), row_blk),
                  pl.BlockSpec((None, D_MODEL, D_EXPERT), w_blk),
                  pl.BlockSpec((None, D_MODEL, D_EXPERT), w_blk),
                  pl.BlockSpec((None, D_EXPERT, D_MODEL), w_blk)],
        out_specs=pl.BlockSpec((bm, D_MODEL), row_blk),
    )
    return pl.pallas_call(
        _experts_kernel,
        grid_spec=grid_spec,
        out_shape=jax.ShapeDtypeStruct((rows, D_MODEL), F32),
        compiler_params=pltpu.CompilerParams(dimension_semantics=("arbitrary",), vmem_limit_bytes=VMEM_LIMIT),
        name="moe_experts",
    )(blk_e, n_valid, xs, wg, wu, wd)


def _sc_combine(y, seg_a, seg_b):
    (pos_a, w_a), (pos_b, w_b) = seg_a, seg_b
    c = COMBINE_CHUNK
    d = y.shape[1]
    n_a, n_b = w_a.shape[0], w_b.shape[0]

    @functools.partial(
        pl.kernel, mesh=_sc_mesh(),
        out_type=[jax.ShapeDtypeStruct((n_a, d), F32), jax.ShapeDtypeStruct((n_b, d), F32)],
        scratch_types=[pltpu.VMEM(pos_a.shape[1:], I32), pltpu.VMEM(pos_b.shape[1:], I32),
                       pltpu.VMEM((2, c, SUBLANES * SC_LANES), F32), pltpu.VMEM((2, TOP_K, c, d), F32),
                       pltpu.VMEM((c, d), F32), pltpu.SemaphoreType.DMA((2,)), pltpu.SemaphoreType.DMA])
    def run(y_hbm, pa_hbm, wa_hbm, pb_hbm, wb_hbm, oa_hbm, ob_hbm, idx_a, idx_b, w_v, buf_v, out_v, sems, out_sem):
        wid = _sc_worker_id()

        def segment(p_hbm, w_hbm, o_hbm, idx_v):
            per_w = idx_v.shape[1]
            nch = per_w // c
            base = wid * per_w
            pltpu.sync_copy(p_hbm.at[wid], idx_v)

            def in_copies(ci, b):
                w_copy = pltpu.make_async_copy(w_hbm.at[pl.ds(base + ci * c, c)], w_v.at[b], sems.at[b])
                return [w_copy] + [
                    pltpu.make_async_copy(y_hbm.at[idx_v.at[k, pl.ds(ci * c, c)]], buf_v.at[b].at[k], sems.at[b])
                    for k in range(TOP_K)]

            def out_copy(ci):
                return pltpu.make_async_copy(out_v, o_hbm.at[pl.ds(base + ci * c, c)], out_sem)

            def reduce_rows(ci, b):
                @pl.when(ci >= 1)
                def _():
                    out_copy(ci - 1).wait()

                @pl.loop(0, c)
                def _(t):
                    ws = [w_v[b, t, pl.ds(k * SC_LANES, SC_LANES)] for k in range(TOP_K)]
                    for j in range(d // SC_LANES):
                        lanes = pl.ds(j * SC_LANES, SC_LANES)
                        acc = buf_v[b, 0, t, lanes] * ws[0]
                        for k in range(1, TOP_K):
                            acc = acc + buf_v[b, k, t, lanes] * ws[k]
                        out_v[t, lanes] = acc

                out_copy(ci).start()

            for cp in in_copies(0, 0):
                cp.start()

            @pl.loop(0, nch, step=2)
            def _(ci):
                for b in range(2):
                    @pl.when(ci + b + 1 < nch)
                    def _():
                        for cp in in_copies(ci + b + 1, 1 - b):
                            cp.start()

                    for cp in in_copies(ci + b, b):
                        cp.wait()
                    reduce_rows(ci + b, b)

            out_copy(nch - 1).wait()

        segment(pa_hbm, wa_hbm, oa_hbm, idx_a)
        segment(pb_hbm, wb_hbm, ob_hbm, idx_b)

    return run(y, pos_a, w_a, pos_b, w_b)


def _shared_kernel(h2_ref, wsg_ref, wsu_ref, wsd_ref, o_ref):
    half = h2_ref.shape[1]
    x_lo, x_hi = _unpack_bf16_pairs(h2_ref[...])
    wsg, wsu = wsg_ref[...], wsu_ref[...]
    a = (jnp.dot(x_lo, wsg[:half], preferred_element_type=F32) + jnp.dot(x_hi, wsg[half:], preferred_element_type=F32))
    b = (jnp.dot(x_lo, wsu[:half], preferred_element_type=F32) + jnp.dot(x_hi, wsu[half:], preferred_element_type=F32))
    o_ref[...] = jnp.dot((_silu(a) * b).astype(BF16), wsd_ref[...], preferred_element_type=F32)


def _shared(h2, wsg, wsu, wsd, tm):
    n = h2.shape[0]
    full = lambda arr: pl.BlockSpec(arr.shape, lambda i: (0,) * arr.ndim)
    return pl.pallas_call(
        _shared_kernel,
        grid=(n // tm,),
        in_specs=[pl.BlockSpec((tm, D_MODEL // 2), lambda i: (i, 0)), full(wsg), full(wsu), full(wsd)],
        out_specs=pl.BlockSpec((tm, D_MODEL), lambda i: (i, 0)),
        out_shape=jax.ShapeDtypeStruct((n, D_MODEL), F32),
        compiler_params=pltpu.CompilerParams(dimension_semantics=("arbitrary",), vmem_limit_bytes=VMEM_LIMIT),
        name="shared_expert",
    )(h2, wsg, wsu, wsd)


def _final_kernel(x1_ref, routed_ref, shared_ref, g2_ref, o_ref):
    o_ref[...] = x1_ref[...] + g2_ref[...] * (routed_ref[...] + shared_ref[...])


def _final(x1, routed, shared, g2, tm, mod_spec):
    n = x1.shape[0]
    tile = pl.BlockSpec((tm, D_MODEL), lambda i: (i, 0))
    return pl.pallas_call(
        _final_kernel,
        grid=(n // tm,),
        in_specs=[tile, tile, tile, mod_spec],
        out_specs=tile,
        out_shape=jax.ShapeDtypeStruct((n, D_MODEL), F32),
        compiler_params=pltpu.CompilerParams(dimension_semantics=("arbitrary",), vmem_limit_bytes=VMEM_LIMIT),
        name="ffn_residual",
    )(x1, routed, shared, g2)


def _block_diag(w):
    nb, bi, bj = w.shape
    return jnp.einsum('nij,nm->nimj', w, jnp.eye(nb, dtype=w.dtype)).reshape(nb * bi, nb * bj)


def kernel(x_prompt, x_sample, c_prompt, c_sample, cache_k_win, cache_v_win, state_conv, state_rnn, ada_w, ada_b, norm_mix, w_in, conv_w, conv_b, gate_a_w, gate_a_b, gate_x_w, gate_x_b, lru_lambda, q_norm, k_norm, attn_sinks, w_out, norm_ffn, router_w, router_bias, exp_w_gate, exp_w_up, exp_w_down, sh_w_gate, sh_w_up, sh_w_down):
    bp, tp, _ = x_prompt.shape
    bs, ts, _ = x_sample.shape
    win = cache_k_win.shape[2]
    n_p, n_s = bp * tp, bs * ts
    row = lambda v: v.reshape(1, -1)

    g_mix, g_ffn = row(norm_mix[0]), row(norm_ffn[0])
    win_bf = w_in[0].astype(BF16)
    cw, cb = conv_w[0], row(conv_b[0])
    wg = jnp.concatenate([_block_diag(gate_a_w[0]), _block_diag(gate_x_w[0])], axis=1).astype(BF16)
    gb = row(jnp.concatenate([gate_a_b[0], gate_x_b[0]]))
    lam = row(lru_lambda[0])
    gq = row(q_norm[0])
    gk2 = row(jnp.tile(k_norm[0], N_KV_HEADS))
    sinks = attn_sinks[0]
    wo_top, wo_bot = w_out[0, :D_RNN].astype(BF16), w_out[0, D_RNN:].astype(BF16)
    wr_t = router_w[0].T
    rb = router_bias[0].reshape(N_EXPERTS, 1)
    wsg, wsu, wsd = sh_w_gate[0].astype(BF16), sh_w_up[0].astype(BF16), sh_w_down[0].astype(BF16)

    mod = _adaln(jnp.concatenate([c_prompt, c_sample], axis=0), ada_w[0], ada_b[0])
    chunks = [mod[:, i * D_MODEL:(i + 1) * D_MODEL] for i in range(6)]
    sh1p, sc1p, g1p, sh2p, sc2p, g2p = [c[:bp].reshape(bp, 1, D_MODEL) for c in chunks]
    sh1s, sc1s, g1s, sh2s, sc2s, g2s = [c[bp:] for c in chunks]

    conv0 = jnp.zeros((bp, SUBLANES, D_RNN), F32)
    h0 = jnp.zeros((bp, 1, D_RNN), F32)
    r_p, q_p, k_p, v_p, cs_p, hs_p = _front_prompt(x_prompt, sc1p, sh1p, g_mix, win_bf, cw, cb, wg, gb, lam, gk2,
                                                   conv0, h0)
    a_p = _attn_prompt(q_p, k_p, v_p, gq, sinks)

    x_s_t = jnp.swapaxes(x_sample, 0, 1)
    r_s, q_s, k_s, v_s, cs_s, hs_s = _front_sample(x_s_t, sc1s, sh1s, g_mix, win_bf, cw, cb, wg, gb, lam, gk2,
                                                   jnp.swapaxes(state_conv[0], 0, 1), state_rnn[0])
    cache_k = cache_k_win[0].reshape(bs, win, D_KV)
    cache_v = cache_v_win[0].reshape(bs, win, D_KV)
    a_s = _attn_sample(q_s, k_s, v_s, cache_k, cache_v, gq, sinks)

    tiles_per_seq = tp // TM_PROMPT
    mod_p = pl.BlockSpec((None, 1, D_MODEL), lambda i, *_: (i // tiles_per_seq, 0, 0))
    mod_s = pl.BlockSpec((bs, D_MODEL), lambda i, *_: (0, 0))
    cnt0 = jnp.zeros((N_EXPERTS, LANES), F32)
    x1_p, h2_p, eidx_p, gw_p, rank_p, cnt_p = _post(
        x_prompt.reshape(n_p, D_MODEL), r_p.reshape(n_p, D_RNN), a_p.reshape(n_p, D_ATT), g1p, sc2p, sh2p, g_ffn,
        wo_top, wo_bot, wr_t, rb, cnt0, TM_PROMPT, mod_p)
    x1_s, h2_s, eidx_s, gw_s, rank_s, cnt_all = _post(
        x_s_t.reshape(n_s, D_MODEL), r_s.reshape(n_s, D_RNN), a_s.reshape(n_s, D_ATT), g1s, sc2s, sh2s, g_ffn,
        wo_top, wo_bot, wr_t, rb, cnt_p, bs, mod_s)

    bm = EXPERT_BLOCK
    n_blocks = -(-((n_p + n_s) * TOP_K) // bm) + N_EXPERTS
    pstart, meta = _plan(cnt_all, bm, n_blocks)
    blk_e, n_valid = meta[0, :n_blocks], meta[1, :n_blocks]

    def sc_layouts(eidx, rank, gw, n):
        per_w = n // SC_WORKERS
        c = min(DISPATCH_CHUNK, per_w)
        outs = _layouts(pstart, eidx, rank, gw, (c,) if c == per_w else (c, per_w))
        return (outs[0], c), (outs[-2], outs[-1])

    (dpos_p, c_p), comb_p = sc_layouts(eidx_p, rank_p, gw_p, n_p)
    (dpos_s, c_s), comb_s = sc_layouts(eidx_s, rank_s, gw_s, n_s)
    xs = _sc_dispatch((h2_p, dpos_p, c_p), (h2_s, dpos_s, c_s), n_blocks * bm)
    y = _experts(blk_e, n_valid, xs, exp_w_gate[0], exp_w_up[0], exp_w_down[0])
    routed_p, routed_s = _sc_combine(y, comb_p, comb_s)
    y_p = _final(x1_p, routed_p, _shared(h2_p, wsg, wsu, wsd, TM_PROMPT), g2p, TM_PROMPT, mod_p)
    y_s = _final(x1_s, routed_s, _shared(h2_s, wsg, wsu, wsd, bs), g2s, bs, mod_s)

    y_prompt = y_p.reshape(bp, tp, D_MODEL)
    y_sample = jnp.swapaxes(y_s.reshape(ts, bs, D_MODEL), 0, 1)
    wk = min(WINDOW, tp)
    k_win_p = k_p[:, tp - wk:].reshape(1, bp, wk, N_KV_HEADS, HEAD_DIM)
    v_win_p = v_p[:, tp - wk:].reshape(1, bp, wk, N_KV_HEADS, HEAD_DIM)
    k_new = jnp.swapaxes(k_s, 0, 1)
    v_new = jnp.swapaxes(v_s, 0, 1)
    k_win_s = jnp.concatenate([cache_k, k_new], axis=1)[:, ts:].reshape(1, bs, win, N_KV_HEADS, HEAD_DIM)
    v_win_s = jnp.concatenate([cache_v, v_new], axis=1)[:, ts:].reshape(1, bs, win, N_KV_HEADS, HEAD_DIM)
    return (y_prompt, y_sample, k_win_p, v_win_p, cs_p[None, :, SUBLANES - (CONV_W - 1):], hs_p.reshape(1, bp, D_RNN),
            k_win_s, v_win_s, jnp.swapaxes(cs_s, 0, 1)[None], hs_s[None])
```

```python
import functools

import jax
import jax.numpy as jnp
from jax import lax
from jax.experimental import pallas as pl
from jax.experimental.pallas import tpu as pltpu
from jax.experimental.pallas import tpu_sc as plsc

F32 = jnp.float32
BF16 = jnp.bfloat16
I32 = jnp.int32

D_MODEL = 1024
D_RNN = 512
N_RNN_BLOCKS = 8
CONV_W = 4
LRU_C = 8.0
HEAD_DIM = 64
N_HEADS = 8
N_KV_HEADS = 2
GQA = N_HEADS // N_KV_HEADS
D_ATT = N_HEADS * HEAD_DIM
D_KV = N_KV_HEADS * HEAD_DIM
WINDOW = 128
N_EXPERTS = 64
TOP_K = 6
N_GROUPS = 8
GROUP_SIZE = N_EXPERTS // N_GROUPS
TOPK_GROUPS = 4
D_EXPERT = 256
D_SHARED = 256
ROUTE_SCALE = 2.5
EPS = 1e-6
NEG = -1e30
D_IN = 2 * D_RNN + D_ATT + 2 * D_KV

SUBLANES = 8
LANES = 128
TM_PROMPT = 256
ATT_BLOCK = WINDOW
ATT_STEP_BLOCKS = 4
EXPERT_BLOCK = 512
SAMPLE_CHUNK = 8
POS_TILE = 2048
VMEM_LIMIT = 48 * 1024 * 1024

SC_CORES = 2
SC_SUBCORES = 16
SC_WORKERS = SC_CORES * SC_SUBCORES
SC_LANES = 16
DISPATCH_CHUNK = 64
COMBINE_CHUNK = 8


def _sigmoid(x):
    return 1.0 / (1.0 + jnp.exp(-x))


def _silu(x):
    return x * _sigmoid(x)


def _gelu_tanh(x):
    c = 0.7978845608028654
    return x * (0.5 * (1.0 + jnp.tanh(c * (x + 0.044715 * (x * x * x)))))


def _log1p(x):
    u = 1.0 + x
    return jnp.where(u == 1.0, x, jnp.log(u) * x / jnp.where(u == 1.0, 1.0, u - 1.0))


def _neg_expm1(x):
    return -jnp.tanh(0.5 * x) * (jnp.exp(x) + 1.0)


def _softplus(z):
    return jnp.maximum(z, 0.0) + _log1p(jnp.exp(-jnp.abs(z)))


def _div_pow2(x, d):
    assert d & (d - 1) == 0
    return lax.shift_right_logical(x, d.bit_length() - 1)


def _mod_pow2(x, d):
    assert d & (d - 1) == 0
    return x & (d - 1)


def _norm_mod(x, g, sc, sh):
    ms = jnp.mean(x * x, axis=-1, keepdims=True)
    return (x * lax.rsqrt(ms + EPS)) * g * (1.0 + sc) + sh


def _bdot(a, b):
    return jnp.dot(a.astype(BF16), b.astype(BF16), preferred_element_type=F32)


def _bdot_nt(a, b):
    return lax.dot_general(a.astype(BF16), b.astype(BF16), (((1,), (1,)), ((), ())),
                           preferred_element_type=F32)


def _pack_bf16_pairs(x):
    w = x.shape[1] // 2

    def rne_bits(v):
        b = pltpu.bitcast(v, I32)
        return b + 0x7FFF + (lax.shift_right_logical(b, 16) & 1)

    lo = lax.shift_right_logical(rne_bits(x[:, :w]), 16)
    hi = rne_bits(x[:, w:]) & jnp.int32(-65536)
    return lo | hi


def _unpack_bf16_pairs(p):
    lo = pltpu.bitcast(lax.shift_left(p, 16), F32)
    hi = pltpu.bitcast(p & jnp.int32(-65536), F32)
    return lo.astype(BF16), hi.astype(BF16)


def _knorm(k, gk2):
    lane = lax.broadcasted_iota(I32, k.shape, 1)
    first = lane < HEAD_DIM
    k2 = k * k
    s0 = jnp.sum(jnp.where(first, k2, 0.0), axis=-1, keepdims=True)
    s1 = jnp.sum(jnp.where(first, 0.0, k2), axis=-1, keepdims=True)
    ms = jnp.where(first, s0, s1) * (1.0 / HEAD_DIM)
    return k * lax.rsqrt(ms + EPS) * gk2


def _pair_heads(w, axis):
    shape = w.shape
    split = shape[:axis] + (N_KV_HEADS, GQA, HEAD_DIM) + shape[axis + 1:]
    return jnp.swapaxes(w.reshape(split), axis, axis + 1).reshape(shape)


def _lru_coeffs(u, wg, gb, lam):
    g = _bdot(u, wg) + gb
    r = _sigmoid(g[:, :D_RNN])
    i = _sigmoid(g[:, D_RNN:])
    log_a = (-LRU_C * r) * _softplus(-lam)
    a = jnp.exp(log_a)
    b = jnp.sqrt(_neg_expm1(2.0 * log_a)) * (i * u)
    return a, b


def _adaln_kernel(c_ref, w_ref, b_ref, o_ref):
    o_ref[...] = _bdot(_silu(c_ref[...]), w_ref[...]) + b_ref[...]


def _adaln(c_all, ada_w, ada_b):
    n = c_all.shape[0]
    return pl.pallas_call(
        _adaln_kernel,
        grid=(6,),
        in_specs=[pl.BlockSpec((n, D_MODEL), lambda j: (0, 0)),
                  pl.BlockSpec((D_MODEL, D_MODEL), lambda j: (0, j)),
                  pl.BlockSpec((1, D_MODEL), lambda j: (0, j))],
        out_specs=pl.BlockSpec((n, D_MODEL), lambda j: (0, j)),
        out_shape=jax.ShapeDtypeStruct((n, 6 * D_MODEL), F32),
        compiler_params=pltpu.CompilerParams(dimension_semantics=("arbitrary",), vmem_limit_bytes=VMEM_LIMIT),
        name="adaln",
    )(c_all, ada_w, ada_b.reshape(1, -1))


def _scan_rows(a, b, h_in):
    n, c = a.shape
    groups = n // SUBLANES
    a = a.reshape(groups, SUBLANES, c)
    b = b.reshape(groups, SUBLANES, c)
    sub = lax.broadcasted_iota(I32, a.shape, 1)
    s = 1
    while s < SUBLANES:
        m = sub >= s
        a_sh = jnp.where(m, pltpu.roll(a, s, 1), 1.0)
        b_sh = jnp.where(m, pltpu.roll(b, s, 1), 0.0)
        b = a * b_sh + b
        a = a * a_sh
        s *= 2
    carry = h_in
    hs = []
    for g in range(groups):
        hg = a[g] * carry + b[g]
        hs.append(hg)
        carry = hg[SUBLANES - 1:SUBLANES]
    return jnp.concatenate(hs, axis=0)


def _front_prompt_kernel(x_ref, sc_ref, sh_ref, g_ref, win_ref, cw_ref, cb_ref, wg_ref, gb_ref, lam_ref, gk_ref,
                         prev_ref, h0_ref, r_ref, q_ref, k_ref, v_ref, cs_ref, hs_ref, tail_ref, hc_ref):
    j = pl.program_id(1)
    tm = x_ref.shape[0]

    @pl.when(j == 0)
    def _():
        tail_ref[...] = prev_ref[...]
        hc_ref[...] = h0_ref[...]

    h = _norm_mod(x_ref[...], g_ref[...], sc_ref[...], sh_ref[...])
    proj = jnp.dot(h.astype(BF16), win_ref[...], preferred_element_type=F32)
    xr = proj[:, 0:D_RNN]
    yr = proj[:, D_RNN:2 * D_RNN]
    q_ref[...] = proj[:, 2 * D_RNN:2 * D_RNN + D_ATT]
    k_ref[...] = _knorm(proj[:, 2 * D_RNN + D_ATT:2 * D_RNN + D_ATT + D_KV], gk_ref[...])
    v_ref[...] = proj[:, 2 * D_RNN + D_ATT + D_KV:D_IN]

    tail = tail_ref[...]
    row8 = lax.broadcasted_iota(I32, tail.shape, 0)

    def shifted(s):
        rolled = pltpu.roll(xr, s, 0)
        top = jnp.where(row8 < s, pltpu.roll(tail, s, 0), rolled[0:SUBLANES])
        return jnp.concatenate([top, rolled[SUBLANES:]], axis=0)

    cw = cw_ref[...]
    u = cb_ref[...] + shifted(3) * cw[0:1]
    u = u + shifted(2) * cw[1:2]
    u = u + shifted(1) * cw[2:3]
    u = u + xr * cw[3:4]
    tail_ref[...] = xr[tm - SUBLANES:tm]

    a, b = _lru_coeffs(u, wg_ref[...], gb_ref[...], lam_ref[...])
    hs = _scan_rows(a, b, hc_ref[...])
    hc_ref[...] = hs[tm - 1:tm]
    r_ref[...] = hs * _gelu_tanh(yr)

    @pl.when(j == pl.num_programs(1) - 1)
    def _():
        cs_ref[...] = xr[tm - SUBLANES:tm]
        hs_ref[...] = hs[tm - 1:tm]


def _front_prompt(x, sc, sh, g, win, cw, cb, wg, gb, lam, gk, prev, h0):
    bsz, t, _ = x.shape
    tm = TM_PROMPT
    full = lambda a: pl.BlockSpec(a.shape, lambda b, j: (0,) * a.ndim)
    per_b = lambda a: pl.BlockSpec((None,) + a.shape[1:], lambda b, j: (b,) + (0,) * (a.ndim - 1))
    tile = lambda w: pl.BlockSpec((None, tm, w), lambda b, j: (b, j, 0))
    return pl.pallas_call(
        _front_prompt_kernel,
        grid=(bsz, t // tm),
        in_specs=[tile(D_MODEL), per_b(sc), per_b(sh), full(g), full(win), full(cw), full(cb), full(wg), full(gb),
                  full(lam), full(gk), per_b(prev), per_b(h0)],
        out_specs=[tile(D_RNN), tile(D_ATT), tile(D_KV), tile(D_KV),
                   pl.BlockSpec((None, SUBLANES, D_RNN), lambda b, j: (b, 0, 0)),
                   pl.BlockSpec((None, 1, D_RNN), lambda b, j: (b, 0, 0))],
        out_shape=[jax.ShapeDtypeStruct((bsz, t, D_RNN), F32), jax.ShapeDtypeStruct((bsz, t, D_ATT), F32),
                   jax.ShapeDtypeStruct((bsz, t, D_KV), F32), jax.ShapeDtypeStruct((bsz, t, D_KV), F32),
                   jax.ShapeDtypeStruct((bsz, SUBLANES, D_RNN), F32), jax.ShapeDtypeStruct((bsz, 1, D_RNN), F32)],
        scratch_shapes=[pltpu.VMEM((SUBLANES, D_RNN), F32), pltpu.VMEM((1, D_RNN), F32)],
        compiler_params=pltpu.CompilerParams(dimension_semantics=("arbitrary", "arbitrary"),
                                             vmem_limit_bytes=VMEM_LIMIT),
        name="front_prompt",
    )(x, sc, sh, g, win, cw, cb, wg, gb, lam, gk, prev, h0)


def _front_sample_kernel(x_ref, sc_ref, sh_ref, g_ref, win_ref, cw_ref, cb_ref, wg_ref, gb_ref, lam_ref, gk_ref,
                         prev_ref, h0_ref, r_ref, q_ref, k_ref, v_ref, cs_ref, hs_ref):
    t_len, bsz, _ = x_ref.shape
    x = x_ref[...]
    ms = jnp.mean(x * x, axis=-1, keepdims=True)
    h = (x * lax.rsqrt(ms + EPS)) * g_ref[...] * (1.0 + sc_ref[...]) + sh_ref[...]
    proj = jnp.dot(h.reshape(t_len * bsz, D_MODEL).astype(BF16), win_ref[...], preferred_element_type=F32)
    xr = proj[:, 0:D_RNN]
    yr = proj[:, D_RNN:2 * D_RNN]
    q_ref[...] = proj[:, 2 * D_RNN:2 * D_RNN + D_ATT].reshape(t_len, bsz, D_ATT)
    k_ref[...] = _knorm(proj[:, 2 * D_RNN + D_ATT:2 * D_RNN + D_ATT + D_KV], gk_ref[...]).reshape(t_len, bsz, D_KV)
    v_ref[...] = proj[:, 2 * D_RNN + D_ATT + D_KV:D_IN].reshape(t_len, bsz, D_KV)

    def at_time(t):
        if t >= 0:
            return xr[t * bsz:(t + 1) * bsz]
        return prev_ref[CONV_W - 1 + t]

    cw = cw_ref[...]
    us = []
    for t in range(t_len):
        u = cb_ref[...] + at_time(t - 3) * cw[0:1]
        u = u + at_time(t - 2) * cw[1:2]
        u = u + at_time(t - 1) * cw[2:3]
        u = u + at_time(t) * cw[3:4]
        us.append(u)
    a, b = _lru_coeffs(jnp.concatenate(us, axis=0), wg_ref[...], gb_ref[...], lam_ref[...])
    hcur = h0_ref[...]
    for t in range(t_len):
        hcur = a[t * bsz:(t + 1) * bsz] * hcur + b[t * bsz:(t + 1) * bsz]
        r_ref[t] = hcur * _gelu_tanh(yr[t * bsz:(t + 1) * bsz])
    hs_ref[...] = hcur
    for s in range(CONV_W - 1):
        cs_ref[s] = at_time(t_len - (CONV_W - 1) + s)


def _front_sample(x_t, sc, sh, g, win, cw, cb, wg, gb, lam, gk, prev_t, h0):
    t_len, bsz, _ = x_t.shape
    return pl.pallas_call(
        _front_sample_kernel,
        out_shape=[jax.ShapeDtypeStruct((t_len, bsz, D_RNN), F32), jax.ShapeDtypeStruct((t_len, bsz, D_ATT), F32),
                   jax.ShapeDtypeStruct((t_len, bsz, D_KV), F32), jax.ShapeDtypeStruct((t_len, bsz, D_KV), F32),
                   jax.ShapeDtypeStruct((CONV_W - 1, bsz, D_RNN), F32), jax.ShapeDtypeStruct((bsz, D_RNN), F32)],
        compiler_params=pltpu.CompilerParams(vmem_limit_bytes=VMEM_LIMIT),
        name="front_sample",
    )(x_t, sc, sh, g, win, cw, cb, wg, gb, lam, gk, prev_t, h0)


def _qnorm(q, gq):
    ms = jnp.mean(q * q, axis=-1, keepdims=True)
    return q * lax.rsqrt(ms + EPS) * gq


def _attn_prompt_kernel(sink_ref, q_ref, kp_ref, kc_ref, vp_ref, vc_ref, gq_ref, o_ref):
    j = pl.program_id(1)
    blk = kp_ref.shape[0]
    k_all = jnp.concatenate([kp_ref[...], kc_ref[...]], axis=0)
    v_all = jnp.concatenate([vp_ref[...], vc_ref[...]], axis=0)
    qi = lax.broadcasted_iota(I32, (blk, 2 * blk), 0)
    kj = lax.broadcasted_iota(I32, (blk, 2 * blk), 1)
    dist = blk + qi - kj
    window = (dist >= 0) & (dist <= WINDOW)
    distf = dist.astype(F32)
    slab = 2 * HEAD_DIM
    first_q = lax.broadcasted_iota(I32, (blk, slab), 1) < HEAD_DIM
    first_kv = lax.broadcasted_iota(I32, (2 * blk, slab), 1) < HEAD_DIM

    def probs(s, h, valid):
        s = s * (HEAD_DIM ** -0.5) - (2.0 ** -(h + 1)) * distf
        s = jnp.where(valid, s, NEG)
        sink = sink_ref[h]
        m = jnp.maximum(jnp.max(s, axis=-1, keepdims=True), sink)
        p = jnp.exp(s - m)
        return p, jnp.sum(p, axis=-1, keepdims=True) + jnp.exp(sink - m)

    for sub in range(q_ref.shape[0] // blk):
        valid = window if sub > 0 else window & ((kj >= blk) | (j > 0))
        q = q_ref[sub * blk:(sub + 1) * blk, :]
        kk = k_all[sub * blk:(sub + 2) * blk]
        vv = v_all[sub * blk:(sub + 2) * blk]
        v_a = jnp.where(first_kv, vv, 0.0)
        v_b = jnp.where(first_kv, 0.0, vv)
        outs = []
        for i in range(GQA):
            q2 = _knorm(q[:, i * slab:(i + 1) * slab], gq_ref[...])
            p_a, l_a = probs(_bdot_nt(jnp.where(first_q, q2, 0.0), kk), i, valid)
            p_b, l_b = probs(_bdot_nt(jnp.where(first_q, 0.0, q2), kk), GQA + i, valid)
            outs.append((_bdot(p_a, v_a) + _bdot(p_b, v_b)) / jnp.where(first_q, l_a, l_b))
        o_ref[sub * blk:(sub + 1) * blk, :] = jnp.concatenate(outs, axis=1)


def _attn_prompt(q, k, v, gq, sinks):
    bsz, t, _ = q.shape
    blk = ATT_BLOCK
    nsub = ATT_STEP_BLOCKS
    cur = lambda w: pl.BlockSpec((None, nsub * blk, w), lambda b, j: (b, j, 0))
    prv = lambda w: pl.BlockSpec((None, blk, w), lambda b, j: (b, jnp.maximum(nsub * j - 1, 0), 0))
    return pl.pallas_call(
        _attn_prompt_kernel,
        grid=(bsz, t // (nsub * blk)),
        in_specs=[pl.BlockSpec(memory_space=pltpu.SMEM), cur(D_ATT), prv(D_KV), cur(D_KV), prv(D_KV), cur(D_KV),
                  pl.BlockSpec(gq.shape, lambda b, j: (0, 0))],
        out_specs=cur(D_ATT),
        out_shape=jax.ShapeDtypeStruct((bsz, t, D_ATT), F32),
        compiler_params=pltpu.CompilerParams(dimension_semantics=("arbitrary", "arbitrary"),
                                             vmem_limit_bytes=VMEM_LIMIT),
        name="attn_prompt",
    )(sinks, q, k, k, v, v, gq)


def _attn_sample_kernel(sink_ref, q_ref, kn_ref, vn_ref, kc_ref, vc_ref, gq_ref, o_ref):
    t_len, cb, _ = q_ref.shape
    win = kc_ref.shape[1]
    rows = GQA * t_len * cb
    kc = kc_ref[...].reshape(cb * win, D_KV)
    vc = vc_ref[...].reshape(cb * win, D_KV)
    kn = kn_ref[...].reshape(t_len * cb, D_KV)
    vn = vn_ref[...].reshape(t_len * cb, D_KV)

    r_c = lax.broadcasted_iota(I32, (rows, cb * win), 0)
    c_c = lax.broadcasted_iota(I32, (rows, cb * win), 1)
    t_c = _div_pow2(_mod_pow2(r_c, t_len * cb), cb)
    valid_c = (_mod_pow2(r_c, cb) == _div_pow2(c_c, win)) & (_mod_pow2(c_c, win) >= t_c)
    dist_c = (win + t_c - _mod_pow2(c_c, win)).astype(F32)
    r_n = lax.broadcasted_iota(I32, (rows, t_len * cb), 0)
    c_n = lax.broadcasted_iota(I32, (rows, t_len * cb), 1)
    t_n = _div_pow2(_mod_pow2(r_n, t_len * cb), cb)
    valid_n = (_mod_pow2(r_n, cb) == _mod_pow2(c_n, cb)) & (_div_pow2(c_n, cb) <= t_n)
    dist_n = (t_n - _div_pow2(c_n, cb)).astype(F32)
    hl = _div_pow2(lax.broadcasted_iota(I32, (rows, 1), 0), t_len * cb)

    per_group = []
    for g in range(N_KV_HEADS):
        slabs = [q_ref[t][:, (g * GQA + i) * HEAD_DIM:(g * GQA + i + 1) * HEAD_DIM]
                 for i in range(GQA) for t in range(t_len)]
        qg = _qnorm(jnp.concatenate(slabs, axis=0), gq_ref[...])
        slope = jnp.zeros((rows, 1), F32)
        sink = jnp.zeros((rows, 1), F32)
        for i in range(GQA):
            slope = jnp.where(hl == i, 2.0 ** -(g * GQA + i + 1), slope)
            sink = jnp.where(hl == i, sink_ref[g * GQA + i], sink)
        lo, hi = g * HEAD_DIM, (g + 1) * HEAD_DIM
        s_c = _bdot_nt(qg, kc[:, lo:hi]) * (HEAD_DIM ** -0.5) - slope * dist_c
        s_n = _bdot_nt(qg, kn[:, lo:hi]) * (HEAD_DIM ** -0.5) - slope * dist_n
        s_c = jnp.where(valid_c, s_c, NEG)
        s_n = jnp.where(valid_n, s_n, NEG)
        m = jnp.maximum(jnp.maximum(jnp.max(s_c, axis=-1, keepdims=True), jnp.max(s_n, axis=-1, keepdims=True)), sink)
        p_c = jnp.exp(s_c - m)
        p_n = jnp.exp(s_n - m)
        l = jnp.sum(p_c, axis=-1, keepdims=True) + jnp.sum(p_n, axis=-1, keepdims=True) + jnp.exp(sink - m)
        per_group.append((_bdot(p_c, vc[:, lo:hi]) + _bdot(p_n, vn[:, lo:hi])) / l)
    for t in range(t_len):
        o_ref[t] = jnp.concatenate(
            [per_group[g][(i * t_len + t) * cb:(i * t_len + t + 1) * cb] for g in range(N_KV_HEADS) for i in range(GQA)],
            axis=1)


def _attn_sample(q_t, k_t, v_t, cache_k, cache_v, gq, sinks):
    t_len, bsz, _ = q_t.shape
    cb = SAMPLE_CHUNK
    win = cache_k.shape[1]
    new = lambda w: pl.BlockSpec((t_len, cb, w), lambda c: (0, c, 0))
    old = pl.BlockSpec((cb, win, D_KV), lambda c: (c, 0, 0))
    return pl.pallas_call(
        _attn_sample_kernel,
        grid=(bsz // cb,),
        in_specs=[pl.BlockSpec(memory_space=pltpu.SMEM), new(D_ATT), new(D_KV), new(D_KV), old, old,
                  pl.BlockSpec(gq.shape, lambda c: (0, 0))],
        out_specs=new(D_ATT),
        out_shape=jax.ShapeDtypeStruct((t_len, bsz, D_ATT), F32),
        compiler_params=pltpu.CompilerParams(dimension_semantics=("arbitrary",), vmem_limit_bytes=VMEM_LIMIT),
        name="attn_sample",
    )(sinks, q_t, k_t, v_t, cache_k, cache_v, gq)


def _route(s_t, sb_t):
    tm = s_t.shape[1]
    i8 = lax.broadcasted_iota(I32, (GROUP_SIZE, tm), 0)
    ninf = -jnp.inf
    sg = [sb_t[GROUP_SIZE * g:GROUP_SIZE * (g + 1)] for g in range(N_GROUPS)]
    gscore = []
    for g in range(N_GROUPS):
        m1 = jnp.max(sg[g], axis=0, keepdims=True)
        i1 = jnp.min(jnp.where(sg[g] == m1, i8, GROUP_SIZE), axis=0, keepdims=True)
        m2 = jnp.max(jnp.where(i8 == i1, ninf, sg[g]), axis=0, keepdims=True)
        gscore.append(m1 + m2)
    gs = jnp.concatenate(gscore, axis=0)
    gsel = jnp.zeros((N_GROUPS, tm), I32)
    for _ in range(TOPK_GROUPS):
        m = jnp.max(gs, axis=0, keepdims=True)
        idx = jnp.min(jnp.where(gs == m, i8, N_GROUPS), axis=0, keepdims=True)
        hit = i8 == idx
        gsel = jnp.where(hit, 1, gsel)
        gs = jnp.where(hit, ninf, gs)
    sm = [jnp.where(gsel[g:g + 1] > 0, sg[g], NEG) for g in range(N_GROUPS)]
    eid = [i8 + GROUP_SIZE * g for g in range(N_GROUPS)]
    sel = [jnp.zeros((GROUP_SIZE, tm), F32) for _ in range(N_GROUPS)]
    idxs, ws = [], []
    for _ in range(TOP_K):
        cm = functools.reduce(jnp.maximum, sm)
        m = jnp.max(cm, axis=0, keepdims=True)
        cand = functools.reduce(jnp.minimum, [jnp.where(sm[g] == m, eid[g], N_EXPERTS) for g in range(N_GROUPS)])
        idx = jnp.min(cand, axis=0, keepdims=True)
        wk = jnp.zeros((GROUP_SIZE, tm), F32)
        for g in range(N_GROUPS):
            hit = eid[g] == idx
            wk = wk + jnp.where(hit, s_t[GROUP_SIZE * g:GROUP_SIZE * (g + 1)], 0.0)
            sel[g] = jnp.where(hit, 1.0, sel[g])
            sm[g] = jnp.where(hit, ninf, sm[g])
        idxs.append(idx)
        ws.append(jnp.sum(wk, axis=0, keepdims=True))
    return idxs, ws, jnp.concatenate(sel, axis=0), eid


def _post_kernel(x_ref, r_ref, a_ref, g1_ref, sc2_ref, sh2_ref, nf_ref, wot_ref, wob_ref, wrt_ref, rb_ref, cin_ref,
                 x1_ref, h2_ref, eidx_ref, gw_ref, rank_ref, cnt_ref, carry_ref):
    tm = x_ref.shape[0]

    @pl.when(pl.program_id(0) == 0)
    def _():
        carry_ref[...] = cin_ref[...]

    mixed = (jnp.dot(r_ref[...].astype(BF16), wot_ref[...], preferred_element_type=F32)
             + jnp.dot(a_ref[...].astype(BF16), wob_ref[...], preferred_element_type=F32))
    x1 = x_ref[...] + g1_ref[...] * mixed
    x1_ref[...] = x1
    h2 = _norm_mod(x1, nf_ref[...], sc2_ref[...], sh2_ref[...])
    h2_ref[...] = _pack_bf16_pairs(h2)

    wr = wrt_ref[...]
    wr_hi = wr.astype(BF16)
    wr_lo = (wr - wr_hi.astype(F32)).astype(BF16)
    h_hi = h2.astype(BF16)
    h_lo = (h2 - h_hi.astype(F32)).astype(BF16)
    logits = _bdot_nt(wr_hi, h_hi) + _bdot_nt(wr_hi, h_lo) + _bdot_nt(wr_lo, h_hi)
    s_t = _sigmoid(logits)
    idxs, ws, sel, eid = _route(s_t, s_t + rb_ref[...])

    rr = lax.broadcasted_iota(I32, (tm, tm), 0)
    cc = lax.broadcasted_iota(I32, (tm, tm), 1)
    before = jnp.where(rr < cc, 1.0, 0.0).astype(BF16)
    carry = carry_ref[...]
    tot = jnp.dot(sel.astype(BF16), before, preferred_element_type=F32) + carry[:, 0:1]
    ranks = []
    for k in range(TOP_K):
        acc = jnp.zeros((GROUP_SIZE, tm), F32)
        for g in range(N_GROUPS):
            acc = acc + jnp.where(eid[g] == idxs[k], tot[GROUP_SIZE * g:GROUP_SIZE * (g + 1)], 0.0)
        ranks.append(jnp.sum(acc, axis=0, keepdims=True))
    carry = carry + jnp.sum(sel, axis=1, keepdims=True)
    carry_ref[...] = carry
    cnt_ref[...] = carry

    wsum = functools.reduce(lambda p, q: p + q, ws)
    pad_i = jnp.zeros((SUBLANES - TOP_K, tm), I32)
    pad_f = jnp.zeros((SUBLANES - TOP_K, tm), F32)
    eidx_ref[...] = jnp.concatenate(idxs + [pad_i], axis=0)
    rank_ref[...] = jnp.concatenate([r.astype(I32) for r in ranks] + [pad_i], axis=0)
    gw_ref[...] = jnp.concatenate([w / wsum * ROUTE_SCALE for w in ws] + [pad_f], axis=0)


def _post(x, r, a, g1, sc2, sh2, nf, wo_top, wo_bot, wr_t, rb, cnt_in, tm, mod_spec):
    n = x.shape[0]
    tile = lambda w: pl.BlockSpec((tm, w), lambda i: (i, 0))
    full = lambda arr: pl.BlockSpec(arr.shape, lambda i: (0,) * arr.ndim)
    slot = pl.BlockSpec((SUBLANES, tm), lambda i: (0, i))
    return pl.pallas_call(
        _post_kernel,
        grid=(n // tm,),
        in_specs=[tile(D_MODEL), tile(D_RNN), tile(D_ATT), mod_spec, mod_spec, mod_spec, full(nf), full(wo_top),
                  full(wo_bot), full(wr_t), full(rb), full(cnt_in)],
        out_specs=[tile(D_MODEL), tile(D_MODEL // 2), slot, slot, slot, full(cnt_in)],
        out_shape=[jax.ShapeDtypeStruct((n, D_MODEL), F32), jax.ShapeDtypeStruct((n, D_MODEL // 2), I32),
                   jax.ShapeDtypeStruct((SUBLANES, n), I32), jax.ShapeDtypeStruct((SUBLANES, n), F32),
                   jax.ShapeDtypeStruct((SUBLANES, n), I32), jax.ShapeDtypeStruct(cnt_in.shape, F32)],
        scratch_shapes=[pltpu.VMEM(cnt_in.shape, F32)],
        compiler_params=pltpu.CompilerParams(dimension_semantics=("arbitrary",), vmem_limit_bytes=VMEM_LIMIT),
        name="post_mix",
    )(x, r, a, g1, sc2, sh2, nf, wo_top, wo_bot, wr_t, rb, cnt_in)


def _plan_kernel(cnt_ref, ps_ref, meta_ref, *, bm):
    cnt = cnt_ref[...]
    padded = jnp.ceil(cnt * (1.0 / bm)) * bm
    row = lax.broadcasted_iota(I32, cnt.shape, 0)
    pend = padded
    s = 1
    while s < N_EXPERTS:
        pend = pend + jnp.where(row >= s, pltpu.roll(pend, s, 0), 0.0)
        s *= 2
    pstart = pend - padded
    ps_ref[...] = pstart
    nb = meta_ref.shape[1]
    start = lax.broadcasted_iota(I32, (N_EXPERTS, nb), 1).astype(F32) * bm
    blk_e = jnp.minimum(jnp.sum(jnp.where(pend[:, 0:1] <= start, 1.0, 0.0), axis=0, keepdims=True), N_EXPERTS - 1.0)
    e_iota = lax.broadcasted_iota(I32, (N_EXPERTS, nb), 0).astype(F32)
    end_b = jnp.sum(jnp.where(e_iota == blk_e, (pstart + cnt)[:, 0:1], 0.0), axis=0, keepdims=True)
    n_valid = jnp.clip(end_b - start[0:1], 0.0, bm)
    n_used = jnp.broadcast_to(pend[N_EXPERTS - 1:N_EXPERTS, 0:1] * (1.0 / bm), (1, nb))
    meta_ref[...] = jnp.concatenate([blk_e.astype(I32), n_valid.astype(I32), n_used.astype(I32),
                                     jnp.zeros((SUBLANES - 3, nb), I32)], axis=0)


def _plan(cnt, bm, n_blocks):
    assert bm & (bm - 1) == 0
    nb = -(-n_blocks // LANES) * LANES
    return pl.pallas_call(
        functools.partial(_plan_kernel, bm=bm),
        out_shape=[jax.ShapeDtypeStruct(cnt.shape, F32), jax.ShapeDtypeStruct((SUBLANES, nb), I32)],
        name="moe_plan",
    )(cnt)


def _layout_kernel(ps_ref, eidx_ref, rank_ref, gw_ref, *out_refs, chunks):
    *pos_refs, w_ref = out_refs
    tn = eidx_ref.shape[1]
    e_iota = lax.broadcasted_iota(I32, (N_EXPERTS, tn), 0)
    ps = ps_ref[...][:, 0:1]
    rows = []
    for k in range(TOP_K):
        hit = e_iota == eidx_ref[k:k + 1, :]
        base = jnp.sum(jnp.where(hit, ps, 0.0), axis=0, keepdims=True)
        rows.append(base.astype(I32) + rank_ref[k:k + 1, :])
    rows.append(jnp.zeros((SUBLANES - TOP_K, tn), I32))
    pos = jnp.concatenate(rows, axis=0)
    for pos_ref, c in zip(pos_refs, chunks):
        for q in range(tn // c):
            pos_ref[q] = pos[:, q * c:(q + 1) * c]
    gw = gw_ref[...]
    rep = jnp.concatenate([jnp.broadcast_to(gw[k:k + 1], (SC_LANES, tn)) for k in range(SUBLANES)], axis=0)
    w_ref[...] = rep.T


def _layouts(pstart, eidx, rank, gw, chunks):
    n = eidx.shape[1]
    tn = min(n, POS_TILE)
    slot = pl.BlockSpec((SUBLANES, tn), lambda i: (0, i))
    return pl.pallas_call(
        functools.partial(_layout_kernel, chunks=chunks),
        grid=(n // tn,),
        in_specs=[pl.BlockSpec(pstart.shape, lambda i: (0, 0)), slot, slot, slot],
        out_specs=[pl.BlockSpec((tn // c, SUBLANES, c), lambda i: (i, 0, 0)) for c in chunks]
        + [pl.BlockSpec((tn, SUBLANES * SC_LANES), lambda i: (i, 0))],
        out_shape=[jax.ShapeDtypeStruct((n // c, SUBLANES, c), I32) for c in chunks]
        + [jax.ShapeDtypeStruct((n, SUBLANES * SC_LANES), F32)],
        compiler_params=pltpu.CompilerParams(dimension_semantics=("arbitrary",), vmem_limit_bytes=VMEM_LIMIT),
        name="moe_layout",
    )(pstart, eidx, rank, gw)


def _sc_worker_id():
    return lax.axis_index("s") * SC_CORES + lax.axis_index("c")


def _sc_mesh():
    return plsc.VectorSubcoreMesh(core_axis_name="c", subcore_axis_name="s")


def _sc_dispatch(seg_a, seg_b, total_rows):
    (h_a, pos_a, c_a), (h_b, pos_b, c_b) = seg_a, seg_b
    width = h_a.shape[1]

    @functools.partial(
        pl.kernel, mesh=_sc_mesh(), out_type=jax.ShapeDtypeStruct((total_rows, width), I32),
        scratch_types=[pltpu.VMEM((SUBLANES, c_a), I32), pltpu.VMEM((c_a, width), I32),
                       pltpu.VMEM((SUBLANES, c_b), I32), pltpu.VMEM((c_b, width), I32), pltpu.SemaphoreType.DMA])
    def run(ha_hbm, pa_hbm, hb_hbm, pb_hbm, xs_hbm, idx_a, rows_a, idx_b, rows_b, sem):
        wid = _sc_worker_id()

        def segment(h_hbm, p_hbm, idx_v, rows_v, c):
            nch = h_hbm.shape[0] // (SC_WORKERS * c)

            @pl.loop(0, nch)
            def _(ci):
                chunk = wid * nch + ci
                pltpu.sync_copy(p_hbm.at[chunk], idx_v)
                pltpu.sync_copy(h_hbm.at[pl.ds(chunk * c, c)], rows_v)
                copies = [pltpu.async_copy(rows_v, xs_hbm.at[idx_v.at[k]], sem) for k in range(TOP_K)]
                for cp in copies:
                    cp.wait()

        segment(ha_hbm, pa_hbm, idx_a, rows_a, c_a)
        segment(hb_hbm, pb_hbm, idx_b, rows_b, c_b)

    return run(h_a, pos_a, h_b, pos_b)


def _experts_kernel(be_ref, nv_ref, nu_ref, xs_ref, wg_ref, wu_ref, wd_ref, y_ref):
    del be_ref

    @pl.when(pl.program_id(0) < nu_ref[0])
    def _():
        half = xs_ref.shape[1]
        row = lax.broadcasted_iota(I32, xs_ref.shape, 0)
        x_lo, x_hi = _unpack_bf16_pairs(jnp.where(row < nv_ref[pl.program_id(0)], xs_ref[...], 0))
        wg = wg_ref[...].astype(BF16)
        wu = wu_ref[...].astype(BF16)
        a = (jnp.dot(x_lo, wg[:half], preferred_element_type=F32)
             + jnp.dot(x_hi, wg[half:], preferred_element_type=F32))
        b = (jnp.dot(x_lo, wu[:half], preferred_element_type=F32)
             + jnp.dot(x_hi, wu[half:], preferred_element_type=F32))
        y_ref[...] = _bdot(_silu(a) * b, wd_ref[...])


def _experts(blk_e, n_valid, n_used, xs, wg, wu, wd):
    rows = xs.shape[0]
    bm = EXPERT_BLOCK
    last = lambda i, nu: jnp.minimum(i, nu[0] - 1)
    row_blk = lambda i, be, nv, nu: (last(i, nu), 0)
    w_blk = lambda i, be, nv, nu: (be[last(i, nu)], 0, 0)
    grid_spec = pltpu.PrefetchScalarGridSpec(
        num_scalar_prefetch=3,
        grid=(rows // bm,),
        in_specs=[pl.BlockSpec((bm, D_MODEL // 2), row_blk),
                  pl.BlockSpec((None, D_MODEL, D_EXPERT), w_blk),
                  pl.BlockSpec((None, D_MODEL, D_EXPERT), w_blk),
                  pl.BlockSpec((None, D_EXPERT, D_MODEL), w_blk)],
        out_specs=pl.BlockSpec((bm, D_MODEL), row_blk),
    )
    return pl.pallas_call(
        _experts_kernel,
        grid_spec=grid_spec,
        out_shape=jax.ShapeDtypeStruct((rows, D_MODEL), F32),
        compiler_params=pltpu.CompilerParams(dimension_semantics=("arbitrary",), vmem_limit_bytes=VMEM_LIMIT),
        name="moe_experts",
    )(blk_e, n_valid, n_used, xs, wg, wu, wd)


def _sc_combine(y, seg_a, seg_b):
    (pos_a, w_a), (pos_b, w_b) = seg_a, seg_b
    c = COMBINE_CHUNK
    d = y.shape[1]
    n_a, n_b = w_a.shape[0], w_b.shape[0]

    @functools.partial(
        pl.kernel, mesh=_sc_mesh(),
        out_type=[jax.ShapeDtypeStruct((n_a, d), F32), jax.ShapeDtypeStruct((n_b, d), F32)],
        scratch_types=[pltpu.VMEM(pos_a.shape[1:], I32), pltpu.VMEM(pos_b.shape[1:], I32),
                       pltpu.VMEM((2, c, SUBLANES * SC_LANES), F32), pltpu.VMEM((2, TOP_K, c, d), F32),
                       pltpu.VMEM((c, d), F32), pltpu.SemaphoreType.DMA((2,)), pltpu.SemaphoreType.DMA])
    def run(y_hbm, pa_hbm, wa_hbm, pb_hbm, wb_hbm, oa_hbm, ob_hbm, idx_a, idx_b, w_v, buf_v, out_v, sems, out_sem):
        wid = _sc_worker_id()

        def segment(p_hbm, w_hbm, o_hbm, idx_v):
            per_w = idx_v.shape[1]
            nch = per_w // c
            base = wid * per_w
            pltpu.sync_copy(p_hbm.at[wid], idx_v)

            def in_copies(ci, b):
                w_copy = pltpu.make_async_copy(w_hbm.at[pl.ds(base + ci * c, c)], w_v.at[b], sems.at[b])
                return [w_copy] + [
                    pltpu.make_async_copy(y_hbm.at[idx_v.at[k, pl.ds(ci * c, c)]], buf_v.at[b].at[k], sems.at[b])
                    for k in range(TOP_K)]

            def out_copy(ci):
                return pltpu.make_async_copy(out_v, o_hbm.at[pl.ds(base + ci * c, c)], out_sem)

            def reduce_rows(ci, b):
                @pl.when(ci >= 1)
                def _():
                    out_copy(ci - 1).wait()

                @pl.loop(0, c)
                def _(t):
                    ws = [w_v[b, t, pl.ds(k * SC_LANES, SC_LANES)] for k in range(TOP_K)]
                    for j in range(d // SC_LANES):
                        lanes = pl.ds(j * SC_LANES, SC_LANES)
                        acc = buf_v[b, 0, t, lanes] * ws[0]
                        for k in range(1, TOP_K):
                            acc = acc + buf_v[b, k, t, lanes] * ws[k]
                        out_v[t, lanes] = acc

                out_copy(ci).start()

            for cp in in_copies(0, 0):
                cp.start()

            @pl.loop(0, nch, step=2)
            def _(ci):
                for b in range(2):
                    @pl.when(ci + b + 1 < nch)
                    def _():
                        for cp in in_copies(ci + b + 1, 1 - b):
                            cp.start()

                    for cp in in_copies(ci + b, b):
                        cp.wait()
                    reduce_rows(ci + b, b)

            out_copy(nch - 1).wait()

        segment(pa_hbm, wa_hbm, oa_hbm, idx_a)
        segment(pb_hbm, wb_hbm, ob_hbm, idx_b)

    return run(y, pos_a, w_a, pos_b, w_b)


def _shared_kernel(h2_ref, wsg_ref, wsu_ref, wsd_ref, o_ref):
    half = h2_ref.shape[1]
    x_lo, x_hi = _unpack_bf16_pairs(h2_ref[...])
    wsg, wsu = wsg_ref[...], wsu_ref[...]
    a = (jnp.dot(x_lo, wsg[:half], preferred_element_type=F32) + jnp.dot(x_hi, wsg[half:], preferred_element_type=F32))
    b = (jnp.dot(x_lo, wsu[:half], preferred_element_type=F32) + jnp.dot(x_hi, wsu[half:], preferred_element_type=F32))
    o_ref[...] = jnp.dot((_silu(a) * b).astype(BF16), wsd_ref[...], preferred_element_type=F32)


def _shared(h2, wsg, wsu, wsd, tm):
    n = h2.shape[0]
    full = lambda arr: pl.BlockSpec(arr.shape, lambda i: (0,) * arr.ndim)
    return pl.pallas_call(
        _shared_kernel,
        grid=(n // tm,),
        in_specs=[pl.BlockSpec((tm, D_MODEL // 2), lambda i: (i, 0)), full(wsg), full(wsu), full(wsd)],
        out_specs=pl.BlockSpec((tm, D_MODEL), lambda i: (i, 0)),
        out_shape=jax.ShapeDtypeStruct((n, D_MODEL), F32),
        compiler_params=pltpu.CompilerParams(dimension_semantics=("arbitrary",), vmem_limit_bytes=VMEM_LIMIT),
        name="shared_expert",
    )(h2, wsg, wsu, wsd)


def _final_kernel(x1_ref, routed_ref, shared_ref, g2_ref, o_ref):
    o_ref[...] = x1_ref[...] + g2_ref[...] * (routed_ref[...] + shared_ref[...])


def _final(x1, routed, shared, g2, tm, mod_spec):
    n = x1.shape[0]
    tile = pl.BlockSpec((tm, D_MODEL), lambda i: (i, 0))
    return pl.pallas_call(
        _final_kernel,
        grid=(n // tm,),
        in_specs=[tile, tile, tile, mod_spec],
        out_specs=tile,
        out_shape=jax.ShapeDtypeStruct((n, D_MODEL), F32),
        compiler_params=pltpu.CompilerParams(dimension_semantics=("arbitrary",), vmem_limit_bytes=VMEM_LIMIT),
        name="ffn_residual",
    )(x1, routed, shared, g2)


def _block_diag(w):
    nb, bi, bj = w.shape
    return jnp.einsum('nij,nm->nimj', w, jnp.eye(nb, dtype=w.dtype)).reshape(nb * bi, nb * bj)


def kernel(x_prompt, x_sample, c_prompt, c_sample, cache_k_win, cache_v_win, state_conv, state_rnn, ada_w, ada_b, norm_mix, w_in, conv_w, conv_b, gate_a_w, gate_a_b, gate_x_w, gate_x_b, lru_lambda, q_norm, k_norm, attn_sinks, w_out, norm_ffn, router_w, router_bias, exp_w_gate, exp_w_up, exp_w_down, sh_w_gate, sh_w_up, sh_w_down):
    bp, tp, _ = x_prompt.shape
    bs, ts, _ = x_sample.shape
    win = cache_k_win.shape[2]
    n_p, n_s = bp * tp, bs * ts
    row = lambda v: v.reshape(1, -1)

    g_mix, g_ffn = row(norm_mix[0]), row(norm_ffn[0])
    win_bf = w_in[0].astype(BF16)
    q0 = 2 * D_RNN
    win_pair = jnp.concatenate([win_bf[:, :q0], _pair_heads(win_bf[:, q0:q0 + D_ATT], 1), win_bf[:, q0 + D_ATT:]],
                               axis=1)
    cw, cb = conv_w[0], row(conv_b[0])
    wg = jnp.concatenate([_block_diag(gate_a_w[0]), _block_diag(gate_x_w[0])], axis=1).astype(BF16)
    gb = row(jnp.concatenate([gate_a_b[0], gate_x_b[0]]))
    lam = row(lru_lambda[0])
    gq = row(q_norm[0])
    gk2 = row(jnp.tile(k_norm[0], N_KV_HEADS))
    sinks = attn_sinks[0]
    wo_top, wo_bot = w_out[0, :D_RNN].astype(BF16), w_out[0, D_RNN:].astype(BF16)
    wr_t = router_w[0].T
    rb = router_bias[0].reshape(N_EXPERTS, 1)
    wsg, wsu, wsd = sh_w_gate[0].astype(BF16), sh_w_up[0].astype(BF16), sh_w_down[0].astype(BF16)

    mod = _adaln(jnp.concatenate([c_prompt, c_sample], axis=0), ada_w[0], ada_b[0])
    chunks = [mod[:, i * D_MODEL:(i + 1) * D_MODEL] for i in range(6)]
    sh1p, sc1p, g1p, sh2p, sc2p, g2p = [c[:bp].reshape(bp, 1, D_MODEL) for c in chunks]
    sh1s, sc1s, g1s, sh2s, sc2s, g2s = [c[bp:] for c in chunks]

    conv0 = jnp.zeros((bp, SUBLANES, D_RNN), F32)
    h0 = jnp.zeros((bp, 1, D_RNN), F32)
    r_p, q_p, k_p, v_p, cs_p, hs_p = _front_prompt(x_prompt, sc1p, sh1p, g_mix, win_pair, cw, cb, wg, gb, lam, gk2,
                                                   conv0, h0)
    a_p = _attn_prompt(q_p, k_p, v_p, row(jnp.tile(q_norm[0], 2)), sinks)

    x_s_t = jnp.swapaxes(x_sample, 0, 1)
    r_s, q_s, k_s, v_s, cs_s, hs_s = _front_sample(x_s_t, sc1s, sh1s, g_mix, win_bf, cw, cb, wg, gb, lam, gk2,
                                                   jnp.swapaxes(state_conv[0], 0, 1), state_rnn[0])
    cache_k = cache_k_win[0].reshape(bs, win, D_KV)
    cache_v = cache_v_win[0].reshape(bs, win, D_KV)
    a_s = _attn_sample(q_s, k_s, v_s, cache_k, cache_v, gq, sinks)

    tiles_per_seq = tp // TM_PROMPT
    mod_p = pl.BlockSpec((None, 1, D_MODEL), lambda i, *_: (i // tiles_per_seq, 0, 0))
    mod_s = pl.BlockSpec((bs, D_MODEL), lambda i, *_: (0, 0))
    cnt0 = jnp.zeros((N_EXPERTS, LANES), F32)
    x1_p, h2_p, eidx_p, gw_p, rank_p, cnt_p = _post(
        x_prompt.reshape(n_p, D_MODEL), r_p.reshape(n_p, D_RNN), a_p.reshape(n_p, D_ATT), g1p, sc2p, sh2p, g_ffn,
        wo_top, _pair_heads(wo_bot, 0), wr_t, rb, cnt0, TM_PROMPT, mod_p)
    x1_s, h2_s, eidx_s, gw_s, rank_s, cnt_all = _post(
        x_s_t.reshape(n_s, D_MODEL), r_s.reshape(n_s, D_RNN), a_s.reshape(n_s, D_ATT), g1s, sc2s, sh2s, g_ffn,
        wo_top, wo_bot, wr_t, rb, cnt_p, bs, mod_s)

    bm = EXPERT_BLOCK
    n_blocks = -(-((n_p + n_s) * TOP_K) // bm) + N_EXPERTS
    pstart, meta = _plan(cnt_all, bm, n_blocks)
    blk_e, n_valid, n_used = meta[0, :n_blocks], meta[1, :n_blocks], meta[2, :1]

    def sc_layouts(eidx, rank, gw, n):
        per_w = n // SC_WORKERS
        c = min(DISPATCH_CHUNK, per_w)
        outs = _layouts(pstart, eidx, rank, gw, (c,) if c == per_w else (c, per_w))
        return (outs[0], c), (outs[-2], outs[-1])

    (dpos_p, c_p), comb_p = sc_layouts(eidx_p, rank_p, gw_p, n_p)
    (dpos_s, c_s), comb_s = sc_layouts(eidx_s, rank_s, gw_s, n_s)
    xs = _sc_dispatch((h2_p, dpos_p, c_p), (h2_s, dpos_s, c_s), n_blocks * bm)
    y = _experts(blk_e, n_valid, n_used, xs, exp_w_gate[0], exp_w_up[0], exp_w_down[0])
    routed_p, routed_s = _sc_combine(y, comb_p, comb_s)
    y_p = _final(x1_p, routed_p, _shared(h2_p, wsg, wsu, wsd, TM_PROMPT), g2p, TM_PROMPT, mod_p)
    y_s = _final(x1_s, routed_s, _shared(h2_s, wsg, wsu, wsd, bs), g2s, bs, mod_s)

    y_prompt = y_p.reshape(bp, tp, D_MODEL)
    y_sample = jnp.swapaxes(y_s.reshape(ts, bs, D_MODEL), 0, 1)
    wk = min(WINDOW, tp)
    k_win_p = k_p[:, tp - wk:].reshape(1, bp, wk, N_KV_HEADS, HEAD_DIM)
    v_win_p = v_p[:, tp - wk:].reshape(1, bp, wk, N_KV_HEADS, HEAD_DIM)
    k_new = jnp.swapaxes(k_s, 0, 1)
    v_new = jnp.swapaxes(v_s, 0, 1)
    k_win_s = jnp.concatenate([cache_k, k_new], axis=1)[:, ts:].reshape(1, bs, win, N_KV_HEADS, HEAD_DIM)
    v_win_s = jnp.concatenate([cache_v, v_new], axis=1)[:, ts:].reshape(1, bs, win, N_KV_HEADS, HEAD_DIM)
    return (y_prompt, y_sample, k_win_p, v_win_p, cs_p[None, :, SUBLANES - (CONV_W - 1):], hs_p.reshape(1, bp, D_RNN),
            k_win_s, v_win_s, jnp.swapaxes(cs_s, 0, 1)[None], hs_s[None])
```

```python
import functools

import jax
import jax.numpy as jnp
from jax import lax
from jax.experimental import pallas as pl
from jax.experimental.pallas import tpu as pltpu
from jax.experimental.pallas import tpu_sc as plsc

F32 = jnp.float32
BF16 = jnp.bfloat16
I32 = jnp.int32

D_MODEL = 1024
D_RNN = 512
N_RNN_BLOCKS = 8
CONV_W = 4
LRU_C = 8.0
HEAD_DIM = 64
N_HEADS = 8
N_KV_HEADS = 2
GQA = N_HEADS // N_KV_HEADS
D_ATT = N_HEADS * HEAD_DIM
D_KV = N_KV_HEADS * HEAD_DIM
WINDOW = 128
N_EXPERTS = 64
TOP_K = 6
N_GROUPS = 8
GROUP_SIZE = N_EXPERTS // N_GROUPS
TOPK_GROUPS = 4
D_EXPERT = 256
D_SHARED = 256
ROUTE_SCALE = 2.5
EPS = 1e-6
NEG = -1e30
D_IN = 2 * D_RNN + D_ATT + 2 * D_KV

SUBLANES = 8
LANES = 128
TM_PROMPT = 256
TM_POST = 512
ATT_BLOCK = WINDOW
ATT_STEP_BLOCKS = 4
EXPERT_BLOCK = 512
SAMPLE_CHUNK = 8
POS_TILE = 2048
VMEM_LIMIT = 48 * 1024 * 1024

SC_CORES = 2
SC_SUBCORES = 16
SC_WORKERS = SC_CORES * SC_SUBCORES
SC_LANES = 16
DISPATCH_CHUNK = 64
COMBINE_CHUNK = 8


def _sigmoid(x):
    return 1.0 / (1.0 + jnp.exp(-x))


def _silu(x):
    return x * _sigmoid(x)


def _gelu_tanh(x):
    c = 0.7978845608028654
    return x * (0.5 * (1.0 + jnp.tanh(c * (x + 0.044715 * (x * x * x)))))


def _log1p(x):
    u = 1.0 + x
    return jnp.where(u == 1.0, x, jnp.log(u) * x / jnp.where(u == 1.0, 1.0, u - 1.0))


def _neg_expm1(x):
    return -jnp.tanh(0.5 * x) * (jnp.exp(x) + 1.0)


def _softplus(z):
    return jnp.maximum(z, 0.0) + _log1p(jnp.exp(-jnp.abs(z)))


def _div_pow2(x, d):
    assert d & (d - 1) == 0
    return lax.shift_right_logical(x, d.bit_length() - 1)


def _mod_pow2(x, d):
    assert d & (d - 1) == 0
    return x & (d - 1)


def _norm_mod(x, g, sc, sh):
    ms = jnp.mean(x * x, axis=-1, keepdims=True)
    return (x * lax.rsqrt(ms + EPS)) * g * (1.0 + sc) + sh


def _bdot(a, b):
    return jnp.dot(a.astype(BF16), b.astype(BF16), preferred_element_type=F32)


def _bdot_nt(a, b):
    return lax.dot_general(a.astype(BF16), b.astype(BF16), (((1,), (1,)), ((), ())),
                           preferred_element_type=F32)


def _pack_bf16_pairs(x):
    w = x.shape[1] // 2

    def rne_bits(v):
        b = pltpu.bitcast(v, I32)
        return b + 0x7FFF + (lax.shift_right_logical(b, 16) & 1)

    lo = lax.shift_right_logical(rne_bits(x[:, :w]), 16)
    hi = rne_bits(x[:, w:]) & jnp.int32(-65536)
    return lo | hi


def _unpack_bf16_pairs(p):
    lo = pltpu.bitcast(lax.shift_left(p, 16), F32)
    hi = pltpu.bitcast(p & jnp.int32(-65536), F32)
    return lo.astype(BF16), hi.astype(BF16)


def _knorm(k, gk2):
    lane = lax.broadcasted_iota(I32, k.shape, 1)
    first = lane < HEAD_DIM
    k2 = k * k
    s0 = jnp.sum(jnp.where(first, k2, 0.0), axis=-1, keepdims=True)
    s1 = jnp.sum(jnp.where(first, 0.0, k2), axis=-1, keepdims=True)
    ms = jnp.where(first, s0, s1) * (1.0 / HEAD_DIM)
    return k * lax.rsqrt(ms + EPS) * gk2


def _pair_heads(w, axis):
    shape = w.shape
    split = shape[:axis] + (N_KV_HEADS, GQA, HEAD_DIM) + shape[axis + 1:]
    return jnp.swapaxes(w.reshape(split), axis, axis + 1).reshape(shape)


def _lru_coeffs(u, wg, gb, lam):
    g = _bdot(u, wg) + gb
    r = _sigmoid(g[:, :D_RNN])
    i = _sigmoid(g[:, D_RNN:])
    log_a = (-LRU_C * r) * _softplus(-lam)
    a = jnp.exp(log_a)
    b = jnp.sqrt(_neg_expm1(2.0 * log_a)) * (i * u)
    return a, b


def _adaln_kernel(c_ref, w_ref, b_ref, o_ref):
    o_ref[...] = _bdot(_silu(c_ref[...]), w_ref[...]) + b_ref[...]


def _adaln(c_all, ada_w, ada_b):
    n = c_all.shape[0]
    return pl.pallas_call(
        _adaln_kernel,
        grid=(6,),
        in_specs=[pl.BlockSpec((n, D_MODEL), lambda j: (0, 0)),
                  pl.BlockSpec((D_MODEL, D_MODEL), lambda j: (0, j)),
                  pl.BlockSpec((1, D_MODEL), lambda j: (0, j))],
        out_specs=pl.BlockSpec((n, D_MODEL), lambda j: (0, j)),
        out_shape=jax.ShapeDtypeStruct((n, 6 * D_MODEL), F32),
        compiler_params=pltpu.CompilerParams(dimension_semantics=("arbitrary",), vmem_limit_bytes=VMEM_LIMIT),
        name="adaln",
    )(c_all, ada_w, ada_b.reshape(1, -1))


def _scan_rows(a, b, h_in):
    n, c = a.shape
    groups = n // SUBLANES
    a = a.reshape(groups, SUBLANES, c)
    b = b.reshape(groups, SUBLANES, c)
    sub = lax.broadcasted_iota(I32, a.shape, 1)
    s = 1
    while s < SUBLANES:
        m = sub >= s
        a_sh = jnp.where(m, pltpu.roll(a, s, 1), 1.0)
        b_sh = jnp.where(m, pltpu.roll(b, s, 1), 0.0)
        b = a * b_sh + b
        a = a * a_sh
        s *= 2
    carry = h_in
    hs = []
    for g in range(groups):
        hg = a[g] * carry + b[g]
        hs.append(hg)
        carry = hg[SUBLANES - 1:SUBLANES]
    return jnp.concatenate(hs, axis=0)


def _front_prompt_kernel(x_ref, sc_ref, sh_ref, g_ref, win_ref, cw_ref, cb_ref, wg_ref, gb_ref, lam_ref, gk_ref,
                         prev_ref, h0_ref, r_ref, q_ref, k_ref, v_ref, cs_ref, hs_ref, tail_ref, hc_ref):
    j = pl.program_id(1)
    tm = x_ref.shape[0]

    @pl.when(j == 0)
    def _():
        tail_ref[...] = prev_ref[...]
        hc_ref[...] = h0_ref[...]

    h = _norm_mod(x_ref[...], g_ref[...], sc_ref[...], sh_ref[...])
    proj = jnp.dot(h.astype(BF16), win_ref[...], preferred_element_type=F32)
    xr = proj[:, 0:D_RNN]
    yr = proj[:, D_RNN:2 * D_RNN]
    q_ref[...] = proj[:, 2 * D_RNN:2 * D_RNN + D_ATT]
    k_ref[...] = _knorm(proj[:, 2 * D_RNN + D_ATT:2 * D_RNN + D_ATT + D_KV], gk_ref[...])
    v_ref[...] = proj[:, 2 * D_RNN + D_ATT + D_KV:D_IN]

    tail = tail_ref[...]
    row8 = lax.broadcasted_iota(I32, tail.shape, 0)

    def shifted(s):
        rolled = pltpu.roll(xr, s, 0)
        top = jnp.where(row8 < s, pltpu.roll(tail, s, 0), rolled[0:SUBLANES])
        return jnp.concatenate([top, rolled[SUBLANES:]], axis=0)

    cw = cw_ref[...]
    u = cb_ref[...] + shifted(3) * cw[0:1]
    u = u + shifted(2) * cw[1:2]
    u = u + shifted(1) * cw[2:3]
    u = u + xr * cw[3:4]
    tail_ref[...] = xr[tm - SUBLANES:tm]

    a, b = _lru_coeffs(u, wg_ref[...], gb_ref[...], lam_ref[...])
    hs = _scan_rows(a, b, hc_ref[...])
    hc_ref[...] = hs[tm - 1:tm]
    r_ref[...] = hs * _gelu_tanh(yr)

    @pl.when(j == pl.num_programs(1) - 1)
    def _():
        cs_ref[...] = xr[tm - SUBLANES:tm]
        hs_ref[...] = hs[tm - 1:tm]


def _front_prompt(x, sc, sh, g, win, cw, cb, wg, gb, lam, gk, prev, h0):
    bsz, t, _ = x.shape
    tm = TM_PROMPT
    full = lambda a: pl.BlockSpec(a.shape, lambda b, j: (0,) * a.ndim)
    per_b = lambda a: pl.BlockSpec((None,) + a.shape[1:], lambda b, j: (b,) + (0,) * (a.ndim - 1))
    tile = lambda w: pl.BlockSpec((None, tm, w), lambda b, j: (b, j, 0))
    return pl.pallas_call(
        _front_prompt_kernel,
        grid=(bsz, t // tm),
        in_specs=[tile(D_MODEL), per_b(sc), per_b(sh), full(g), full(win), full(cw), full(cb), full(wg), full(gb),
                  full(lam), full(gk), per_b(prev), per_b(h0)],
        out_specs=[tile(D_RNN), tile(D_ATT), tile(D_KV), tile(D_KV),
                   pl.BlockSpec((None, SUBLANES, D_RNN), lambda b, j: (b, 0, 0)),
                   pl.BlockSpec((None, 1, D_RNN), lambda b, j: (b, 0, 0))],
        out_shape=[jax.ShapeDtypeStruct((bsz, t, D_RNN), F32), jax.ShapeDtypeStruct((bsz, t, D_ATT), F32),
                   jax.ShapeDtypeStruct((bsz, t, D_KV), F32), jax.ShapeDtypeStruct((bsz, t, D_KV), F32),
                   jax.ShapeDtypeStruct((bsz, SUBLANES, D_RNN), F32), jax.ShapeDtypeStruct((bsz, 1, D_RNN), F32)],
        scratch_shapes=[pltpu.VMEM((SUBLANES, D_RNN), F32), pltpu.VMEM((1, D_RNN), F32)],
        compiler_params=pltpu.CompilerParams(dimension_semantics=("arbitrary", "arbitrary"),
                                             vmem_limit_bytes=VMEM_LIMIT),
        name="front_prompt",
    )(x, sc, sh, g, win, cw, cb, wg, gb, lam, gk, prev, h0)


def _front_sample_kernel(x_ref, sc_ref, sh_ref, g_ref, win_ref, cw_ref, cb_ref, wg_ref, gb_ref, lam_ref, gk_ref,
                         prev_ref, h0_ref, r_ref, q_ref, k_ref, v_ref, cs_ref, hs_ref):
    t_len, bsz, _ = x_ref.shape
    x = x_ref[...]
    ms = jnp.mean(x * x, axis=-1, keepdims=True)
    h = (x * lax.rsqrt(ms + EPS)) * g_ref[...] * (1.0 + sc_ref[...]) + sh_ref[...]
    proj = jnp.dot(h.reshape(t_len * bsz, D_MODEL).astype(BF16), win_ref[...], preferred_element_type=F32)
    xr = proj[:, 0:D_RNN]
    yr = proj[:, D_RNN:2 * D_RNN]
    q_ref[...] = proj[:, 2 * D_RNN:2 * D_RNN + D_ATT].reshape(t_len, bsz, D_ATT)
    k_ref[...] = _knorm(proj[:, 2 * D_RNN + D_ATT:2 * D_RNN + D_ATT + D_KV], gk_ref[...]).reshape(t_len, bsz, D_KV)
    v_ref[...] = proj[:, 2 * D_RNN + D_ATT + D_KV:D_IN].reshape(t_len, bsz, D_KV)

    def at_time(t):
        if t >= 0:
            return xr[t * bsz:(t + 1) * bsz]
        return prev_ref[CONV_W - 1 + t]

    cw = cw_ref[...]
    us = []
    for t in range(t_len):
        u = cb_ref[...] + at_time(t - 3) * cw[0:1]
        u = u + at_time(t - 2) * cw[1:2]
        u = u + at_time(t - 1) * cw[2:3]
        u = u + at_time(t) * cw[3:4]
        us.append(u)
    a, b = _lru_coeffs(jnp.concatenate(us, axis=0), wg_ref[...], gb_ref[...], lam_ref[...])
    hcur = h0_ref[...]
    for t in range(t_len):
        hcur = a[t * bsz:(t + 1) * bsz] * hcur + b[t * bsz:(t + 1) * bsz]
        r_ref[t] = hcur * _gelu_tanh(yr[t * bsz:(t + 1) * bsz])
    hs_ref[...] = hcur
    for s in range(CONV_W - 1):
        cs_ref[s] = at_time(t_len - (CONV_W - 1) + s)


def _front_sample(x_t, sc, sh, g, win, cw, cb, wg, gb, lam, gk, prev_t, h0):
    t_len, bsz, _ = x_t.shape
    return pl.pallas_call(
        _front_sample_kernel,
        out_shape=[jax.ShapeDtypeStruct((t_len, bsz, D_RNN), F32), jax.ShapeDtypeStruct((t_len, bsz, D_ATT), F32),
                   jax.ShapeDtypeStruct((t_len, bsz, D_KV), F32), jax.ShapeDtypeStruct((t_len, bsz, D_KV), F32),
                   jax.ShapeDtypeStruct((CONV_W - 1, bsz, D_RNN), F32), jax.ShapeDtypeStruct((bsz, D_RNN), F32)],
        compiler_params=pltpu.CompilerParams(vmem_limit_bytes=VMEM_LIMIT),
        name="front_sample",
    )(x_t, sc, sh, g, win, cw, cb, wg, gb, lam, gk, prev_t, h0)


def _qnorm(q, gq):
    ms = jnp.mean(q * q, axis=-1, keepdims=True)
    return q * lax.rsqrt(ms + EPS) * gq


def _attn_prompt_kernel(sink_ref, q_ref, kp_ref, kc_ref, vp_ref, vc_ref, gq_ref, o_ref):
    j = pl.program_id(1)
    blk = kp_ref.shape[0]
    k_all = jnp.concatenate([kp_ref[...], kc_ref[...]], axis=0)
    v_all = jnp.concatenate([vp_ref[...], vc_ref[...]], axis=0)
    qi = lax.broadcasted_iota(I32, (blk, 2 * blk), 0)
    kj = lax.broadcasted_iota(I32, (blk, 2 * blk), 1)
    dist = blk + qi - kj
    window = (dist >= 0) & (dist <= WINDOW)
    distf = dist.astype(F32)
    slab = 2 * HEAD_DIM
    first_q = lax.broadcasted_iota(I32, (blk, slab), 1) < HEAD_DIM
    first_kv = lax.broadcasted_iota(I32, (2 * blk, slab), 1) < HEAD_DIM

    def probs(s, h, valid):
        s = s * (HEAD_DIM ** -0.5) - (2.0 ** -(h + 1)) * distf
        s = jnp.where(valid, s, NEG)
        sink = sink_ref[h]
        m = jnp.maximum(jnp.max(s, axis=-1, keepdims=True), sink)
        p = jnp.exp(s - m)
        return p, jnp.sum(p, axis=-1, keepdims=True) + jnp.exp(sink - m)

    for sub in range(q_ref.shape[0] // blk):
        valid = window if sub > 0 else window & ((kj >= blk) | (j > 0))
        q = q_ref[sub * blk:(sub + 1) * blk, :]
        kk = k_all[sub * blk:(sub + 2) * blk]
        vv = v_all[sub * blk:(sub + 2) * blk]
        v_a = jnp.where(first_kv, vv, 0.0)
        v_b = jnp.where(first_kv, 0.0, vv)
        outs = []
        for i in range(GQA):
            q2 = _knorm(q[:, i * slab:(i + 1) * slab], gq_ref[...])
            p_a, l_a = probs(_bdot_nt(jnp.where(first_q, q2, 0.0), kk), i, valid)
            p_b, l_b = probs(_bdot_nt(jnp.where(first_q, 0.0, q2), kk), GQA + i, valid)
            outs.append((_bdot(p_a, v_a) + _bdot(p_b, v_b)) / jnp.where(first_q, l_a, l_b))
        o_ref[sub * blk:(sub + 1) * blk, :] = jnp.concatenate(outs, axis=1)


def _attn_prompt(q, k, v, gq, sinks):
    bsz, t, _ = q.shape
    blk = ATT_BLOCK
    nsub = ATT_STEP_BLOCKS
    cur = lambda w: pl.BlockSpec((None, nsub * blk, w), lambda b, j: (b, j, 0))
    prv = lambda w: pl.BlockSpec((None, blk, w), lambda b, j: (b, jnp.maximum(nsub * j - 1, 0), 0))
    return pl.pallas_call(
        _attn_prompt_kernel,
        grid=(bsz, t // (nsub * blk)),
        in_specs=[pl.BlockSpec(memory_space=pltpu.SMEM), cur(D_ATT), prv(D_KV), cur(D_KV), prv(D_KV), cur(D_KV),
                  pl.BlockSpec(gq.shape, lambda b, j: (0, 0))],
        out_specs=cur(D_ATT),
        out_shape=jax.ShapeDtypeStruct((bsz, t, D_ATT), F32),
        compiler_params=pltpu.CompilerParams(dimension_semantics=("arbitrary", "arbitrary"),
                                             vmem_limit_bytes=VMEM_LIMIT),
        name="attn_prompt",
    )(sinks, q, k, k, v, v, gq)


def _attn_sample_kernel(sink_ref, q_ref, kn_ref, vn_ref, kc_ref, vc_ref, gq_ref, o_ref):
    t_len, cb, _ = q_ref.shape
    win = kc_ref.shape[1]
    rows = GQA * t_len * cb
    kc = kc_ref[...].reshape(cb * win, D_KV)
    vc = vc_ref[...].reshape(cb * win, D_KV)
    kn = kn_ref[...].reshape(t_len * cb, D_KV)
    vn = vn_ref[...].reshape(t_len * cb, D_KV)

    r_c = lax.broadcasted_iota(I32, (rows, cb * win), 0)
    c_c = lax.broadcasted_iota(I32, (rows, cb * win), 1)
    t_c = _div_pow2(_mod_pow2(r_c, t_len * cb), cb)
    valid_c = (_mod_pow2(r_c, cb) == _div_pow2(c_c, win)) & (_mod_pow2(c_c, win) >= t_c)
    dist_c = (win + t_c - _mod_pow2(c_c, win)).astype(F32)
    r_n = lax.broadcasted_iota(I32, (rows, t_len * cb), 0)
    c_n = lax.broadcasted_iota(I32, (rows, t_len * cb), 1)
    t_n = _div_pow2(_mod_pow2(r_n, t_len * cb), cb)
    valid_n = (_mod_pow2(r_n, cb) == _mod_pow2(c_n, cb)) & (_div_pow2(c_n, cb) <= t_n)
    dist_n = (t_n - _div_pow2(c_n, cb)).astype(F32)
    hl = _div_pow2(lax.broadcasted_iota(I32, (rows, 1), 0), t_len * cb)

    per_group = []
    for g in range(N_KV_HEADS):
        slabs = [q_ref[t][:, (g * GQA + i) * HEAD_DIM:(g * GQA + i + 1) * HEAD_DIM]
                 for i in range(GQA) for t in range(t_len)]
        qg = _qnorm(jnp.concatenate(slabs, axis=0), gq_ref[...])
        slope = jnp.zeros((rows, 1), F32)
        sink = jnp.zeros((rows, 1), F32)
        for i in range(GQA):
            slope = jnp.where(hl == i, 2.0 ** -(g * GQA + i + 1), slope)
            sink = jnp.where(hl == i, sink_ref[g * GQA + i], sink)
        lo, hi = g * HEAD_DIM, (g + 1) * HEAD_DIM
        s_c = _bdot_nt(qg, kc[:, lo:hi]) * (HEAD_DIM ** -0.5) - slope * dist_c
        s_n = _bdot_nt(qg, kn[:, lo:hi]) * (HEAD_DIM ** -0.5) - slope * dist_n
        s_c = jnp.where(valid_c, s_c, NEG)
        s_n = jnp.where(valid_n, s_n, NEG)
        m = jnp.maximum(jnp.maximum(jnp.max(s_c, axis=-1, keepdims=True), jnp.max(s_n, axis=-1, keepdims=True)), sink)
        p_c = jnp.exp(s_c - m)
        p_n = jnp.exp(s_n - m)
        l = jnp.sum(p_c, axis=-1, keepdims=True) + jnp.sum(p_n, axis=-1, keepdims=True) + jnp.exp(sink - m)
        per_group.append((_bdot(p_c, vc[:, lo:hi]) + _bdot(p_n, vn[:, lo:hi])) / l)
    for t in range(t_len):
        o_ref[t] = jnp.concatenate(
            [per_group[g][(i * t_len + t) * cb:(i * t_len + t + 1) * cb] for g in range(N_KV_HEADS) for i in range(GQA)],
            axis=1)


def _attn_sample(q_t, k_t, v_t, cache_k, cache_v, gq, sinks):
    t_len, bsz, _ = q_t.shape
    cb = SAMPLE_CHUNK
    win = cache_k.shape[1]
    new = lambda w: pl.BlockSpec((t_len, cb, w), lambda c: (0, c, 0))
    old = pl.BlockSpec((cb, win, D_KV), lambda c: (c, 0, 0))
    return pl.pallas_call(
        _attn_sample_kernel,
        grid=(bsz // cb,),
        in_specs=[pl.BlockSpec(memory_space=pltpu.SMEM), new(D_ATT), new(D_KV), new(D_KV), old, old,
                  pl.BlockSpec(gq.shape, lambda c: (0, 0))],
        out_specs=new(D_ATT),
        out_shape=jax.ShapeDtypeStruct((t_len, bsz, D_ATT), F32),
        compiler_params=pltpu.CompilerParams(dimension_semantics=("arbitrary",), vmem_limit_bytes=VMEM_LIMIT),
        name="attn_sample",
    )(sinks, q_t, k_t, v_t, cache_k, cache_v, gq)


def _route(s_t, sb_t):
    tm = s_t.shape[1]
    i8 = lax.broadcasted_iota(I32, (GROUP_SIZE, tm), 0)
    ninf = -jnp.inf
    sg = [sb_t[GROUP_SIZE * g:GROUP_SIZE * (g + 1)] for g in range(N_GROUPS)]
    gscore = []
    for g in range(N_GROUPS):
        m1 = jnp.max(sg[g], axis=0, keepdims=True)
        i1 = jnp.min(jnp.where(sg[g] == m1, i8, GROUP_SIZE), axis=0, keepdims=True)
        m2 = jnp.max(jnp.where(i8 == i1, ninf, sg[g]), axis=0, keepdims=True)
        gscore.append(m1 + m2)
    gs = jnp.concatenate(gscore, axis=0)
    gsel = jnp.zeros((N_GROUPS, tm), I32)
    for _ in range(TOPK_GROUPS):
        m = jnp.max(gs, axis=0, keepdims=True)
        idx = jnp.min(jnp.where(gs == m, i8, N_GROUPS), axis=0, keepdims=True)
        hit = i8 == idx
        gsel = jnp.where(hit, 1, gsel)
        gs = jnp.where(hit, ninf, gs)
    sm = [jnp.where(gsel[g:g + 1] > 0, sg[g], NEG) for g in range(N_GROUPS)]
    eid = [i8 + GROUP_SIZE * g for g in range(N_GROUPS)]
    sel = [jnp.zeros((GROUP_SIZE, tm), F32) for _ in range(N_GROUPS)]
    idxs, ws = [], []
    for _ in range(TOP_K):
        cm = functools.reduce(jnp.maximum, sm)
        m = jnp.max(cm, axis=0, keepdims=True)
        cand = functools.reduce(jnp.minimum, [jnp.where(sm[g] == m, eid[g], N_EXPERTS) for g in range(N_GROUPS)])
        idx = jnp.min(cand, axis=0, keepdims=True)
        wk = jnp.zeros((GROUP_SIZE, tm), F32)
        for g in range(N_GROUPS):
            hit = eid[g] == idx
            wk = wk + jnp.where(hit, s_t[GROUP_SIZE * g:GROUP_SIZE * (g + 1)], 0.0)
            sel[g] = jnp.where(hit, 1.0, sel[g])
            sm[g] = jnp.where(hit, ninf, sm[g])
        idxs.append(idx)
        ws.append(jnp.sum(wk, axis=0, keepdims=True))
    return idxs, ws, jnp.concatenate(sel, axis=0), eid


def _post_kernel(x_ref, r_ref, a_ref, g1_ref, sc2_ref, sh2_ref, nf_ref, wot_ref, wob_ref, wrt_ref, rb_ref, cin_ref,
                 x1_ref, h2_ref, eidx_ref, gw_ref, rank_ref, cnt_ref, carry_ref):
    tm = x_ref.shape[0]

    @pl.when(pl.program_id(0) == 0)
    def _():
        carry_ref[...] = cin_ref[...]

    mixed = (jnp.dot(r_ref[...].astype(BF16), wot_ref[...], preferred_element_type=F32)
             + jnp.dot(a_ref[...].astype(BF16), wob_ref[...], preferred_element_type=F32))
    x1 = x_ref[...] + g1_ref[...] * mixed
    x1_ref[...] = x1
    h2 = _norm_mod(x1, nf_ref[...], sc2_ref[...], sh2_ref[...])
    h2_ref[...] = _pack_bf16_pairs(h2)

    wr = wrt_ref[...]
    wr_hi = wr.astype(BF16)
    wr_lo = (wr - wr_hi.astype(F32)).astype(BF16)
    h_hi = h2.astype(BF16)
    h_lo = (h2 - h_hi.astype(F32)).astype(BF16)
    logits = _bdot_nt(wr_hi, h_hi) + _bdot_nt(wr_hi, h_lo) + _bdot_nt(wr_lo, h_hi)
    s_t = _sigmoid(logits)
    idxs, ws, sel, eid = _route(s_t, s_t + rb_ref[...])

    rr = lax.broadcasted_iota(I32, (tm, tm), 0)
    cc = lax.broadcasted_iota(I32, (tm, tm), 1)
    before = jnp.where(rr < cc, 1.0, 0.0).astype(BF16)
    carry = carry_ref[...]
    tot = jnp.dot(sel.astype(BF16), before, preferred_element_type=F32) + carry[:, 0:1]
    ranks = []
    for k in range(TOP_K):
        acc = jnp.zeros((GROUP_SIZE, tm), F32)
        for g in range(N_GROUPS):
            acc = acc + jnp.where(eid[g] == idxs[k], tot[GROUP_SIZE * g:GROUP_SIZE * (g + 1)], 0.0)
        ranks.append(jnp.sum(acc, axis=0, keepdims=True))
    carry = carry + jnp.sum(sel, axis=1, keepdims=True)
    carry_ref[...] = carry
    cnt_ref[...] = carry

    wsum = functools.reduce(lambda p, q: p + q, ws)
    pad_i = jnp.zeros((SUBLANES - TOP_K, tm), I32)
    pad_f = jnp.zeros((SUBLANES - TOP_K, tm), F32)
    eidx_ref[...] = jnp.concatenate(idxs + [pad_i], axis=0)
    rank_ref[...] = jnp.concatenate([r.astype(I32) for r in ranks] + [pad_i], axis=0)
    gw_ref[...] = jnp.concatenate([w / wsum * ROUTE_SCALE for w in ws] + [pad_f], axis=0)


def _post(x, r, a, g1, sc2, sh2, nf, wo_top, wo_bot, wr_t, rb, cnt_in, tm, mod_spec):
    n = x.shape[0]
    tile = lambda w: pl.BlockSpec((tm, w), lambda i: (i, 0))
    full = lambda arr: pl.BlockSpec(arr.shape, lambda i: (0,) * arr.ndim)
    slot = pl.BlockSpec((SUBLANES, tm), lambda i: (0, i))
    return pl.pallas_call(
        _post_kernel,
        grid=(n // tm,),
        in_specs=[tile(D_MODEL), tile(D_RNN), tile(D_ATT), mod_spec, mod_spec, mod_spec, full(nf), full(wo_top),
                  full(wo_bot), full(wr_t), full(rb), full(cnt_in)],
        out_specs=[tile(D_MODEL), tile(D_MODEL // 2), slot, slot, slot, full(cnt_in)],
        out_shape=[jax.ShapeDtypeStruct((n, D_MODEL), F32), jax.ShapeDtypeStruct((n, D_MODEL // 2), I32),
                   jax.ShapeDtypeStruct((SUBLANES, n), I32), jax.ShapeDtypeStruct((SUBLANES, n), F32),
                   jax.ShapeDtypeStruct((SUBLANES, n), I32), jax.ShapeDtypeStruct(cnt_in.shape, F32)],
        scratch_shapes=[pltpu.VMEM(cnt_in.shape, F32)],
        compiler_params=pltpu.CompilerParams(dimension_semantics=("arbitrary",), vmem_limit_bytes=VMEM_LIMIT),
        name="post_mix",
    )(x, r, a, g1, sc2, sh2, nf, wo_top, wo_bot, wr_t, rb, cnt_in)


def _plan_kernel(cnt_ref, ps_ref, meta_ref, *, bm):
    cnt = cnt_ref[...]
    padded = jnp.ceil(cnt * (1.0 / bm)) * bm
    row = lax.broadcasted_iota(I32, cnt.shape, 0)
    pend = padded
    s = 1
    while s < N_EXPERTS:
        pend = pend + jnp.where(row >= s, pltpu.roll(pend, s, 0), 0.0)
        s *= 2
    pstart = pend - padded
    ps_ref[...] = pstart
    nb = meta_ref.shape[1]
    start = lax.broadcasted_iota(I32, (N_EXPERTS, nb), 1).astype(F32) * bm
    blk_e = jnp.minimum(jnp.sum(jnp.where(pend[:, 0:1] <= start, 1.0, 0.0), axis=0, keepdims=True), N_EXPERTS - 1.0)
    e_iota = lax.broadcasted_iota(I32, (N_EXPERTS, nb), 0).astype(F32)
    end_b = jnp.sum(jnp.where(e_iota == blk_e, (pstart + cnt)[:, 0:1], 0.0), axis=0, keepdims=True)
    n_valid = jnp.clip(end_b - start[0:1], 0.0, bm)
    n_used = jnp.broadcast_to(pend[N_EXPERTS - 1:N_EXPERTS, 0:1] * (1.0 / bm), (1, nb))
    meta_ref[...] = jnp.concatenate([blk_e.astype(I32), n_valid.astype(I32), n_used.astype(I32),
                                     jnp.zeros((SUBLANES - 3, nb), I32)], axis=0)


def _plan(cnt, bm, n_blocks):
    assert bm & (bm - 1) == 0
    nb = -(-n_blocks // LANES) * LANES
    return pl.pallas_call(
        functools.partial(_plan_kernel, bm=bm),
        out_shape=[jax.ShapeDtypeStruct(cnt.shape, F32), jax.ShapeDtypeStruct((SUBLANES, nb), I32)],
        name="moe_plan",
    )(cnt)


def _layout_kernel(ps_ref, eidx_ref, rank_ref, gw_ref, *out_refs, chunks):
    *pos_refs, w_ref = out_refs
    tn = eidx_ref.shape[1]
    e_iota = lax.broadcasted_iota(I32, (N_EXPERTS, tn), 0)
    ps = ps_ref[...][:, 0:1]
    rows = []
    for k in range(TOP_K):
        hit = e_iota == eidx_ref[k:k + 1, :]
        base = jnp.sum(jnp.where(hit, ps, 0.0), axis=0, keepdims=True)
        rows.append(base.astype(I32) + rank_ref[k:k + 1, :])
    rows.append(jnp.zeros((SUBLANES - TOP_K, tn), I32))
    pos = jnp.concatenate(rows, axis=0)
    for pos_ref, c in zip(pos_refs, chunks):
        for q in range(tn // c):
            pos_ref[q] = pos[:, q * c:(q + 1) * c]
    gw = gw_ref[...]
    rep = jnp.concatenate([jnp.broadcast_to(gw[k:k + 1], (SC_LANES, tn)) for k in range(SUBLANES)], axis=0)
    w_ref[...] = rep.T


def _layouts(pstart, eidx, rank, gw, chunks):
    n = eidx.shape[1]
    tn = min(n, POS_TILE)
    slot = pl.BlockSpec((SUBLANES, tn), lambda i: (0, i))
    return pl.pallas_call(
        functools.partial(_layout_kernel, chunks=chunks),
        grid=(n // tn,),
        in_specs=[pl.BlockSpec(pstart.shape, lambda i: (0, 0)), slot, slot, slot],
        out_specs=[pl.BlockSpec((tn // c, SUBLANES, c), lambda i: (i, 0, 0)) for c in chunks]
        + [pl.BlockSpec((tn, SUBLANES * SC_LANES), lambda i: (i, 0))],
        out_shape=[jax.ShapeDtypeStruct((n // c, SUBLANES, c), I32) for c in chunks]
        + [jax.ShapeDtypeStruct((n, SUBLANES * SC_LANES), F32)],
        compiler_params=pltpu.CompilerParams(dimension_semantics=("arbitrary",), vmem_limit_bytes=VMEM_LIMIT),
        name="moe_layout",
    )(pstart, eidx, rank, gw)


def _sc_worker_id():
    return lax.axis_index("s") * SC_CORES + lax.axis_index("c")


def _sc_mesh():
    return plsc.VectorSubcoreMesh(core_axis_name="c", subcore_axis_name="s")


def _sc_dispatch(seg_a, seg_b, total_rows):
    (h_a, pos_a, c_a), (h_b, pos_b, c_b) = seg_a, seg_b
    width = h_a.shape[1]

    @functools.partial(
        pl.kernel, mesh=_sc_mesh(), out_type=jax.ShapeDtypeStruct((total_rows, width), I32),
        scratch_types=[pltpu.VMEM((2, SUBLANES, c_a), I32), pltpu.VMEM((2, c_a, width), I32),
                       pltpu.VMEM((2, SUBLANES, c_b), I32), pltpu.VMEM((2, c_b, width), I32),
                       pltpu.SemaphoreType.DMA((2,)), pltpu.SemaphoreType.DMA((2,))])
    def run(ha_hbm, pa_hbm, hb_hbm, pb_hbm, xs_hbm, idx_a, rows_a, idx_b, rows_b, in_sems, out_sems):
        wid = _sc_worker_id()

        def segment(h_hbm, p_hbm, idx_v, rows_v, c):
            nch = h_hbm.shape[0] // (SC_WORKERS * c)
            assert nch == 1 or nch % 2 == 0
            chunk0 = wid * nch

            def loads(ci, b):
                return [pltpu.make_async_copy(p_hbm.at[chunk0 + ci], idx_v.at[b], in_sems.at[b]),
                        pltpu.make_async_copy(h_hbm.at[pl.ds((chunk0 + ci) * c, c)], rows_v.at[b], in_sems.at[b])]

            def scatters(b):
                return [pltpu.make_async_copy(rows_v.at[b], xs_hbm.at[idx_v.at[b].at[k]], out_sems.at[b])
                        for k in range(TOP_K)]

            def start(copies):
                for cp in copies:
                    cp.start()

            def wait(copies):
                for cp in copies:
                    cp.wait()

            start(loads(0, 0))
            if nch == 1:
                wait(loads(0, 0))
                start(scatters(0))
                wait(scatters(0))
                return

            @pl.loop(0, nch, step=2)
            def _(ci):
                for b in range(2):
                    wait(loads(ci + b, b))
                    start(scatters(b))

                    @pl.when(ci + b + 1 < nch)
                    def _():
                        @pl.when(ci + b >= 1)
                        def _():
                            wait(scatters(1 - b))

                        start(loads(ci + b + 1, 1 - b))

            wait(scatters(0))
            wait(scatters(1))

        segment(ha_hbm, pa_hbm, idx_a, rows_a, c_a)
        segment(hb_hbm, pb_hbm, idx_b, rows_b, c_b)

    return run(h_a, pos_a, h_b, pos_b)


def _experts_kernel(be_ref, nv_ref, nu_ref, xs_ref, wg_ref, wu_ref, wd_ref, y_ref):
    del be_ref

    @pl.when(pl.program_id(0) < nu_ref[0])
    def _():
        half = xs_ref.shape[1]
        row = lax.broadcasted_iota(I32, xs_ref.shape, 0)
        x_lo, x_hi = _unpack_bf16_pairs(jnp.where(row < nv_ref[pl.program_id(0)], xs_ref[...], 0))
        wg = wg_ref[...].astype(BF16)
        wu = wu_ref[...].astype(BF16)
        a = (jnp.dot(x_lo, wg[:half], preferred_element_type=F32)
             + jnp.dot(x_hi, wg[half:], preferred_element_type=F32))
        b = (jnp.dot(x_lo, wu[:half], preferred_element_type=F32)
             + jnp.dot(x_hi, wu[half:], preferred_element_type=F32))
        y_ref[...] = _bdot(_silu(a) * b, wd_ref[...])


def _experts(blk_e, n_valid, n_used, xs, wg, wu, wd):
    rows = xs.shape[0]
    bm = EXPERT_BLOCK
    last = lambda i, nu: jnp.minimum(i, nu[0] - 1)
    row_blk = lambda i, be, nv, nu: (last(i, nu), 0)
    w_blk = lambda i, be, nv, nu: (be[last(i, nu)], 0, 0)
    grid_spec = pltpu.PrefetchScalarGridSpec(
        num_scalar_prefetch=3,
        grid=(rows // bm,),
        in_specs=[pl.BlockSpec((bm, D_MODEL // 2), row_blk),
                  pl.BlockSpec((None, D_MODEL, D_EXPERT), w_blk),
                  pl.BlockSpec((None, D_MODEL, D_EXPERT), w_blk),
                  pl.BlockSpec((None, D_EXPERT, D_MODEL), w_blk)],
        out_specs=pl.BlockSpec((bm, D_MODEL), row_blk),
    )
    return pl.pallas_call(
        _experts_kernel,
        grid_spec=grid_spec,
        out_shape=jax.ShapeDtypeStruct((rows, D_MODEL), F32),
        compiler_params=pltpu.CompilerParams(dimension_semantics=("arbitrary",), vmem_limit_bytes=VMEM_LIMIT),
        name="moe_experts",
    )(blk_e, n_valid, n_used, xs, wg, wu, wd)


def _sc_combine(y, seg_a, seg_b):
    (pos_a, w_a), (pos_b, w_b) = seg_a, seg_b
    c = COMBINE_CHUNK
    d = y.shape[1]
    n_a, n_b = w_a.shape[0], w_b.shape[0]

    @functools.partial(
        pl.kernel, mesh=_sc_mesh(),
        out_type=[jax.ShapeDtypeStruct((n_a, d), F32), jax.ShapeDtypeStruct((n_b, d), F32)],
        scratch_types=[pltpu.VMEM(pos_a.shape[1:], I32), pltpu.VMEM(pos_b.shape[1:], I32),
                       pltpu.VMEM((2, c, SUBLANES * SC_LANES), F32), pltpu.VMEM((2, TOP_K, c, d), F32),
                       pltpu.VMEM((c, d), F32), pltpu.SemaphoreType.DMA((2,)), pltpu.SemaphoreType.DMA])
    def run(y_hbm, pa_hbm, wa_hbm, pb_hbm, wb_hbm, oa_hbm, ob_hbm, idx_a, idx_b, w_v, buf_v, out_v, sems, out_sem):
        wid = _sc_worker_id()

        def segment(p_hbm, w_hbm, o_hbm, idx_v):
            per_w = idx_v.shape[1]
            nch = per_w // c
            base = wid * per_w
            pltpu.sync_copy(p_hbm.at[wid], idx_v)

            def in_copies(ci, b):
                w_copy = pltpu.make_async_copy(w_hbm.at[pl.ds(base + ci * c, c)], w_v.at[b], sems.at[b])
                return [w_copy] + [
                    pltpu.make_async_copy(y_hbm.at[idx_v.at[k, pl.ds(ci * c, c)]], buf_v.at[b].at[k], sems.at[b])
                    for k in range(TOP_K)]

            def out_copy(ci):
                return pltpu.make_async_copy(out_v, o_hbm.at[pl.ds(base + ci * c, c)], out_sem)

            def reduce_rows(ci, b):
                @pl.when(ci >= 1)
                def _():
                    out_copy(ci - 1).wait()

                @pl.loop(0, c)
                def _(t):
                    ws = [w_v[b, t, pl.ds(k * SC_LANES, SC_LANES)] for k in range(TOP_K)]
                    for j in range(d // SC_LANES):
                        lanes = pl.ds(j * SC_LANES, SC_LANES)
                        acc = buf_v[b, 0, t, lanes] * ws[0]
                        for k in range(1, TOP_K):
                            acc = acc + buf_v[b, k, t, lanes] * ws[k]
                        out_v[t, lanes] = acc

                out_copy(ci).start()

            for cp in in_copies(0, 0):
                cp.start()

            @pl.loop(0, nch, step=2)
            def _(ci):
                for b in range(2):
                    @pl.when(ci + b + 1 < nch)
                    def _():
                        for cp in in_copies(ci + b + 1, 1 - b):
                            cp.start()

                    for cp in in_copies(ci + b, b):
                        cp.wait()
                    reduce_rows(ci + b, b)

            out_copy(nch - 1).wait()

        segment(pa_hbm, wa_hbm, oa_hbm, idx_a)
        segment(pb_hbm, wb_hbm, ob_hbm, idx_b)

    return run(y, pos_a, w_a, pos_b, w_b)


def _shared_kernel(x1_ref, h2_ref, g2_ref, wsg_ref, wsu_ref, wsd_ref, o_ref):
    half = h2_ref.shape[1]
    x_lo, x_hi = _unpack_bf16_pairs(h2_ref[...])
    wsg, wsu = wsg_ref[...], wsu_ref[...]
    a = (jnp.dot(x_lo, wsg[:half], preferred_element_type=F32) + jnp.dot(x_hi, wsg[half:], preferred_element_type=F32))
    b = (jnp.dot(x_lo, wsu[:half], preferred_element_type=F32) + jnp.dot(x_hi, wsu[half:], preferred_element_type=F32))
    shared = jnp.dot((_silu(a) * b).astype(BF16), wsd_ref[...], preferred_element_type=F32)
    o_ref[...] = x1_ref[...] + g2_ref[...] * shared


def _shared(x1, h2, g2, wsg, wsu, wsd, tm, mod_spec):
    n = h2.shape[0]
    full = lambda arr: pl.BlockSpec(arr.shape, lambda i: (0,) * arr.ndim)
    tile = pl.BlockSpec((tm, D_MODEL), lambda i: (i, 0))
    return pl.pallas_call(
        _shared_kernel,
        grid=(n // tm,),
        in_specs=[tile, pl.BlockSpec((tm, D_MODEL // 2), lambda i: (i, 0)), mod_spec, full(wsg), full(wsu), full(wsd)],
        out_specs=tile,
        out_shape=jax.ShapeDtypeStruct((n, D_MODEL), F32),
        compiler_params=pltpu.CompilerParams(dimension_semantics=("arbitrary",), vmem_limit_bytes=VMEM_LIMIT),
        name="shared_expert",
    )(x1, h2, g2, wsg, wsu, wsd)


def _final_kernel(base_ref, routed_ref, g2_ref, o_ref):
    o_ref[...] = base_ref[...] + g2_ref[...] * routed_ref[...]


def _final(base, routed, g2, tm, mod_spec):
    n = base.shape[0]
    tile = pl.BlockSpec((tm, D_MODEL), lambda i: (i, 0))
    return pl.pallas_call(
        _final_kernel,
        grid=(n // tm,),
        in_specs=[tile, tile, mod_spec],
        out_specs=tile,
        out_shape=jax.ShapeDtypeStruct((n, D_MODEL), F32),
        compiler_params=pltpu.CompilerParams(dimension_semantics=("arbitrary",), vmem_limit_bytes=VMEM_LIMIT),
        name="ffn_residual",
    )(base, routed, g2)


def _block_diag(w):
    nb, bi, bj = w.shape
    return jnp.einsum('nij,nm->nimj', w, jnp.eye(nb, dtype=w.dtype)).reshape(nb * bi, nb * bj)


def kernel(x_prompt, x_sample, c_prompt, c_sample, cache_k_win, cache_v_win, state_conv, state_rnn, ada_w, ada_b, norm_mix, w_in, conv_w, conv_b, gate_a_w, gate_a_b, gate_x_w, gate_x_b, lru_lambda, q_norm, k_norm, attn_sinks, w_out, norm_ffn, router_w, router_bias, exp_w_gate, exp_w_up, exp_w_down, sh_w_gate, sh_w_up, sh_w_down):
    bp, tp, _ = x_prompt.shape
    bs, ts, _ = x_sample.shape
    win = cache_k_win.shape[2]
    n_p, n_s = bp * tp, bs * ts
    row = lambda v: v.reshape(1, -1)

    g_mix, g_ffn = row(norm_mix[0]), row(norm_ffn[0])
    win_bf = w_in[0].astype(BF16)
    q0 = 2 * D_RNN
    win_pair = jnp.concatenate([win_bf[:, :q0], _pair_heads(win_bf[:, q0:q0 + D_ATT], 1), win_bf[:, q0 + D_ATT:]],
                               axis=1)
    cw, cb = conv_w[0], row(conv_b[0])
    wg = jnp.concatenate([_block_diag(gate_a_w[0]), _block_diag(gate_x_w[0])], axis=1).astype(BF16)
    gb = row(jnp.concatenate([gate_a_b[0], gate_x_b[0]]))
    lam = row(lru_lambda[0])
    gq = row(q_norm[0])
    gk2 = row(jnp.tile(k_norm[0], N_KV_HEADS))
    sinks = attn_sinks[0]
    wo_top, wo_bot = w_out[0, :D_RNN].astype(BF16), w_out[0, D_RNN:].astype(BF16)
    wr_t = router_w[0].T
    rb = router_bias[0].reshape(N_EXPERTS, 1)
    wsg, wsu, wsd = sh_w_gate[0].astype(BF16), sh_w_up[0].astype(BF16), sh_w_down[0].astype(BF16)

    mod = _adaln(jnp.concatenate([c_prompt, c_sample], axis=0), ada_w[0], ada_b[0])
    chunks = [mod[:, i * D_MODEL:(i + 1) * D_MODEL] for i in range(6)]
    sh1p, sc1p, g1p, sh2p, sc2p, g2p = [c[:bp].reshape(bp, 1, D_MODEL) for c in chunks]
    sh1s, sc1s, g1s, sh2s, sc2s, g2s = [c[bp:] for c in chunks]

    conv0 = jnp.zeros((bp, SUBLANES, D_RNN), F32)
    h0 = jnp.zeros((bp, 1, D_RNN), F32)
    r_p, q_p, k_p, v_p, cs_p, hs_p = _front_prompt(x_prompt, sc1p, sh1p, g_mix, win_pair, cw, cb, wg, gb, lam, gk2,
                                                   conv0, h0)
    a_p = _attn_prompt(q_p, k_p, v_p, row(jnp.tile(q_norm[0], 2)), sinks)

    x_s_t = jnp.swapaxes(x_sample, 0, 1)
    r_s, q_s, k_s, v_s, cs_s, hs_s = _front_sample(x_s_t, sc1s, sh1s, g_mix, win_bf, cw, cb, wg, gb, lam, gk2,
                                                   jnp.swapaxes(state_conv[0], 0, 1), state_rnn[0])
    cache_k = cache_k_win[0].reshape(bs, win, D_KV)
    cache_v = cache_v_win[0].reshape(bs, win, D_KV)
    a_s = _attn_sample(q_s, k_s, v_s, cache_k, cache_v, gq, sinks)

    tiles_per_seq = tp // TM_POST
    mod_p = pl.BlockSpec((None, 1, D_MODEL), lambda i, *_: (i // tiles_per_seq, 0, 0))
    mod_s = pl.BlockSpec((bs, D_MODEL), lambda i, *_: (0, 0))
    cnt0 = jnp.zeros((N_EXPERTS, LANES), F32)
    x1_p, h2_p, eidx_p, gw_p, rank_p, cnt_p = _post(
        x_prompt.reshape(n_p, D_MODEL), r_p.reshape(n_p, D_RNN), a_p.reshape(n_p, D_ATT), g1p, sc2p, sh2p, g_ffn,
        wo_top, _pair_heads(wo_bot, 0), wr_t, rb, cnt0, TM_POST, mod_p)
    x1_s, h2_s, eidx_s, gw_s, rank_s, cnt_all = _post(
        x_s_t.reshape(n_s, D_MODEL), r_s.reshape(n_s, D_RNN), a_s.reshape(n_s, D_ATT), g1s, sc2s, sh2s, g_ffn,
        wo_top, wo_bot, wr_t, rb, cnt_p, bs, mod_s)

    bm = EXPERT_BLOCK
    n_blocks = -(-((n_p + n_s) * TOP_K) // bm) + N_EXPERTS
    pstart, meta = _plan(cnt_all, bm, n_blocks)
    blk_e, n_valid, n_used = meta[0, :n_blocks], meta[1, :n_blocks], meta[2, :1]

    def sc_layouts(eidx, rank, gw, n):
        per_w = n // SC_WORKERS
        c = min(DISPATCH_CHUNK, per_w)
        outs = _layouts(pstart, eidx, rank, gw, (c,) if c == per_w else (c, per_w))
        return (outs[0], c), (outs[-2], outs[-1])

    (dpos_p, c_p), comb_p = sc_layouts(eidx_p, rank_p, gw_p, n_p)
    (dpos_s, c_s), comb_s = sc_layouts(eidx_s, rank_s, gw_s, n_s)
    xs = _sc_dispatch((h2_p, dpos_p, c_p), (h2_s, dpos_s, c_s), n_blocks * bm)
    y = _experts(blk_e, n_valid, n_used, xs, exp_w_gate[0], exp_w_up[0], exp_w_down[0])
    routed_p, routed_s = _sc_combine(y, comb_p, comb_s)
    y_p = _final(_shared(x1_p, h2_p, g2p, wsg, wsu, wsd, TM_POST, mod_p), routed_p, g2p, TM_POST, mod_p)
    y_s = _final(_shared(x1_s, h2_s, g2s, wsg, wsu, wsd, bs, mod_s), routed_s, g2s, bs, mod_s)

    y_prompt = y_p.reshape(bp, tp, D_MODEL)
    y_sample = jnp.swapaxes(y_s.reshape(ts, bs, D_MODEL), 0, 1)
    wk = min(WINDOW, tp)
    k_win_p = k_p[:, tp - wk:].reshape(1, bp, wk, N_KV_HEADS, HEAD_DIM)
    v_win_p = v_p[:, tp - wk:].reshape(1, bp, wk, N_KV_HEADS, HEAD_DIM)
    k_new = jnp.swapaxes(k_s, 0, 1)
    v_new = jnp.swapaxes(v_s, 0, 1)
    heads = lambda a: a.reshape(bs, ts, N_KV_HEADS, HEAD_DIM)
    k_win_s = jnp.concatenate([cache_k_win[0], heads(k_new)], axis=1)[None, :, ts:]
    v_win_s = jnp.concatenate([cache_v_win[0], heads(v_new)], axis=1)[None, :, ts:]
    return (y_prompt, y_sample, k_win_p, v_win_p, cs_p[None, :, SUBLANES - (CONV_W - 1):], hs_p.reshape(1, bp, D_RNN),
            k_win_s, v_win_s, jnp.swapaxes(cs_s, 0, 1)[None], hs_s[None])
```

```python
import functools

import jax
import jax.numpy as jnp
from jax import lax
from jax.experimental import pallas as pl
from jax.experimental.pallas import tpu as pltpu
from jax.experimental.pallas import tpu_sc as plsc

F32 = jnp.float32
BF16 = jnp.bfloat16
I32 = jnp.int32

D_MODEL = 1024
D_RNN = 512
N_RNN_BLOCKS = 8
CONV_W = 4
LRU_C = 8.0
HEAD_DIM = 64
N_HEADS = 8
N_KV_HEADS = 2
GQA = N_HEADS // N_KV_HEADS
D_ATT = N_HEADS * HEAD_DIM
D_KV = N_KV_HEADS * HEAD_DIM
WINDOW = 128
N_EXPERTS = 64
TOP_K = 6
N_GROUPS = 8
GROUP_SIZE = N_EXPERTS // N_GROUPS
TOPK_GROUPS = 4
D_EXPERT = 256
D_SHARED = 256
ROUTE_SCALE = 2.5
EPS = 1e-6
NEG = -1e30
D_IN = 2 * D_RNN + D_ATT + 2 * D_KV

SUBLANES = 8
LANES = 128
TM_PROMPT = 256
TM_POST = 512
ATT_BLOCK = WINDOW
ATT_STEP_BLOCKS = 4
EXPERT_BLOCK = 256
EXPERT_RING = 4
SAMPLE_CHUNK = 8
POS_TILE = 2048
VMEM_LIMIT = 48 * 1024 * 1024

SC_CORES = 2
SC_SUBCORES = 16
SC_WORKERS = SC_CORES * SC_SUBCORES
SC_LANES = 16
DISPATCH_CHUNK = 64
COMBINE_CHUNK = 8


def _sigmoid(x):
    return 1.0 / (1.0 + jnp.exp(-x))


def _silu(x):
    return x * _sigmoid(x)


def _gelu_tanh(x):
    c = 0.7978845608028654
    return x * (0.5 * (1.0 + jnp.tanh(c * (x + 0.044715 * (x * x * x)))))


def _log1p(x):
    u = 1.0 + x
    return jnp.where(u == 1.0, x, jnp.log(u) * x / jnp.where(u == 1.0, 1.0, u - 1.0))


def _neg_expm1(x):
    return -jnp.tanh(0.5 * x) * (jnp.exp(x) + 1.0)


def _softplus(z):
    return jnp.maximum(z, 0.0) + _log1p(jnp.exp(-jnp.abs(z)))


def _div_pow2(x, d):
    assert d & (d - 1) == 0
    return lax.shift_right_logical(x, d.bit_length() - 1)


def _mod_pow2(x, d):
    assert d & (d - 1) == 0
    return x & (d - 1)


def _norm_mod(x, g, sc, sh):
    ms = jnp.mean(x * x, axis=-1, keepdims=True)
    return (x * lax.rsqrt(ms + EPS)) * g * (1.0 + sc) + sh


def _bdot(a, b):
    return jnp.dot(a.astype(BF16), b.astype(BF16), preferred_element_type=F32)


def _bdot_nt(a, b):
    return lax.dot_general(a.astype(BF16), b.astype(BF16), (((1,), (1,)), ((), ())),
                           preferred_element_type=F32)


def _pack_bf16_pairs(x):
    w = x.shape[1] // 2

    def rne_bits(v):
        b = pltpu.bitcast(v, I32)
        return b + 0x7FFF + (lax.shift_right_logical(b, 16) & 1)

    lo = lax.shift_right_logical(rne_bits(x[:, :w]), 16)
    hi = rne_bits(x[:, w:]) & jnp.int32(-65536)
    return lo | hi


def _unpack_bf16_pairs(p):
    lo = pltpu.bitcast(lax.shift_left(p, 16), F32)
    hi = pltpu.bitcast(p & jnp.int32(-65536), F32)
    return lo.astype(BF16), hi.astype(BF16)


def _knorm(k, gk2):
    lane = lax.broadcasted_iota(I32, k.shape, 1)
    first = lane < HEAD_DIM
    k2 = k * k
    s0 = jnp.sum(jnp.where(first, k2, 0.0), axis=-1, keepdims=True)
    s1 = jnp.sum(jnp.where(first, 0.0, k2), axis=-1, keepdims=True)
    ms = jnp.where(first, s0, s1) * (1.0 / HEAD_DIM)
    return k * lax.rsqrt(ms + EPS) * gk2


def _pair_heads(w, axis):
    shape = w.shape
    split = shape[:axis] + (N_KV_HEADS, GQA, HEAD_DIM) + shape[axis + 1:]
    return jnp.swapaxes(w.reshape(split), axis, axis + 1).reshape(shape)


def _lru_coeffs(u, wg, gb, lam):
    g = _bdot(u, wg) + gb
    r = _sigmoid(g[:, :D_RNN])
    i = _sigmoid(g[:, D_RNN:])
    log_a = (-LRU_C * r) * _softplus(-lam)
    a = jnp.exp(log_a)
    b = jnp.sqrt(_neg_expm1(2.0 * log_a)) * (i * u)
    return a, b


def _adaln_kernel(c_ref, w_ref, b_ref, o_ref):
    o_ref[...] = _bdot(_silu(c_ref[...]), w_ref[...]) + b_ref[...]


def _adaln(c_all, ada_w, ada_b):
    n = c_all.shape[0]
    return pl.pallas_call(
        _adaln_kernel,
        grid=(6,),
        in_specs=[pl.BlockSpec((n, D_MODEL), lambda j: (0, 0)),
                  pl.BlockSpec((D_MODEL, D_MODEL), lambda j: (0, j)),
                  pl.BlockSpec((1, D_MODEL), lambda j: (0, j))],
        out_specs=pl.BlockSpec((n, D_MODEL), lambda j: (0, j)),
        out_shape=jax.ShapeDtypeStruct((n, 6 * D_MODEL), F32),
        compiler_params=pltpu.CompilerParams(dimension_semantics=("arbitrary",), vmem_limit_bytes=VMEM_LIMIT),
        name="adaln",
    )(c_all, ada_w, ada_b.reshape(1, -1))


def _scan_rows(a, b, h_in):
    n, c = a.shape
    groups = n // SUBLANES
    a = a.reshape(groups, SUBLANES, c)
    b = b.reshape(groups, SUBLANES, c)
    sub = lax.broadcasted_iota(I32, a.shape, 1)
    s = 1
    while s < SUBLANES:
        m = sub >= s
        a_sh = jnp.where(m, pltpu.roll(a, s, 1), 1.0)
        b_sh = jnp.where(m, pltpu.roll(b, s, 1), 0.0)
        b = a * b_sh + b
        a = a * a_sh
        s *= 2
    carry = h_in
    hs = []
    for g in range(groups):
        hg = a[g] * carry + b[g]
        hs.append(hg)
        carry = hg[SUBLANES - 1:SUBLANES]
    return jnp.concatenate(hs, axis=0)


def _front_prompt_kernel(x_ref, sc_ref, sh_ref, g_ref, win_ref, cw_ref, cb_ref, wg_ref, gb_ref, lam_ref, gk_ref,
                         prev_ref, h0_ref, r_ref, q_ref, k_ref, v_ref, cs_ref, hs_ref, tail_ref, hc_ref):
    j = pl.program_id(1)
    tm = x_ref.shape[0]

    @pl.when(j == 0)
    def _():
        tail_ref[...] = prev_ref[...]
        hc_ref[...] = h0_ref[...]

    h = _norm_mod(x_ref[...], g_ref[...], sc_ref[...], sh_ref[...])
    proj = jnp.dot(h.astype(BF16), win_ref[...], preferred_element_type=F32)
    xr = proj[:, 0:D_RNN]
    yr = proj[:, D_RNN:2 * D_RNN]
    q_ref[...] = proj[:, 2 * D_RNN:2 * D_RNN + D_ATT]
    k_ref[...] = _knorm(proj[:, 2 * D_RNN + D_ATT:2 * D_RNN + D_ATT + D_KV], gk_ref[...])
    v_ref[...] = proj[:, 2 * D_RNN + D_ATT + D_KV:D_IN]

    tail = tail_ref[...]
    row8 = lax.broadcasted_iota(I32, tail.shape, 0)

    def shifted(s):
        rolled = pltpu.roll(xr, s, 0)
        top = jnp.where(row8 < s, pltpu.roll(tail, s, 0), rolled[0:SUBLANES])
        return jnp.concatenate([top, rolled[SUBLANES:]], axis=0)

    cw = cw_ref[...]
    u = cb_ref[...] + shifted(3) * cw[0:1]
    u = u + shifted(2) * cw[1:2]
    u = u + shifted(1) * cw[2:3]
    u = u + xr * cw[3:4]
    tail_ref[...] = xr[tm - SUBLANES:tm]

    a, b = _lru_coeffs(u, wg_ref[...], gb_ref[...], lam_ref[...])
    hs = _scan_rows(a, b, hc_ref[...])
    hc_ref[...] = hs[tm - 1:tm]
    r_ref[...] = hs * _gelu_tanh(yr)

    @pl.when(j == pl.num_programs(1) - 1)
    def _():
        cs_ref[...] = xr[tm - SUBLANES:tm]
        hs_ref[...] = hs[tm - 1:tm]


def _front_prompt(x, sc, sh, g, win, cw, cb, wg, gb, lam, gk, prev, h0):
    bsz, t, _ = x.shape
    tm = TM_PROMPT
    full = lambda a: pl.BlockSpec(a.shape, lambda b, j: (0,) * a.ndim)
    per_b = lambda a: pl.BlockSpec((None,) + a.shape[1:], lambda b, j: (b,) + (0,) * (a.ndim - 1))
    tile = lambda w: pl.BlockSpec((None, tm, w), lambda b, j: (b, j, 0))
    return pl.pallas_call(
        _front_prompt_kernel,
        grid=(bsz, t // tm),
        in_specs=[tile(D_MODEL), per_b(sc), per_b(sh), full(g), full(win), full(cw), full(cb), full(wg), full(gb),
                  full(lam), full(gk), per_b(prev), per_b(h0)],
        out_specs=[tile(D_RNN), tile(D_ATT), tile(D_KV), tile(D_KV),
                   pl.BlockSpec((None, SUBLANES, D_RNN), lambda b, j: (b, 0, 0)),
                   pl.BlockSpec((None, 1, D_RNN), lambda b, j: (b, 0, 0))],
        out_shape=[jax.ShapeDtypeStruct((bsz, t, D_RNN), F32), jax.ShapeDtypeStruct((bsz, t, D_ATT), F32),
                   jax.ShapeDtypeStruct((bsz, t, D_KV), F32), jax.ShapeDtypeStruct((bsz, t, D_KV), F32),
                   jax.ShapeDtypeStruct((bsz, SUBLANES, D_RNN), F32), jax.ShapeDtypeStruct((bsz, 1, D_RNN), F32)],
        scratch_shapes=[pltpu.VMEM((SUBLANES, D_RNN), F32), pltpu.VMEM((1, D_RNN), F32)],
        compiler_params=pltpu.CompilerParams(dimension_semantics=("arbitrary", "arbitrary"),
                                             vmem_limit_bytes=VMEM_LIMIT),
        name="front_prompt",
    )(x, sc, sh, g, win, cw, cb, wg, gb, lam, gk, prev, h0)


def _front_sample_kernel(x_ref, sc_ref, sh_ref, g_ref, win_ref, cw_ref, cb_ref, wg_ref, gb_ref, lam_ref, gk_ref,
                         prev_ref, h0_ref, r_ref, q_ref, k_ref, v_ref, cs_ref, hs_ref):
    t_len, bsz, _ = x_ref.shape
    x = x_ref[...]
    ms = jnp.mean(x * x, axis=-1, keepdims=True)
    h = (x * lax.rsqrt(ms + EPS)) * g_ref[...] * (1.0 + sc_ref[...]) + sh_ref[...]
    proj = jnp.dot(h.reshape(t_len * bsz, D_MODEL).astype(BF16), win_ref[...], preferred_element_type=F32)
    xr = proj[:, 0:D_RNN]
    yr = proj[:, D_RNN:2 * D_RNN]
    q_ref[...] = proj[:, 2 * D_RNN:2 * D_RNN + D_ATT].reshape(t_len, bsz, D_ATT)
    k_ref[...] = _knorm(proj[:, 2 * D_RNN + D_ATT:2 * D_RNN + D_ATT + D_KV], gk_ref[...]).reshape(t_len, bsz, D_KV)
    v_ref[...] = proj[:, 2 * D_RNN + D_ATT + D_KV:D_IN].reshape(t_len, bsz, D_KV)

    def at_time(t):
        if t >= 0:
            return xr[t * bsz:(t + 1) * bsz]
        return prev_ref[CONV_W - 1 + t]

    cw = cw_ref[...]
    us = []
    for t in range(t_len):
        u = cb_ref[...] + at_time(t - 3) * cw[0:1]
        u = u + at_time(t - 2) * cw[1:2]
        u = u + at_time(t - 1) * cw[2:3]
        u = u + at_time(t) * cw[3:4]
        us.append(u)
    a, b = _lru_coeffs(jnp.concatenate(us, axis=0), wg_ref[...], gb_ref[...], lam_ref[...])
    hcur = h0_ref[...]
    for t in range(t_len):
        hcur = a[t * bsz:(t + 1) * bsz] * hcur + b[t * bsz:(t + 1) * bsz]
        r_ref[t] = hcur * _gelu_tanh(yr[t * bsz:(t + 1) * bsz])
    hs_ref[...] = hcur
    for s in range(CONV_W - 1):
        cs_ref[s] = at_time(t_len - (CONV_W - 1) + s)


def _front_sample(x_t, sc, sh, g, win, cw, cb, wg, gb, lam, gk, prev_t, h0):
    t_len, bsz, _ = x_t.shape
    return pl.pallas_call(
        _front_sample_kernel,
        out_shape=[jax.ShapeDtypeStruct((t_len, bsz, D_RNN), F32), jax.ShapeDtypeStruct((t_len, bsz, D_ATT), F32),
                   jax.ShapeDtypeStruct((t_len, bsz, D_KV), F32), jax.ShapeDtypeStruct((t_len, bsz, D_KV), F32),
                   jax.ShapeDtypeStruct((CONV_W - 1, bsz, D_RNN), F32), jax.ShapeDtypeStruct((bsz, D_RNN), F32)],
        compiler_params=pltpu.CompilerParams(vmem_limit_bytes=VMEM_LIMIT),
        name="front_sample",
    )(x_t, sc, sh, g, win, cw, cb, wg, gb, lam, gk, prev_t, h0)


def _qnorm(q, gq):
    ms = jnp.mean(q * q, axis=-1, keepdims=True)
    return q * lax.rsqrt(ms + EPS) * gq


def _attn_prompt_kernel(sink_ref, q_ref, kp_ref, kc_ref, vp_ref, vc_ref, gq_ref, o_ref):
    j = pl.program_id(1)
    blk = kp_ref.shape[0]
    k_all = jnp.concatenate([kp_ref[...], kc_ref[...]], axis=0)
    v_all = jnp.concatenate([vp_ref[...], vc_ref[...]], axis=0)
    qi = lax.broadcasted_iota(I32, (blk, 2 * blk), 0)
    kj = lax.broadcasted_iota(I32, (blk, 2 * blk), 1)
    dist = blk + qi - kj
    window = (dist >= 0) & (dist <= WINDOW)
    distf = dist.astype(F32)
    slab = 2 * HEAD_DIM
    first_q = lax.broadcasted_iota(I32, (blk, slab), 1) < HEAD_DIM
    first_kv = lax.broadcasted_iota(I32, (2 * blk, slab), 1) < HEAD_DIM

    def probs(s, h, valid):
        s = s * (HEAD_DIM ** -0.5) - (2.0 ** -(h + 1)) * distf
        s = jnp.where(valid, s, NEG)
        sink = sink_ref[h]
        m = jnp.maximum(jnp.max(s, axis=-1, keepdims=True), sink)
        p = jnp.exp(s - m)
        return p, jnp.sum(p, axis=-1, keepdims=True) + jnp.exp(sink - m)

    for sub in range(q_ref.shape[0] // blk):
        valid = window if sub > 0 else window & ((kj >= blk) | (j > 0))
        q = q_ref[sub * blk:(sub + 1) * blk, :]
        kk = k_all[sub * blk:(sub + 2) * blk]
        vv = v_all[sub * blk:(sub + 2) * blk]
        v_a = jnp.where(first_kv, vv, 0.0)
        v_b = jnp.where(first_kv, 0.0, vv)
        outs = []
        for i in range(GQA):
            q2 = _knorm(q[:, i * slab:(i + 1) * slab], gq_ref[...])
            p_a, l_a = probs(_bdot_nt(jnp.where(first_q, q2, 0.0), kk), i, valid)
            p_b, l_b = probs(_bdot_nt(jnp.where(first_q, 0.0, q2), kk), GQA + i, valid)
            outs.append((_bdot(p_a, v_a) + _bdot(p_b, v_b)) / jnp.where(first_q, l_a, l_b))
        o_ref[sub * blk:(sub + 1) * blk, :] = jnp.concatenate(outs, axis=1)


def _attn_prompt(q, k, v, gq, sinks):
    bsz, t, _ = q.shape
    blk = ATT_BLOCK
    nsub = ATT_STEP_BLOCKS
    cur = lambda w: pl.BlockSpec((None, nsub * blk, w), lambda b, j: (b, j, 0))
    prv = lambda w: pl.BlockSpec((None, blk, w), lambda b, j: (b, jnp.maximum(nsub * j - 1, 0), 0))
    return pl.pallas_call(
        _attn_prompt_kernel,
        grid=(bsz, t // (nsub * blk)),
        in_specs=[pl.BlockSpec(memory_space=pltpu.SMEM), cur(D_ATT), prv(D_KV), cur(D_KV), prv(D_KV), cur(D_KV),
                  pl.BlockSpec(gq.shape, lambda b, j: (0, 0))],
        out_specs=cur(D_ATT),
        out_shape=jax.ShapeDtypeStruct((bsz, t, D_ATT), F32),
        compiler_params=pltpu.CompilerParams(dimension_semantics=("arbitrary", "arbitrary"),
                                             vmem_limit_bytes=VMEM_LIMIT),
        name="attn_prompt",
    )(sinks, q, k, k, v, v, gq)


def _attn_sample_kernel(sink_ref, q_ref, kn_ref, vn_ref, kc_ref, vc_ref, gq_ref, o_ref):
    t_len, cb, _ = q_ref.shape
    win = kc_ref.shape[1]
    rows = GQA * t_len * cb
    kc = kc_ref[...].reshape(cb * win, D_KV)
    vc = vc_ref[...].reshape(cb * win, D_KV)
    kn = kn_ref[...].reshape(t_len * cb, D_KV)
    vn = vn_ref[...].reshape(t_len * cb, D_KV)

    r_c = lax.broadcasted_iota(I32, (rows, cb * win), 0)
    c_c = lax.broadcasted_iota(I32, (rows, cb * win), 1)
    t_c = _div_pow2(_mod_pow2(r_c, t_len * cb), cb)
    valid_c = (_mod_pow2(r_c, cb) == _div_pow2(c_c, win)) & (_mod_pow2(c_c, win) >= t_c)
    dist_c = (win + t_c - _mod_pow2(c_c, win)).astype(F32)
    r_n = lax.broadcasted_iota(I32, (rows, t_len * cb), 0)
    c_n = lax.broadcasted_iota(I32, (rows, t_len * cb), 1)
    t_n = _div_pow2(_mod_pow2(r_n, t_len * cb), cb)
    valid_n = (_mod_pow2(r_n, cb) == _mod_pow2(c_n, cb)) & (_div_pow2(c_n, cb) <= t_n)
    dist_n = (t_n - _div_pow2(c_n, cb)).astype(F32)
    hl = _div_pow2(lax.broadcasted_iota(I32, (rows, 1), 0), t_len * cb)

    per_group = []
    for g in range(N_KV_HEADS):
        slabs = [q_ref[t][:, (g * GQA + i) * HEAD_DIM:(g * GQA + i + 1) * HEAD_DIM]
                 for i in range(GQA) for t in range(t_len)]
        qg = _qnorm(jnp.concatenate(slabs, axis=0), gq_ref[...])
        slope = jnp.zeros((rows, 1), F32)
        sink = jnp.zeros((rows, 1), F32)
        for i in range(GQA):
            slope = jnp.where(hl == i, 2.0 ** -(g * GQA + i + 1), slope)
            sink = jnp.where(hl == i, sink_ref[g * GQA + i], sink)
        lo, hi = g * HEAD_DIM, (g + 1) * HEAD_DIM
        s_c = _bdot_nt(qg, kc[:, lo:hi]) * (HEAD_DIM ** -0.5) - slope * dist_c
        s_n = _bdot_nt(qg, kn[:, lo:hi]) * (HEAD_DIM ** -0.5) - slope * dist_n
        s_c = jnp.where(valid_c, s_c, NEG)
        s_n = jnp.where(valid_n, s_n, NEG)
        m = jnp.maximum(jnp.maximum(jnp.max(s_c, axis=-1, keepdims=True), jnp.max(s_n, axis=-1, keepdims=True)), sink)
        p_c = jnp.exp(s_c - m)
        p_n = jnp.exp(s_n - m)
        l = jnp.sum(p_c, axis=-1, keepdims=True) + jnp.sum(p_n, axis=-1, keepdims=True) + jnp.exp(sink - m)
        per_group.append((_bdot(p_c, vc[:, lo:hi]) + _bdot(p_n, vn[:, lo:hi])) / l)
    for t in range(t_len):
        o_ref[t] = jnp.concatenate(
            [per_group[g][(i * t_len + t) * cb:(i * t_len + t + 1) * cb] for g in range(N_KV_HEADS) for i in range(GQA)],
            axis=1)


def _attn_sample(q_t, k_t, v_t, cache_k, cache_v, gq, sinks):
    t_len, bsz, _ = q_t.shape
    cb = SAMPLE_CHUNK
    win = cache_k.shape[1]
    new = lambda w: pl.BlockSpec((t_len, cb, w), lambda c: (0, c, 0))
    old = pl.BlockSpec((cb, win, D_KV), lambda c: (c, 0, 0))
    return pl.pallas_call(
        _attn_sample_kernel,
        grid=(bsz // cb,),
        in_specs=[pl.BlockSpec(memory_space=pltpu.SMEM), new(D_ATT), new(D_KV), new(D_KV), old, old,
                  pl.BlockSpec(gq.shape, lambda c: (0, 0))],
        out_specs=new(D_ATT),
        out_shape=jax.ShapeDtypeStruct((t_len, bsz, D_ATT), F32),
        compiler_params=pltpu.CompilerParams(dimension_semantics=("arbitrary",), vmem_limit_bytes=VMEM_LIMIT),
        name="attn_sample",
    )(sinks, q_t, k_t, v_t, cache_k, cache_v, gq)


def _route(s_t, sb_t):
    tm = s_t.shape[1]
    i8 = lax.broadcasted_iota(I32, (GROUP_SIZE, tm), 0)
    ninf = -jnp.inf
    sg = [sb_t[GROUP_SIZE * g:GROUP_SIZE * (g + 1)] for g in range(N_GROUPS)]
    gscore = []
    for g in range(N_GROUPS):
        m1 = jnp.max(sg[g], axis=0, keepdims=True)
        i1 = jnp.min(jnp.where(sg[g] == m1, i8, GROUP_SIZE), axis=0, keepdims=True)
        m2 = jnp.max(jnp.where(i8 == i1, ninf, sg[g]), axis=0, keepdims=True)
        gscore.append(m1 + m2)
    gs = jnp.concatenate(gscore, axis=0)
    gsel = jnp.zeros((N_GROUPS, tm), I32)
    for _ in range(TOPK_GROUPS):
        m = jnp.max(gs, axis=0, keepdims=True)
        idx = jnp.min(jnp.where(gs == m, i8, N_GROUPS), axis=0, keepdims=True)
        hit = i8 == idx
        gsel = jnp.where(hit, 1, gsel)
        gs = jnp.where(hit, ninf, gs)
    sm = [jnp.where(gsel[g:g + 1] > 0, sg[g], NEG) for g in range(N_GROUPS)]
    eid = [i8 + GROUP_SIZE * g for g in range(N_GROUPS)]
    sel = [jnp.zeros((GROUP_SIZE, tm), F32) for _ in range(N_GROUPS)]
    idxs, ws = [], []
    for _ in range(TOP_K):
        cm = functools.reduce(jnp.maximum, sm)
        m = jnp.max(cm, axis=0, keepdims=True)
        cand = functools.reduce(jnp.minimum, [jnp.where(sm[g] == m, eid[g], N_EXPERTS) for g in range(N_GROUPS)])
        idx = jnp.min(cand, axis=0, keepdims=True)
        wk = jnp.zeros((GROUP_SIZE, tm), F32)
        for g in range(N_GROUPS):
            hit = eid[g] == idx
            wk = wk + jnp.where(hit, s_t[GROUP_SIZE * g:GROUP_SIZE * (g + 1)], 0.0)
            sel[g] = jnp.where(hit, 1.0, sel[g])
            sm[g] = jnp.where(hit, ninf, sm[g])
        idxs.append(idx)
        ws.append(jnp.sum(wk, axis=0, keepdims=True))
    return idxs, ws, jnp.concatenate(sel, axis=0), eid


def _post_kernel(x_ref, r_ref, a_ref, g1_ref, sc2_ref, sh2_ref, nf_ref, wot_ref, wob_ref, wrt_ref, rb_ref, cin_ref,
                 x1_ref, h2_ref, eidx_ref, gw_ref, rank_ref, cnt_ref, carry_ref):
    tm = x_ref.shape[0]

    @pl.when(pl.program_id(0) == 0)
    def _():
        carry_ref[...] = cin_ref[...]

    mixed = (jnp.dot(r_ref[...].astype(BF16), wot_ref[...], preferred_element_type=F32)
             + jnp.dot(a_ref[...].astype(BF16), wob_ref[...], preferred_element_type=F32))
    x1 = x_ref[...] + g1_ref[...] * mixed
    x1_ref[...] = x1
    h2 = _norm_mod(x1, nf_ref[...], sc2_ref[...], sh2_ref[...])
    h2_ref[...] = _pack_bf16_pairs(h2)

    wr = wrt_ref[...]
    wr_hi = wr.astype(BF16)
    wr_lo = (wr - wr_hi.astype(F32)).astype(BF16)
    h_hi = h2.astype(BF16)
    h_lo = (h2 - h_hi.astype(F32)).astype(BF16)
    logits = _bdot_nt(wr_hi, h_hi) + _bdot_nt(wr_hi, h_lo) + _bdot_nt(wr_lo, h_hi)
    s_t = _sigmoid(logits)
    idxs, ws, sel, eid = _route(s_t, s_t + rb_ref[...])

    rr = lax.broadcasted_iota(I32, (tm, tm), 0)
    cc = lax.broadcasted_iota(I32, (tm, tm), 1)
    before = jnp.where(rr < cc, 1.0, 0.0).astype(BF16)
    carry = carry_ref[...]
    tot = jnp.dot(sel.astype(BF16), before, preferred_element_type=F32) + carry[:, 0:1]
    ranks = []
    for k in range(TOP_K):
        acc = jnp.zeros((GROUP_SIZE, tm), F32)
        for g in range(N_GROUPS):
            acc = acc + jnp.where(eid[g] == idxs[k], tot[GROUP_SIZE * g:GROUP_SIZE * (g + 1)], 0.0)
        ranks.append(jnp.sum(acc, axis=0, keepdims=True))
    carry = carry + jnp.sum(sel, axis=1, keepdims=True)
    carry_ref[...] = carry
    cnt_ref[...] = carry

    wsum = functools.reduce(lambda p, q: p + q, ws)
    pad_i = jnp.zeros((SUBLANES - TOP_K, tm), I32)
    pad_f = jnp.zeros((SUBLANES - TOP_K, tm), F32)
    eidx_ref[...] = jnp.concatenate(idxs + [pad_i], axis=0)
    rank_ref[...] = jnp.concatenate([r.astype(I32) for r in ranks] + [pad_i], axis=0)
    gw_ref[...] = jnp.concatenate([w / wsum * ROUTE_SCALE for w in ws] + [pad_f], axis=0)


def _post(x, r, a, g1, sc2, sh2, nf, wo_top, wo_bot, wr_t, rb, cnt_in, tm, mod_spec):
    n = x.shape[0]
    tile = lambda w: pl.BlockSpec((tm, w), lambda i: (i, 0))
    full = lambda arr: pl.BlockSpec(arr.shape, lambda i: (0,) * arr.ndim)
    slot = pl.BlockSpec((SUBLANES, tm), lambda i: (0, i))
    return pl.pallas_call(
        _post_kernel,
        grid=(n // tm,),
        in_specs=[tile(D_MODEL), tile(D_RNN), tile(D_ATT), mod_spec, mod_spec, mod_spec, full(nf), full(wo_top),
                  full(wo_bot), full(wr_t), full(rb), full(cnt_in)],
        out_specs=[tile(D_MODEL), tile(D_MODEL // 2), slot, slot, slot, full(cnt_in)],
        out_shape=[jax.ShapeDtypeStruct((n, D_MODEL), F32), jax.ShapeDtypeStruct((n, D_MODEL // 2), I32),
                   jax.ShapeDtypeStruct((SUBLANES, n), I32), jax.ShapeDtypeStruct((SUBLANES, n), F32),
                   jax.ShapeDtypeStruct((SUBLANES, n), I32), jax.ShapeDtypeStruct(cnt_in.shape, F32)],
        scratch_shapes=[pltpu.VMEM(cnt_in.shape, F32)],
        compiler_params=pltpu.CompilerParams(dimension_semantics=("arbitrary",), vmem_limit_bytes=VMEM_LIMIT),
        name="post_mix",
    )(x, r, a, g1, sc2, sh2, nf, wo_top, wo_bot, wr_t, rb, cnt_in)


def _plan_kernel(cnt_ref, ps_ref, meta_ref, *, bm):
    cnt = cnt_ref[...]
    padded = jnp.ceil(cnt * (1.0 / bm)) * bm
    row = lax.broadcasted_iota(I32, cnt.shape, 0)
    pend = padded
    s = 1
    while s < N_EXPERTS:
        pend = pend + jnp.where(row >= s, pltpu.roll(pend, s, 0), 0.0)
        s *= 2
    pstart = pend - padded
    ps_ref[...] = pstart
    nb = meta_ref.shape[1]
    start = lax.broadcasted_iota(I32, (N_EXPERTS, nb), 1).astype(F32) * bm
    blk_e = jnp.minimum(jnp.sum(jnp.where(pend[:, 0:1] <= start, 1.0, 0.0), axis=0, keepdims=True), N_EXPERTS - 1.0)
    e_iota = lax.broadcasted_iota(I32, (N_EXPERTS, nb), 0).astype(F32)
    end_b = jnp.sum(jnp.where(e_iota == blk_e, (pstart + cnt)[:, 0:1], 0.0), axis=0, keepdims=True)
    n_valid = jnp.clip(end_b - start[0:1], 0.0, bm)
    n_used = jnp.broadcast_to(pend[N_EXPERTS - 1:N_EXPERTS, 0:1] * (1.0 / bm), (1, nb))
    meta_ref[...] = jnp.concatenate([blk_e.astype(I32), n_valid.astype(I32), n_used.astype(I32),
                                     jnp.zeros((SUBLANES - 3, nb), I32)], axis=0)


def _plan(cnt, bm, n_blocks):
    assert bm & (bm - 1) == 0
    nb = -(-n_blocks // LANES) * LANES
    return pl.pallas_call(
        functools.partial(_plan_kernel, bm=bm),
        out_shape=[jax.ShapeDtypeStruct(cnt.shape, F32), jax.ShapeDtypeStruct((SUBLANES, nb), I32)],
        name="moe_plan",
    )(cnt)


def _layout_kernel(ps_ref, eidx_ref, rank_ref, gw_ref, *out_refs, chunks):
    *pos_refs, w_ref = out_refs
    tn = eidx_ref.shape[1]
    e_iota = lax.broadcasted_iota(I32, (N_EXPERTS, tn), 0)
    ps = ps_ref[...][:, 0:1]
    rows = []
    for k in range(TOP_K):
        hit = e_iota == eidx_ref[k:k + 1, :]
        base = jnp.sum(jnp.where(hit, ps, 0.0), axis=0, keepdims=True)
        rows.append(base.astype(I32) + rank_ref[k:k + 1, :])
    rows.append(jnp.zeros((SUBLANES - TOP_K, tn), I32))
    pos = jnp.concatenate(rows, axis=0)
    for pos_ref, c in zip(pos_refs, chunks):
        for q in range(tn // c):
            pos_ref[q] = pos[:, q * c:(q + 1) * c]
    gw = gw_ref[...]
    rep = jnp.concatenate([jnp.broadcast_to(gw[k:k + 1], (SC_LANES, tn)) for k in range(SUBLANES)], axis=0)
    w_ref[...] = rep.T


def _layouts(pstart, eidx, rank, gw, chunks):
    n = eidx.shape[1]
    tn = min(n, POS_TILE)
    slot = pl.BlockSpec((SUBLANES, tn), lambda i: (0, i))
    return pl.pallas_call(
        functools.partial(_layout_kernel, chunks=chunks),
        grid=(n // tn,),
        in_specs=[pl.BlockSpec(pstart.shape, lambda i: (0, 0)), slot, slot, slot],
        out_specs=[pl.BlockSpec((tn // c, SUBLANES, c), lambda i: (i, 0, 0)) for c in chunks]
        + [pl.BlockSpec((tn, SUBLANES * SC_LANES), lambda i: (i, 0))],
        out_shape=[jax.ShapeDtypeStruct((n // c, SUBLANES, c), I32) for c in chunks]
        + [jax.ShapeDtypeStruct((n, SUBLANES * SC_LANES), F32)],
        compiler_params=pltpu.CompilerParams(dimension_semantics=("arbitrary",), vmem_limit_bytes=VMEM_LIMIT),
        name="moe_layout",
    )(pstart, eidx, rank, gw)


def _sc_worker_id():
    return lax.axis_index("s") * SC_CORES + lax.axis_index("c")


def _sc_mesh():
    return plsc.VectorSubcoreMesh(core_axis_name="c", subcore_axis_name="s")


def _sc_dispatch(seg_a, seg_b, total_rows):
    (h_a, pos_a, c_a), (h_b, pos_b, c_b) = seg_a, seg_b
    width = h_a.shape[1]

    @functools.partial(
        pl.kernel, mesh=_sc_mesh(), out_type=jax.ShapeDtypeStruct((total_rows, width), I32),
        scratch_types=[pltpu.VMEM((2, SUBLANES, c_a), I32), pltpu.VMEM((2, c_a, width), I32),
                       pltpu.VMEM((2, SUBLANES, c_b), I32), pltpu.VMEM((2, c_b, width), I32),
                       pltpu.SemaphoreType.DMA((2,)), pltpu.SemaphoreType.DMA((2,))])
    def run(ha_hbm, pa_hbm, hb_hbm, pb_hbm, xs_hbm, idx_a, rows_a, idx_b, rows_b, in_sems, out_sems):
        wid = _sc_worker_id()

        def segment(h_hbm, p_hbm, idx_v, rows_v, c):
            nch = h_hbm.shape[0] // (SC_WORKERS * c)
            assert nch == 1 or nch % 2 == 0
            chunk0 = wid * nch

            def loads(ci, b):
                return [pltpu.make_async_copy(p_hbm.at[chunk0 + ci], idx_v.at[b], in_sems.at[b]),
                        pltpu.make_async_copy(h_hbm.at[pl.ds((chunk0 + ci) * c, c)], rows_v.at[b], in_sems.at[b])]

            def scatters(b):
                return [pltpu.make_async_copy(rows_v.at[b], xs_hbm.at[idx_v.at[b].at[k]], out_sems.at[b])
                        for k in range(TOP_K)]

            def start(copies):
                for cp in copies:
                    cp.start()

            def wait(copies):
                for cp in copies:
                    cp.wait()

            start(loads(0, 0))
            if nch == 1:
                wait(loads(0, 0))
                start(scatters(0))
                wait(scatters(0))
                return

            @pl.loop(0, nch, step=2)
            def _(ci):
                for b in range(2):
                    wait(loads(ci + b, b))
                    start(scatters(b))

                    @pl.when(ci + b + 1 < nch)
                    def _():
                        @pl.when(ci + b >= 1)
                        def _():
                            wait(scatters(1 - b))

                        start(loads(ci + b + 1, 1 - b))

            wait(scatters(0))
            wait(scatters(1))

        segment(ha_hbm, pa_hbm, idx_a, rows_a, c_a)
        segment(hb_hbm, pb_hbm, idx_b, rows_b, c_b)

    return run(h_a, pos_a, h_b, pos_b)


def _experts_kernel(sb_ref, nb_ref, cnt_ref, nu_ref, xs_hbm, wg_ref, wu_ref, wd_ref, y_hbm,
                    xbuf, ybuf, wgb, wub, wdb, in_sems, out_sems):
    e = pl.program_id(0)
    ring, bm, half = xbuf.shape
    n_used = nu_ref[0]

    def slot_of(g):
        return g & (ring - 1)

    def in_copy(g):
        s = slot_of(g)
        return pltpu.make_async_copy(xs_hbm.at[pl.ds(pl.multiple_of(g * bm, bm), bm), :], xbuf.at[s], in_sems.at[s])

    def out_copy(g):
        s = slot_of(g)
        return pltpu.make_async_copy(ybuf.at[s], y_hbm.at[pl.ds(pl.multiple_of(g * bm, bm), bm), :], out_sems.at[s])

    @pl.when(e == 0)
    def _():
        for g in range(ring - 1):
            @pl.when(g < n_used)
            def _():
                in_copy(g).start()

    wgb[...] = wg_ref[...].astype(BF16)
    wub[...] = wu_ref[...].astype(BF16)
    wdb[...] = wd_ref[...].astype(BF16)
    row = lax.broadcasted_iota(I32, (bm, half), 0)

    def block(b, carry):
        g = sb_ref[e] + b

        @pl.when(g + ring - 1 < n_used)
        def _():
            in_copy(g + ring - 1).start()

        in_copy(g).wait()

        @pl.when(g >= ring)
        def _():
            out_copy(g - ring).wait()

        s = slot_of(g)
        x_lo, x_hi = _unpack_bf16_pairs(jnp.where(row < cnt_ref[e] - b * bm, xbuf[s], 0))
        a = (jnp.dot(x_lo, wgb[:half], preferred_element_type=F32)
             + jnp.dot(x_hi, wgb[half:], preferred_element_type=F32))
        u = (jnp.dot(x_lo, wub[:half], preferred_element_type=F32)
             + jnp.dot(x_hi, wub[half:], preferred_element_type=F32))
        ybuf[s] = jnp.dot((_silu(a) * u).astype(BF16), wdb[...], preferred_element_type=F32)
        out_copy(g).start()
        return carry

    lax.fori_loop(0, nb_ref[e], block, 0)

    @pl.when(e == pl.num_programs(0) - 1)
    def _():
        for r in range(ring):
            @pl.when(n_used - 1 - r >= 0)
            def _():
                out_copy(n_used - 1 - r).wait()


def _experts(first_blk, n_blk, counts, n_used, xs, wg, wu, wd):
    rows = xs.shape[0]
    bm, ring = EXPERT_BLOCK, EXPERT_RING
    assert ring & (ring - 1) == 0
    w_blk = lambda e, *_: (e, 0, 0)
    grid_spec = pltpu.PrefetchScalarGridSpec(
        num_scalar_prefetch=4,
        grid=(N_EXPERTS,),
        in_specs=[pl.BlockSpec(memory_space=pl.ANY),
                  pl.BlockSpec((None, D_MODEL, D_EXPERT), w_blk),
                  pl.BlockSpec((None, D_MODEL, D_EXPERT), w_blk),
                  pl.BlockSpec((None, D_EXPERT, D_MODEL), w_blk)],
        out_specs=pl.BlockSpec(memory_space=pl.ANY),
        scratch_shapes=[pltpu.VMEM((ring, bm, D_MODEL // 2), I32), pltpu.VMEM((ring, bm, D_MODEL), F32),
                        pltpu.VMEM((D_MODEL, D_EXPERT), BF16), pltpu.VMEM((D_MODEL, D_EXPERT), BF16),
                        pltpu.VMEM((D_EXPERT, D_MODEL), BF16),
                        pltpu.SemaphoreType.DMA((ring,)), pltpu.SemaphoreType.DMA((ring,))],
    )
    return pl.pallas_call(
        _experts_kernel,
        grid_spec=grid_spec,
        out_shape=jax.ShapeDtypeStruct((rows, D_MODEL), F32),
        compiler_params=pltpu.CompilerParams(dimension_semantics=("arbitrary",), vmem_limit_bytes=VMEM_LIMIT),
        name="moe_experts",
    )(first_blk, n_blk, counts, n_used, xs, wg, wu, wd)


def _sc_combine(y, seg_a, seg_b):
    (pos_a, w_a), (pos_b, w_b) = seg_a, seg_b
    c = COMBINE_CHUNK
    d = y.shape[1]
    n_a, n_b = w_a.shape[0], w_b.shape[0]

    @functools.partial(
        pl.kernel, mesh=_sc_mesh(),
        out_type=[jax.ShapeDtypeStruct((n_a, d), F32), jax.ShapeDtypeStruct((n_b, d), F32)],
        scratch_types=[pltpu.VMEM(pos_a.shape[1:], I32), pltpu.VMEM(pos_b.shape[1:], I32),
                       pltpu.VMEM((2, c, SUBLANES * SC_LANES), F32), pltpu.VMEM((2, TOP_K, c, d), F32),
                       pltpu.VMEM((c, d), F32), pltpu.SemaphoreType.DMA((2,)), pltpu.SemaphoreType.DMA])
    def run(y_hbm, pa_hbm, wa_hbm, pb_hbm, wb_hbm, oa_hbm, ob_hbm, idx_a, idx_b, w_v, buf_v, out_v, sems, out_sem):
        wid = _sc_worker_id()

        def segment(p_hbm, w_hbm, o_hbm, idx_v):
            per_w = idx_v.shape[1]
            nch = per_w // c
            base = wid * per_w
            pltpu.sync_copy(p_hbm.at[wid], idx_v)

            def in_copies(ci, b):
                w_copy = pltpu.make_async_copy(w_hbm.at[pl.ds(base + ci * c, c)], w_v.at[b], sems.at[b])
                return [w_copy] + [
                    pltpu.make_async_copy(y_hbm.at[idx_v.at[k, pl.ds(ci * c, c)]], buf_v.at[b].at[k], sems.at[b])
                    for k in range(TOP_K)]

            def out_copy(ci):
                return pltpu.make_async_copy(out_v, o_hbm.at[pl.ds(base + ci * c, c)], out_sem)

            def reduce_rows(ci, b):
                @pl.when(ci >= 1)
                def _():
                    out_copy(ci - 1).wait()

                @pl.loop(0, c)
                def _(t):
                    ws = [w_v[b, t, pl.ds(k * SC_LANES, SC_LANES)] for k in range(TOP_K)]
                    for j in range(d // SC_LANES):
                        lanes = pl.ds(j * SC_LANES, SC_LANES)
                        acc = buf_v[b, 0, t, lanes] * ws[0]
                        for k in range(1, TOP_K):
                            acc = acc + buf_v[b, k, t, lanes] * ws[k]
                        out_v[t, lanes] = acc

                out_copy(ci).start()

            for cp in in_copies(0, 0):
                cp.start()

            @pl.loop(0, nch, step=2)
            def _(ci):
                for b in range(2):
                    @pl.when(ci + b + 1 < nch)
                    def _():
                        for cp in in_copies(ci + b + 1, 1 - b):
                            cp.start()

                    for cp in in_copies(ci + b, b):
                        cp.wait()
                    reduce_rows(ci + b, b)

            out_copy(nch - 1).wait()

        segment(pa_hbm, wa_hbm, oa_hbm, idx_a)
        segment(pb_hbm, wb_hbm, ob_hbm, idx_b)

    return run(y, pos_a, w_a, pos_b, w_b)


def _shared_kernel(x1_ref, h2_ref, g2_ref, wsg_ref, wsu_ref, wsd_ref, o_ref):
    half = h2_ref.shape[1]
    x_lo, x_hi = _unpack_bf16_pairs(h2_ref[...])
    wsg, wsu = wsg_ref[...], wsu_ref[...]
    a = (jnp.dot(x_lo, wsg[:half], preferred_element_type=F32) + jnp.dot(x_hi, wsg[half:], preferred_element_type=F32))
    b = (jnp.dot(x_lo, wsu[:half], preferred_element_type=F32) + jnp.dot(x_hi, wsu[half:], preferred_element_type=F32))
    shared = jnp.dot((_silu(a) * b).astype(BF16), wsd_ref[...], preferred_element_type=F32)
    o_ref[...] = x1_ref[...] + g2_ref[...] * shared


def _shared(x1, h2, g2, wsg, wsu, wsd, tm, mod_spec):
    n = h2.shape[0]
    full = lambda arr: pl.BlockSpec(arr.shape, lambda i: (0,) * arr.ndim)
    tile = pl.BlockSpec((tm, D_MODEL), lambda i: (i, 0))
    return pl.pallas_call(
        _shared_kernel,
        grid=(n // tm,),
        in_specs=[tile, pl.BlockSpec((tm, D_MODEL // 2), lambda i: (i, 0)), mod_spec, full(wsg), full(wsu), full(wsd)],
        out_specs=tile,
        out_shape=jax.ShapeDtypeStruct((n, D_MODEL), F32),
        compiler_params=pltpu.CompilerParams(dimension_semantics=("arbitrary",), vmem_limit_bytes=VMEM_LIMIT),
        name="shared_expert",
    )(x1, h2, g2, wsg, wsu, wsd)


def _final_kernel(base_ref, routed_ref, g2_ref, o_ref):
    o_ref[...] = base_ref[...] + g2_ref[...] * routed_ref[...]


def _final(base, routed, g2, tm, mod_spec):
    n = base.shape[0]
    tile = pl.BlockSpec((tm, D_MODEL), lambda i: (i, 0))
    return pl.pallas_call(
        _final_kernel,
        grid=(n // tm,),
        in_specs=[tile, tile, mod_spec],
        out_specs=tile,
        out_shape=jax.ShapeDtypeStruct((n, D_MODEL), F32),
        compiler_params=pltpu.CompilerParams(dimension_semantics=("arbitrary",), vmem_limit_bytes=VMEM_LIMIT),
        name="ffn_residual",
    )(base, routed, g2)


def _block_diag(w):
    nb, bi, bj = w.shape
    return jnp.einsum('nij,nm->nimj', w, jnp.eye(nb, dtype=w.dtype)).reshape(nb * bi, nb * bj)


def kernel(x_prompt, x_sample, c_prompt, c_sample, cache_k_win, cache_v_win, state_conv, state_rnn, ada_w, ada_b, norm_mix, w_in, conv_w, conv_b, gate_a_w, gate_a_b, gate_x_w, gate_x_b, lru_lambda, q_norm, k_norm, attn_sinks, w_out, norm_ffn, router_w, router_bias, exp_w_gate, exp_w_up, exp_w_down, sh_w_gate, sh_w_up, sh_w_down):
    bp, tp, _ = x_prompt.shape
    bs, ts, _ = x_sample.shape
    win = cache_k_win.shape[2]
    n_p, n_s = bp * tp, bs * ts
    row = lambda v: v.reshape(1, -1)

    g_mix, g_ffn = row(norm_mix[0]), row(norm_ffn[0])
    win_bf = w_in[0].astype(BF16)
    q0 = 2 * D_RNN
    win_pair = jnp.concatenate([win_bf[:, :q0], _pair_heads(win_bf[:, q0:q0 + D_ATT], 1), win_bf[:, q0 + D_ATT:]],
                               axis=1)
    cw, cb = conv_w[0], row(conv_b[0])
    wg = jnp.concatenate([_block_diag(gate_a_w[0]), _block_diag(gate_x_w[0])], axis=1).astype(BF16)
    gb = row(jnp.concatenate([gate_a_b[0], gate_x_b[0]]))
    lam = row(lru_lambda[0])
    gq = row(q_norm[0])
    gk2 = row(jnp.tile(k_norm[0], N_KV_HEADS))
    sinks = attn_sinks[0]
    wo_top, wo_bot = w_out[0, :D_RNN].astype(BF16), w_out[0, D_RNN:].astype(BF16)
    wr_t = router_w[0].T
    rb = router_bias[0].reshape(N_EXPERTS, 1)
    wsg, wsu, wsd = sh_w_gate[0].astype(BF16), sh_w_up[0].astype(BF16), sh_w_down[0].astype(BF16)

    mod = _adaln(jnp.concatenate([c_prompt, c_sample], axis=0), ada_w[0], ada_b[0])
    chunks = [mod[:, i * D_MODEL:(i + 1) * D_MODEL] for i in range(6)]
    sh1p, sc1p, g1p, sh2p, sc2p, g2p = [c[:bp].reshape(bp, 1, D_MODEL) for c in chunks]
    sh1s, sc1s, g1s, sh2s, sc2s, g2s = [c[bp:] for c in chunks]

    conv0 = jnp.zeros((bp, SUBLANES, D_RNN), F32)
    h0 = jnp.zeros((bp, 1, D_RNN), F32)
    r_p, q_p, k_p, v_p, cs_p, hs_p = _front_prompt(x_prompt, sc1p, sh1p, g_mix, win_pair, cw, cb, wg, gb, lam, gk2,
                                                   conv0, h0)
    a_p = _attn_prompt(q_p, k_p, v_p, row(jnp.tile(q_norm[0], 2)), sinks)

    x_s_t = jnp.swapaxes(x_sample, 0, 1)
    r_s, q_s, k_s, v_s, cs_s, hs_s = _front_sample(x_s_t, sc1s, sh1s, g_mix, win_bf, cw, cb, wg, gb, lam, gk2,
                                                   jnp.swapaxes(state_conv[0], 0, 1), state_rnn[0])
    cache_k = cache_k_win[0].reshape(bs, win, D_KV)
    cache_v = cache_v_win[0].reshape(bs, win, D_KV)
    a_s = _attn_sample(q_s, k_s, v_s, cache_k, cache_v, gq, sinks)

    tiles_per_seq = tp // TM_POST
    mod_p = pl.BlockSpec((None, 1, D_MODEL), lambda i, *_: (i // tiles_per_seq, 0, 0))
    mod_s = pl.BlockSpec((bs, D_MODEL), lambda i, *_: (0, 0))
    cnt0 = jnp.zeros((N_EXPERTS, LANES), F32)
    x1_p, h2_p, eidx_p, gw_p, rank_p, cnt_p = _post(
        x_prompt.reshape(n_p, D_MODEL), r_p.reshape(n_p, D_RNN), a_p.reshape(n_p, D_ATT), g1p, sc2p, sh2p, g_ffn,
        wo_top, _pair_heads(wo_bot, 0), wr_t, rb, cnt0, TM_POST, mod_p)
    x1_s, h2_s, eidx_s, gw_s, rank_s, cnt_all = _post(
        x_s_t.reshape(n_s, D_MODEL), r_s.reshape(n_s, D_RNN), a_s.reshape(n_s, D_ATT), g1s, sc2s, sh2s, g_ffn,
        wo_top, wo_bot, wr_t, rb, cnt_p, bs, mod_s)

    bm = EXPERT_BLOCK
    n_blocks = -(-((n_p + n_s) * TOP_K) // bm) + N_EXPERTS
    pstart, meta = _plan(cnt_all, bm, n_blocks)
    counts = cnt_all[:, 0].astype(I32)
    first_blk = (pstart[:, 0] * (1.0 / bm)).astype(I32)
    n_blk = (counts + (bm - 1)) // bm
    n_used = meta[2, :1]

    def sc_layouts(eidx, rank, gw, n):
        per_w = n // SC_WORKERS
        c = min(DISPATCH_CHUNK, per_w)
        outs = _layouts(pstart, eidx, rank, gw, (c,) if c == per_w else (c, per_w))
        return (outs[0], c), (outs[-2], outs[-1])

    (dpos_p, c_p), comb_p = sc_layouts(eidx_p, rank_p, gw_p, n_p)
    (dpos_s, c_s), comb_s = sc_layouts(eidx_s, rank_s, gw_s, n_s)
    xs = _sc_dispatch((h2_p, dpos_p, c_p), (h2_s, dpos_s, c_s), n_blocks * bm)
    base_p = _shared(x1_p, h2_p, g2p, wsg, wsu, wsd, TM_POST, mod_p)
    base_s = _shared(x1_s, h2_s, g2s, wsg, wsu, wsd, bs, mod_s)
    y = _experts(first_blk, n_blk, counts, n_used, xs, exp_w_gate[0], exp_w_up[0], exp_w_down[0])
    routed_p, routed_s = _sc_combine(y, comb_p, comb_s)
    y_p = _final(base_p, routed_p, g2p, TM_POST, mod_p)
    y_s = _final(base_s, routed_s, g2s, bs, mod_s)

    y_prompt = y_p.reshape(bp, tp, D_MODEL)
    y_sample = jnp.swapaxes(y_s.reshape(ts, bs, D_MODEL), 0, 1)
    wk = min(WINDOW, tp)
    k_win_p = k_p[:, tp - wk:].reshape(1, bp, wk, N_KV_HEADS, HEAD_DIM)
    v_win_p = v_p[:, tp - wk:].reshape(1, bp, wk, N_KV_HEADS, HEAD_DIM)
    k_new = jnp.swapaxes(k_s, 0, 1)
    v_new = jnp.swapaxes(v_s, 0, 1)
    heads = lambda a: a.reshape(bs, ts, N_KV_HEADS, HEAD_DIM)
    k_win_s = jnp.concatenate([cache_k_win[0], heads(k_new)], axis=1)[None, :, ts:]
    v_win_s = jnp.concatenate([cache_v_win[0], heads(v_new)], axis=1)[None, :, ts:]
    return (y_prompt, y_sample, k_win_p, v_win_p, cs_p[None, :, SUBLANES - (CONV_W - 1):], hs_p.reshape(1, bp, D_RNN),
            k_win_s, v_win_s, jnp.swapaxes(cs_s, 0, 1)[None], hs_s[None])
```

```python
import functools

import jax
import jax.numpy as jnp
from jax import lax
from jax.experimental import pallas as pl
from jax.experimental.pallas import tpu as pltpu
from jax.experimental.pallas import tpu_sc as plsc

F32 = jnp.float32
BF16 = jnp.bfloat16
I32 = jnp.int32

D_MODEL = 1024
D_RNN = 512
N_RNN_BLOCKS = 8
CONV_W = 4
LRU_C = 8.0
HEAD_DIM = 64
N_HEADS = 8
N_KV_HEADS = 2
GQA = N_HEADS // N_KV_HEADS
D_ATT = N_HEADS * HEAD_DIM
D_KV = N_KV_HEADS * HEAD_DIM
WINDOW = 128
N_EXPERTS = 64
TOP_K = 6
N_GROUPS = 8
GROUP_SIZE = N_EXPERTS // N_GROUPS
TOPK_GROUPS = 4
D_EXPERT = 256
D_SHARED = 256
ROUTE_SCALE = 2.5
EPS = 1e-6
NEG = -1e30
D_IN = 2 * D_RNN + D_ATT + 2 * D_KV

SUBLANES = 8
LANES = 128
TM_PROMPT = 256
TM_POST = 512
ATT_BLOCK = WINDOW
ATT_STEP_BLOCKS = 4
EXPERT_BLOCK = 512
EXPERT_RING = 4
SAMPLE_CHUNK = 8
POS_TILE = 2048
VMEM_LIMIT = 48 * 1024 * 1024

SC_CORES = 2
SC_SUBCORES = 16
SC_WORKERS = SC_CORES * SC_SUBCORES
SC_LANES = 16
DISPATCH_CHUNK = 64
COMBINE_CHUNK = 8


def _sigmoid(x):
    return 0.5 * jnp.tanh(0.5 * x) + 0.5


def _silu(x):
    return x * _sigmoid(x)


def _gelu_tanh(x):
    c = 0.7978845608028654
    return x * (0.5 * (1.0 + jnp.tanh(c * (x + 0.044715 * (x * x * x)))))


def _log1p(x):
    u = 1.0 + x
    return jnp.where(u == 1.0, x, jnp.log(u) * x / jnp.where(u == 1.0, 1.0, u - 1.0))


def _neg_expm1_2x(x, exp_x):
    return -jnp.tanh(x) * (exp_x * exp_x + 1.0)


def _softplus(z):
    return jnp.maximum(z, 0.0) + _log1p(jnp.exp(-jnp.abs(z)))


def _div_pow2(x, d):
    assert d & (d - 1) == 0
    return lax.shift_right_logical(x, d.bit_length() - 1)


def _mod_pow2(x, d):
    assert d & (d - 1) == 0
    return x & (d - 1)


def _norm_mod(x, g, sc, sh):
    ms = jnp.mean(x * x, axis=-1, keepdims=True)
    return (x * lax.rsqrt(ms + EPS)) * g * (1.0 + sc) + sh


def _bdot(a, b):
    return jnp.dot(a.astype(BF16), b.astype(BF16), preferred_element_type=F32)


def _bdot_nt(a, b):
    return lax.dot_general(a.astype(BF16), b.astype(BF16), (((1,), (1,)), ((), ())),
                           preferred_element_type=F32)


def _pack_bf16_pairs(x):
    w = x.shape[1] // 2

    def rne_bits(v):
        b = pltpu.bitcast(v, I32)
        return b + 0x7FFF + (lax.shift_right_logical(b, 16) & 1)

    lo = lax.shift_right_logical(rne_bits(x[:, :w]), 16)
    hi = rne_bits(x[:, w:]) & jnp.int32(-65536)
    return lo | hi


def _unpack_bf16_pairs(p):
    lo = pltpu.bitcast(lax.shift_left(p, 16), F32)
    hi = pltpu.bitcast(p & jnp.int32(-65536), F32)
    return lo.astype(BF16), hi.astype(BF16)


def _knorm(k, gk2):
    lane = lax.broadcasted_iota(I32, k.shape, 1)
    first = lane < HEAD_DIM
    k2 = k * k
    s0 = jnp.sum(jnp.where(first, k2, 0.0), axis=-1, keepdims=True)
    s1 = jnp.sum(jnp.where(first, 0.0, k2), axis=-1, keepdims=True)
    ms = jnp.where(first, s0, s1) * (1.0 / HEAD_DIM)
    return k * lax.rsqrt(ms + EPS) * gk2


def _pair_heads(w, axis):
    shape = w.shape
    split = shape[:axis] + (N_KV_HEADS, GQA, HEAD_DIM) + shape[axis + 1:]
    return jnp.swapaxes(w.reshape(split), axis, axis + 1).reshape(shape)


def _lru_coeffs(u, wg, gb, lam):
    g = _bdot(u, wg) + gb
    r = _sigmoid(g[:, :D_RNN])
    i = _sigmoid(g[:, D_RNN:])
    log_a = (-LRU_C * r) * _softplus(-lam)
    a = jnp.exp(log_a)
    b = jnp.sqrt(_neg_expm1_2x(log_a, a)) * (i * u)
    return a, b


def _adaln_kernel(c_ref, w_ref, b_ref, o_ref):
    o_ref[...] = _bdot(_silu(c_ref[...]), w_ref[...]) + b_ref[...]


def _adaln(c_all, ada_w, ada_b):
    n = c_all.shape[0]
    return pl.pallas_call(
        _adaln_kernel,
        grid=(6,),
        in_specs=[pl.BlockSpec((n, D_MODEL), lambda j: (0, 0)),
                  pl.BlockSpec((D_MODEL, D_MODEL), lambda j: (0, j)),
                  pl.BlockSpec((1, D_MODEL), lambda j: (0, j))],
        out_specs=pl.BlockSpec((n, D_MODEL), lambda j: (0, j)),
        out_shape=jax.ShapeDtypeStruct((n, 6 * D_MODEL), F32),
        compiler_params=pltpu.CompilerParams(dimension_semantics=("arbitrary",), vmem_limit_bytes=VMEM_LIMIT),
        name="adaln",
    )(c_all, ada_w, ada_b.reshape(1, -1))


def _scan_rows(a, b, h_in):
    n, c = a.shape
    groups = n // SUBLANES
    a = a.reshape(groups, SUBLANES, c)
    b = b.reshape(groups, SUBLANES, c)
    sub = lax.broadcasted_iota(I32, a.shape, 1)
    s = 1
    while s < SUBLANES:
        m = sub >= s
        a_sh = jnp.where(m, pltpu.roll(a, s, 1), 1.0)
        b_sh = jnp.where(m, pltpu.roll(b, s, 1), 0.0)
        b = a * b_sh + b
        a = a * a_sh
        s *= 2
    carry = h_in
    hs = []
    for g in range(groups):
        hg = a[g] * carry + b[g]
        hs.append(hg)
        carry = hg[SUBLANES - 1:SUBLANES]
    return jnp.concatenate(hs, axis=0)


def _front_prompt_kernel(x_ref, sc_ref, sh_ref, g_ref, win_ref, cw_ref, cb_ref, wg_ref, gb_ref, lam_ref, gk_ref,
                         prev_ref, h0_ref, r_ref, q_ref, k_ref, v_ref, cs_ref, hs_ref, tail_ref, hc_ref):
    j = pl.program_id(1)
    tm = x_ref.shape[0]

    @pl.when(j == 0)
    def _():
        tail_ref[...] = prev_ref[...]
        hc_ref[...] = h0_ref[...]

    h = _norm_mod(x_ref[...], g_ref[...], sc_ref[...], sh_ref[...])
    proj = jnp.dot(h.astype(BF16), win_ref[...], preferred_element_type=F32)
    xr = proj[:, 0:D_RNN]
    yr = proj[:, D_RNN:2 * D_RNN]
    q_ref[...] = proj[:, 2 * D_RNN:2 * D_RNN + D_ATT]
    k_ref[...] = _knorm(proj[:, 2 * D_RNN + D_ATT:2 * D_RNN + D_ATT + D_KV], gk_ref[...])
    v_ref[...] = proj[:, 2 * D_RNN + D_ATT + D_KV:D_IN]

    tail = tail_ref[...]
    row8 = lax.broadcasted_iota(I32, tail.shape, 0)

    def shifted(s):
        rolled = pltpu.roll(xr, s, 0)
        top = jnp.where(row8 < s, pltpu.roll(tail, s, 0), rolled[0:SUBLANES])
        return jnp.concatenate([top, rolled[SUBLANES:]], axis=0)

    cw = cw_ref[...]
    u = cb_ref[...] + shifted(3) * cw[0:1]
    u = u + shifted(2) * cw[1:2]
    u = u + shifted(1) * cw[2:3]
    u = u + xr * cw[3:4]
    tail_ref[...] = xr[tm - SUBLANES:tm]

    a, b = _lru_coeffs(u, wg_ref[...], gb_ref[...], lam_ref[...])
    hs = _scan_rows(a, b, hc_ref[...])
    hc_ref[...] = hs[tm - 1:tm]
    r_ref[...] = hs * _gelu_tanh(yr)

    @pl.when(j == pl.num_programs(1) - 1)
    def _():
        cs_ref[...] = xr[tm - SUBLANES:tm]
        hs_ref[...] = hs[tm - 1:tm]


def _front_prompt(x, sc, sh, g, win, cw, cb, wg, gb, lam, gk, prev, h0):
    bsz, t, _ = x.shape
    tm = TM_PROMPT
    full = lambda a: pl.BlockSpec(a.shape, lambda b, j: (0,) * a.ndim)
    per_b = lambda a: pl.BlockSpec((None,) + a.shape[1:], lambda b, j: (b,) + (0,) * (a.ndim - 1))
    tile = lambda w: pl.BlockSpec((None, tm, w), lambda b, j: (b, j, 0))
    return pl.pallas_call(
        _front_prompt_kernel,
        grid=(bsz, t // tm),
        in_specs=[tile(D_MODEL), per_b(sc), per_b(sh), full(g), full(win), full(cw), full(cb), full(wg), full(gb),
                  full(lam), full(gk), per_b(prev), per_b(h0)],
        out_specs=[tile(D_RNN), tile(D_ATT), tile(D_KV), tile(D_KV),
                   pl.BlockSpec((None, SUBLANES, D_RNN), lambda b, j: (b, 0, 0)),
                   pl.BlockSpec((None, 1, D_RNN), lambda b, j: (b, 0, 0))],
        out_shape=[jax.ShapeDtypeStruct((bsz, t, D_RNN), F32), jax.ShapeDtypeStruct((bsz, t, D_ATT), F32),
                   jax.ShapeDtypeStruct((bsz, t, D_KV), F32), jax.ShapeDtypeStruct((bsz, t, D_KV), F32),
                   jax.ShapeDtypeStruct((bsz, SUBLANES, D_RNN), F32), jax.ShapeDtypeStruct((bsz, 1, D_RNN), F32)],
        scratch_shapes=[pltpu.VMEM((SUBLANES, D_RNN), F32), pltpu.VMEM((1, D_RNN), F32)],
        compiler_params=pltpu.CompilerParams(dimension_semantics=("arbitrary", "arbitrary"),
                                             vmem_limit_bytes=VMEM_LIMIT),
        name="front_prompt",
    )(x, sc, sh, g, win, cw, cb, wg, gb, lam, gk, prev, h0)


def _front_sample_kernel(x_ref, sc_ref, sh_ref, g_ref, win_ref, cw_ref, cb_ref, wg_ref, gb_ref, lam_ref, gk_ref,
                         prev_ref, h0_ref, r_ref, q_ref, k_ref, v_ref, cs_ref, hs_ref):
    t_len, bsz, _ = x_ref.shape
    x = x_ref[...]
    ms = jnp.mean(x * x, axis=-1, keepdims=True)
    h = (x * lax.rsqrt(ms + EPS)) * g_ref[...] * (1.0 + sc_ref[...]) + sh_ref[...]
    proj = jnp.dot(h.reshape(t_len * bsz, D_MODEL).astype(BF16), win_ref[...], preferred_element_type=F32)
    xr = proj[:, 0:D_RNN]
    yr = proj[:, D_RNN:2 * D_RNN]
    q_ref[...] = proj[:, 2 * D_RNN:2 * D_RNN + D_ATT].reshape(t_len, bsz, D_ATT)
    k_ref[...] = _knorm(proj[:, 2 * D_RNN + D_ATT:2 * D_RNN + D_ATT + D_KV], gk_ref[...]).reshape(t_len, bsz, D_KV)
    v_ref[...] = proj[:, 2 * D_RNN + D_ATT + D_KV:D_IN].reshape(t_len, bsz, D_KV)

    def at_time(t):
        if t >= 0:
            return xr[t * bsz:(t + 1) * bsz]
        return prev_ref[CONV_W - 1 + t]

    cw = cw_ref[...]
    us = []
    for t in range(t_len):
        u = cb_ref[...] + at_time(t - 3) * cw[0:1]
        u = u + at_time(t - 2) * cw[1:2]
        u = u + at_time(t - 1) * cw[2:3]
        u = u + at_time(t) * cw[3:4]
        us.append(u)
    a, b = _lru_coeffs(jnp.concatenate(us, axis=0), wg_ref[...], gb_ref[...], lam_ref[...])
    hcur = h0_ref[...]
    for t in range(t_len):
        hcur = a[t * bsz:(t + 1) * bsz] * hcur + b[t * bsz:(t + 1) * bsz]
        r_ref[t] = hcur * _gelu_tanh(yr[t * bsz:(t + 1) * bsz])
    hs_ref[...] = hcur
    for s in range(CONV_W - 1):
        cs_ref[s] = at_time(t_len - (CONV_W - 1) + s)


def _front_sample(x_t, sc, sh, g, win, cw, cb, wg, gb, lam, gk, prev_t, h0):
    t_len, bsz, _ = x_t.shape
    return pl.pallas_call(
        _front_sample_kernel,
        out_shape=[jax.ShapeDtypeStruct((t_len, bsz, D_RNN), F32), jax.ShapeDtypeStruct((t_len, bsz, D_ATT), F32),
                   jax.ShapeDtypeStruct((t_len, bsz, D_KV), F32), jax.ShapeDtypeStruct((t_len, bsz, D_KV), F32),
                   jax.ShapeDtypeStruct((CONV_W - 1, bsz, D_RNN), F32), jax.ShapeDtypeStruct((bsz, D_RNN), F32)],
        compiler_params=pltpu.CompilerParams(vmem_limit_bytes=VMEM_LIMIT),
        name="front_sample",
    )(x_t, sc, sh, g, win, cw, cb, wg, gb, lam, gk, prev_t, h0)


def _qnorm(q, gq):
    ms = jnp.mean(q * q, axis=-1, keepdims=True)
    return q * lax.rsqrt(ms + EPS) * gq


def _attn_prompt_kernel(sink_ref, q_ref, kp_ref, kc_ref, vp_ref, vc_ref, gq_ref, o_ref):
    j = pl.program_id(1)
    blk = kp_ref.shape[0]
    k_all = jnp.concatenate([kp_ref[...], kc_ref[...]], axis=0)
    v_all = jnp.concatenate([vp_ref[...], vc_ref[...]], axis=0)
    qi = lax.broadcasted_iota(I32, (blk, 2 * blk), 0)
    kj = lax.broadcasted_iota(I32, (blk, 2 * blk), 1)
    dist = blk + qi - kj
    window = (dist >= 0) & (dist <= WINDOW)
    distf = dist.astype(F32)
    slab = 2 * HEAD_DIM
    first_q = lax.broadcasted_iota(I32, (blk, slab), 1) < HEAD_DIM
    first_kv = lax.broadcasted_iota(I32, (2 * blk, slab), 1) < HEAD_DIM

    bias = [jnp.where(window, (-(2.0 ** -(h + 1))) * distf, NEG) for h in range(N_HEADS)]
    no_prev = (kj < blk) & (j == 0)
    gq = gq_ref[...] * (HEAD_DIM ** -0.5)

    def probs(s, h, first_block):
        b = jnp.where(no_prev, NEG, bias[h]) if first_block else bias[h]
        s = s + b
        sink = sink_ref[h]
        m = jnp.maximum(jnp.max(s, axis=-1, keepdims=True), sink)
        p = jnp.exp(s - m)
        return p, jnp.sum(p, axis=-1, keepdims=True) + jnp.exp(sink - m)

    for sub in range(q_ref.shape[0] // blk):
        first = sub == 0
        q = q_ref[sub * blk:(sub + 1) * blk, :]
        kk = k_all[sub * blk:(sub + 2) * blk]
        vv = v_all[sub * blk:(sub + 2) * blk]
        v_a = jnp.where(first_kv, vv, 0.0)
        v_b = jnp.where(first_kv, 0.0, vv)
        outs = []
        for i in range(GQA):
            q2 = _knorm(q[:, i * slab:(i + 1) * slab], gq)
            p_a, l_a = probs(_bdot_nt(jnp.where(first_q, q2, 0.0), kk), i, first)
            p_b, l_b = probs(_bdot_nt(jnp.where(first_q, 0.0, q2), kk), GQA + i, first)
            outs.append((_bdot(p_a, v_a) + _bdot(p_b, v_b)) / jnp.where(first_q, l_a, l_b))
        o_ref[sub * blk:(sub + 1) * blk, :] = jnp.concatenate(outs, axis=1)


def _attn_prompt(q, k, v, gq, sinks):
    bsz, t, _ = q.shape
    blk = ATT_BLOCK
    nsub = ATT_STEP_BLOCKS
    cur = lambda w: pl.BlockSpec((None, nsub * blk, w), lambda b, j: (b, j, 0))
    prv = lambda w: pl.BlockSpec((None, blk, w), lambda b, j: (b, jnp.maximum(nsub * j - 1, 0), 0))
    return pl.pallas_call(
        _attn_prompt_kernel,
        grid=(bsz, t // (nsub * blk)),
        in_specs=[pl.BlockSpec(memory_space=pltpu.SMEM), cur(D_ATT), prv(D_KV), cur(D_KV), prv(D_KV), cur(D_KV),
                  pl.BlockSpec(gq.shape, lambda b, j: (0, 0))],
        out_specs=cur(D_ATT),
        out_shape=jax.ShapeDtypeStruct((bsz, t, D_ATT), F32),
        compiler_params=pltpu.CompilerParams(dimension_semantics=("arbitrary", "arbitrary"),
                                             vmem_limit_bytes=VMEM_LIMIT),
        name="attn_prompt",
    )(sinks, q, k, k, v, v, gq)


def _attn_sample_kernel(sink_ref, q_ref, kn_ref, vn_ref, kc_ref, vc_ref, gq_ref, o_ref):
    t_len, cb, _ = q_ref.shape
    win = kc_ref.shape[1]
    rows = GQA * t_len * cb
    kc = kc_ref[...].reshape(cb * win, D_KV)
    vc = vc_ref[...].reshape(cb * win, D_KV)
    kn = kn_ref[...].reshape(t_len * cb, D_KV)
    vn = vn_ref[...].reshape(t_len * cb, D_KV)

    r_c = lax.broadcasted_iota(I32, (rows, cb * win), 0)
    c_c = lax.broadcasted_iota(I32, (rows, cb * win), 1)
    t_c = _div_pow2(_mod_pow2(r_c, t_len * cb), cb)
    valid_c = (_mod_pow2(r_c, cb) == _div_pow2(c_c, win)) & (_mod_pow2(c_c, win) >= t_c)
    dist_c = (win + t_c - _mod_pow2(c_c, win)).astype(F32)
    r_n = lax.broadcasted_iota(I32, (rows, t_len * cb), 0)
    c_n = lax.broadcasted_iota(I32, (rows, t_len * cb), 1)
    t_n = _div_pow2(_mod_pow2(r_n, t_len * cb), cb)
    valid_n = (_mod_pow2(r_n, cb) == _mod_pow2(c_n, cb)) & (_div_pow2(c_n, cb) <= t_n)
    dist_n = (t_n - _div_pow2(c_n, cb)).astype(F32)
    hl = _div_pow2(lax.broadcasted_iota(I32, (rows, 1), 0), t_len * cb)

    per_group = []
    for g in range(N_KV_HEADS):
        slabs = [q_ref[t][:, (g * GQA + i) * HEAD_DIM:(g * GQA + i + 1) * HEAD_DIM]
                 for i in range(GQA) for t in range(t_len)]
        qg = _qnorm(jnp.concatenate(slabs, axis=0), gq_ref[...])
        slope = jnp.zeros((rows, 1), F32)
        sink = jnp.zeros((rows, 1), F32)
        for i in range(GQA):
            slope = jnp.where(hl == i, 2.0 ** -(g * GQA + i + 1), slope)
            sink = jnp.where(hl == i, sink_ref[g * GQA + i], sink)
        lo, hi = g * HEAD_DIM, (g + 1) * HEAD_DIM
        s_c = _bdot_nt(qg, kc[:, lo:hi]) * (HEAD_DIM ** -0.5) - slope * dist_c
        s_n = _bdot_nt(qg, kn[:, lo:hi]) * (HEAD_DIM ** -0.5) - slope * dist_n
        s_c = jnp.where(valid_c, s_c, NEG)
        s_n = jnp.where(valid_n, s_n, NEG)
        m = jnp.maximum(jnp.maximum(jnp.max(s_c, axis=-1, keepdims=True), jnp.max(s_n, axis=-1, keepdims=True)), sink)
        p_c = jnp.exp(s_c - m)
        p_n = jnp.exp(s_n - m)
        l = jnp.sum(p_c, axis=-1, keepdims=True) + jnp.sum(p_n, axis=-1, keepdims=True) + jnp.exp(sink - m)
        per_group.append((_bdot(p_c, vc[:, lo:hi]) + _bdot(p_n, vn[:, lo:hi])) / l)
    for t in range(t_len):
        o_ref[t] = jnp.concatenate(
            [per_group[g][(i * t_len + t) * cb:(i * t_len + t + 1) * cb] for g in range(N_KV_HEADS) for i in range(GQA)],
            axis=1)


def _attn_sample(q_t, k_t, v_t, cache_k, cache_v, gq, sinks):
    t_len, bsz, _ = q_t.shape
    cb = SAMPLE_CHUNK
    win = cache_k.shape[1]
    new = lambda w: pl.BlockSpec((t_len, cb, w), lambda c: (0, c, 0))
    old = pl.BlockSpec((cb, win, D_KV), lambda c: (c, 0, 0))
    return pl.pallas_call(
        _attn_sample_kernel,
        grid=(bsz // cb,),
        in_specs=[pl.BlockSpec(memory_space=pltpu.SMEM), new(D_ATT), new(D_KV), new(D_KV), old, old,
                  pl.BlockSpec(gq.shape, lambda c: (0, 0))],
        out_specs=new(D_ATT),
        out_shape=jax.ShapeDtypeStruct((t_len, bsz, D_ATT), F32),
        compiler_params=pltpu.CompilerParams(dimension_semantics=("arbitrary",), vmem_limit_bytes=VMEM_LIMIT),
        name="attn_sample",
    )(sinks, q_t, k_t, v_t, cache_k, cache_v, gq)


def _route(s_t, sb_t):
    tm = s_t.shape[1]
    i8 = lax.broadcasted_iota(I32, (GROUP_SIZE, tm), 0)
    ninf = -jnp.inf
    sg = [sb_t[GROUP_SIZE * g:GROUP_SIZE * (g + 1)] for g in range(N_GROUPS)]
    gscore = []
    for g in range(N_GROUPS):
        m1 = jnp.max(sg[g], axis=0, keepdims=True)
        i1 = jnp.min(jnp.where(sg[g] == m1, i8, GROUP_SIZE), axis=0, keepdims=True)
        m2 = jnp.max(jnp.where(i8 == i1, ninf, sg[g]), axis=0, keepdims=True)
        gscore.append(m1 + m2)
    gs = jnp.concatenate(gscore, axis=0)
    gsel = jnp.zeros((N_GROUPS, tm), I32)
    for _ in range(TOPK_GROUPS):
        m = jnp.max(gs, axis=0, keepdims=True)
        idx = jnp.min(jnp.where(gs == m, i8, N_GROUPS), axis=0, keepdims=True)
        hit = i8 == idx
        gsel = jnp.where(hit, 1, gsel)
        gs = jnp.where(hit, ninf, gs)
    sm = [jnp.where(gsel[g:g + 1] > 0, sg[g], NEG) for g in range(N_GROUPS)]
    eid = [i8 + GROUP_SIZE * g for g in range(N_GROUPS)]
    sel = [jnp.zeros((GROUP_SIZE, tm), F32) for _ in range(N_GROUPS)]
    idxs, ws = [], []
    for _ in range(TOP_K):
        cm = functools.reduce(jnp.maximum, sm)
        m = jnp.max(cm, axis=0, keepdims=True)
        cand = functools.reduce(jnp.minimum, [jnp.where(sm[g] == m, eid[g], N_EXPERTS) for g in range(N_GROUPS)])
        idx = jnp.min(cand, axis=0, keepdims=True)
        wk = jnp.zeros((GROUP_SIZE, tm), F32)
        for g in range(N_GROUPS):
            hit = eid[g] == idx
            wk = wk + jnp.where(hit, s_t[GROUP_SIZE * g:GROUP_SIZE * (g + 1)], 0.0)
            sel[g] = jnp.where(hit, 1.0, sel[g])
            sm[g] = jnp.where(hit, ninf, sm[g])
        idxs.append(idx)
        ws.append(jnp.sum(wk, axis=0, keepdims=True))
    return idxs, ws, jnp.concatenate(sel, axis=0), eid


def _post_kernel(x_ref, r_ref, a_ref, g1_ref, sc2_ref, sh2_ref, nf_ref, wot_ref, wob_ref, wrt_ref, rb_ref, cin_ref,
                 x1_ref, h2_ref, eidx_ref, gw_ref, rank_ref, cnt_ref, carry_ref):
    tm = x_ref.shape[0]

    @pl.when(pl.program_id(0) == 0)
    def _():
        carry_ref[...] = cin_ref[...]

    mixed = (jnp.dot(r_ref[...].astype(BF16), wot_ref[...], preferred_element_type=F32)
             + jnp.dot(a_ref[...].astype(BF16), wob_ref[...], preferred_element_type=F32))
    x1 = x_ref[...] + g1_ref[...] * mixed
    x1_ref[...] = x1
    h2 = _norm_mod(x1, nf_ref[...], sc2_ref[...], sh2_ref[...])
    h2_ref[...] = _pack_bf16_pairs(h2)

    wr = wrt_ref[...]
    wr_hi = wr.astype(BF16)
    wr_lo = (wr - wr_hi.astype(F32)).astype(BF16)
    h_hi = h2.astype(BF16)
    h_lo = (h2 - h_hi.astype(F32)).astype(BF16)
    logits = _bdot_nt(wr_hi, h_hi) + _bdot_nt(wr_hi, h_lo) + _bdot_nt(wr_lo, h_hi)
    s_t = _sigmoid(logits)
    idxs, ws, sel, eid = _route(s_t, s_t + rb_ref[...])

    rr = lax.broadcasted_iota(I32, (tm, tm), 0)
    cc = lax.broadcasted_iota(I32, (tm, tm), 1)
    before = jnp.where(rr < cc, 1.0, 0.0).astype(BF16)
    carry = carry_ref[...]
    tot = jnp.dot(sel.astype(BF16), before, preferred_element_type=F32) + carry[:, 0:1]
    ranks = []
    for k in range(TOP_K):
        acc = jnp.zeros((GROUP_SIZE, tm), F32)
        for g in range(N_GROUPS):
            acc = acc + jnp.where(eid[g] == idxs[k], tot[GROUP_SIZE * g:GROUP_SIZE * (g + 1)], 0.0)
        ranks.append(jnp.sum(acc, axis=0, keepdims=True))
    carry = carry + jnp.sum(sel, axis=1, keepdims=True)
    carry_ref[...] = carry
    cnt_ref[...] = carry

    wsum = functools.reduce(lambda p, q: p + q, ws)
    pad_i = jnp.zeros((SUBLANES - TOP_K, tm), I32)
    pad_f = jnp.zeros((SUBLANES - TOP_K, tm), F32)
    eidx_ref[...] = jnp.concatenate(idxs + [pad_i], axis=0)
    rank_ref[...] = jnp.concatenate([r.astype(I32) for r in ranks] + [pad_i], axis=0)
    gw_ref[...] = jnp.concatenate([w / wsum * ROUTE_SCALE for w in ws] + [pad_f], axis=0)


def _post(x, r, a, g1, sc2, sh2, nf, wo_top, wo_bot, wr_t, rb, cnt_in, tm, mod_spec):
    n = x.shape[0]
    tile = lambda w: pl.BlockSpec((tm, w), lambda i: (i, 0))
    full = lambda arr: pl.BlockSpec(arr.shape, lambda i: (0,) * arr.ndim)
    slot = pl.BlockSpec((SUBLANES, tm), lambda i: (0, i))
    return pl.pallas_call(
        _post_kernel,
        grid=(n // tm,),
        in_specs=[tile(D_MODEL), tile(D_RNN), tile(D_ATT), mod_spec, mod_spec, mod_spec, full(nf), full(wo_top),
                  full(wo_bot), full(wr_t), full(rb), full(cnt_in)],
        out_specs=[tile(D_MODEL), tile(D_MODEL // 2), slot, slot, slot, full(cnt_in)],
        out_shape=[jax.ShapeDtypeStruct((n, D_MODEL), F32), jax.ShapeDtypeStruct((n, D_MODEL // 2), I32),
                   jax.ShapeDtypeStruct((SUBLANES, n), I32), jax.ShapeDtypeStruct((SUBLANES, n), F32),
                   jax.ShapeDtypeStruct((SUBLANES, n), I32), jax.ShapeDtypeStruct(cnt_in.shape, F32)],
        scratch_shapes=[pltpu.VMEM(cnt_in.shape, F32)],
        compiler_params=pltpu.CompilerParams(dimension_semantics=("arbitrary",), vmem_limit_bytes=VMEM_LIMIT),
        name="post_mix",
    )(x, r, a, g1, sc2, sh2, nf, wo_top, wo_bot, wr_t, rb, cnt_in)


def _plan_kernel(cnt_ref, ps_ref, meta_ref, *, bm):
    cnt = cnt_ref[...]
    padded = jnp.ceil(cnt * (1.0 / bm)) * bm
    row = lax.broadcasted_iota(I32, cnt.shape, 0)
    pend = padded
    s = 1
    while s < N_EXPERTS:
        pend = pend + jnp.where(row >= s, pltpu.roll(pend, s, 0), 0.0)
        s *= 2
    pstart = pend - padded
    ps_ref[...] = pstart
    nb = meta_ref.shape[1]
    start = lax.broadcasted_iota(I32, (N_EXPERTS, nb), 1).astype(F32) * bm
    blk_e = jnp.minimum(jnp.sum(jnp.where(pend[:, 0:1] <= start, 1.0, 0.0), axis=0, keepdims=True), N_EXPERTS - 1.0)
    e_iota = lax.broadcasted_iota(I32, (N_EXPERTS, nb), 0).astype(F32)
    end_b = jnp.sum(jnp.where(e_iota == blk_e, (pstart + cnt)[:, 0:1], 0.0), axis=0, keepdims=True)
    n_valid = jnp.clip(end_b - start[0:1], 0.0, bm)
    n_used = jnp.broadcast_to(pend[N_EXPERTS - 1:N_EXPERTS, 0:1] * (1.0 / bm), (1, nb))
    meta_ref[...] = jnp.concatenate([blk_e.astype(I32), n_valid.astype(I32), n_used.astype(I32),
                                     jnp.zeros((SUBLANES - 3, nb), I32)], axis=0)


def _plan(cnt, bm, n_blocks):
    assert bm & (bm - 1) == 0
    nb = -(-n_blocks // LANES) * LANES
    return pl.pallas_call(
        functools.partial(_plan_kernel, bm=bm),
        out_shape=[jax.ShapeDtypeStruct(cnt.shape, F32), jax.ShapeDtypeStruct((SUBLANES, nb), I32)],
        name="moe_plan",
    )(cnt)


def _layout_kernel(ps_ref, eidx_ref, rank_ref, gw_ref, *out_refs, chunks):
    *pos_refs, w_ref = out_refs
    tn = eidx_ref.shape[1]
    e_iota = lax.broadcasted_iota(I32, (N_EXPERTS, tn), 0)
    ps = ps_ref[...][:, 0:1]
    rows = []
    for k in range(TOP_K):
        hit = e_iota == eidx_ref[k:k + 1, :]
        base = jnp.sum(jnp.where(hit, ps, 0.0), axis=0, keepdims=True)
        rows.append(base.astype(I32) + rank_ref[k:k + 1, :])
    rows.append(jnp.zeros((SUBLANES - TOP_K, tn), I32))
    pos = jnp.concatenate(rows, axis=0)
    for pos_ref, c in zip(pos_refs, chunks):
        for q in range(tn // c):
            pos_ref[q] = pos[:, q * c:(q + 1) * c]
    gw = gw_ref[...]
    rep = jnp.concatenate([jnp.broadcast_to(gw[k:k + 1], (SC_LANES, tn)) for k in range(SUBLANES)], axis=0)
    w_ref[...] = rep.T


def _layouts(pstart, eidx, rank, gw, chunks):
    n = eidx.shape[1]
    tn = min(n, POS_TILE)
    slot = pl.BlockSpec((SUBLANES, tn), lambda i: (0, i))
    return pl.pallas_call(
        functools.partial(_layout_kernel, chunks=chunks),
        grid=(n // tn,),
        in_specs=[pl.BlockSpec(pstart.shape, lambda i: (0, 0)), slot, slot, slot],
        out_specs=[pl.BlockSpec((tn // c, SUBLANES, c), lambda i: (i, 0, 0)) for c in chunks]
        + [pl.BlockSpec((tn, SUBLANES * SC_LANES), lambda i: (i, 0))],
        out_shape=[jax.ShapeDtypeStruct((n // c, SUBLANES, c), I32) for c in chunks]
        + [jax.ShapeDtypeStruct((n, SUBLANES * SC_LANES), F32)],
        compiler_params=pltpu.CompilerParams(dimension_semantics=("arbitrary",), vmem_limit_bytes=VMEM_LIMIT),
        name="moe_layout",
    )(pstart, eidx, rank, gw)


def _sc_worker_id():
    return lax.axis_index("s") * SC_CORES + lax.axis_index("c")


def _sc_mesh():
    return plsc.VectorSubcoreMesh(core_axis_name="c", subcore_axis_name="s")


def _sc_dispatch(seg_a, seg_b, total_rows):
    (h_a, pos_a, c_a), (h_b, pos_b, c_b) = seg_a, seg_b
    width = h_a.shape[1]

    @functools.partial(
        pl.kernel, mesh=_sc_mesh(), out_type=jax.ShapeDtypeStruct((total_rows, width), I32),
        scratch_types=[pltpu.VMEM((2, SUBLANES, c_a), I32), pltpu.VMEM((2, c_a, width), I32),
                       pltpu.VMEM((2, SUBLANES, c_b), I32), pltpu.VMEM((2, c_b, width), I32),
                       pltpu.SemaphoreType.DMA((2,)), pltpu.SemaphoreType.DMA((2,))])
    def run(ha_hbm, pa_hbm, hb_hbm, pb_hbm, xs_hbm, idx_a, rows_a, idx_b, rows_b, in_sems, out_sems):
        wid = _sc_worker_id()

        def segment(h_hbm, p_hbm, idx_v, rows_v, c):
            nch = h_hbm.shape[0] // (SC_WORKERS * c)
            assert nch == 1 or nch % 2 == 0
            chunk0 = wid * nch

            def loads(ci, b):
                return [pltpu.make_async_copy(p_hbm.at[chunk0 + ci], idx_v.at[b], in_sems.at[b]),
                        pltpu.make_async_copy(h_hbm.at[pl.ds((chunk0 + ci) * c, c)], rows_v.at[b], in_sems.at[b])]

            def scatters(b):
                return [pltpu.make_async_copy(rows_v.at[b], xs_hbm.at[idx_v.at[b].at[k]], out_sems.at[b])
                        for k in range(TOP_K)]

            def start(copies):
                for cp in copies:
                    cp.start()

            def wait(copies):
                for cp in copies:
                    cp.wait()

            start(loads(0, 0))
            if nch == 1:
                wait(loads(0, 0))
                start(scatters(0))
                wait(scatters(0))
                return

            @pl.loop(0, nch, step=2)
            def _(ci):
                for b in range(2):
                    wait(loads(ci + b, b))
                    start(scatters(b))

                    @pl.when(ci + b + 1 < nch)
                    def _():
                        @pl.when(ci + b >= 1)
                        def _():
                            wait(scatters(1 - b))

                        start(loads(ci + b + 1, 1 - b))

            wait(scatters(0))
            wait(scatters(1))

        segment(ha_hbm, pa_hbm, idx_a, rows_a, c_a)
        segment(hb_hbm, pb_hbm, idx_b, rows_b, c_b)

    return run(h_a, pos_a, h_b, pos_b)


def _experts_kernel(sb_ref, nb_ref, cnt_ref, nu_ref, xs_hbm, wg_ref, wu_ref, wd_ref, y_hbm,
                    xbuf, ybuf, wgb, wub, wdb, in_sems, out_sems):
    e = pl.program_id(0)
    ring, bm, half = xbuf.shape
    n_used = nu_ref[0]

    def slot_of(g):
        return g & (ring - 1)

    def in_copy(g):
        s = slot_of(g)
        return pltpu.make_async_copy(xs_hbm.at[pl.ds(pl.multiple_of(g * bm, bm), bm), :], xbuf.at[s], in_sems.at[s])

    def out_copy(g):
        s = slot_of(g)
        return pltpu.make_async_copy(ybuf.at[s], y_hbm.at[pl.ds(pl.multiple_of(g * bm, bm), bm), :], out_sems.at[s])

    @pl.when(e == 0)
    def _():
        for g in range(ring - 1):
            @pl.when(g < n_used)
            def _():
                in_copy(g).start()

    wgb[...] = wg_ref[...].astype(BF16)
    wub[...] = wu_ref[...].astype(BF16)
    wdb[...] = wd_ref[...].astype(BF16)
    row = lax.broadcasted_iota(I32, (bm, half), 0)

    def block(b, carry):
        g = sb_ref[e] + b

        @pl.when(g + ring - 1 < n_used)
        def _():
            in_copy(g + ring - 1).start()

        in_copy(g).wait()

        @pl.when(g >= ring)
        def _():
            out_copy(g - ring).wait()

        s = slot_of(g)
        x_lo, x_hi = _unpack_bf16_pairs(jnp.where(row < cnt_ref[e] - b * bm, xbuf[s], 0))
        a = (jnp.dot(x_lo, wgb[:half], preferred_element_type=F32)
             + jnp.dot(x_hi, wgb[half:], preferred_element_type=F32))
        u = (jnp.dot(x_lo, wub[:half], preferred_element_type=F32)
             + jnp.dot(x_hi, wub[half:], preferred_element_type=F32))
        ybuf[s] = jnp.dot((_silu(a) * u).astype(BF16), wdb[...], preferred_element_type=F32)
        out_copy(g).start()
        return carry

    lax.fori_loop(0, nb_ref[e], block, 0)

    @pl.when(e == pl.num_programs(0) - 1)
    def _():
        for r in range(ring):
            @pl.when(n_used - 1 - r >= 0)
            def _():
                out_copy(n_used - 1 - r).wait()


def _experts(first_blk, n_blk, counts, n_used, xs, wg, wu, wd):
    rows = xs.shape[0]
    bm, ring = EXPERT_BLOCK, EXPERT_RING
    assert ring & (ring - 1) == 0
    w_blk = lambda e, *_: (e, 0, 0)
    grid_spec = pltpu.PrefetchScalarGridSpec(
        num_scalar_prefetch=4,
        grid=(N_EXPERTS,),
        in_specs=[pl.BlockSpec(memory_space=pl.ANY),
                  pl.BlockSpec((None, D_MODEL, D_EXPERT), w_blk),
                  pl.BlockSpec((None, D_MODEL, D_EXPERT), w_blk),
                  pl.BlockSpec((None, D_EXPERT, D_MODEL), w_blk)],
        out_specs=pl.BlockSpec(memory_space=pl.ANY),
        scratch_shapes=[pltpu.VMEM((ring, bm, D_MODEL // 2), I32), pltpu.VMEM((ring, bm, D_MODEL), F32),
                        pltpu.VMEM((D_MODEL, D_EXPERT), BF16), pltpu.VMEM((D_MODEL, D_EXPERT), BF16),
                        pltpu.VMEM((D_EXPERT, D_MODEL), BF16),
                        pltpu.SemaphoreType.DMA((ring,)), pltpu.SemaphoreType.DMA((ring,))],
    )
    return pl.pallas_call(
        _experts_kernel,
        grid_spec=grid_spec,
        out_shape=jax.ShapeDtypeStruct((rows, D_MODEL), F32),
        compiler_params=pltpu.CompilerParams(dimension_semantics=("arbitrary",), vmem_limit_bytes=VMEM_LIMIT),
        name="moe_experts",
    )(first_blk, n_blk, counts, n_used, xs, wg, wu, wd)


def _sc_combine(y, seg_a, seg_b):
    (pos_a, w_a), (pos_b, w_b) = seg_a, seg_b
    c = COMBINE_CHUNK
    d = y.shape[1]
    n_a, n_b = w_a.shape[0], w_b.shape[0]

    @functools.partial(
        pl.kernel, mesh=_sc_mesh(),
        out_type=[jax.ShapeDtypeStruct((n_a, d), F32), jax.ShapeDtypeStruct((n_b, d), F32)],
        scratch_types=[pltpu.VMEM(pos_a.shape[1:], I32), pltpu.VMEM(pos_b.shape[1:], I32),
                       pltpu.VMEM((2, c, SUBLANES * SC_LANES), F32), pltpu.VMEM((2, TOP_K, c, d), F32),
                       pltpu.VMEM((c, d), F32), pltpu.SemaphoreType.DMA((2,)), pltpu.SemaphoreType.DMA])
    def run(y_hbm, pa_hbm, wa_hbm, pb_hbm, wb_hbm, oa_hbm, ob_hbm, idx_a, idx_b, w_v, buf_v, out_v, sems, out_sem):
        wid = _sc_worker_id()

        def segment(p_hbm, w_hbm, o_hbm, idx_v):
            per_w = idx_v.shape[1]
            nch = per_w // c
            base = wid * per_w
            pltpu.sync_copy(p_hbm.at[wid], idx_v)

            def in_copies(ci, b):
                w_copy = pltpu.make_async_copy(w_hbm.at[pl.ds(base + ci * c, c)], w_v.at[b], sems.at[b])
                return [w_copy] + [
                    pltpu.make_async_copy(y_hbm.at[idx_v.at[k, pl.ds(ci * c, c)]], buf_v.at[b].at[k], sems.at[b])
                    for k in range(TOP_K)]

            def out_copy(ci):
                return pltpu.make_async_copy(out_v, o_hbm.at[pl.ds(base + ci * c, c)], out_sem)

            def reduce_rows(ci, b):
                @pl.when(ci >= 1)
                def _():
                    out_copy(ci - 1).wait()

                @pl.loop(0, c)
                def _(t):
                    ws = [w_v[b, t, pl.ds(k * SC_LANES, SC_LANES)] for k in range(TOP_K)]
                    for j in range(d // SC_LANES):
                        lanes = pl.ds(j * SC_LANES, SC_LANES)
                        acc = buf_v[b, 0, t, lanes] * ws[0]
                        for k in range(1, TOP_K):
                            acc = acc + buf_v[b, k, t, lanes] * ws[k]
                        out_v[t, lanes] = acc

                out_copy(ci).start()

            for cp in in_copies(0, 0):
                cp.start()

            @pl.loop(0, nch, step=2)
            def _(ci):
                for b in range(2):
                    @pl.when(ci + b + 1 < nch)
                    def _():
                        for cp in in_copies(ci + b + 1, 1 - b):
                            cp.start()

                    for cp in in_copies(ci + b, b):
                        cp.wait()
                    reduce_rows(ci + b, b)

            out_copy(nch - 1).wait()

        segment(pa_hbm, wa_hbm, oa_hbm, idx_a)
        segment(pb_hbm, wb_hbm, ob_hbm, idx_b)

    return run(y, pos_a, w_a, pos_b, w_b)


def _shared_kernel(x1_ref, h2_ref, g2_ref, wsg_ref, wsu_ref, wsd_ref, o_ref):
    half = h2_ref.shape[1]
    x_lo, x_hi = _unpack_bf16_pairs(h2_ref[...])
    wsg, wsu = wsg_ref[...], wsu_ref[...]
    a = (jnp.dot(x_lo, wsg[:half], preferred_element_type=F32) + jnp.dot(x_hi, wsg[half:], preferred_element_type=F32))
    b = (jnp.dot(x_lo, wsu[:half], preferred_element_type=F32) + jnp.dot(x_hi, wsu[half:], preferred_element_type=F32))
    shared = jnp.dot((_silu(a) * b).astype(BF16), wsd_ref[...], preferred_element_type=F32)
    o_ref[...] = x1_ref[...] + g2_ref[...] * shared


def _shared(x1, h2, g2, wsg, wsu, wsd, tm, mod_spec):
    n = h2.shape[0]
    full = lambda arr: pl.BlockSpec(arr.shape, lambda i: (0,) * arr.ndim)
    tile = pl.BlockSpec((tm, D_MODEL), lambda i: (i, 0))
    return pl.pallas_call(
        _shared_kernel,
        grid=(n // tm,),
        in_specs=[tile, pl.BlockSpec((tm, D_MODEL // 2), lambda i: (i, 0)), mod_spec, full(wsg), full(wsu), full(wsd)],
        out_specs=tile,
        out_shape=jax.ShapeDtypeStruct((n, D_MODEL), F32),
        compiler_params=pltpu.CompilerParams(dimension_semantics=("arbitrary",), vmem_limit_bytes=VMEM_LIMIT),
        name="shared_expert",
    )(x1, h2, g2, wsg, wsu, wsd)


def _final_kernel(base_ref, routed_ref, g2_ref, o_ref):
    o_ref[...] = base_ref[...] + g2_ref[...] * routed_ref[...]


def _final(base, routed, g2, tm, mod_spec):
    n = base.shape[0]
    tile = pl.BlockSpec((tm, D_MODEL), lambda i: (i, 0))
    return pl.pallas_call(
        _final_kernel,
        grid=(n // tm,),
        in_specs=[tile, tile, mod_spec],
        out_specs=tile,
        out_shape=jax.ShapeDtypeStruct((n, D_MODEL), F32),
        compiler_params=pltpu.CompilerParams(dimension_semantics=("arbitrary",), vmem_limit_bytes=VMEM_LIMIT),
        name="ffn_residual",
    )(base, routed, g2)


def _block_diag(w):
    nb, bi, bj = w.shape
    return jnp.einsum('nij,nm->nimj', w, jnp.eye(nb, dtype=w.dtype)).reshape(nb * bi, nb * bj)


def kernel(x_prompt, x_sample, c_prompt, c_sample, cache_k_win, cache_v_win, state_conv, state_rnn, ada_w, ada_b, norm_mix, w_in, conv_w, conv_b, gate_a_w, gate_a_b, gate_x_w, gate_x_b, lru_lambda, q_norm, k_norm, attn_sinks, w_out, norm_ffn, router_w, router_bias, exp_w_gate, exp_w_up, exp_w_down, sh_w_gate, sh_w_up, sh_w_down):
    bp, tp, _ = x_prompt.shape
    bs, ts, _ = x_sample.shape
    win = cache_k_win.shape[2]
    n_p, n_s = bp * tp, bs * ts
    row = lambda v: v.reshape(1, -1)

    g_mix, g_ffn = row(norm_mix[0]), row(norm_ffn[0])
    win_bf = w_in[0].astype(BF16)
    q0 = 2 * D_RNN
    win_pair = jnp.concatenate([win_bf[:, :q0], _pair_heads(win_bf[:, q0:q0 + D_ATT], 1), win_bf[:, q0 + D_ATT:]],
                               axis=1)
    cw, cb = conv_w[0], row(conv_b[0])
    wg = jnp.concatenate([_block_diag(gate_a_w[0]), _block_diag(gate_x_w[0])], axis=1).astype(BF16)
    gb = row(jnp.concatenate([gate_a_b[0], gate_x_b[0]]))
    lam = row(lru_lambda[0])
    gq = row(q_norm[0])
    gk2 = row(jnp.tile(k_norm[0], N_KV_HEADS))
    sinks = attn_sinks[0]
    wo_top, wo_bot = w_out[0, :D_RNN].astype(BF16), w_out[0, D_RNN:].astype(BF16)
    wr_t = router_w[0].T
    rb = router_bias[0].reshape(N_EXPERTS, 1)
    wsg, wsu, wsd = sh_w_gate[0].astype(BF16), sh_w_up[0].astype(BF16), sh_w_down[0].astype(BF16)

    mod = _adaln(jnp.concatenate([c_prompt, c_sample], axis=0), ada_w[0], ada_b[0])
    chunks = [mod[:, i * D_MODEL:(i + 1) * D_MODEL] for i in range(6)]
    sh1p, sc1p, g1p, sh2p, sc2p, g2p = [c[:bp].reshape(bp, 1, D_MODEL) for c in chunks]
    sh1s, sc1s, g1s, sh2s, sc2s, g2s = [c[bp:] for c in chunks]

    conv0 = jnp.zeros((bp, SUBLANES, D_RNN), F32)
    h0 = jnp.zeros((bp, 1, D_RNN), F32)
    r_p, q_p, k_p, v_p, cs_p, hs_p = _front_prompt(x_prompt, sc1p, sh1p, g_mix, win_pair, cw, cb, wg, gb, lam, gk2,
                                                   conv0, h0)
    a_p = _attn_prompt(q_p, k_p, v_p, row(jnp.tile(q_norm[0], 2)), sinks)

    x_s_t = jnp.swapaxes(x_sample, 0, 1)
    r_s, q_s, k_s, v_s, cs_s, hs_s = _front_sample(x_s_t, sc1s, sh1s, g_mix, win_bf, cw, cb, wg, gb, lam, gk2,
                                                   jnp.swapaxes(state_conv[0], 0, 1), state_rnn[0])
    cache_k = cache_k_win[0].reshape(bs, win, D_KV)
    cache_v = cache_v_win[0].reshape(bs, win, D_KV)
    a_s = _attn_sample(q_s, k_s, v_s, cache_k, cache_v, gq, sinks)

    tiles_per_seq = tp // TM_POST
    mod_p = pl.BlockSpec((None, 1, D_MODEL), lambda i, *_: (i // tiles_per_seq, 0, 0))
    mod_s = pl.BlockSpec((bs, D_MODEL), lambda i, *_: (0, 0))
    cnt0 = jnp.zeros((N_EXPERTS, LANES), F32)
    x1_p, h2_p, eidx_p, gw_p, rank_p, cnt_p = _post(
        x_prompt.reshape(n_p, D_MODEL), r_p.reshape(n_p, D_RNN), a_p.reshape(n_p, D_ATT), g1p, sc2p, sh2p, g_ffn,
        wo_top, _pair_heads(wo_bot, 0), wr_t, rb, cnt0, TM_POST, mod_p)
    x1_s, h2_s, eidx_s, gw_s, rank_s, cnt_all = _post(
        x_s_t.reshape(n_s, D_MODEL), r_s.reshape(n_s, D_RNN), a_s.reshape(n_s, D_ATT), g1s, sc2s, sh2s, g_ffn,
        wo_top, wo_bot, wr_t, rb, cnt_p, bs, mod_s)

    bm = EXPERT_BLOCK
    n_blocks = -(-((n_p + n_s) * TOP_K) // bm) + N_EXPERTS
    pstart, meta = _plan(cnt_all, bm, n_blocks)
    counts = cnt_all[:, 0].astype(I32)
    first_blk = (pstart[:, 0] * (1.0 / bm)).astype(I32)
    n_blk = (counts + (bm - 1)) // bm
    n_used = meta[2, :1]

    def sc_layouts(eidx, rank, gw, n):
        per_w = n // SC_WORKERS
        c = min(DISPATCH_CHUNK, per_w)
        outs = _layouts(pstart, eidx, rank, gw, (c,) if c == per_w else (c, per_w))
        return (outs[0], c), (outs[-2], outs[-1])

    (dpos_p, c_p), comb_p = sc_layouts(eidx_p, rank_p, gw_p, n_p)
    (dpos_s, c_s), comb_s = sc_layouts(eidx_s, rank_s, gw_s, n_s)
    xs = _sc_dispatch((h2_p, dpos_p, c_p), (h2_s, dpos_s, c_s), n_blocks * bm)
    base_p = _shared(x1_p, h2_p, g2p, wsg, wsu, wsd, TM_POST, mod_p)
    base_s = _shared(x1_s, h2_s, g2s, wsg, wsu, wsd, bs, mod_s)
    y = _experts(first_blk, n_blk, counts, n_used, xs, exp_w_gate[0], exp_w_up[0], exp_w_down[0])
    routed_p, routed_s = _sc_combine(y, comb_p, comb_s)
    y_p = _final(base_p, routed_p, g2p, TM_POST, mod_p)
    y_s = _final(base_s, routed_s, g2s, bs, mod_s)

    y_prompt = y_p.reshape(bp, tp, D_MODEL)
    y_sample = jnp.swapaxes(y_s.reshape(ts, bs, D_MODEL), 0, 1)
    wk = min(WINDOW, tp)
    k_win_p = k_p[:, tp - wk:].reshape(1, bp, wk, N_KV_HEADS, HEAD_DIM)
    v_win_p = v_p[:, tp - wk:].reshape(1, bp, wk, N_KV_HEADS, HEAD_DIM)
    k_new = jnp.swapaxes(k_s, 0, 1)
    v_new = jnp.swapaxes(v_s, 0, 1)
    heads = lambda a: a.reshape(bs, ts, N_KV_HEADS, HEAD_DIM)
    k_win_s = jnp.concatenate([cache_k_win[0], heads(k_new)], axis=1)[None, :, ts:]
    v_win_s = jnp.concatenate([cache_v_win[0], heads(v_new)], axis=1)[None, :, ts:]
    return (y_prompt, y_sample, k_win_p, v_win_p, cs_p[None, :, SUBLANES - (CONV_W - 1):], hs_p.reshape(1, bp, D_RNN),
            k_win_s, v_win_s, jnp.swapaxes(cs_s, 0, 1)[None], hs_s[None])
```

```python
import functools

import jax
import jax.numpy as jnp
from jax import lax
from jax.experimental import pallas as pl
from jax.experimental.pallas import tpu as pltpu
from jax.experimental.pallas import tpu_sc as plsc

F32 = jnp.float32
BF16 = jnp.bfloat16
I32 = jnp.int32

D_MODEL = 1024
D_RNN = 512
N_RNN_BLOCKS = 8
CONV_W = 4
LRU_C = 8.0
HEAD_DIM = 64
N_HEADS = 8
N_KV_HEADS = 2
GQA = N_HEADS // N_KV_HEADS
D_ATT = N_HEADS * HEAD_DIM
D_KV = N_KV_HEADS * HEAD_DIM
WINDOW = 128
N_EXPERTS = 64
TOP_K = 6
N_GROUPS = 8
GROUP_SIZE = N_EXPERTS // N_GROUPS
TOPK_GROUPS = 4
D_EXPERT = 256
D_SHARED = 256
ROUTE_SCALE = 2.5
EPS = 1e-6
NEG = -1e30
F32_TINY = 1.1754944e-38
D_IN = 2 * D_RNN + D_ATT + 2 * D_KV

SUBLANES = 8
LANES = 128
TM_PROMPT = 256
TM_POST = 512
ATT_BLOCK = WINDOW
ATT_STEP_BLOCKS = 4
EXPERT_BLOCK = 512
EXPERT_RING = 4
SAMPLE_CHUNK = 8
POS_TILE = 2048
VMEM_LIMIT = 48 * 1024 * 1024

SC_CORES = 2
SC_SUBCORES = 16
SC_WORKERS = SC_CORES * SC_SUBCORES
SC_LANES = 16
DISPATCH_CHUNK = 64
COMBINE_CHUNK = 8
COMBINE_TOKEN_GROUP = 4


def _sigmoid(x):
    return 0.5 * jnp.tanh(0.5 * x) + 0.5


def _silu(x):
    return x * _sigmoid(x)


def _gelu_tanh(x):
    c = 0.7978845608028654
    return x * (0.5 * (1.0 + jnp.tanh(c * (x + 0.044715 * (x * x * x)))))


def _log1p(x):
    u = 1.0 + x
    return jnp.where(u == 1.0, x, jnp.log(u) * x / jnp.where(u == 1.0, 1.0, u - 1.0))


def _neg_expm1_2x(x, exp_x):
    return -jnp.tanh(x) * (exp_x * exp_x + 1.0)


def _softplus(z):
    return jnp.maximum(z, 0.0) + _log1p(jnp.exp(-jnp.abs(z)))


def _div_pow2(x, d):
    assert d & (d - 1) == 0
    return lax.shift_right_logical(x, d.bit_length() - 1)


def _mod_pow2(x, d):
    assert d & (d - 1) == 0
    return x & (d - 1)


def _norm_mod(x, g, sc, sh):
    ms = jnp.mean(x * x, axis=-1, keepdims=True)
    return (x * lax.rsqrt(ms + EPS)) * g * (1.0 + sc) + sh


def _bdot(a, b):
    return jnp.dot(a.astype(BF16), b.astype(BF16), preferred_element_type=F32)


def _bdot_nt(a, b):
    return lax.dot_general(a.astype(BF16), b.astype(BF16), (((1,), (1,)), ((), ())),
                           preferred_element_type=F32)


def _pack_bf16_pairs(x):
    w = x.shape[1] // 2

    def rne_bits(v):
        b = pltpu.bitcast(v, I32)
        return b + 0x7FFF + (lax.shift_right_logical(b, 16) & 1)

    lo = lax.shift_right_logical(rne_bits(x[:, :w]), 16)
    hi = rne_bits(x[:, w:]) & jnp.int32(-65536)
    return lo | hi


def _unpack_bf16_pairs(p):
    lo = pltpu.bitcast(lax.shift_left(p, 16), F32)
    hi = pltpu.bitcast(p & jnp.int32(-65536), F32)
    return lo.astype(BF16), hi.astype(BF16)


def _knorm(k, gk2):
    lane = lax.broadcasted_iota(I32, k.shape, 1)
    first = lane < HEAD_DIM
    k2 = k * k
    s0 = jnp.sum(jnp.where(first, k2, 0.0), axis=-1, keepdims=True)
    s1 = jnp.sum(jnp.where(first, 0.0, k2), axis=-1, keepdims=True)
    ms = jnp.where(first, s0, s1) * (1.0 / HEAD_DIM)
    return k * lax.rsqrt(ms + EPS) * gk2


def _pair_heads(w, axis):
    shape = w.shape
    split = shape[:axis] + (N_KV_HEADS, GQA, HEAD_DIM) + shape[axis + 1:]
    return jnp.swapaxes(w.reshape(split), axis, axis + 1).reshape(shape)


def _lru_coeffs(u, wg, gb, lam):
    g = _bdot(u, wg) + gb
    r = _sigmoid(g[:, :D_RNN])
    i = _sigmoid(g[:, D_RNN:])
    log_a = (-LRU_C * r) * _softplus(-lam)
    a = jnp.exp(log_a)
    om = _neg_expm1_2x(log_a, a)
    b = (om * lax.rsqrt(jnp.maximum(om, F32_TINY))) * (i * u)
    return a, b


def _adaln_kernel(c_ref, w_ref, b_ref, o_ref):
    o_ref[...] = _bdot(_silu(c_ref[...]), w_ref[...]) + b_ref[...]


def _adaln(c_all, ada_w, ada_b):
    n = c_all.shape[0]
    return pl.pallas_call(
        _adaln_kernel,
        grid=(6,),
        in_specs=[pl.BlockSpec((n, D_MODEL), lambda j: (0, 0)),
                  pl.BlockSpec((D_MODEL, D_MODEL), lambda j: (0, j)),
                  pl.BlockSpec((1, D_MODEL), lambda j: (0, j))],
        out_specs=pl.BlockSpec((n, D_MODEL), lambda j: (0, j)),
        out_shape=jax.ShapeDtypeStruct((n, 6 * D_MODEL), F32),
        compiler_params=pltpu.CompilerParams(dimension_semantics=("arbitrary",), vmem_limit_bytes=VMEM_LIMIT),
        name="adaln",
    )(c_all, ada_w, ada_b.reshape(1, -1))


def _scan_rows(a, b, h_in):
    n, c = a.shape
    groups = n // SUBLANES
    a = a.reshape(groups, SUBLANES, c)
    b = b.reshape(groups, SUBLANES, c)
    sub = lax.broadcasted_iota(I32, a.shape, 1)
    s = 1
    while s < SUBLANES:
        m = sub >= s
        a_sh = jnp.where(m, pltpu.roll(a, s, 1), 1.0)
        b_sh = jnp.where(m, pltpu.roll(b, s, 1), 0.0)
        b = a * b_sh + b
        a = a * a_sh
        s *= 2
    carry = h_in
    hs = []
    for g in range(groups):
        hg = a[g] * carry + b[g]
        hs.append(hg)
        carry = hg[SUBLANES - 1:SUBLANES]
    return jnp.concatenate(hs, axis=0)


def _front_prompt_kernel(x_ref, sc_ref, sh_ref, g_ref, win_ref, cw_ref, cb_ref, wg_ref, gb_ref, lam_ref, gk_ref,
                         prev_ref, h0_ref, r_ref, q_ref, k_ref, v_ref, cs_ref, hs_ref, tail_ref, hc_ref):
    j = pl.program_id(1)
    tm = x_ref.shape[0]

    @pl.when(j == 0)
    def _():
        tail_ref[...] = prev_ref[...]
        hc_ref[...] = h0_ref[...]

    h = _norm_mod(x_ref[...], g_ref[...], sc_ref[...], sh_ref[...])
    proj = jnp.dot(h.astype(BF16), win_ref[...], preferred_element_type=F32)
    xr = proj[:, 0:D_RNN]
    yr = proj[:, D_RNN:2 * D_RNN]
    q_ref[...] = proj[:, 2 * D_RNN:2 * D_RNN + D_ATT]
    k_ref[...] = _knorm(proj[:, 2 * D_RNN + D_ATT:2 * D_RNN + D_ATT + D_KV], gk_ref[...])
    v_ref[...] = proj[:, 2 * D_RNN + D_ATT + D_KV:D_IN]

    tail = tail_ref[...]
    row8 = lax.broadcasted_iota(I32, tail.shape, 0)

    def shifted(s):
        rolled = pltpu.roll(xr, s, 0)
        top = jnp.where(row8 < s, pltpu.roll(tail, s, 0), rolled[0:SUBLANES])
        return jnp.concatenate([top, rolled[SUBLANES:]], axis=0)

    cw = cw_ref[...]
    u = cb_ref[...] + shifted(3) * cw[0:1]
    u = u + shifted(2) * cw[1:2]
    u = u + shifted(1) * cw[2:3]
    u = u + xr * cw[3:4]
    tail_ref[...] = xr[tm - SUBLANES:tm]

    a, b = _lru_coeffs(u, wg_ref[...], gb_ref[...], lam_ref[...])
    hs = _scan_rows(a, b, hc_ref[...])
    hc_ref[...] = hs[tm - 1:tm]
    r_ref[...] = hs * _gelu_tanh(yr)

    @pl.when(j == pl.num_programs(1) - 1)
    def _():
        cs_ref[...] = xr[tm - SUBLANES:tm]
        hs_ref[...] = hs[tm - 1:tm]


def _front_prompt(x, sc, sh, g, win, cw, cb, wg, gb, lam, gk, prev, h0):
    bsz, t, _ = x.shape
    tm = TM_PROMPT
    full = lambda a: pl.BlockSpec(a.shape, lambda b, j: (0,) * a.ndim)
    per_b = lambda a: pl.BlockSpec((None,) + a.shape[1:], lambda b, j: (b,) + (0,) * (a.ndim - 1))
    tile = lambda w: pl.BlockSpec((None, tm, w), lambda b, j: (b, j, 0))
    return pl.pallas_call(
        _front_prompt_kernel,
        grid=(bsz, t // tm),
        in_specs=[tile(D_MODEL), per_b(sc), per_b(sh), full(g), full(win), full(cw), full(cb), full(wg), full(gb),
                  full(lam), full(gk), per_b(prev), per_b(h0)],
        out_specs=[tile(D_RNN), tile(D_ATT), tile(D_KV), tile(D_KV),
                   pl.BlockSpec((None, SUBLANES, D_RNN), lambda b, j: (b, 0, 0)),
                   pl.BlockSpec((None, 1, D_RNN), lambda b, j: (b, 0, 0))],
        out_shape=[jax.ShapeDtypeStruct((bsz, t, D_RNN), F32), jax.ShapeDtypeStruct((bsz, t, D_ATT), F32),
                   jax.ShapeDtypeStruct((bsz, t, D_KV), F32), jax.ShapeDtypeStruct((bsz, t, D_KV), F32),
                   jax.ShapeDtypeStruct((bsz, SUBLANES, D_RNN), F32), jax.ShapeDtypeStruct((bsz, 1, D_RNN), F32)],
        scratch_shapes=[pltpu.VMEM((SUBLANES, D_RNN), F32), pltpu.VMEM((1, D_RNN), F32)],
        compiler_params=pltpu.CompilerParams(dimension_semantics=("arbitrary", "arbitrary"),
                                             vmem_limit_bytes=VMEM_LIMIT),
        name="front_prompt",
    )(x, sc, sh, g, win, cw, cb, wg, gb, lam, gk, prev, h0)


def _front_sample_kernel(x_ref, sc_ref, sh_ref, g_ref, win_ref, cw_ref, cb_ref, wg_ref, gb_ref, lam_ref, gk_ref,
                         prev_ref, h0_ref, r_ref, q_ref, k_ref, v_ref, cs_ref, hs_ref):
    t_len, bsz, _ = x_ref.shape
    x = x_ref[...]
    ms = jnp.mean(x * x, axis=-1, keepdims=True)
    h = (x * lax.rsqrt(ms + EPS)) * g_ref[...] * (1.0 + sc_ref[...]) + sh_ref[...]
    proj = jnp.dot(h.reshape(t_len * bsz, D_MODEL).astype(BF16), win_ref[...], preferred_element_type=F32)
    xr = proj[:, 0:D_RNN]
    yr = proj[:, D_RNN:2 * D_RNN]
    q_ref[...] = proj[:, 2 * D_RNN:2 * D_RNN + D_ATT].reshape(t_len, bsz, D_ATT)
    k_ref[...] = _knorm(proj[:, 2 * D_RNN + D_ATT:2 * D_RNN + D_ATT + D_KV], gk_ref[...]).reshape(t_len, bsz, D_KV)
    v_ref[...] = proj[:, 2 * D_RNN + D_ATT + D_KV:D_IN].reshape(t_len, bsz, D_KV)

    def at_time(t):
        if t >= 0:
            return xr[t * bsz:(t + 1) * bsz]
        return prev_ref[CONV_W - 1 + t]

    cw = cw_ref[...]
    us = []
    for t in range(t_len):
        u = cb_ref[...] + at_time(t - 3) * cw[0:1]
        u = u + at_time(t - 2) * cw[1:2]
        u = u + at_time(t - 1) * cw[2:3]
        u = u + at_time(t) * cw[3:4]
        us.append(u)
    a, b = _lru_coeffs(jnp.concatenate(us, axis=0), wg_ref[...], gb_ref[...], lam_ref[...])
    hcur = h0_ref[...]
    for t in range(t_len):
        hcur = a[t * bsz:(t + 1) * bsz] * hcur + b[t * bsz:(t + 1) * bsz]
        r_ref[t] = hcur * _gelu_tanh(yr[t * bsz:(t + 1) * bsz])
    hs_ref[...] = hcur
    for s in range(CONV_W - 1):
        cs_ref[s] = at_time(t_len - (CONV_W - 1) + s)


def _front_sample(x_t, sc, sh, g, win, cw, cb, wg, gb, lam, gk, prev_t, h0):
    t_len, bsz, _ = x_t.shape
    return pl.pallas_call(
        _front_sample_kernel,
        out_shape=[jax.ShapeDtypeStruct((t_len, bsz, D_RNN), F32), jax.ShapeDtypeStruct((t_len, bsz, D_ATT), F32),
                   jax.ShapeDtypeStruct((t_len, bsz, D_KV), F32), jax.ShapeDtypeStruct((t_len, bsz, D_KV), F32),
                   jax.ShapeDtypeStruct((CONV_W - 1, bsz, D_RNN), F32), jax.ShapeDtypeStruct((bsz, D_RNN), F32)],
        compiler_params=pltpu.CompilerParams(vmem_limit_bytes=VMEM_LIMIT),
        name="front_sample",
    )(x_t, sc, sh, g, win, cw, cb, wg, gb, lam, gk, prev_t, h0)


def _qnorm(q, gq):
    ms = jnp.mean(q * q, axis=-1, keepdims=True)
    return q * lax.rsqrt(ms + EPS) * gq


def _attn_prompt_kernel(sink_ref, q_ref, kp_ref, kc_ref, vp_ref, vc_ref, gq_ref, o_ref):
    j = pl.program_id(1)
    blk = kp_ref.shape[0]
    k_all = jnp.concatenate([kp_ref[...], kc_ref[...]], axis=0)
    v_all = jnp.concatenate([vp_ref[...], vc_ref[...]], axis=0)
    qi = lax.broadcasted_iota(I32, (blk, 2 * blk), 0)
    kj = lax.broadcasted_iota(I32, (blk, 2 * blk), 1)
    dist = blk + qi - kj
    window = (dist >= 0) & (dist <= WINDOW)
    distf = dist.astype(F32)
    slab = 2 * HEAD_DIM
    first_q = lax.broadcasted_iota(I32, (blk, slab), 1) < HEAD_DIM
    first_kv = lax.broadcasted_iota(I32, (2 * blk, slab), 1) < HEAD_DIM

    bias = [jnp.where(window, (-(2.0 ** -(h + 1))) * distf, NEG) for h in range(N_HEADS)]
    no_prev = (kj < blk) & (j == 0)
    gq = gq_ref[...] * (HEAD_DIM ** -0.5)

    def probs(s, h, first_block):
        b = jnp.where(no_prev, NEG, bias[h]) if first_block else bias[h]
        s = s + b
        sink = sink_ref[h]
        m = jnp.maximum(jnp.max(s, axis=-1, keepdims=True), sink)
        p = jnp.exp(s - m)
        return p, jnp.sum(p, axis=-1, keepdims=True) + jnp.exp(sink - m)

    for sub in range(q_ref.shape[0] // blk):
        first = sub == 0
        q = q_ref[sub * blk:(sub + 1) * blk, :]
        kk = k_all[sub * blk:(sub + 2) * blk]
        vv = v_all[sub * blk:(sub + 2) * blk]
        v_a = jnp.where(first_kv, vv, 0.0)
        v_b = jnp.where(first_kv, 0.0, vv)
        outs = []
        for i in range(GQA):
            q2 = _knorm(q[:, i * slab:(i + 1) * slab], gq)
            p_a, l_a = probs(_bdot_nt(jnp.where(first_q, q2, 0.0), kk), i, first)
            p_b, l_b = probs(_bdot_nt(jnp.where(first_q, 0.0, q2), kk), GQA + i, first)
            outs.append((_bdot(p_a, v_a) + _bdot(p_b, v_b)) / jnp.where(first_q, l_a, l_b))
        o_ref[sub * blk:(sub + 1) * blk, :] = jnp.concatenate(outs, axis=1)


def _attn_prompt(q, k, v, gq, sinks):
    bsz, t, _ = q.shape
    blk = ATT_BLOCK
    nsub = ATT_STEP_BLOCKS
    cur = lambda w: pl.BlockSpec((None, nsub * blk, w), lambda b, j: (b, j, 0))
    prv = lambda w: pl.BlockSpec((None, blk, w), lambda b, j: (b, jnp.maximum(nsub * j - 1, 0), 0))
    return pl.pallas_call(
        _attn_prompt_kernel,
        grid=(bsz, t // (nsub * blk)),
        in_specs=[pl.BlockSpec(memory_space=pltpu.SMEM), cur(D_ATT), prv(D_KV), cur(D_KV), prv(D_KV), cur(D_KV),
                  pl.BlockSpec(gq.shape, lambda b, j: (0, 0))],
        out_specs=cur(D_ATT),
        out_shape=jax.ShapeDtypeStruct((bsz, t, D_ATT), F32),
        compiler_params=pltpu.CompilerParams(dimension_semantics=("arbitrary", "arbitrary"),
                                             vmem_limit_bytes=VMEM_LIMIT),
        name="attn_prompt",
    )(sinks, q, k, k, v, v, gq)


def _attn_sample_kernel(sink_ref, q_ref, kn_ref, vn_ref, kc_ref, vc_ref, gq_ref, o_ref):
    t_len, cb, _ = q_ref.shape
    win = kc_ref.shape[1]
    rows = GQA * t_len * cb
    kc = kc_ref[...].reshape(cb * win, D_KV)
    vc = vc_ref[...].reshape(cb * win, D_KV)
    kn = kn_ref[...].reshape(t_len * cb, D_KV)
    vn = vn_ref[...].reshape(t_len * cb, D_KV)

    r_c = lax.broadcasted_iota(I32, (rows, cb * win), 0)
    c_c = lax.broadcasted_iota(I32, (rows, cb * win), 1)
    t_c = _div_pow2(_mod_pow2(r_c, t_len * cb), cb)
    valid_c = (_mod_pow2(r_c, cb) == _div_pow2(c_c, win)) & (_mod_pow2(c_c, win) >= t_c)
    dist_c = (win + t_c - _mod_pow2(c_c, win)).astype(F32)
    r_n = lax.broadcasted_iota(I32, (rows, t_len * cb), 0)
    c_n = lax.broadcasted_iota(I32, (rows, t_len * cb), 1)
    t_n = _div_pow2(_mod_pow2(r_n, t_len * cb), cb)
    valid_n = (_mod_pow2(r_n, cb) == _mod_pow2(c_n, cb)) & (_div_pow2(c_n, cb) <= t_n)
    dist_n = (t_n - _div_pow2(c_n, cb)).astype(F32)
    hl = _div_pow2(lax.broadcasted_iota(I32, (rows, 1), 0), t_len * cb)

    per_group = []
    for g in range(N_KV_HEADS):
        slabs = [q_ref[t][:, (g * GQA + i) * HEAD_DIM:(g * GQA + i + 1) * HEAD_DIM]
                 for i in range(GQA) for t in range(t_len)]
        qg = _qnorm(jnp.concatenate(slabs, axis=0), gq_ref[...])
        slope = jnp.zeros((rows, 1), F32)
        sink = jnp.zeros((rows, 1), F32)
        for i in range(GQA):
            slope = jnp.where(hl == i, 2.0 ** -(g * GQA + i + 1), slope)
            sink = jnp.where(hl == i, sink_ref[g * GQA + i], sink)
        lo, hi = g * HEAD_DIM, (g + 1) * HEAD_DIM
        s_c = _bdot_nt(qg, kc[:, lo:hi]) * (HEAD_DIM ** -0.5) - slope * dist_c
        s_n = _bdot_nt(qg, kn[:, lo:hi]) * (HEAD_DIM ** -0.5) - slope * dist_n
        s_c = jnp.where(valid_c, s_c, NEG)
        s_n = jnp.where(valid_n, s_n, NEG)
        m = jnp.maximum(jnp.maximum(jnp.max(s_c, axis=-1, keepdims=True), jnp.max(s_n, axis=-1, keepdims=True)), sink)
        p_c = jnp.exp(s_c - m)
        p_n = jnp.exp(s_n - m)
        l = jnp.sum(p_c, axis=-1, keepdims=True) + jnp.sum(p_n, axis=-1, keepdims=True) + jnp.exp(sink - m)
        per_group.append((_bdot(p_c, vc[:, lo:hi]) + _bdot(p_n, vn[:, lo:hi])) / l)
    for t in range(t_len):
        o_ref[t] = jnp.concatenate(
            [per_group[g][(i * t_len + t) * cb:(i * t_len + t + 1) * cb] for g in range(N_KV_HEADS) for i in range(GQA)],
            axis=1)


def _attn_sample(q_t, k_t, v_t, cache_k, cache_v, gq, sinks):
    t_len, bsz, _ = q_t.shape
    cb = SAMPLE_CHUNK
    win = cache_k.shape[1]
    new = lambda w: pl.BlockSpec((t_len, cb, w), lambda c: (0, c, 0))
    old = pl.BlockSpec((cb, win, D_KV), lambda c: (c, 0, 0))
    return pl.pallas_call(
        _attn_sample_kernel,
        grid=(bsz // cb,),
        in_specs=[pl.BlockSpec(memory_space=pltpu.SMEM), new(D_ATT), new(D_KV), new(D_KV), old, old,
                  pl.BlockSpec(gq.shape, lambda c: (0, 0))],
        out_specs=new(D_ATT),
        out_shape=jax.ShapeDtypeStruct((t_len, bsz, D_ATT), F32),
        compiler_params=pltpu.CompilerParams(dimension_semantics=("arbitrary",), vmem_limit_bytes=VMEM_LIMIT),
        name="attn_sample",
    )(sinks, q_t, k_t, v_t, cache_k, cache_v, gq)


def _route(s_t, sb_t):
    tm = s_t.shape[1]
    i8 = lax.broadcasted_iota(I32, (GROUP_SIZE, tm), 0)
    ninf = -jnp.inf
    sg = [sb_t[GROUP_SIZE * g:GROUP_SIZE * (g + 1)] for g in range(N_GROUPS)]
    gscore = []
    for g in range(N_GROUPS):
        m1 = jnp.max(sg[g], axis=0, keepdims=True)
        i1 = jnp.min(jnp.where(sg[g] == m1, i8, GROUP_SIZE), axis=0, keepdims=True)
        m2 = jnp.max(jnp.where(i8 == i1, ninf, sg[g]), axis=0, keepdims=True)
        gscore.append(m1 + m2)
    gs = jnp.concatenate(gscore, axis=0)
    gsel = jnp.zeros((N_GROUPS, tm), I32)
    for _ in range(TOPK_GROUPS):
        m = jnp.max(gs, axis=0, keepdims=True)
        idx = jnp.min(jnp.where(gs == m, i8, N_GROUPS), axis=0, keepdims=True)
        hit = i8 == idx
        gsel = jnp.where(hit, 1, gsel)
        gs = jnp.where(hit, ninf, gs)
    sm = [jnp.where(gsel[g:g + 1] > 0, sg[g], NEG) for g in range(N_GROUPS)]
    eid = [i8 + GROUP_SIZE * g for g in range(N_GROUPS)]
    sel = [jnp.zeros((GROUP_SIZE, tm), F32) for _ in range(N_GROUPS)]
    idxs, ws = [], []
    for _ in range(TOP_K):
        cm = functools.reduce(jnp.maximum, sm)
        m = jnp.max(cm, axis=0, keepdims=True)
        cand = functools.reduce(jnp.minimum, [jnp.where(sm[g] == m, eid[g], N_EXPERTS) for g in range(N_GROUPS)])
        idx = jnp.min(cand, axis=0, keepdims=True)
        wk = jnp.zeros((GROUP_SIZE, tm), F32)
        for g in range(N_GROUPS):
            hit = eid[g] == idx
            wk = wk + jnp.where(hit, s_t[GROUP_SIZE * g:GROUP_SIZE * (g + 1)], 0.0)
            sel[g] = jnp.where(hit, 1.0, sel[g])
            sm[g] = jnp.where(hit, ninf, sm[g])
        idxs.append(idx)
        ws.append(jnp.sum(wk, axis=0, keepdims=True))
    return idxs, ws, jnp.concatenate(sel, axis=0), eid


def _post_kernel(x_ref, r_ref, a_ref, g1_ref, sc2_ref, sh2_ref, nf_ref, wot_ref, wob_ref, wrt_ref, rb_ref, cin_ref,
                 x1_ref, h2_ref, eidx_ref, gw_ref, rank_ref, cnt_ref, carry_ref, before_ref):
    tm = x_ref.shape[0]

    @pl.when(pl.program_id(0) == 0)
    def _():
        carry_ref[...] = cin_ref[...]
        rr = lax.broadcasted_iota(I32, (tm, tm), 0)
        cc = lax.broadcasted_iota(I32, (tm, tm), 1)
        before_ref[...] = jnp.where(rr < cc, 1.0, 0.0).astype(BF16)

    mixed = (jnp.dot(r_ref[...].astype(BF16), wot_ref[...], preferred_element_type=F32)
             + jnp.dot(a_ref[...].astype(BF16), wob_ref[...], preferred_element_type=F32))
    x1 = x_ref[...] + g1_ref[...] * mixed
    x1_ref[...] = x1
    h2 = _norm_mod(x1, nf_ref[...], sc2_ref[...], sh2_ref[...])
    h2_ref[...] = _pack_bf16_pairs(h2)

    wr = wrt_ref[...]
    wr_hi = wr.astype(BF16)
    wr_lo = (wr - wr_hi.astype(F32)).astype(BF16)
    h_hi = h2.astype(BF16)
    h_lo = (h2 - h_hi.astype(F32)).astype(BF16)
    logits = _bdot_nt(wr_hi, h_hi) + _bdot_nt(wr_hi, h_lo) + _bdot_nt(wr_lo, h_hi)
    s_t = _sigmoid(logits)
    idxs, ws, sel, eid = _route(s_t, s_t + rb_ref[...])

    carry = carry_ref[...]
    tot = jnp.dot(sel.astype(BF16), before_ref[...], preferred_element_type=F32) + carry[:, 0:1]
    ranks = []
    for k in range(TOP_K):
        acc = jnp.zeros((GROUP_SIZE, tm), F32)
        for g in range(N_GROUPS):
            acc = acc + jnp.where(eid[g] == idxs[k], tot[GROUP_SIZE * g:GROUP_SIZE * (g + 1)], 0.0)
        ranks.append(jnp.sum(acc, axis=0, keepdims=True))
    carry = carry + jnp.sum(sel, axis=1, keepdims=True)
    carry_ref[...] = carry
    cnt_ref[...] = carry

    wsum = functools.reduce(lambda p, q: p + q, ws)
    pad_i = jnp.zeros((SUBLANES - TOP_K, tm), I32)
    pad_f = jnp.zeros((SUBLANES - TOP_K, tm), F32)
    eidx_ref[...] = jnp.concatenate(idxs + [pad_i], axis=0)
    rank_ref[...] = jnp.concatenate([r.astype(I32) for r in ranks] + [pad_i], axis=0)
    gw_ref[...] = jnp.concatenate([w / wsum * ROUTE_SCALE for w in ws] + [pad_f], axis=0)


def _post(x, r, a, g1, sc2, sh2, nf, wo_top, wo_bot, wr_t, rb, cnt_in, tm, mod_spec):
    n = x.shape[0]
    tile = lambda w: pl.BlockSpec((tm, w), lambda i: (i, 0))
    full = lambda arr: pl.BlockSpec(arr.shape, lambda i: (0,) * arr.ndim)
    slot = pl.BlockSpec((SUBLANES, tm), lambda i: (0, i))
    return pl.pallas_call(
        _post_kernel,
        grid=(n // tm,),
        in_specs=[tile(D_MODEL), tile(D_RNN), tile(D_ATT), mod_spec, mod_spec, mod_spec, full(nf), full(wo_top),
                  full(wo_bot), full(wr_t), full(rb), full(cnt_in)],
        out_specs=[tile(D_MODEL), tile(D_MODEL // 2), slot, slot, slot, full(cnt_in)],
        out_shape=[jax.ShapeDtypeStruct((n, D_MODEL), F32), jax.ShapeDtypeStruct((n, D_MODEL // 2), I32),
                   jax.ShapeDtypeStruct((SUBLANES, n), I32), jax.ShapeDtypeStruct((SUBLANES, n), F32),
                   jax.ShapeDtypeStruct((SUBLANES, n), I32), jax.ShapeDtypeStruct(cnt_in.shape, F32)],
        scratch_shapes=[pltpu.VMEM(cnt_in.shape, F32), pltpu.VMEM((tm, tm), BF16)],
        compiler_params=pltpu.CompilerParams(dimension_semantics=("arbitrary",), vmem_limit_bytes=VMEM_LIMIT),
        name="post_mix",
    )(x, r, a, g1, sc2, sh2, nf, wo_top, wo_bot, wr_t, rb, cnt_in)


def _plan_kernel(cnt_ref, ps_ref, meta_ref, *, bm):
    cnt = cnt_ref[...]
    padded = jnp.ceil(cnt * (1.0 / bm)) * bm
    row = lax.broadcasted_iota(I32, cnt.shape, 0)
    pend = padded
    s = 1
    while s < N_EXPERTS:
        pend = pend + jnp.where(row >= s, pltpu.roll(pend, s, 0), 0.0)
        s *= 2
    pstart = pend - padded
    ps_ref[...] = pstart
    nb = meta_ref.shape[1]
    start = lax.broadcasted_iota(I32, (N_EXPERTS, nb), 1).astype(F32) * bm
    blk_e = jnp.minimum(jnp.sum(jnp.where(pend[:, 0:1] <= start, 1.0, 0.0), axis=0, keepdims=True), N_EXPERTS - 1.0)
    e_iota = lax.broadcasted_iota(I32, (N_EXPERTS, nb), 0).astype(F32)
    end_b = jnp.sum(jnp.where(e_iota == blk_e, (pstart + cnt)[:, 0:1], 0.0), axis=0, keepdims=True)
    n_valid = jnp.clip(end_b - start[0:1], 0.0, bm)
    n_used = jnp.broadcast_to(pend[N_EXPERTS - 1:N_EXPERTS, 0:1] * (1.0 / bm), (1, nb))
    meta_ref[...] = jnp.concatenate([blk_e.astype(I32), n_valid.astype(I32), n_used.astype(I32),
                                     jnp.zeros((SUBLANES - 3, nb), I32)], axis=0)


def _plan(cnt, bm, n_blocks):
    assert bm & (bm - 1) == 0
    nb = -(-n_blocks // LANES) * LANES
    return pl.pallas_call(
        functools.partial(_plan_kernel, bm=bm),
        out_shape=[jax.ShapeDtypeStruct(cnt.shape, F32), jax.ShapeDtypeStruct((SUBLANES, nb), I32)],
        name="moe_plan",
    )(cnt)


def _layout_kernel(ps_ref, eidx_ref, rank_ref, gw_ref, *out_refs, chunks):
    *pos_refs, w_ref = out_refs
    tn = eidx_ref.shape[1]
    e_iota = lax.broadcasted_iota(I32, (N_EXPERTS, tn), 0)
    ps = ps_ref[...][:, 0:1]
    rows = []
    for k in range(TOP_K):
        hit = e_iota == eidx_ref[k:k + 1, :]
        base = jnp.sum(jnp.where(hit, ps, 0.0), axis=0, keepdims=True)
        rows.append(base.astype(I32) + rank_ref[k:k + 1, :])
    rows.append(jnp.zeros((SUBLANES - TOP_K, tn), I32))
    pos = jnp.concatenate(rows, axis=0)
    for pos_ref, c in zip(pos_refs, chunks):
        for q in range(tn // c):
            pos_ref[q] = pos[:, q * c:(q + 1) * c]
    gw = gw_ref[...]
    rep = jnp.concatenate([jnp.broadcast_to(gw[k:k + 1], (SC_LANES, tn)) for k in range(SUBLANES)], axis=0)
    w_ref[...] = rep.T


def _layouts(pstart, eidx, rank, gw, chunks):
    n = eidx.shape[1]
    tn = min(n, POS_TILE)
    slot = pl.BlockSpec((SUBLANES, tn), lambda i: (0, i))
    return pl.pallas_call(
        functools.partial(_layout_kernel, chunks=chunks),
        grid=(n // tn,),
        in_specs=[pl.BlockSpec(pstart.shape, lambda i: (0, 0)), slot, slot, slot],
        out_specs=[pl.BlockSpec((tn // c, SUBLANES, c), lambda i: (i, 0, 0)) for c in chunks]
        + [pl.BlockSpec((tn, SUBLANES * SC_LANES), lambda i: (i, 0))],
        out_shape=[jax.ShapeDtypeStruct((n // c, SUBLANES, c), I32) for c in chunks]
        + [jax.ShapeDtypeStruct((n, SUBLANES * SC_LANES), F32)],
        compiler_params=pltpu.CompilerParams(dimension_semantics=("arbitrary",), vmem_limit_bytes=VMEM_LIMIT),
        name="moe_layout",
    )(pstart, eidx, rank, gw)


def _sc_worker_id():
    return lax.axis_index("s") * SC_CORES + lax.axis_index("c")


def _sc_mesh():
    return plsc.VectorSubcoreMesh(core_axis_name="c", subcore_axis_name="s")


def _sc_dispatch(seg_a, seg_b, total_rows):
    (h_a, pos_a, c_a), (h_b, pos_b, c_b) = seg_a, seg_b
    width = h_a.shape[1]

    @functools.partial(
        pl.kernel, mesh=_sc_mesh(), out_type=jax.ShapeDtypeStruct((total_rows, width), I32),
        scratch_types=[pltpu.VMEM((2, SUBLANES, c_a), I32), pltpu.VMEM((2, c_a, width), I32),
                       pltpu.VMEM((2, SUBLANES, c_b), I32), pltpu.VMEM((2, c_b, width), I32),
                       pltpu.SemaphoreType.DMA((2,)), pltpu.SemaphoreType.DMA((2,))])
    def run(ha_hbm, pa_hbm, hb_hbm, pb_hbm, xs_hbm, idx_a, rows_a, idx_b, rows_b, in_sems, out_sems):
        wid = _sc_worker_id()

        def segment(h_hbm, p_hbm, idx_v, rows_v, c):
            nch = h_hbm.shape[0] // (SC_WORKERS * c)
            assert nch == 1 or nch % 2 == 0
            chunk0 = wid * nch

            def loads(ci, b):
                return [pltpu.make_async_copy(p_hbm.at[chunk0 + ci], idx_v.at[b], in_sems.at[b]),
                        pltpu.make_async_copy(h_hbm.at[pl.ds((chunk0 + ci) * c, c)], rows_v.at[b], in_sems.at[b])]

            def scatters(b):
                return [pltpu.make_async_copy(rows_v.at[b], xs_hbm.at[idx_v.at[b].at[k]], out_sems.at[b])
                        for k in range(TOP_K)]

            def start(copies):
                for cp in copies:
                    cp.start()

            def wait(copies):
                for cp in copies:
                    cp.wait()

            start(loads(0, 0))
            if nch == 1:
                wait(loads(0, 0))
                start(scatters(0))
                wait(scatters(0))
                return

            @pl.loop(0, nch, step=2)
            def _(ci):
                for b in range(2):
                    wait(loads(ci + b, b))
                    start(scatters(b))

                    @pl.when(ci + b + 1 < nch)
                    def _():
                        @pl.when(ci + b >= 1)
                        def _():
                            wait(scatters(1 - b))

                        start(loads(ci + b + 1, 1 - b))

            wait(scatters(0))
            wait(scatters(1))

        segment(ha_hbm, pa_hbm, idx_a, rows_a, c_a)
        segment(hb_hbm, pb_hbm, idx_b, rows_b, c_b)

    return run(h_a, pos_a, h_b, pos_b)


def _experts_kernel(sb_ref, nb_ref, cnt_ref, nu_ref, xs_hbm, wg_ref, wu_ref, wd_ref, y_hbm,
                    xbuf, ybuf, wgb, wub, wdb, in_sems, out_sems):
    e = pl.program_id(0)
    ring, bm, half = xbuf.shape
    n_used = nu_ref[0]

    def slot_of(g):
        return g & (ring - 1)

    def in_copy(g):
        s = slot_of(g)
        return pltpu.make_async_copy(xs_hbm.at[pl.ds(pl.multiple_of(g * bm, bm), bm), :], xbuf.at[s], in_sems.at[s])

    def out_copy(g):
        s = slot_of(g)
        return pltpu.make_async_copy(ybuf.at[s], y_hbm.at[pl.ds(pl.multiple_of(g * bm, bm), bm), :], out_sems.at[s])

    @pl.when(e == 0)
    def _():
        for g in range(ring - 1):
            @pl.when(g < n_used)
            def _():
                in_copy(g).start()

    wgb[...] = wg_ref[...].astype(BF16)
    wub[...] = wu_ref[...].astype(BF16)
    wdb[...] = wd_ref[...].astype(BF16)
    row = lax.broadcasted_iota(I32, (bm, half), 0)

    def block(b, carry):
        g = sb_ref[e] + b

        @pl.when(g + ring - 1 < n_used)
        def _():
            in_copy(g + ring - 1).start()

        in_copy(g).wait()

        @pl.when(g >= ring)
        def _():
            out_copy(g - ring).wait()

        s = slot_of(g)
        x_lo, x_hi = _unpack_bf16_pairs(jnp.where(row < cnt_ref[e] - b * bm, xbuf[s], 0))
        a = (jnp.dot(x_lo, wgb[:half], preferred_element_type=F32)
             + jnp.dot(x_hi, wgb[half:], preferred_element_type=F32))
        u = (jnp.dot(x_lo, wub[:half], preferred_element_type=F32)
             + jnp.dot(x_hi, wub[half:], preferred_element_type=F32))
        ybuf[s] = jnp.dot((_silu(a) * u).astype(BF16), wdb[...], preferred_element_type=F32)
        out_copy(g).start()
        return carry

    lax.fori_loop(0, nb_ref[e], block, 0)

    @pl.when(e == pl.num_programs(0) - 1)
    def _():
        for r in range(ring):
            @pl.when(n_used - 1 - r >= 0)
            def _():
                out_copy(n_used - 1 - r).wait()


def _experts(first_blk, n_blk, counts, n_used, xs, wg, wu, wd):
    rows = xs.shape[0]
    bm, ring = EXPERT_BLOCK, EXPERT_RING
    assert ring & (ring - 1) == 0
    w_blk = lambda e, *_: (e, 0, 0)
    grid_spec = pltpu.PrefetchScalarGridSpec(
        num_scalar_prefetch=4,
        grid=(N_EXPERTS,),
        in_specs=[pl.BlockSpec(memory_space=pl.ANY),
                  pl.BlockSpec((None, D_MODEL, D_EXPERT), w_blk),
                  pl.BlockSpec((None, D_MODEL, D_EXPERT), w_blk),
                  pl.BlockSpec((None, D_EXPERT, D_MODEL), w_blk)],
        out_specs=pl.BlockSpec(memory_space=pl.ANY),
        scratch_shapes=[pltpu.VMEM((ring, bm, D_MODEL // 2), I32), pltpu.VMEM((ring, bm, D_MODEL), F32),
                        pltpu.VMEM((D_MODEL, D_EXPERT), BF16), pltpu.VMEM((D_MODEL, D_EXPERT), BF16),
                        pltpu.VMEM((D_EXPERT, D_MODEL), BF16),
                        pltpu.SemaphoreType.DMA((ring,)), pltpu.SemaphoreType.DMA((ring,))],
    )
    return pl.pallas_call(
        _experts_kernel,
        grid_spec=grid_spec,
        out_shape=jax.ShapeDtypeStruct((rows, D_MODEL), F32),
        compiler_params=pltpu.CompilerParams(dimension_semantics=("arbitrary",), vmem_limit_bytes=VMEM_LIMIT),
        name="moe_experts",
    )(first_blk, n_blk, counts, n_used, xs, wg, wu, wd)


def _sc_combine(y, seg_a, seg_b):
    (pos_a, w_a), (pos_b, w_b) = seg_a, seg_b
    c = COMBINE_CHUNK
    d = y.shape[1]
    n_a, n_b = w_a.shape[0], w_b.shape[0]

    @functools.partial(
        pl.kernel, mesh=_sc_mesh(),
        out_type=[jax.ShapeDtypeStruct((n_a, d), F32), jax.ShapeDtypeStruct((n_b, d), F32)],
        scratch_types=[pltpu.VMEM(pos_a.shape[1:], I32), pltpu.VMEM(pos_b.shape[1:], I32),
                       pltpu.VMEM((2, c, SUBLANES * SC_LANES), F32), pltpu.VMEM((2, TOP_K, c, d), F32),
                       pltpu.VMEM((c, d), F32), pltpu.SemaphoreType.DMA((2,)), pltpu.SemaphoreType.DMA])
    def run(y_hbm, pa_hbm, wa_hbm, pb_hbm, wb_hbm, oa_hbm, ob_hbm, idx_a, idx_b, w_v, buf_v, out_v, sems, out_sem):
        wid = _sc_worker_id()

        def segment(p_hbm, w_hbm, o_hbm, idx_v):
            per_w = idx_v.shape[1]
            nch = per_w // c
            base = wid * per_w
            pltpu.sync_copy(p_hbm.at[wid], idx_v)

            def in_copies(ci, b):
                w_copy = pltpu.make_async_copy(w_hbm.at[pl.ds(base + ci * c, c)], w_v.at[b], sems.at[b])
                return [w_copy] + [
                    pltpu.make_async_copy(y_hbm.at[idx_v.at[k, pl.ds(ci * c, c)]], buf_v.at[b].at[k], sems.at[b])
                    for k in range(TOP_K)]

            def out_copy(ci):
                return pltpu.make_async_copy(out_v, o_hbm.at[pl.ds(base + ci * c, c)], out_sem)

            def reduce_rows(ci, b):
                @pl.when(ci >= 1)
                def _():
                    out_copy(ci - 1).wait()

                for t0 in range(0, c, COMBINE_TOKEN_GROUP):
                    group = range(t0, t0 + COMBINE_TOKEN_GROUP)
                    ws = [[w_v[b, t, pl.ds(k * SC_LANES, SC_LANES)] for k in range(TOP_K)] for t in group]

                    @plsc.parallel_loop(0, d // SC_LANES, unroll=2)
                    def _(j):
                        lanes = pl.ds(pl.multiple_of(j * SC_LANES, SC_LANES), SC_LANES)
                        for t, wt in zip(group, ws):
                            acc = buf_v[b, 0, t, lanes] * wt[0]
                            for k in range(1, TOP_K):
                                acc = acc + buf_v[b, k, t, lanes] * wt[k]
                            out_v[t, lanes] = acc

                out_copy(ci).start()

            for cp in in_copies(0, 0):
                cp.start()

            @pl.loop(0, nch, step=2)
            def _(ci):
                for b in range(2):
                    @pl.when(ci + b + 1 < nch)
                    def _():
                        for cp in in_copies(ci + b + 1, 1 - b):
                            cp.start()

                    for cp in in_copies(ci + b, b):
                        cp.wait()
                    reduce_rows(ci + b, b)

            out_copy(nch - 1).wait()

        segment(pa_hbm, wa_hbm, oa_hbm, idx_a)
        segment(pb_hbm, wb_hbm, ob_hbm, idx_b)

    return run(y, pos_a, w_a, pos_b, w_b)


def _shared_kernel(x1_ref, h2_ref, g2_ref, wsg_ref, wsu_ref, wsd_ref, o_ref):
    half = h2_ref.shape[1]
    x_lo, x_hi = _unpack_bf16_pairs(h2_ref[...])
    wsg, wsu = wsg_ref[...], wsu_ref[...]
    a = (jnp.dot(x_lo, wsg[:half], preferred_element_type=F32) + jnp.dot(x_hi, wsg[half:], preferred_element_type=F32))
    b = (jnp.dot(x_lo, wsu[:half], preferred_element_type=F32) + jnp.dot(x_hi, wsu[half:], preferred_element_type=F32))
    shared = jnp.dot((_silu(a) * b).astype(BF16), wsd_ref[...], preferred_element_type=F32)
    o_ref[...] = x1_ref[...] + g2_ref[...] * shared


def _shared(x1, h2, g2, wsg, wsu, wsd, tm, mod_spec):
    n = h2.shape[0]
    full = lambda arr: pl.BlockSpec(arr.shape, lambda i: (0,) * arr.ndim)
    tile = pl.BlockSpec((tm, D_MODEL), lambda i: (i, 0))
    return pl.pallas_call(
        _shared_kernel,
        grid=(n // tm,),
        in_specs=[tile, pl.BlockSpec((tm, D_MODEL // 2), lambda i: (i, 0)), mod_spec, full(wsg), full(wsu), full(wsd)],
        out_specs=tile,
        out_shape=jax.ShapeDtypeStruct((n, D_MODEL), F32),
        compiler_params=pltpu.CompilerParams(dimension_semantics=("arbitrary",), vmem_limit_bytes=VMEM_LIMIT),
        name="shared_expert",
    )(x1, h2, g2, wsg, wsu, wsd)


def _final_kernel(base_ref, routed_ref, g2_ref, o_ref):
    o_ref[...] = base_ref[...] + g2_ref[...] * routed_ref[...]


def _final(base, routed, g2, tm, mod_spec):
    n = base.shape[0]
    tile = pl.BlockSpec((tm, D_MODEL), lambda i: (i, 0))
    return pl.pallas_call(
        _final_kernel,
        grid=(n // tm,),
        in_specs=[tile, tile, mod_spec],
        out_specs=tile,
        out_shape=jax.ShapeDtypeStruct((n, D_MODEL), F32),
        compiler_params=pltpu.CompilerParams(dimension_semantics=("arbitrary",), vmem_limit_bytes=VMEM_LIMIT),
        name="ffn_residual",
    )(base, routed, g2)


def _block_diag(w):
    nb, bi, bj = w.shape
    return jnp.einsum('nij,nm->nimj', w, jnp.eye(nb, dtype=w.dtype)).reshape(nb * bi, nb * bj)


def kernel(x_prompt, x_sample, c_prompt, c_sample, cache_k_win, cache_v_win, state_conv, state_rnn, ada_w, ada_b, norm_mix, w_in, conv_w, conv_b, gate_a_w, gate_a_b, gate_x_w, gate_x_b, lru_lambda, q_norm, k_norm, attn_sinks, w_out, norm_ffn, router_w, router_bias, exp_w_gate, exp_w_up, exp_w_down, sh_w_gate, sh_w_up, sh_w_down):
    bp, tp, _ = x_prompt.shape
    bs, ts, _ = x_sample.shape
    win = cache_k_win.shape[2]
    n_p, n_s = bp * tp, bs * ts
    row = lambda v: v.reshape(1, -1)

    g_mix, g_ffn = row(norm_mix[0]), row(norm_ffn[0])
    win_bf = w_in[0].astype(BF16)
    q0 = 2 * D_RNN
    win_pair = jnp.concatenate([win_bf[:, :q0], _pair_heads(win_bf[:, q0:q0 + D_ATT], 1), win_bf[:, q0 + D_ATT:]],
                               axis=1)
    cw, cb = conv_w[0], row(conv_b[0])
    wg = jnp.concatenate([_block_diag(gate_a_w[0]), _block_diag(gate_x_w[0])], axis=1).astype(BF16)
    gb = row(jnp.concatenate([gate_a_b[0], gate_x_b[0]]))
    lam = row(lru_lambda[0])
    gq = row(q_norm[0])
    gk2 = row(jnp.tile(k_norm[0], N_KV_HEADS))
    sinks = attn_sinks[0]
    wo_top, wo_bot = w_out[0, :D_RNN].astype(BF16), w_out[0, D_RNN:].astype(BF16)
    wr_t = router_w[0].T
    rb = router_bias[0].reshape(N_EXPERTS, 1)
    wsg, wsu, wsd = sh_w_gate[0].astype(BF16), sh_w_up[0].astype(BF16), sh_w_down[0].astype(BF16)

    mod = _adaln(jnp.concatenate([c_prompt, c_sample], axis=0), ada_w[0], ada_b[0])
    chunks = [mod[:, i * D_MODEL:(i + 1) * D_MODEL] for i in range(6)]
    sh1p, sc1p, g1p, sh2p, sc2p, g2p = [c[:bp].reshape(bp, 1, D_MODEL) for c in chunks]
    sh1s, sc1s, g1s, sh2s, sc2s, g2s = [c[bp:] for c in chunks]

    conv0 = jnp.zeros((bp, SUBLANES, D_RNN), F32)
    h0 = jnp.zeros((bp, 1, D_RNN), F32)
    r_p, q_p, k_p, v_p, cs_p, hs_p = _front_prompt(x_prompt, sc1p, sh1p, g_mix, win_pair, cw, cb, wg, gb, lam, gk2,
                                                   conv0, h0)
    a_p = _attn_prompt(q_p, k_p, v_p, row(jnp.tile(q_norm[0], 2)), sinks)

    x_s_t = jnp.swapaxes(x_sample, 0, 1)
    r_s, q_s, k_s, v_s, cs_s, hs_s = _front_sample(x_s_t, sc1s, sh1s, g_mix, win_bf, cw, cb, wg, gb, lam, gk2,
                                                   jnp.swapaxes(state_conv[0], 0, 1), state_rnn[0])
    cache_k = cache_k_win[0].reshape(bs, win, D_KV)
    cache_v = cache_v_win[0].reshape(bs, win, D_KV)
    a_s = _attn_sample(q_s, k_s, v_s, cache_k, cache_v, gq, sinks)

    tiles_per_seq = tp // TM_POST
    mod_p = pl.BlockSpec((None, 1, D_MODEL), lambda i, *_: (i // tiles_per_seq, 0, 0))
    mod_s = pl.BlockSpec((bs, D_MODEL), lambda i, *_: (0, 0))
    cnt0 = jnp.zeros((N_EXPERTS, LANES), F32)
    x1_p, h2_p, eidx_p, gw_p, rank_p, cnt_p = _post(
        x_prompt.reshape(n_p, D_MODEL), r_p.reshape(n_p, D_RNN), a_p.reshape(n_p, D_ATT), g1p, sc2p, sh2p, g_ffn,
        wo_top, _pair_heads(wo_bot, 0), wr_t, rb, cnt0, TM_POST, mod_p)
    x1_s, h2_s, eidx_s, gw_s, rank_s, cnt_all = _post(
        x_s_t.reshape(n_s, D_MODEL), r_s.reshape(n_s, D_RNN), a_s.reshape(n_s, D_ATT), g1s, sc2s, sh2s, g_ffn,
        wo_top, wo_bot, wr_t, rb, cnt_p, bs, mod_s)

    bm = EXPERT_BLOCK
    n_blocks = -(-((n_p + n_s) * TOP_K) // bm) + N_EXPERTS
    pstart, meta = _plan(cnt_all, bm, n_blocks)
    counts = cnt_all[:, 0].astype(I32)
    first_blk = (pstart[:, 0] * (1.0 / bm)).astype(I32)
    n_blk = (counts + (bm - 1)) // bm
    n_used = meta[2, :1]

    def sc_layouts(eidx, rank, gw, n):
        per_w = n // SC_WORKERS
        c = min(DISPATCH_CHUNK, per_w)
        outs = _layouts(pstart, eidx, rank, gw, (c,) if c == per_w else (c, per_w))
        return (outs[0], c), (outs[-2], outs[-1])

    (dpos_p, c_p), comb_p = sc_layouts(eidx_p, rank_p, gw_p, n_p)
    (dpos_s, c_s), comb_s = sc_layouts(eidx_s, rank_s, gw_s, n_s)
    xs = _sc_dispatch((h2_p, dpos_p, c_p), (h2_s, dpos_s, c_s), n_blocks * bm)
    base_p = _shared(x1_p, h2_p, g2p, wsg, wsu, wsd, TM_POST, mod_p)
    base_s = _shared(x1_s, h2_s, g2s, wsg, wsu, wsd, bs, mod_s)
    y = _experts(first_blk, n_blk, counts, n_used, xs, exp_w_gate[0], exp_w_up[0], exp_w_down[0])
    routed_p, routed_s = _sc_combine(y, comb_p, comb_s)
    y_p = _final(base_p, routed_p, g2p, TM_POST, mod_p)
    y_s = _final(base_s, routed_s, g2s, bs, mod_s)

    y_prompt = y_p.reshape(bp, tp, D_MODEL)
    y_sample = jnp.swapaxes(y_s.reshape(ts, bs, D_MODEL), 0, 1)
    wk = min(WINDOW, tp)
    k_win_p = k_p[:, tp - wk:].reshape(1, bp, wk, N_KV_HEADS, HEAD_DIM)
    v_win_p = v_p[:, tp - wk:].reshape(1, bp, wk, N_KV_HEADS, HEAD_DIM)
    k_new = jnp.swapaxes(k_s, 0, 1)
    v_new = jnp.swapaxes(v_s, 0, 1)
    heads = lambda a: a.reshape(bs, ts, N_KV_HEADS, HEAD_DIM)
    k_win_s = jnp.concatenate([cache_k_win[0], heads(k_new)], axis=1)[None, :, ts:]
    v_win_s = jnp.concatenate([cache_v_win[0], heads(v_new)], axis=1)[None, :, ts:]
    return (y_prompt, y_sample, k_win_p, v_win_p, cs_p[None, :, SUBLANES - (CONV_W - 1):], hs_p.reshape(1, bp, D_RNN),
            k_win_s, v_win_s, jnp.swapaxes(cs_s, 0, 1)[None], hs_s[None])
```

```python
import functools

import jax
import jax.numpy as jnp
from jax import lax
from jax.experimental import pallas as pl
from jax.experimental.pallas import tpu as pltpu
from jax.experimental.pallas import tpu_sc as plsc

F32 = jnp.float32
BF16 = jnp.bfloat16
I32 = jnp.int32

D_MODEL = 1024
D_RNN = 512
N_RNN_BLOCKS = 8
CONV_W = 4
LRU_C = 8.0
HEAD_DIM = 64
N_HEADS = 8
N_KV_HEADS = 2
GQA = N_HEADS // N_KV_HEADS
D_ATT = N_HEADS * HEAD_DIM
D_KV = N_KV_HEADS * HEAD_DIM
WINDOW = 128
N_EXPERTS = 64
TOP_K = 6
N_GROUPS = 8
GROUP_SIZE = N_EXPERTS // N_GROUPS
TOPK_GROUPS = 4
D_EXPERT = 256
D_SHARED = 256
ROUTE_SCALE = 2.5
EPS = 1e-6
NEG = -1e30
F32_TINY = 1.1754944e-38
D_IN = 2 * D_RNN + D_ATT + 2 * D_KV

SUBLANES = 8
LANES = 128
TM_PROMPT = 512
TM_POST = 512
ATT_BLOCK = WINDOW
ATT_STEP_BLOCKS = 8
EXPERT_BLOCK = 512
EXPERT_RING = 4
SAMPLE_CHUNK = 8
POS_TILE = 2048
VMEM_LIMIT = 48 * 1024 * 1024

SC_CORES = 2
SC_SUBCORES = 16
SC_WORKERS = SC_CORES * SC_SUBCORES
SC_LANES = 16
DISPATCH_CHUNK = 64
COMBINE_CHUNK = 8
COMBINE_TOKEN_GROUP = 4


def _sigmoid(x):
    return 0.5 * jnp.tanh(0.5 * x) + 0.5


def _silu(x):
    return x * _sigmoid(x)


def _gelu_tanh(x):
    c = 0.7978845608028654
    return x * (0.5 * (1.0 + jnp.tanh(c * (x + 0.044715 * (x * x * x)))))


def _log1p(x):
    u = 1.0 + x
    return jnp.where(u == 1.0, x, jnp.log(u) * x / jnp.where(u == 1.0, 1.0, u - 1.0))


def _neg_expm1_2x(x, exp_x):
    return -jnp.tanh(x) * (exp_x * exp_x + 1.0)


def _softplus(z):
    return jnp.maximum(z, 0.0) + _log1p(jnp.exp(-jnp.abs(z)))


def _div_pow2(x, d):
    assert d & (d - 1) == 0
    return lax.shift_right_logical(x, d.bit_length() - 1)


def _mod_pow2(x, d):
    assert d & (d - 1) == 0
    return x & (d - 1)


def _norm_mod(x, g, sc, sh):
    ms = jnp.mean(x * x, axis=-1, keepdims=True)
    return (x * lax.rsqrt(ms + EPS)) * g * (1.0 + sc) + sh


def _bdot(a, b):
    return jnp.dot(a.astype(BF16), b.astype(BF16), preferred_element_type=F32)


def _bdot_nt(a, b):
    return lax.dot_general(a.astype(BF16), b.astype(BF16), (((1,), (1,)), ((), ())),
                           preferred_element_type=F32)


def _pack_bf16_pairs(x):
    w = x.shape[1] // 2

    def rne_bits(v):
        b = pltpu.bitcast(v, I32)
        return b + 0x7FFF + (lax.shift_right_logical(b, 16) & 1)

    lo = lax.shift_right_logical(rne_bits(x[:, :w]), 16)
    hi = rne_bits(x[:, w:]) & jnp.int32(-65536)
    return lo | hi


def _unpack_bf16_pairs(p):
    lo = pltpu.bitcast(lax.shift_left(p, 16), F32)
    hi = pltpu.bitcast(p & jnp.int32(-65536), F32)
    return lo.astype(BF16), hi.astype(BF16)


def _knorm(k, gk2):
    lane = lax.broadcasted_iota(I32, k.shape, 1)
    first = lane < HEAD_DIM
    k2 = k * k
    s0 = jnp.sum(jnp.where(first, k2, 0.0), axis=-1, keepdims=True)
    s1 = jnp.sum(jnp.where(first, 0.0, k2), axis=-1, keepdims=True)
    ms = jnp.where(first, s0, s1) * (1.0 / HEAD_DIM)
    return k * lax.rsqrt(ms + EPS) * gk2


def _pair_heads(w, axis):
    shape = w.shape
    split = shape[:axis] + (N_KV_HEADS, GQA, HEAD_DIM) + shape[axis + 1:]
    return jnp.swapaxes(w.reshape(split), axis, axis + 1).reshape(shape)


def _lru_coeffs(u, wg, gb, lam):
    g = _bdot(u, wg) + gb
    r = _sigmoid(g[:, :D_RNN])
    i = _sigmoid(g[:, D_RNN:])
    log_a = (-LRU_C * r) * _softplus(-lam)
    a = jnp.exp(log_a)
    om = _neg_expm1_2x(log_a, a)
    b = (om * lax.rsqrt(jnp.maximum(om, F32_TINY))) * (i * u)
    return a, b


def _adaln_kernel(c_ref, w_ref, b_ref, o_ref):
    o_ref[...] = _bdot(_silu(c_ref[...]), w_ref[...]) + b_ref[...]


def _adaln(c_all, ada_w, ada_b):
    n = c_all.shape[0]
    return pl.pallas_call(
        _adaln_kernel,
        grid=(6,),
        in_specs=[pl.BlockSpec((n, D_MODEL), lambda j: (0, 0)),
                  pl.BlockSpec((D_MODEL, D_MODEL), lambda j: (0, j)),
                  pl.BlockSpec((1, D_MODEL), lambda j: (0, j))],
        out_specs=pl.BlockSpec((n, D_MODEL), lambda j: (0, j)),
        out_shape=jax.ShapeDtypeStruct((n, 6 * D_MODEL), F32),
        compiler_params=pltpu.CompilerParams(dimension_semantics=("arbitrary",), vmem_limit_bytes=VMEM_LIMIT),
        name="adaln",
    )(c_all, ada_w, ada_b.reshape(1, -1))


def _scan_rows(a, b, h_in):
    n, c = a.shape
    groups = n // SUBLANES
    a = a.reshape(groups, SUBLANES, c)
    b = b.reshape(groups, SUBLANES, c)
    sub = lax.broadcasted_iota(I32, a.shape, 1)
    s = 1
    while s < SUBLANES:
        m = sub >= s
        a_sh = jnp.where(m, pltpu.roll(a, s, 1), 1.0)
        b_sh = jnp.where(m, pltpu.roll(b, s, 1), 0.0)
        b = a * b_sh + b
        a = a * a_sh
        s *= 2
    carry = h_in
    hs = []
    for g in range(groups):
        hg = a[g] * carry + b[g]
        hs.append(hg)
        carry = hg[SUBLANES - 1:SUBLANES]
    return jnp.concatenate(hs, axis=0)


def _front_prompt_kernel(x_ref, sc_ref, sh_ref, g_ref, win_ref, cw_ref, cb_ref, wg_ref, gb_ref, lam_ref, gk_ref,
                         prev_ref, h0_ref, r_ref, q_ref, k_ref, v_ref, cs_ref, hs_ref, tail_ref, hc_ref):
    j = pl.program_id(1)
    tm = x_ref.shape[0]

    @pl.when(j == 0)
    def _():
        tail_ref[...] = prev_ref[...]
        hc_ref[...] = h0_ref[...]

    h = _norm_mod(x_ref[...], g_ref[...], sc_ref[...], sh_ref[...])
    proj = jnp.dot(h.astype(BF16), win_ref[...], preferred_element_type=F32)
    xr = proj[:, 0:D_RNN]
    yr = proj[:, D_RNN:2 * D_RNN]
    q_ref[...] = proj[:, 2 * D_RNN:2 * D_RNN + D_ATT]
    k_ref[...] = _knorm(proj[:, 2 * D_RNN + D_ATT:2 * D_RNN + D_ATT + D_KV], gk_ref[...])
    v_ref[...] = proj[:, 2 * D_RNN + D_ATT + D_KV:D_IN]

    tail = tail_ref[...]
    row8 = lax.broadcasted_iota(I32, tail.shape, 0)

    def shifted(s):
        rolled = pltpu.roll(xr, s, 0)
        top = jnp.where(row8 < s, pltpu.roll(tail, s, 0), rolled[0:SUBLANES])
        return jnp.concatenate([top, rolled[SUBLANES:]], axis=0)

    cw = cw_ref[...]
    u = cb_ref[...] + shifted(3) * cw[0:1]
    u = u + shifted(2) * cw[1:2]
    u = u + shifted(1) * cw[2:3]
    u = u + xr * cw[3:4]
    tail_ref[...] = xr[tm - SUBLANES:tm]

    a, b = _lru_coeffs(u, wg_ref[...], gb_ref[...], lam_ref[...])
    hs = _scan_rows(a, b, hc_ref[...])
    hc_ref[...] = hs[tm - 1:tm]
    r_ref[...] = hs * _gelu_tanh(yr)

    @pl.when(j == pl.num_programs(1) - 1)
    def _():
        cs_ref[...] = xr[tm - SUBLANES:tm]
        hs_ref[...] = hs[tm - 1:tm]


def _front_prompt(x, sc, sh, g, win, cw, cb, wg, gb, lam, gk, prev, h0):
    bsz, t, _ = x.shape
    tm = TM_PROMPT
    full = lambda a: pl.BlockSpec(a.shape, lambda b, j: (0,) * a.ndim)
    per_b = lambda a: pl.BlockSpec((None,) + a.shape[1:], lambda b, j: (b,) + (0,) * (a.ndim - 1))
    tile = lambda w: pl.BlockSpec((None, tm, w), lambda b, j: (b, j, 0))
    return pl.pallas_call(
        _front_prompt_kernel,
        grid=(bsz, t // tm),
        in_specs=[tile(D_MODEL), per_b(sc), per_b(sh), full(g), full(win), full(cw), full(cb), full(wg), full(gb),
                  full(lam), full(gk), per_b(prev), per_b(h0)],
        out_specs=[tile(D_RNN), tile(D_ATT), tile(D_KV), tile(D_KV),
                   pl.BlockSpec((None, SUBLANES, D_RNN), lambda b, j: (b, 0, 0)),
                   pl.BlockSpec((None, 1, D_RNN), lambda b, j: (b, 0, 0))],
        out_shape=[jax.ShapeDtypeStruct((bsz, t, D_RNN), F32), jax.ShapeDtypeStruct((bsz, t, D_ATT), F32),
                   jax.ShapeDtypeStruct((bsz, t, D_KV), F32), jax.ShapeDtypeStruct((bsz, t, D_KV), F32),
                   jax.ShapeDtypeStruct((bsz, SUBLANES, D_RNN), F32), jax.ShapeDtypeStruct((bsz, 1, D_RNN), F32)],
        scratch_shapes=[pltpu.VMEM((SUBLANES, D_RNN), F32), pltpu.VMEM((1, D_RNN), F32)],
        compiler_params=pltpu.CompilerParams(dimension_semantics=("arbitrary", "arbitrary"),
                                             vmem_limit_bytes=VMEM_LIMIT),
        name="front_prompt",
    )(x, sc, sh, g, win, cw, cb, wg, gb, lam, gk, prev, h0)


def _front_sample_kernel(x_ref, sc_ref, sh_ref, g_ref, win_ref, cw_ref, cb_ref, wg_ref, gb_ref, lam_ref, gk_ref,
                         prev_ref, h0_ref, r_ref, q_ref, k_ref, v_ref, cs_ref, hs_ref):
    t_len, bsz, _ = x_ref.shape
    x = x_ref[...]
    ms = jnp.mean(x * x, axis=-1, keepdims=True)
    h = (x * lax.rsqrt(ms + EPS)) * g_ref[...] * (1.0 + sc_ref[...]) + sh_ref[...]
    proj = jnp.dot(h.reshape(t_len * bsz, D_MODEL).astype(BF16), win_ref[...], preferred_element_type=F32)
    xr = proj[:, 0:D_RNN]
    yr = proj[:, D_RNN:2 * D_RNN]
    q_ref[...] = proj[:, 2 * D_RNN:2 * D_RNN + D_ATT].reshape(t_len, bsz, D_ATT)
    k_ref[...] = _knorm(proj[:, 2 * D_RNN + D_ATT:2 * D_RNN + D_ATT + D_KV], gk_ref[...]).reshape(t_len, bsz, D_KV)
    v_ref[...] = proj[:, 2 * D_RNN + D_ATT + D_KV:D_IN].reshape(t_len, bsz, D_KV)

    def at_time(t):
        if t >= 0:
            return xr[t * bsz:(t + 1) * bsz]
        return prev_ref[CONV_W - 1 + t]

    cw = cw_ref[...]
    us = []
    for t in range(t_len):
        u = cb_ref[...] + at_time(t - 3) * cw[0:1]
        u = u + at_time(t - 2) * cw[1:2]
        u = u + at_time(t - 1) * cw[2:3]
        u = u + at_time(t) * cw[3:4]
        us.append(u)
    a, b = _lru_coeffs(jnp.concatenate(us, axis=0), wg_ref[...], gb_ref[...], lam_ref[...])
    hcur = h0_ref[...]
    for t in range(t_len):
        hcur = a[t * bsz:(t + 1) * bsz] * hcur + b[t * bsz:(t + 1) * bsz]
        r_ref[t] = hcur * _gelu_tanh(yr[t * bsz:(t + 1) * bsz])
    hs_ref[...] = hcur
    for s in range(CONV_W - 1):
        cs_ref[s] = at_time(t_len - (CONV_W - 1) + s)


def _front_sample(x_t, sc, sh, g, win, cw, cb, wg, gb, lam, gk, prev_t, h0):
    t_len, bsz, _ = x_t.shape
    return pl.pallas_call(
        _front_sample_kernel,
        out_shape=[jax.ShapeDtypeStruct((t_len, bsz, D_RNN), F32), jax.ShapeDtypeStruct((t_len, bsz, D_ATT), F32),
                   jax.ShapeDtypeStruct((t_len, bsz, D_KV), F32), jax.ShapeDtypeStruct((t_len, bsz, D_KV), F32),
                   jax.ShapeDtypeStruct((CONV_W - 1, bsz, D_RNN), F32), jax.ShapeDtypeStruct((bsz, D_RNN), F32)],
        compiler_params=pltpu.CompilerParams(vmem_limit_bytes=VMEM_LIMIT),
        name="front_sample",
    )(x_t, sc, sh, g, win, cw, cb, wg, gb, lam, gk, prev_t, h0)


def _qnorm(q, gq):
    ms = jnp.mean(q * q, axis=-1, keepdims=True)
    return q * lax.rsqrt(ms + EPS) * gq


def _attn_prompt_kernel(sink_ref, q_ref, kp_ref, kc_ref, vp_ref, vc_ref, gq_ref, o_ref):
    j = pl.program_id(1)
    blk = kp_ref.shape[0]
    k_all = jnp.concatenate([kp_ref[...], kc_ref[...]], axis=0)
    v_all = jnp.concatenate([vp_ref[...], vc_ref[...]], axis=0)
    qi = lax.broadcasted_iota(I32, (blk, 2 * blk), 0)
    kj = lax.broadcasted_iota(I32, (blk, 2 * blk), 1)
    dist = blk + qi - kj
    window = (dist >= 0) & (dist <= WINDOW)
    distf = dist.astype(F32)
    slab = 2 * HEAD_DIM
    first_q = lax.broadcasted_iota(I32, (blk, slab), 1) < HEAD_DIM
    first_kv = lax.broadcasted_iota(I32, (2 * blk, slab), 1) < HEAD_DIM

    bias = [jnp.where(window, (-(2.0 ** -(h + 1))) * distf, NEG) for h in range(N_HEADS)]
    no_prev = (kj < blk) & (j == 0)
    gq = gq_ref[...] * (HEAD_DIM ** -0.5)

    def probs(s, h, first_block):
        b = jnp.where(no_prev, NEG, bias[h]) if first_block else bias[h]
        s = s + b
        sink = sink_ref[h]
        m = jnp.maximum(jnp.max(s, axis=-1, keepdims=True), sink)
        p = jnp.exp(s - m)
        return p, jnp.sum(p, axis=-1, keepdims=True) + jnp.exp(sink - m)

    for sub in range(q_ref.shape[0] // blk):
        first = sub == 0
        q = q_ref[sub * blk:(sub + 1) * blk, :]
        kk = k_all[sub * blk:(sub + 2) * blk]
        vv = v_all[sub * blk:(sub + 2) * blk]
        v_a = jnp.where(first_kv, vv, 0.0)
        v_b = jnp.where(first_kv, 0.0, vv)
        outs = []
        for i in range(GQA):
            q2 = _knorm(q[:, i * slab:(i + 1) * slab], gq)
            p_a, l_a = probs(_bdot_nt(jnp.where(first_q, q2, 0.0), kk), i, first)
            p_b, l_b = probs(_bdot_nt(jnp.where(first_q, 0.0, q2), kk), GQA + i, first)
            outs.append((_bdot(p_a, v_a) + _bdot(p_b, v_b)) / jnp.where(first_q, l_a, l_b))
        o_ref[sub * blk:(sub + 1) * blk, :] = jnp.concatenate(outs, axis=1)


def _attn_prompt(q, k, v, gq, sinks):
    bsz, t, _ = q.shape
    blk = ATT_BLOCK
    nsub = ATT_STEP_BLOCKS
    cur = lambda w: pl.BlockSpec((None, nsub * blk, w), lambda b, j: (b, j, 0))
    prv = lambda w: pl.BlockSpec((None, blk, w), lambda b, j: (b, jnp.maximum(nsub * j - 1, 0), 0))
    return pl.pallas_call(
        _attn_prompt_kernel,
        grid=(bsz, t // (nsub * blk)),
        in_specs=[pl.BlockSpec(memory_space=pltpu.SMEM), cur(D_ATT), prv(D_KV), cur(D_KV), prv(D_KV), cur(D_KV),
                  pl.BlockSpec(gq.shape, lambda b, j: (0, 0))],
        out_specs=cur(D_ATT),
        out_shape=jax.ShapeDtypeStruct((bsz, t, D_ATT), F32),
        compiler_params=pltpu.CompilerParams(dimension_semantics=("arbitrary", "arbitrary"),
                                             vmem_limit_bytes=VMEM_LIMIT),
        name="attn_prompt",
    )(sinks, q, k, k, v, v, gq)


def _attn_sample_kernel(sink_ref, q_ref, kn_ref, vn_ref, kc_ref, vc_ref, gq_ref, o_ref):
    t_len, cb, _ = q_ref.shape
    win = kc_ref.shape[1]
    rows = GQA * t_len * cb
    kc = kc_ref[...].reshape(cb * win, D_KV)
    vc = vc_ref[...].reshape(cb * win, D_KV)
    kn = kn_ref[...].reshape(t_len * cb, D_KV)
    vn = vn_ref[...].reshape(t_len * cb, D_KV)

    r_c = lax.broadcasted_iota(I32, (rows, cb * win), 0)
    c_c = lax.broadcasted_iota(I32, (rows, cb * win), 1)
    t_c = _div_pow2(_mod_pow2(r_c, t_len * cb), cb)
    valid_c = (_mod_pow2(r_c, cb) == _div_pow2(c_c, win)) & (_mod_pow2(c_c, win) >= t_c)
    dist_c = (win + t_c - _mod_pow2(c_c, win)).astype(F32)
    r_n = lax.broadcasted_iota(I32, (rows, t_len * cb), 0)
    c_n = lax.broadcasted_iota(I32, (rows, t_len * cb), 1)
    t_n = _div_pow2(_mod_pow2(r_n, t_len * cb), cb)
    valid_n = (_mod_pow2(r_n, cb) == _mod_pow2(c_n, cb)) & (_div_pow2(c_n, cb) <= t_n)
    dist_n = (t_n - _div_pow2(c_n, cb)).astype(F32)
    hl = _div_pow2(lax.broadcasted_iota(I32, (rows, 1), 0), t_len * cb)

    per_group = []
    for g in range(N_KV_HEADS):
        slabs = [q_ref[t][:, (g * GQA + i) * HEAD_DIM:(g * GQA + i + 1) * HEAD_DIM]
                 for i in range(GQA) for t in range(t_len)]
        qg = _qnorm(jnp.concatenate(slabs, axis=0), gq_ref[...])
        slope = jnp.zeros((rows, 1), F32)
        sink = jnp.zeros((rows, 1), F32)
        for i in range(GQA):
            slope = jnp.where(hl == i, 2.0 ** -(g * GQA + i + 1), slope)
            sink = jnp.where(hl == i, sink_ref[g * GQA + i], sink)
        lo, hi = g * HEAD_DIM, (g + 1) * HEAD_DIM
        s_c = _bdot_nt(qg, kc[:, lo:hi]) * (HEAD_DIM ** -0.5) - slope * dist_c
        s_n = _bdot_nt(qg, kn[:, lo:hi]) * (HEAD_DIM ** -0.5) - slope * dist_n
        s_c = jnp.where(valid_c, s_c, NEG)
        s_n = jnp.where(valid_n, s_n, NEG)
        m = jnp.maximum(jnp.maximum(jnp.max(s_c, axis=-1, keepdims=True), jnp.max(s_n, axis=-1, keepdims=True)), sink)
        p_c = jnp.exp(s_c - m)
        p_n = jnp.exp(s_n - m)
        l = jnp.sum(p_c, axis=-1, keepdims=True) + jnp.sum(p_n, axis=-1, keepdims=True) + jnp.exp(sink - m)
        per_group.append((_bdot(p_c, vc[:, lo:hi]) + _bdot(p_n, vn[:, lo:hi])) / l)
    for t in range(t_len):
        o_ref[t] = jnp.concatenate(
            [per_group[g][(i * t_len + t) * cb:(i * t_len + t + 1) * cb] for g in range(N_KV_HEADS) for i in range(GQA)],
            axis=1)


def _attn_sample(q_t, k_t, v_t, cache_k, cache_v, gq, sinks):
    t_len, bsz, _ = q_t.shape
    cb = SAMPLE_CHUNK
    win = cache_k.shape[1]
    new = lambda w: pl.BlockSpec((t_len, cb, w), lambda c: (0, c, 0))
    old = pl.BlockSpec((cb, win, D_KV), lambda c: (c, 0, 0))
    return pl.pallas_call(
        _attn_sample_kernel,
        grid=(bsz // cb,),
        in_specs=[pl.BlockSpec(memory_space=pltpu.SMEM), new(D_ATT), new(D_KV), new(D_KV), old, old,
                  pl.BlockSpec(gq.shape, lambda c: (0, 0))],
        out_specs=new(D_ATT),
        out_shape=jax.ShapeDtypeStruct((t_len, bsz, D_ATT), F32),
        compiler_params=pltpu.CompilerParams(dimension_semantics=("arbitrary",), vmem_limit_bytes=VMEM_LIMIT),
        name="attn_sample",
    )(sinks, q_t, k_t, v_t, cache_k, cache_v, gq)


def _route(s_t, sb_t):
    tm = s_t.shape[1]
    i8 = lax.broadcasted_iota(I32, (GROUP_SIZE, tm), 0)
    ninf = -jnp.inf
    sg = [sb_t[GROUP_SIZE * g:GROUP_SIZE * (g + 1)] for g in range(N_GROUPS)]
    gscore = []
    for g in range(N_GROUPS):
        m1 = jnp.max(sg[g], axis=0, keepdims=True)
        i1 = jnp.min(jnp.where(sg[g] == m1, i8, GROUP_SIZE), axis=0, keepdims=True)
        m2 = jnp.max(jnp.where(i8 == i1, ninf, sg[g]), axis=0, keepdims=True)
        gscore.append(m1 + m2)
    gs = jnp.concatenate(gscore, axis=0)
    gsel = jnp.zeros((N_GROUPS, tm), I32)
    for _ in range(TOPK_GROUPS):
        m = jnp.max(gs, axis=0, keepdims=True)
        idx = jnp.min(jnp.where(gs == m, i8, N_GROUPS), axis=0, keepdims=True)
        hit = i8 == idx
        gsel = jnp.where(hit, 1, gsel)
        gs = jnp.where(hit, ninf, gs)
    sm = [jnp.where(gsel[g:g + 1] > 0, sg[g], NEG) for g in range(N_GROUPS)]
    eid = [i8 + GROUP_SIZE * g for g in range(N_GROUPS)]
    sel = [jnp.zeros((GROUP_SIZE, tm), F32) for _ in range(N_GROUPS)]
    idxs, ws = [], []
    for _ in range(TOP_K):
        cm = functools.reduce(jnp.maximum, sm)
        m = jnp.max(cm, axis=0, keepdims=True)
        cand = functools.reduce(jnp.minimum, [jnp.where(sm[g] == m, eid[g], N_EXPERTS) for g in range(N_GROUPS)])
        idx = jnp.min(cand, axis=0, keepdims=True)
        wk = jnp.zeros((GROUP_SIZE, tm), F32)
        for g in range(N_GROUPS):
            hit = eid[g] == idx
            wk = wk + jnp.where(hit, s_t[GROUP_SIZE * g:GROUP_SIZE * (g + 1)], 0.0)
            sel[g] = jnp.where(hit, 1.0, sel[g])
            sm[g] = jnp.where(hit, ninf, sm[g])
        idxs.append(idx)
        ws.append(jnp.sum(wk, axis=0, keepdims=True))
    return idxs, ws, jnp.concatenate(sel, axis=0), eid


def _post_kernel(x_ref, r_ref, a_ref, g1_ref, sc2_ref, sh2_ref, nf_ref, wot_ref, wob_ref, wrt_ref, rb_ref, cin_ref,
                 x1_ref, h2_ref, eidx_ref, gw_ref, rank_ref, cnt_ref, carry_ref, before_ref):
    tm = x_ref.shape[0]

    @pl.when(pl.program_id(0) == 0)
    def _():
        carry_ref[...] = cin_ref[...]
        rr = lax.broadcasted_iota(I32, (tm, tm), 0)
        cc = lax.broadcasted_iota(I32, (tm, tm), 1)
        before_ref[...] = jnp.where(rr < cc, 1.0, 0.0).astype(BF16)

    mixed = (jnp.dot(r_ref[...].astype(BF16), wot_ref[...], preferred_element_type=F32)
             + jnp.dot(a_ref[...].astype(BF16), wob_ref[...], preferred_element_type=F32))
    x1 = x_ref[...] + g1_ref[...] * mixed
    x1_ref[...] = x1
    h2 = _norm_mod(x1, nf_ref[...], sc2_ref[...], sh2_ref[...])
    h2_ref[...] = _pack_bf16_pairs(h2)

    wr = wrt_ref[...]
    wr_hi = wr.astype(BF16)
    wr_lo = (wr - wr_hi.astype(F32)).astype(BF16)
    h_hi = h2.astype(BF16)
    h_lo = (h2 - h_hi.astype(F32)).astype(BF16)
    logits = _bdot_nt(wr_hi, h_hi) + _bdot_nt(wr_hi, h_lo) + _bdot_nt(wr_lo, h_hi)
    s_t = _sigmoid(logits)
    idxs, ws, sel, eid = _route(s_t, s_t + rb_ref[...])

    carry = carry_ref[...]
    tot = jnp.dot(sel.astype(BF16), before_ref[...], preferred_element_type=F32) + carry[:, 0:1]
    ranks = []
    for k in range(TOP_K):
        acc = jnp.zeros((GROUP_SIZE, tm), F32)
        for g in range(N_GROUPS):
            acc = acc + jnp.where(eid[g] == idxs[k], tot[GROUP_SIZE * g:GROUP_SIZE * (g + 1)], 0.0)
        ranks.append(jnp.sum(acc, axis=0, keepdims=True))
    carry = carry + jnp.sum(sel, axis=1, keepdims=True)
    carry_ref[...] = carry
    cnt_ref[...] = carry

    wsum = functools.reduce(lambda p, q: p + q, ws)
    pad_i = jnp.zeros((SUBLANES - TOP_K, tm), I32)
    pad_f = jnp.zeros((SUBLANES - TOP_K, tm), F32)
    eidx_ref[...] = jnp.concatenate(idxs + [pad_i], axis=0)
    rank_ref[...] = jnp.concatenate([r.astype(I32) for r in ranks] + [pad_i], axis=0)
    gw_ref[...] = jnp.concatenate([w / wsum * ROUTE_SCALE for w in ws] + [pad_f], axis=0)


def _post(x, r, a, g1, sc2, sh2, nf, wo_top, wo_bot, wr_t, rb, cnt_in, tm, mod_spec):
    n = x.shape[0]
    tile = lambda w: pl.BlockSpec((tm, w), lambda i: (i, 0))
    full = lambda arr: pl.BlockSpec(arr.shape, lambda i: (0,) * arr.ndim)
    slot = pl.BlockSpec((SUBLANES, tm), lambda i: (0, i))
    return pl.pallas_call(
        _post_kernel,
        grid=(n // tm,),
        in_specs=[tile(D_MODEL), tile(D_RNN), tile(D_ATT), mod_spec, mod_spec, mod_spec, full(nf), full(wo_top),
                  full(wo_bot), full(wr_t), full(rb), full(cnt_in)],
        out_specs=[tile(D_MODEL), tile(D_MODEL // 2), slot, slot, slot, full(cnt_in)],
        out_shape=[jax.ShapeDtypeStruct((n, D_MODEL), F32), jax.ShapeDtypeStruct((n, D_MODEL // 2), I32),
                   jax.ShapeDtypeStruct((SUBLANES, n), I32), jax.ShapeDtypeStruct((SUBLANES, n), F32),
                   jax.ShapeDtypeStruct((SUBLANES, n), I32), jax.ShapeDtypeStruct(cnt_in.shape, F32)],
        scratch_shapes=[pltpu.VMEM(cnt_in.shape, F32), pltpu.VMEM((tm, tm), BF16)],
        compiler_params=pltpu.CompilerParams(dimension_semantics=("arbitrary",), vmem_limit_bytes=VMEM_LIMIT),
        name="post_mix",
    )(x, r, a, g1, sc2, sh2, nf, wo_top, wo_bot, wr_t, rb, cnt_in)


def _plan_kernel(cnt_ref, ps_ref, meta_ref, *, bm):
    cnt = cnt_ref[...]
    padded = jnp.ceil(cnt * (1.0 / bm)) * bm
    row = lax.broadcasted_iota(I32, cnt.shape, 0)
    pend = padded
    s = 1
    while s < N_EXPERTS:
        pend = pend + jnp.where(row >= s, pltpu.roll(pend, s, 0), 0.0)
        s *= 2
    pstart = pend - padded
    ps_ref[...] = pstart
    nb = meta_ref.shape[1]
    start = lax.broadcasted_iota(I32, (N_EXPERTS, nb), 1).astype(F32) * bm
    blk_e = jnp.minimum(jnp.sum(jnp.where(pend[:, 0:1] <= start, 1.0, 0.0), axis=0, keepdims=True), N_EXPERTS - 1.0)
    e_iota = lax.broadcasted_iota(I32, (N_EXPERTS, nb), 0).astype(F32)
    end_b = jnp.sum(jnp.where(e_iota == blk_e, (pstart + cnt)[:, 0:1], 0.0), axis=0, keepdims=True)
    n_valid = jnp.clip(end_b - start[0:1], 0.0, bm)
    n_used = jnp.broadcast_to(pend[N_EXPERTS - 1:N_EXPERTS, 0:1] * (1.0 / bm), (1, nb))
    meta_ref[...] = jnp.concatenate([blk_e.astype(I32), n_valid.astype(I32), n_used.astype(I32),
                                     jnp.zeros((SUBLANES - 3, nb), I32)], axis=0)


def _plan(cnt, bm, n_blocks):
    assert bm & (bm - 1) == 0
    nb = -(-n_blocks // LANES) * LANES
    return pl.pallas_call(
        functools.partial(_plan_kernel, bm=bm),
        out_shape=[jax.ShapeDtypeStruct(cnt.shape, F32), jax.ShapeDtypeStruct((SUBLANES, nb), I32)],
        name="moe_plan",
    )(cnt)


def _layout_kernel(ps_ref, eidx_ref, rank_ref, gw_ref, *out_refs, chunks):
    *pos_refs, w_ref = out_refs
    tn = eidx_ref.shape[1]
    e_iota = lax.broadcasted_iota(I32, (N_EXPERTS, tn), 0)
    ps = ps_ref[...][:, 0:1]
    rows = []
    for k in range(TOP_K):
        hit = e_iota == eidx_ref[k:k + 1, :]
        base = jnp.sum(jnp.where(hit, ps, 0.0), axis=0, keepdims=True)
        rows.append(base.astype(I32) + rank_ref[k:k + 1, :])
    rows.append(jnp.zeros((SUBLANES - TOP_K, tn), I32))
    pos = jnp.concatenate(rows, axis=0)
    for pos_ref, c in zip(pos_refs, chunks):
        for q in range(tn // c):
            pos_ref[q] = pos[:, q * c:(q + 1) * c]
    gw = gw_ref[...]
    rep = jnp.concatenate([jnp.broadcast_to(gw[k:k + 1], (SC_LANES, tn)) for k in range(SUBLANES)], axis=0)
    w_ref[...] = rep.T


def _layouts(pstart, eidx, rank, gw, chunks):
    n = eidx.shape[1]
    tn = min(n, POS_TILE)
    slot = pl.BlockSpec((SUBLANES, tn), lambda i: (0, i))
    return pl.pallas_call(
        functools.partial(_layout_kernel, chunks=chunks),
        grid=(n // tn,),
        in_specs=[pl.BlockSpec(pstart.shape, lambda i: (0, 0)), slot, slot, slot],
        out_specs=[pl.BlockSpec((tn // c, SUBLANES, c), lambda i: (i, 0, 0)) for c in chunks]
        + [pl.BlockSpec((tn, SUBLANES * SC_LANES), lambda i: (i, 0))],
        out_shape=[jax.ShapeDtypeStruct((n // c, SUBLANES, c), I32) for c in chunks]
        + [jax.ShapeDtypeStruct((n, SUBLANES * SC_LANES), F32)],
        compiler_params=pltpu.CompilerParams(dimension_semantics=("arbitrary",), vmem_limit_bytes=VMEM_LIMIT),
        name="moe_layout",
    )(pstart, eidx, rank, gw)


def _sc_worker_id():
    return lax.axis_index("s") * SC_CORES + lax.axis_index("c")


def _sc_mesh():
    return plsc.VectorSubcoreMesh(core_axis_name="c", subcore_axis_name="s")


def _sc_dispatch(seg_a, seg_b, total_rows):
    (h_a, pos_a, c_a), (h_b, pos_b, c_b) = seg_a, seg_b
    width = h_a.shape[1]

    @functools.partial(
        pl.kernel, mesh=_sc_mesh(), out_type=jax.ShapeDtypeStruct((total_rows, width), I32),
        scratch_types=[pltpu.VMEM((2, SUBLANES, c_a), I32), pltpu.VMEM((2, c_a, width), I32),
                       pltpu.VMEM((2, SUBLANES, c_b), I32), pltpu.VMEM((2, c_b, width), I32),
                       pltpu.SemaphoreType.DMA((2,)), pltpu.SemaphoreType.DMA((2,))])
    def run(ha_hbm, pa_hbm, hb_hbm, pb_hbm, xs_hbm, idx_a, rows_a, idx_b, rows_b, in_sems, out_sems):
        wid = _sc_worker_id()

        def segment(h_hbm, p_hbm, idx_v, rows_v, c):
            nch = h_hbm.shape[0] // (SC_WORKERS * c)
            assert nch == 1 or nch % 2 == 0
            chunk0 = wid * nch

            def loads(ci, b):
                return [pltpu.make_async_copy(p_hbm.at[chunk0 + ci], idx_v.at[b], in_sems.at[b]),
                        pltpu.make_async_copy(h_hbm.at[pl.ds((chunk0 + ci) * c, c)], rows_v.at[b], in_sems.at[b])]

            def scatters(b):
                return [pltpu.make_async_copy(rows_v.at[b], xs_hbm.at[idx_v.at[b].at[k]], out_sems.at[b])
                        for k in range(TOP_K)]

            def start(copies):
                for cp in copies:
                    cp.start()

            def wait(copies):
                for cp in copies:
                    cp.wait()

            start(loads(0, 0))
            if nch == 1:
                wait(loads(0, 0))
                start(scatters(0))
                wait(scatters(0))
                return

            @pl.loop(0, nch, step=2)
            def _(ci):
                for b in range(2):
                    wait(loads(ci + b, b))
                    start(scatters(b))

                    @pl.when(ci + b + 1 < nch)
                    def _():
                        @pl.when(ci + b >= 1)
                        def _():
                            wait(scatters(1 - b))

                        start(loads(ci + b + 1, 1 - b))

            wait(scatters(0))
            wait(scatters(1))

        segment(ha_hbm, pa_hbm, idx_a, rows_a, c_a)
        segment(hb_hbm, pb_hbm, idx_b, rows_b, c_b)

    return run(h_a, pos_a, h_b, pos_b)


def _experts_kernel(sb_ref, nb_ref, cnt_ref, nu_ref, xs_hbm, wg_ref, wu_ref, wd_ref, y_hbm,
                    xbuf, ybuf, wgb, wub, wdb, in_sems, out_sems):
    e = pl.program_id(0)
    ring, bm, half = xbuf.shape
    n_used = nu_ref[0]

    def slot_of(g):
        return g & (ring - 1)

    def in_copy(g):
        s = slot_of(g)
        return pltpu.make_async_copy(xs_hbm.at[pl.ds(pl.multiple_of(g * bm, bm), bm), :], xbuf.at[s], in_sems.at[s])

    def out_copy(g):
        s = slot_of(g)
        return pltpu.make_async_copy(ybuf.at[s], y_hbm.at[pl.ds(pl.multiple_of(g * bm, bm), bm), :], out_sems.at[s])

    @pl.when(e == 0)
    def _():
        for g in range(ring - 1):
            @pl.when(g < n_used)
            def _():
                in_copy(g).start()

    wgb[...] = wg_ref[...].astype(BF16)
    wub[...] = wu_ref[...].astype(BF16)
    wdb[...] = wd_ref[...].astype(BF16)
    row = lax.broadcasted_iota(I32, (bm, half), 0)

    def block(b, carry):
        g = sb_ref[e] + b

        @pl.when(g + ring - 1 < n_used)
        def _():
            in_copy(g + ring - 1).start()

        in_copy(g).wait()

        @pl.when(g >= ring)
        def _():
            out_copy(g - ring).wait()

        s = slot_of(g)
        x_lo, x_hi = _unpack_bf16_pairs(jnp.where(row < cnt_ref[e] - b * bm, xbuf[s], 0))
        a = (jnp.dot(x_lo, wgb[:half], preferred_element_type=F32)
             + jnp.dot(x_hi, wgb[half:], preferred_element_type=F32))
        u = (jnp.dot(x_lo, wub[:half], preferred_element_type=F32)
             + jnp.dot(x_hi, wub[half:], preferred_element_type=F32))
        ybuf[s] = jnp.dot((_silu(a) * u).astype(BF16), wdb[...], preferred_element_type=F32)
        out_copy(g).start()
        return carry

    lax.fori_loop(0, nb_ref[e], block, 0)

    @pl.when(e == pl.num_programs(0) - 1)
    def _():
        for r in range(ring):
            @pl.when(n_used - 1 - r >= 0)
            def _():
                out_copy(n_used - 1 - r).wait()


def _experts(first_blk, n_blk, counts, n_used, xs, wg, wu, wd):
    rows = xs.shape[0]
    bm, ring = EXPERT_BLOCK, EXPERT_RING
    assert ring & (ring - 1) == 0
    w_blk = lambda e, *_: (e, 0, 0)
    grid_spec = pltpu.PrefetchScalarGridSpec(
        num_scalar_prefetch=4,
        grid=(N_EXPERTS,),
        in_specs=[pl.BlockSpec(memory_space=pl.ANY),
                  pl.BlockSpec((None, D_MODEL, D_EXPERT), w_blk),
                  pl.BlockSpec((None, D_MODEL, D_EXPERT), w_blk),
                  pl.BlockSpec((None, D_EXPERT, D_MODEL), w_blk)],
        out_specs=pl.BlockSpec(memory_space=pl.ANY),
        scratch_shapes=[pltpu.VMEM((ring, bm, D_MODEL // 2), I32), pltpu.VMEM((ring, bm, D_MODEL), F32),
                        pltpu.VMEM((D_MODEL, D_EXPERT), BF16), pltpu.VMEM((D_MODEL, D_EXPERT), BF16),
                        pltpu.VMEM((D_EXPERT, D_MODEL), BF16),
                        pltpu.SemaphoreType.DMA((ring,)), pltpu.SemaphoreType.DMA((ring,))],
    )
    return pl.pallas_call(
        _experts_kernel,
        grid_spec=grid_spec,
        out_shape=jax.ShapeDtypeStruct((rows, D_MODEL), F32),
        compiler_params=pltpu.CompilerParams(dimension_semantics=("arbitrary",), vmem_limit_bytes=VMEM_LIMIT),
        name="moe_experts",
    )(first_blk, n_blk, counts, n_used, xs, wg, wu, wd)


def _sc_combine(y, seg_a, seg_b):
    (pos_a, w_a), (pos_b, w_b) = seg_a, seg_b
    c = COMBINE_CHUNK
    d = y.shape[1]
    n_a, n_b = w_a.shape[0], w_b.shape[0]

    @functools.partial(
        pl.kernel, mesh=_sc_mesh(),
        out_type=[jax.ShapeDtypeStruct((n_a, d), F32), jax.ShapeDtypeStruct((n_b, d), F32)],
        scratch_types=[pltpu.VMEM(pos_a.shape[1:], I32), pltpu.VMEM(pos_b.shape[1:], I32),
                       pltpu.VMEM((2, c, SUBLANES * SC_LANES), F32), pltpu.VMEM((2, TOP_K, c, d), F32),
                       pltpu.VMEM((c, d), F32), pltpu.SemaphoreType.DMA((2,)), pltpu.SemaphoreType.DMA])
    def run(y_hbm, pa_hbm, wa_hbm, pb_hbm, wb_hbm, oa_hbm, ob_hbm, idx_a, idx_b, w_v, buf_v, out_v, sems, out_sem):
        wid = _sc_worker_id()

        def segment(p_hbm, w_hbm, o_hbm, idx_v):
            per_w = idx_v.shape[1]
            nch = per_w // c
            base = wid * per_w
            pltpu.sync_copy(p_hbm.at[wid], idx_v)

            def in_copies(ci, b):
                w_copy = pltpu.make_async_copy(w_hbm.at[pl.ds(base + ci * c, c)], w_v.at[b], sems.at[b])
                return [w_copy] + [
                    pltpu.make_async_copy(y_hbm.at[idx_v.at[k, pl.ds(ci * c, c)]], buf_v.at[b].at[k], sems.at[b])
                    for k in range(TOP_K)]

            def out_copy(ci):
                return pltpu.make_async_copy(out_v, o_hbm.at[pl.ds(base + ci * c, c)], out_sem)

            def reduce_rows(ci, b):
                @pl.when(ci >= 1)
                def _():
                    out_copy(ci - 1).wait()

                for t0 in range(0, c, COMBINE_TOKEN_GROUP):
                    group = range(t0, t0 + COMBINE_TOKEN_GROUP)
                    ws = [[w_v[b, t, pl.ds(k * SC_LANES, SC_LANES)] for k in range(TOP_K)] for t in group]

                    @plsc.parallel_loop(0, d // SC_LANES, unroll=2)
                    def _(j):
                        lanes = pl.ds(pl.multiple_of(j * SC_LANES, SC_LANES), SC_LANES)
                        for t, wt in zip(group, ws):
                            acc = buf_v[b, 0, t, lanes] * wt[0]
                            for k in range(1, TOP_K):
                                acc = acc + buf_v[b, k, t, lanes] * wt[k]
                            out_v[t, lanes] = acc

                out_copy(ci).start()

            for cp in in_copies(0, 0):
                cp.start()

            @pl.loop(0, nch, step=2)
            def _(ci):
                for b in range(2):
                    @pl.when(ci + b + 1 < nch)
                    def _():
                        for cp in in_copies(ci + b + 1, 1 - b):
                            cp.start()

                    for cp in in_copies(ci + b, b):
                        cp.wait()
                    reduce_rows(ci + b, b)

            out_copy(nch - 1).wait()

        segment(pa_hbm, wa_hbm, oa_hbm, idx_a)
        segment(pb_hbm, wb_hbm, ob_hbm, idx_b)

    return run(y, pos_a, w_a, pos_b, w_b)


def _shared_kernel(x1_ref, h2_ref, g2_ref, wsg_ref, wsu_ref, wsd_ref, o_ref):
    half = h2_ref.shape[1]
    x_lo, x_hi = _unpack_bf16_pairs(h2_ref[...])
    wsg, wsu = wsg_ref[...], wsu_ref[...]
    a = (jnp.dot(x_lo, wsg[:half], preferred_element_type=F32) + jnp.dot(x_hi, wsg[half:], preferred_element_type=F32))
    b = (jnp.dot(x_lo, wsu[:half], preferred_element_type=F32) + jnp.dot(x_hi, wsu[half:], preferred_element_type=F32))
    shared = jnp.dot((_silu(a) * b).astype(BF16), wsd_ref[...], preferred_element_type=F32)
    o_ref[...] = x1_ref[...] + g2_ref[...] * shared


def _shared(x1, h2, g2, wsg, wsu, wsd, tm, mod_spec):
    n = h2.shape[0]
    full = lambda arr: pl.BlockSpec(arr.shape, lambda i: (0,) * arr.ndim)
    tile = pl.BlockSpec((tm, D_MODEL), lambda i: (i, 0))
    return pl.pallas_call(
        _shared_kernel,
        grid=(n // tm,),
        in_specs=[tile, pl.BlockSpec((tm, D_MODEL // 2), lambda i: (i, 0)), mod_spec, full(wsg), full(wsu), full(wsd)],
        out_specs=tile,
        out_shape=jax.ShapeDtypeStruct((n, D_MODEL), F32),
        compiler_params=pltpu.CompilerParams(dimension_semantics=("arbitrary",), vmem_limit_bytes=VMEM_LIMIT),
        name="shared_expert",
    )(x1, h2, g2, wsg, wsu, wsd)


def _final_kernel(base_ref, routed_ref, g2_ref, o_ref):
    o_ref[...] = base_ref[...] + g2_ref[...] * routed_ref[...]


def _final(base, routed, g2, tm, mod_spec):
    n = base.shape[0]
    tile = pl.BlockSpec((tm, D_MODEL), lambda i: (i, 0))
    return pl.pallas_call(
        _final_kernel,
        grid=(n // tm,),
        in_specs=[tile, tile, mod_spec],
        out_specs=tile,
        out_shape=jax.ShapeDtypeStruct((n, D_MODEL), F32),
        compiler_params=pltpu.CompilerParams(dimension_semantics=("arbitrary",), vmem_limit_bytes=VMEM_LIMIT),
        name="ffn_residual",
    )(base, routed, g2)


def _block_diag(w):
    nb, bi, bj = w.shape
    return jnp.einsum('nij,nm->nimj', w, jnp.eye(nb, dtype=w.dtype)).reshape(nb * bi, nb * bj)


def kernel(x_prompt, x_sample, c_prompt, c_sample, cache_k_win, cache_v_win, state_conv, state_rnn, ada_w, ada_b, norm_mix, w_in, conv_w, conv_b, gate_a_w, gate_a_b, gate_x_w, gate_x_b, lru_lambda, q_norm, k_norm, attn_sinks, w_out, norm_ffn, router_w, router_bias, exp_w_gate, exp_w_up, exp_w_down, sh_w_gate, sh_w_up, sh_w_down):
    bp, tp, _ = x_prompt.shape
    bs, ts, _ = x_sample.shape
    win = cache_k_win.shape[2]
    n_p, n_s = bp * tp, bs * ts
    row = lambda v: v.reshape(1, -1)

    g_mix, g_ffn = row(norm_mix[0]), row(norm_ffn[0])
    win_bf = w_in[0].astype(BF16)
    q0 = 2 * D_RNN
    win_pair = jnp.concatenate([win_bf[:, :q0], _pair_heads(win_bf[:, q0:q0 + D_ATT], 1), win_bf[:, q0 + D_ATT:]],
                               axis=1)
    cw, cb = conv_w[0], row(conv_b[0])
    wg = jnp.concatenate([_block_diag(gate_a_w[0]), _block_diag(gate_x_w[0])], axis=1).astype(BF16)
    gb = row(jnp.concatenate([gate_a_b[0], gate_x_b[0]]))
    lam = row(lru_lambda[0])
    gq = row(q_norm[0])
    gk2 = row(jnp.tile(k_norm[0], N_KV_HEADS))
    sinks = attn_sinks[0]
    wo_top, wo_bot = w_out[0, :D_RNN].astype(BF16), w_out[0, D_RNN:].astype(BF16)
    wr_t = router_w[0].T
    rb = router_bias[0].reshape(N_EXPERTS, 1)
    wsg, wsu, wsd = sh_w_gate[0].astype(BF16), sh_w_up[0].astype(BF16), sh_w_down[0].astype(BF16)

    mod = _adaln(jnp.concatenate([c_prompt, c_sample], axis=0), ada_w[0], ada_b[0])
    chunks = [mod[:, i * D_MODEL:(i + 1) * D_MODEL] for i in range(6)]
    sh1p, sc1p, g1p, sh2p, sc2p, g2p = [c[:bp].reshape(bp, 1, D_MODEL) for c in chunks]
    sh1s, sc1s, g1s, sh2s, sc2s, g2s = [c[bp:] for c in chunks]

    conv0 = jnp.zeros((bp, SUBLANES, D_RNN), F32)
    h0 = jnp.zeros((bp, 1, D_RNN), F32)
    r_p, q_p, k_p, v_p, cs_p, hs_p = _front_prompt(x_prompt, sc1p, sh1p, g_mix, win_pair, cw, cb, wg, gb, lam, gk2,
                                                   conv0, h0)
    a_p = _attn_prompt(q_p, k_p, v_p, row(jnp.tile(q_norm[0], 2)), sinks)

    x_s_t = jnp.swapaxes(x_sample, 0, 1)
    r_s, q_s, k_s, v_s, cs_s, hs_s = _front_sample(x_s_t, sc1s, sh1s, g_mix, win_bf, cw, cb, wg, gb, lam, gk2,
                                                   jnp.swapaxes(state_conv[0], 0, 1), state_rnn[0])
    cache_k = cache_k_win[0].reshape(bs, win, D_KV)
    cache_v = cache_v_win[0].reshape(bs, win, D_KV)
    a_s = _attn_sample(q_s, k_s, v_s, cache_k, cache_v, gq, sinks)

    tiles_per_seq = tp // TM_POST
    mod_p = pl.BlockSpec((None, 1, D_MODEL), lambda i, *_: (i // tiles_per_seq, 0, 0))
    mod_s = pl.BlockSpec((bs, D_MODEL), lambda i, *_: (0, 0))
    cnt0 = jnp.zeros((N_EXPERTS, LANES), F32)
    x1_p, h2_p, eidx_p, gw_p, rank_p, cnt_p = _post(
        x_prompt.reshape(n_p, D_MODEL), r_p.reshape(n_p, D_RNN), a_p.reshape(n_p, D_ATT), g1p, sc2p, sh2p, g_ffn,
        wo_top, _pair_heads(wo_bot, 0), wr_t, rb, cnt0, TM_POST, mod_p)
    x1_s, h2_s, eidx_s, gw_s, rank_s, cnt_all = _post(
        x_s_t.reshape(n_s, D_MODEL), r_s.reshape(n_s, D_RNN), a_s.reshape(n_s, D_ATT), g1s, sc2s, sh2s, g_ffn,
        wo_top, wo_bot, wr_t, rb, cnt_p, bs, mod_s)

    bm = EXPERT_BLOCK
    n_blocks = -(-((n_p + n_s) * TOP_K) // bm) + N_EXPERTS
    pstart, meta = _plan(cnt_all, bm, n_blocks)
    counts = cnt_all[:, 0].astype(I32)
    first_blk = (pstart[:, 0] * (1.0 / bm)).astype(I32)
    n_blk = (counts + (bm - 1)) // bm
    n_used = meta[2, :1]

    def sc_layouts(eidx, rank, gw, n):
        per_w = n // SC_WORKERS
        c = min(DISPATCH_CHUNK, per_w)
        outs = _layouts(pstart, eidx, rank, gw, (c,) if c == per_w else (c, per_w))
        return (outs[0], c), (outs[-2], outs[-1])

    (dpos_p, c_p), comb_p = sc_layouts(eidx_p, rank_p, gw_p, n_p)
    (dpos_s, c_s), comb_s = sc_layouts(eidx_s, rank_s, gw_s, n_s)
    xs = _sc_dispatch((h2_p, dpos_p, c_p), (h2_s, dpos_s, c_s), n_blocks * bm)
    base_p = _shared(x1_p, h2_p, g2p, wsg, wsu, wsd, TM_POST, mod_p)
    base_s = _shared(x1_s, h2_s, g2s, wsg, wsu, wsd, bs, mod_s)
    y = _experts(first_blk, n_blk, counts, n_used, xs, exp_w_gate[0], exp_w_up[0], exp_w_down[0])
    routed_p, routed_s = _sc_combine(y, comb_p, comb_s)
    y_p = _final(base_p, routed_p, g2p, TM_POST, mod_p)
    y_s = _final(base_s, routed_s, g2s, bs, mod_s)

    y_prompt = y_p.reshape(bp, tp, D_MODEL)
    y_sample = jnp.swapaxes(y_s.reshape(ts, bs, D_MODEL), 0, 1)
    wk = min(WINDOW, tp)
    k_win_p = k_p[:, tp - wk:].reshape(1, bp, wk, N_KV_HEADS, HEAD_DIM)
    v_win_p = v_p[:, tp - wk:].reshape(1, bp, wk, N_KV_HEADS, HEAD_DIM)
    k_new = jnp.swapaxes(k_s, 0, 1)
    v_new = jnp.swapaxes(v_s, 0, 1)
    heads = lambda a: a.reshape(bs, ts, N_KV_HEADS, HEAD_DIM)
    k_win_s = jnp.concatenate([cache_k_win[0], heads(k_new)], axis=1)[None, :, ts:]
    v_win_s = jnp.concatenate([cache_v_win[0], heads(v_new)], axis=1)[None, :, ts:]
    return (y_prompt, y_sample, k_win_p, v_win_p, cs_p[None, :, SUBLANES - (CONV_W - 1):], hs_p.reshape(1, bp, D_RNN),
            k_win_s, v_win_s, jnp.swapaxes(cs_s, 0, 1)[None], hs_s[None])
```

```python
import dataclasses
import functools

import jax
import jax.numpy as jnp
from jax import lax
from jax.experimental import pallas as pl
from jax.experimental.pallas import tpu as pltpu
from jax.experimental.pallas import tpu_sc as plsc

F32 = jnp.float32
BF16 = jnp.bfloat16
I32 = jnp.int32

D_MODEL = 1024
D_RNN = 512
N_RNN_BLOCKS = 8
CONV_W = 4
LRU_C = 8.0
HEAD_DIM = 64
N_HEADS = 8
N_KV_HEADS = 2
GQA = N_HEADS // N_KV_HEADS
D_ATT = N_HEADS * HEAD_DIM
D_KV = N_KV_HEADS * HEAD_DIM
WINDOW = 128
N_EXPERTS = 64
TOP_K = 6
N_GROUPS = 8
GROUP_SIZE = N_EXPERTS // N_GROUPS
TOPK_GROUPS = 4
D_EXPERT = 256
D_SHARED = 256
ROUTE_SCALE = 2.5
EPS = 1e-6
NEG = -1e30
F32_TINY = 1.1754944e-38
D_IN = 2 * D_RNN + D_ATT + 2 * D_KV

SUBLANES = 8
LANES = 128
TM_PROMPT = 512
TM_POST = 512
ATT_BLOCK = WINDOW
ATT_STEP_BLOCKS = 8
EXPERT_BLOCK = 512
EXPERT_RING = 4
SAMPLE_CHUNK = 8
POS_TILE = 2048
VMEM_LIMIT = 48 * 1024 * 1024

SC_CORES = 2
SC_SUBCORES = 16
SC_WORKERS = SC_CORES * SC_SUBCORES
SC_LANES = 16
DISPATCH_CHUNK = 64
COMBINE_CHUNK = 8
COMBINE_TOKEN_GROUP = 4


def _sigmoid(x):
    return 0.5 * jnp.tanh(0.5 * x) + 0.5


def _silu(x):
    return x * _sigmoid(x)


def _gelu_tanh(x):
    c = 0.7978845608028654
    return x * (0.5 * (1.0 + jnp.tanh(c * (x + 0.044715 * (x * x * x)))))


def _log1p(x):
    u = 1.0 + x
    return jnp.where(u == 1.0, x, jnp.log(u) * x / jnp.where(u == 1.0, 1.0, u - 1.0))


def _neg_expm1_2x(x, exp_x):
    return -jnp.tanh(x) * (exp_x * exp_x + 1.0)


def _softplus(z):
    return jnp.maximum(z, 0.0) + _log1p(jnp.exp(-jnp.abs(z)))


def _div_pow2(x, d):
    assert d & (d - 1) == 0
    return lax.shift_right_logical(x, d.bit_length() - 1)


def _mod_pow2(x, d):
    assert d & (d - 1) == 0
    return x & (d - 1)


def _norm_mod(x, g, sc, sh):
    ms = jnp.mean(x * x, axis=-1, keepdims=True)
    return (x * lax.rsqrt(ms + EPS)) * g * (1.0 + sc) + sh


def _bdot(a, b):
    return jnp.dot(a.astype(BF16), b.astype(BF16), preferred_element_type=F32)


def _bdot_nt(a, b):
    return lax.dot_general(a.astype(BF16), b.astype(BF16), (((1,), (1,)), ((), ())),
                           preferred_element_type=F32)


def _pack_bf16_pairs(x):
    w = x.shape[1] // 2

    def bf16_bits(v):
        return pltpu.bitcast(v.astype(BF16).astype(F32), I32)

    return lax.shift_right_logical(bf16_bits(x[:, :w]), 16) | (bf16_bits(x[:, w:]) & jnp.int32(-65536))


def _unpack_bf16_pairs(p):
    lo = pltpu.bitcast(lax.shift_left(p, 16), F32)
    hi = pltpu.bitcast(p & jnp.int32(-65536), F32)
    return lo.astype(BF16), hi.astype(BF16)


def _knorm(k, gk2):
    lane = lax.broadcasted_iota(I32, k.shape, 1)
    first = lane < HEAD_DIM
    k2 = k * k
    s0 = jnp.sum(jnp.where(first, k2, 0.0), axis=-1, keepdims=True)
    s1 = jnp.sum(jnp.where(first, 0.0, k2), axis=-1, keepdims=True)
    ms = jnp.where(first, s0, s1) * (1.0 / HEAD_DIM)
    return k * lax.rsqrt(ms + EPS) * gk2


def _pair_heads(w, axis):
    shape = w.shape
    split = shape[:axis] + (N_KV_HEADS, GQA, HEAD_DIM) + shape[axis + 1:]
    return jnp.swapaxes(w.reshape(split), axis, axis + 1).reshape(shape)


def _lru_coeffs(u, wg, gb, lam):
    g = _bdot(u, wg) + gb
    r = _sigmoid(g[:, :D_RNN])
    i = _sigmoid(g[:, D_RNN:])
    log_a = (-LRU_C * r) * _softplus(-lam)
    a = jnp.exp(log_a)
    om = _neg_expm1_2x(log_a, a)
    b = (om * lax.rsqrt(jnp.maximum(om, F32_TINY))) * (i * u)
    return a, b


def _adaln_kernel(c_ref, w_ref, b_ref, o_ref):
    o_ref[...] = _bdot(_silu(c_ref[...]), w_ref[...]) + b_ref[...]


def _adaln(c_all, ada_w, ada_b):
    n = c_all.shape[0]
    return pl.pallas_call(
        _adaln_kernel,
        grid=(6,),
        in_specs=[pl.BlockSpec((n, D_MODEL), lambda j: (0, 0)),
                  pl.BlockSpec((D_MODEL, D_MODEL), lambda j: (0, j)),
                  pl.BlockSpec((1, D_MODEL), lambda j: (0, j))],
        out_specs=pl.BlockSpec((n, D_MODEL), lambda j: (0, j)),
        out_shape=jax.ShapeDtypeStruct((n, 6 * D_MODEL), F32),
        compiler_params=pltpu.CompilerParams(dimension_semantics=("arbitrary",), vmem_limit_bytes=VMEM_LIMIT),
        name="adaln",
    )(c_all, ada_w, ada_b.reshape(1, -1))


def _scan_rows(a, b, h_in):
    n, c = a.shape
    groups = n // SUBLANES
    a = a.reshape(groups, SUBLANES, c)
    b = b.reshape(groups, SUBLANES, c)
    sub = lax.broadcasted_iota(I32, a.shape, 1)
    s = 1
    while s < SUBLANES:
        m = sub >= s
        a_sh = jnp.where(m, pltpu.roll(a, s, 1), 1.0)
        b_sh = jnp.where(m, pltpu.roll(b, s, 1), 0.0)
        b = a * b_sh + b
        a = a * a_sh
        s *= 2
    carry = h_in
    hs = []
    for g in range(groups):
        hg = a[g] * carry + b[g]
        hs.append(hg)
        carry = hg[SUBLANES - 1:SUBLANES]
    return jnp.concatenate(hs, axis=0)


def _front_prompt_kernel(x_ref, sc_ref, sh_ref, g_ref, win_ref, cw_ref, cb_ref, wg_ref, gb_ref, lam_ref, gk_ref,
                         prev_ref, h0_ref, r_ref, q_ref, k_ref, v_ref, cs_ref, hs_ref, tail_ref, hc_ref):
    j = pl.program_id(1)
    tm = x_ref.shape[0]

    @pl.when(j == 0)
    def _():
        tail_ref[...] = prev_ref[...]
        hc_ref[...] = h0_ref[...]

    h = _norm_mod(x_ref[...], g_ref[...], sc_ref[...], sh_ref[...])
    proj = jnp.dot(h.astype(BF16), win_ref[...], preferred_element_type=F32)
    xr = proj[:, 0:D_RNN]
    yr = proj[:, D_RNN:2 * D_RNN]
    q_ref[...] = proj[:, 2 * D_RNN:2 * D_RNN + D_ATT]
    k_ref[...] = _knorm(proj[:, 2 * D_RNN + D_ATT:2 * D_RNN + D_ATT + D_KV], gk_ref[...])
    v_ref[...] = proj[:, 2 * D_RNN + D_ATT + D_KV:D_IN]

    tail = tail_ref[...]
    row8 = lax.broadcasted_iota(I32, tail.shape, 0)

    def shifted(s):
        rolled = pltpu.roll(xr, s, 0)
        top = jnp.where(row8 < s, pltpu.roll(tail, s, 0), rolled[0:SUBLANES])
        return jnp.concatenate([top, rolled[SUBLANES:]], axis=0)

    cw = cw_ref[...]
    u = cb_ref[...] + shifted(3) * cw[0:1]
    u = u + shifted(2) * cw[1:2]
    u = u + shifted(1) * cw[2:3]
    u = u + xr * cw[3:4]
    tail_ref[...] = xr[tm - SUBLANES:tm]

    a, b = _lru_coeffs(u, wg_ref[...], gb_ref[...], lam_ref[...])
    hs = _scan_rows(a, b, hc_ref[...])
    hc_ref[...] = hs[tm - 1:tm]
    r_ref[...] = hs * _gelu_tanh(yr)

    @pl.when(j == pl.num_programs(1) - 1)
    def _():
        cs_ref[...] = xr[tm - SUBLANES:tm]
        hs_ref[...] = hs[tm - 1:tm]


def _front_prompt(x, sc, sh, g, win, cw, cb, wg, gb, lam, gk, prev, h0):
    bsz, t, _ = x.shape
    tm = TM_PROMPT
    full = lambda a: pl.BlockSpec(a.shape, lambda b, j: (0,) * a.ndim)
    per_b = lambda a: pl.BlockSpec((None,) + a.shape[1:], lambda b, j: (b,) + (0,) * (a.ndim - 1))
    tile = lambda w: pl.BlockSpec((None, tm, w), lambda b, j: (b, j, 0))
    return pl.pallas_call(
        _front_prompt_kernel,
        grid=(bsz, t // tm),
        in_specs=[tile(D_MODEL), per_b(sc), per_b(sh), full(g), full(win), full(cw), full(cb), full(wg), full(gb),
                  full(lam), full(gk), per_b(prev), per_b(h0)],
        out_specs=[tile(D_RNN), tile(D_ATT), tile(D_KV), tile(D_KV),
                   pl.BlockSpec((None, SUBLANES, D_RNN), lambda b, j: (b, 0, 0)),
                   pl.BlockSpec((None, 1, D_RNN), lambda b, j: (b, 0, 0))],
        out_shape=[jax.ShapeDtypeStruct((bsz, t, D_RNN), F32), jax.ShapeDtypeStruct((bsz, t, D_ATT), F32),
                   jax.ShapeDtypeStruct((bsz, t, D_KV), F32), jax.ShapeDtypeStruct((bsz, t, D_KV), F32),
                   jax.ShapeDtypeStruct((bsz, SUBLANES, D_RNN), F32), jax.ShapeDtypeStruct((bsz, 1, D_RNN), F32)],
        scratch_shapes=[pltpu.VMEM((SUBLANES, D_RNN), F32), pltpu.VMEM((1, D_RNN), F32)],
        compiler_params=pltpu.CompilerParams(dimension_semantics=("arbitrary", "arbitrary"),
                                             vmem_limit_bytes=VMEM_LIMIT),
        name="front_prompt",
    )(x, sc, sh, g, win, cw, cb, wg, gb, lam, gk, prev, h0)


def _front_sample_kernel(x_ref, sc_ref, sh_ref, g_ref, win_ref, cw_ref, cb_ref, wg_ref, gb_ref, lam_ref, gk_ref,
                         prev_ref, h0_ref, r_ref, q_ref, k_ref, v_ref, cs_ref, hs_ref):
    t_len, bsz, _ = x_ref.shape
    x = x_ref[...]
    ms = jnp.mean(x * x, axis=-1, keepdims=True)
    h = (x * lax.rsqrt(ms + EPS)) * g_ref[...] * (1.0 + sc_ref[...]) + sh_ref[...]
    proj = jnp.dot(h.reshape(t_len * bsz, D_MODEL).astype(BF16), win_ref[...], preferred_element_type=F32)
    xr = proj[:, 0:D_RNN]
    yr = proj[:, D_RNN:2 * D_RNN]
    q_ref[...] = proj[:, 2 * D_RNN:2 * D_RNN + D_ATT].reshape(t_len, bsz, D_ATT)
    k_ref[...] = _knorm(proj[:, 2 * D_RNN + D_ATT:2 * D_RNN + D_ATT + D_KV], gk_ref[...]).reshape(t_len, bsz, D_KV)
    v_ref[...] = proj[:, 2 * D_RNN + D_ATT + D_KV:D_IN].reshape(t_len, bsz, D_KV)

    def at_time(t):
        if t >= 0:
            return xr[t * bsz:(t + 1) * bsz]
        return prev_ref[CONV_W - 1 + t]

    cw = cw_ref[...]
    us = []
    for t in range(t_len):
        u = cb_ref[...] + at_time(t - 3) * cw[0:1]
        u = u + at_time(t - 2) * cw[1:2]
        u = u + at_time(t - 1) * cw[2:3]
        u = u + at_time(t) * cw[3:4]
        us.append(u)
    a, b = _lru_coeffs(jnp.concatenate(us, axis=0), wg_ref[...], gb_ref[...], lam_ref[...])
    hcur = h0_ref[...]
    for t in range(t_len):
        hcur = a[t * bsz:(t + 1) * bsz] * hcur + b[t * bsz:(t + 1) * bsz]
        r_ref[t] = hcur * _gelu_tanh(yr[t * bsz:(t + 1) * bsz])
    hs_ref[...] = hcur
    for s in range(CONV_W - 1):
        cs_ref[s] = at_time(t_len - (CONV_W - 1) + s)


def _front_sample(x_t, sc, sh, g, win, cw, cb, wg, gb, lam, gk, prev_t, h0):
    t_len, bsz, _ = x_t.shape
    return pl.pallas_call(
        _front_sample_kernel,
        out_shape=[jax.ShapeDtypeStruct((t_len, bsz, D_RNN), F32), jax.ShapeDtypeStruct((t_len, bsz, D_ATT), F32),
                   jax.ShapeDtypeStruct((t_len, bsz, D_KV), F32), jax.ShapeDtypeStruct((t_len, bsz, D_KV), F32),
                   jax.ShapeDtypeStruct((CONV_W - 1, bsz, D_RNN), F32), jax.ShapeDtypeStruct((bsz, D_RNN), F32)],
        compiler_params=pltpu.CompilerParams(vmem_limit_bytes=VMEM_LIMIT),
        name="front_sample",
    )(x_t, sc, sh, g, win, cw, cb, wg, gb, lam, gk, prev_t, h0)


def _qnorm(q, gq):
    ms = jnp.mean(q * q, axis=-1, keepdims=True)
    return q * lax.rsqrt(ms + EPS) * gq


def _attn_prompt_kernel(sink_ref, q_ref, kp_ref, kc_ref, vp_ref, vc_ref, gq_ref, o_ref):
    j = pl.program_id(1)
    blk = kp_ref.shape[0]
    k_all = jnp.concatenate([kp_ref[...], kc_ref[...]], axis=0)
    v_all = jnp.concatenate([vp_ref[...], vc_ref[...]], axis=0)
    qi = lax.broadcasted_iota(I32, (blk, 2 * blk), 0)
    kj = lax.broadcasted_iota(I32, (blk, 2 * blk), 1)
    dist = blk + qi - kj
    window = (dist >= 0) & (dist <= WINDOW)
    distf = dist.astype(F32)
    slab = 2 * HEAD_DIM
    first_q = lax.broadcasted_iota(I32, (blk, slab), 1) < HEAD_DIM
    first_kv = lax.broadcasted_iota(I32, (2 * blk, slab), 1) < HEAD_DIM

    bias = [jnp.where(window, (-(2.0 ** -(h + 1))) * distf, NEG) for h in range(N_HEADS)]
    no_prev = (kj < blk) & (j == 0)
    gq = gq_ref[...] * (HEAD_DIM ** -0.5)

    def probs(s, h, first_block):
        b = jnp.where(no_prev, NEG, bias[h]) if first_block else bias[h]
        s = s + b
        sink = sink_ref[h]
        m = jnp.maximum(jnp.max(s, axis=-1, keepdims=True), sink)
        p = jnp.exp(s - m)
        return p, jnp.sum(p, axis=-1, keepdims=True) + jnp.exp(sink - m)

    for sub in range(q_ref.shape[0] // blk):
        first = sub == 0
        q = q_ref[sub * blk:(sub + 1) * blk, :]
        kk = k_all[sub * blk:(sub + 2) * blk]
        vv = v_all[sub * blk:(sub + 2) * blk]
        v_a = jnp.where(first_kv, vv, 0.0)
        v_b = jnp.where(first_kv, 0.0, vv)
        outs = []
        for i in range(GQA):
            q2 = _knorm(q[:, i * slab:(i + 1) * slab], gq)
            p_a, l_a = probs(_bdot_nt(jnp.where(first_q, q2, 0.0), kk), i, first)
            p_b, l_b = probs(_bdot_nt(jnp.where(first_q, 0.0, q2), kk), GQA + i, first)
            outs.append((_bdot(p_a, v_a) + _bdot(p_b, v_b)) / jnp.where(first_q, l_a, l_b))
        o_ref[sub * blk:(sub + 1) * blk, :] = jnp.concatenate(outs, axis=1)


def _attn_prompt(q, k, v, gq, sinks):
    bsz, t, _ = q.shape
    blk = ATT_BLOCK
    nsub = ATT_STEP_BLOCKS
    cur = lambda w: pl.BlockSpec((None, nsub * blk, w), lambda b, j: (b, j, 0))
    prv = lambda w: pl.BlockSpec((None, blk, w), lambda b, j: (b, jnp.maximum(nsub * j - 1, 0), 0))
    return pl.pallas_call(
        _attn_prompt_kernel,
        grid=(bsz, t // (nsub * blk)),
        in_specs=[pl.BlockSpec(memory_space=pltpu.SMEM), cur(D_ATT), prv(D_KV), cur(D_KV), prv(D_KV), cur(D_KV),
                  pl.BlockSpec(gq.shape, lambda b, j: (0, 0))],
        out_specs=cur(D_ATT),
        out_shape=jax.ShapeDtypeStruct((bsz, t, D_ATT), F32),
        compiler_params=pltpu.CompilerParams(dimension_semantics=("arbitrary", "arbitrary"),
                                             vmem_limit_bytes=VMEM_LIMIT),
        name="attn_prompt",
    )(sinks, q, k, k, v, v, gq)


def _attn_sample_kernel(sink_ref, q_ref, kn_ref, vn_ref, kc_ref, vc_ref, gq_ref, o_ref):
    t_len, cb, _ = q_ref.shape
    win = kc_ref.shape[1]
    rows = GQA * t_len * cb
    kc = kc_ref[...].reshape(cb * win, D_KV)
    vc = vc_ref[...].reshape(cb * win, D_KV)
    kn = kn_ref[...].reshape(t_len * cb, D_KV)
    vn = vn_ref[...].reshape(t_len * cb, D_KV)

    r_c = lax.broadcasted_iota(I32, (rows, cb * win), 0)
    c_c = lax.broadcasted_iota(I32, (rows, cb * win), 1)
    t_c = _div_pow2(_mod_pow2(r_c, t_len * cb), cb)
    valid_c = (_mod_pow2(r_c, cb) == _div_pow2(c_c, win)) & (_mod_pow2(c_c, win) >= t_c)
    dist_c = (win + t_c - _mod_pow2(c_c, win)).astype(F32)
    r_n = lax.broadcasted_iota(I32, (rows, t_len * cb), 0)
    c_n = lax.broadcasted_iota(I32, (rows, t_len * cb), 1)
    t_n = _div_pow2(_mod_pow2(r_n, t_len * cb), cb)
    valid_n = (_mod_pow2(r_n, cb) == _mod_pow2(c_n, cb)) & (_div_pow2(c_n, cb) <= t_n)
    dist_n = (t_n - _div_pow2(c_n, cb)).astype(F32)
    hl = _div_pow2(lax.broadcasted_iota(I32, (rows, 1), 0), t_len * cb)

    per_group = []
    for g in range(N_KV_HEADS):
        slabs = [q_ref[t][:, (g * GQA + i) * HEAD_DIM:(g * GQA + i + 1) * HEAD_DIM]
                 for i in range(GQA) for t in range(t_len)]
        qg = _qnorm(jnp.concatenate(slabs, axis=0), gq_ref[...])
        slope = jnp.zeros((rows, 1), F32)
        sink = jnp.zeros((rows, 1), F32)
        for i in range(GQA):
            slope = jnp.where(hl == i, 2.0 ** -(g * GQA + i + 1), slope)
            sink = jnp.where(hl == i, sink_ref[g * GQA + i], sink)
        lo, hi = g * HEAD_DIM, (g + 1) * HEAD_DIM
        s_c = _bdot_nt(qg, kc[:, lo:hi]) * (HEAD_DIM ** -0.5) - slope * dist_c
        s_n = _bdot_nt(qg, kn[:, lo:hi]) * (HEAD_DIM ** -0.5) - slope * dist_n
        s_c = jnp.where(valid_c, s_c, NEG)
        s_n = jnp.where(valid_n, s_n, NEG)
        m = jnp.maximum(jnp.maximum(jnp.max(s_c, axis=-1, keepdims=True), jnp.max(s_n, axis=-1, keepdims=True)), sink)
        p_c = jnp.exp(s_c - m)
        p_n = jnp.exp(s_n - m)
        l = jnp.sum(p_c, axis=-1, keepdims=True) + jnp.sum(p_n, axis=-1, keepdims=True) + jnp.exp(sink - m)
        per_group.append((_bdot(p_c, vc[:, lo:hi]) + _bdot(p_n, vn[:, lo:hi])) / l)
    for t in range(t_len):
        o_ref[t] = jnp.concatenate(
            [per_group[g][(i * t_len + t) * cb:(i * t_len + t + 1) * cb] for g in range(N_KV_HEADS) for i in range(GQA)],
            axis=1)


def _attn_sample(q_t, k_t, v_t, cache_k, cache_v, gq, sinks):
    t_len, bsz, _ = q_t.shape
    cb = SAMPLE_CHUNK
    win = cache_k.shape[1]
    new = lambda w: pl.BlockSpec((t_len, cb, w), lambda c: (0, c, 0))
    old = pl.BlockSpec((cb, win, D_KV), lambda c: (c, 0, 0))
    return pl.pallas_call(
        _attn_sample_kernel,
        grid=(bsz // cb,),
        in_specs=[pl.BlockSpec(memory_space=pltpu.SMEM), new(D_ATT), new(D_KV), new(D_KV), old, old,
                  pl.BlockSpec(gq.shape, lambda c: (0, 0))],
        out_specs=new(D_ATT),
        out_shape=jax.ShapeDtypeStruct((t_len, bsz, D_ATT), F32),
        compiler_params=pltpu.CompilerParams(dimension_semantics=("arbitrary",), vmem_limit_bytes=VMEM_LIMIT),
        name="attn_sample",
    )(sinks, q_t, k_t, v_t, cache_k, cache_v, gq)


def _route(s_t, sb_t):
    tm = s_t.shape[1]
    i8 = lax.broadcasted_iota(I32, (GROUP_SIZE, tm), 0)
    ninf = -jnp.inf
    sg = [sb_t[GROUP_SIZE * g:GROUP_SIZE * (g + 1)] for g in range(N_GROUPS)]
    gscore = []
    for g in range(N_GROUPS):
        m1 = jnp.max(sg[g], axis=0, keepdims=True)
        i1 = jnp.min(jnp.where(sg[g] == m1, i8, GROUP_SIZE), axis=0, keepdims=True)
        m2 = jnp.max(jnp.where(i8 == i1, ninf, sg[g]), axis=0, keepdims=True)
        gscore.append(m1 + m2)
    gs = jnp.concatenate(gscore, axis=0)
    gsel = jnp.zeros((N_GROUPS, tm), I32)
    for _ in range(TOPK_GROUPS):
        m = jnp.max(gs, axis=0, keepdims=True)
        idx = jnp.min(jnp.where(gs == m, i8, N_GROUPS), axis=0, keepdims=True)
        hit = i8 == idx
        gsel = jnp.where(hit, 1, gsel)
        gs = jnp.where(hit, ninf, gs)
    sm = [jnp.where(gsel[g:g + 1] > 0, sg[g], NEG) for g in range(N_GROUPS)]
    eid = [i8 + GROUP_SIZE * g for g in range(N_GROUPS)]
    sel = [jnp.zeros((GROUP_SIZE, tm), F32) for _ in range(N_GROUPS)]
    idxs, ws = [], []
    for _ in range(TOP_K):
        cm = functools.reduce(jnp.maximum, sm)
        m = jnp.max(cm, axis=0, keepdims=True)
        cand = functools.reduce(jnp.minimum, [jnp.where(sm[g] == m, eid[g], N_EXPERTS) for g in range(N_GROUPS)])
        idx = jnp.min(cand, axis=0, keepdims=True)
        wk = jnp.zeros((GROUP_SIZE, tm), F32)
        for g in range(N_GROUPS):
            hit = eid[g] == idx
            wk = wk + jnp.where(hit, s_t[GROUP_SIZE * g:GROUP_SIZE * (g + 1)], 0.0)
            sel[g] = jnp.where(hit, 1.0, sel[g])
            sm[g] = jnp.where(hit, ninf, sm[g])
        idxs.append(idx)
        ws.append(jnp.sum(wk, axis=0, keepdims=True))
    return idxs, ws, jnp.concatenate(sel, axis=0), eid


def _post_kernel(x_ref, r_ref, a_ref, g1_ref, sc2_ref, sh2_ref, nf_ref, wot_ref, wob_ref, wrt_ref, rb_ref, cin_ref,
                 x1_ref, h2_ref, eidx_ref, gw_ref, rank_ref, cnt_ref, carry_ref, before_ref):
    tm = x_ref.shape[0]

    @pl.when(pl.program_id(0) == 0)
    def _():
        carry_ref[...] = cin_ref[...]
        rr = lax.broadcasted_iota(I32, (tm, tm), 0)
        cc = lax.broadcasted_iota(I32, (tm, tm), 1)
        before_ref[...] = jnp.where(rr < cc, 1.0, 0.0).astype(BF16)

    mixed = (jnp.dot(r_ref[...].astype(BF16), wot_ref[...], preferred_element_type=F32)
             + jnp.dot(a_ref[...].astype(BF16), wob_ref[...], preferred_element_type=F32))
    x1 = x_ref[...] + g1_ref[...] * mixed
    x1_ref[...] = x1
    h2 = _norm_mod(x1, nf_ref[...], sc2_ref[...], sh2_ref[...])
    h2_ref[...] = _pack_bf16_pairs(h2)

    wr = wrt_ref[...]
    wr_hi = wr.astype(BF16)
    wr_lo = (wr - wr_hi.astype(F32)).astype(BF16)
    h_hi = h2.astype(BF16)
    h_lo = (h2 - h_hi.astype(F32)).astype(BF16)
    logits = _bdot_nt(wr_hi, h_hi) + _bdot_nt(wr_hi, h_lo) + _bdot_nt(wr_lo, h_hi)
    s_t = _sigmoid(logits)
    idxs, ws, sel, eid = _route(s_t, s_t + rb_ref[...])

    carry = carry_ref[...]
    tot = jnp.dot(sel.astype(BF16), before_ref[...], preferred_element_type=F32) + carry[:, 0:1]
    ranks = []
    for k in range(TOP_K):
        acc = jnp.zeros((GROUP_SIZE, tm), F32)
        for g in range(N_GROUPS):
            acc = acc + jnp.where(eid[g] == idxs[k], tot[GROUP_SIZE * g:GROUP_SIZE * (g + 1)], 0.0)
        ranks.append(jnp.sum(acc, axis=0, keepdims=True))
    carry = carry + jnp.sum(sel, axis=1, keepdims=True)
    carry_ref[...] = carry
    cnt_ref[...] = carry

    wsum = functools.reduce(lambda p, q: p + q, ws)
    pad_i = jnp.zeros((SUBLANES - TOP_K, tm), I32)
    pad_f = jnp.zeros((SUBLANES - TOP_K, tm), F32)
    eidx_ref[...] = jnp.concatenate(idxs + [pad_i], axis=0)
    rank_ref[...] = jnp.concatenate([r.astype(I32) for r in ranks] + [pad_i], axis=0)
    gw_ref[...] = jnp.concatenate([w / wsum * ROUTE_SCALE for w in ws] + [pad_f], axis=0)


def _post(x, r, a, g1, sc2, sh2, nf, wo_top, wo_bot, wr_t, rb, cnt_in, tm, mod_spec):
    n = x.shape[0]
    tile = lambda w: pl.BlockSpec((tm, w), lambda i: (i, 0))
    full = lambda arr: pl.BlockSpec(arr.shape, lambda i: (0,) * arr.ndim)
    slot = pl.BlockSpec((SUBLANES, tm), lambda i: (0, i))
    return pl.pallas_call(
        _post_kernel,
        grid=(n // tm,),
        in_specs=[tile(D_MODEL), tile(D_RNN), tile(D_ATT), mod_spec, mod_spec, mod_spec, full(nf), full(wo_top),
                  full(wo_bot), full(wr_t), full(rb), full(cnt_in)],
        out_specs=[tile(D_MODEL), tile(D_MODEL // 2), slot, slot, slot, full(cnt_in)],
        out_shape=[jax.ShapeDtypeStruct((n, D_MODEL), F32), jax.ShapeDtypeStruct((n, D_MODEL // 2), I32),
                   jax.ShapeDtypeStruct((SUBLANES, n), I32), jax.ShapeDtypeStruct((SUBLANES, n), F32),
                   jax.ShapeDtypeStruct((SUBLANES, n), I32), jax.ShapeDtypeStruct(cnt_in.shape, F32)],
        scratch_shapes=[pltpu.VMEM(cnt_in.shape, F32), pltpu.VMEM((tm, tm), BF16)],
        compiler_params=pltpu.CompilerParams(dimension_semantics=("arbitrary",), vmem_limit_bytes=VMEM_LIMIT),
        name="post_mix",
    )(x, r, a, g1, sc2, sh2, nf, wo_top, wo_bot, wr_t, rb, cnt_in)


def _plan_kernel(cnt_ref, ps_ref, meta_ref, *, bm):
    cnt = cnt_ref[...]
    padded = jnp.ceil(cnt * (1.0 / bm)) * bm
    row = lax.broadcasted_iota(I32, cnt.shape, 0)
    pend = padded
    s = 1
    while s < N_EXPERTS:
        pend = pend + jnp.where(row >= s, pltpu.roll(pend, s, 0), 0.0)
        s *= 2
    pstart = pend - padded
    ps_ref[...] = pstart
    nb = meta_ref.shape[1]
    start = lax.broadcasted_iota(I32, (N_EXPERTS, nb), 1).astype(F32) * bm
    blk_e = jnp.minimum(jnp.sum(jnp.where(pend[:, 0:1] <= start, 1.0, 0.0), axis=0, keepdims=True), N_EXPERTS - 1.0)
    e_iota = lax.broadcasted_iota(I32, (N_EXPERTS, nb), 0).astype(F32)
    end_b = jnp.sum(jnp.where(e_iota == blk_e, (pstart + cnt)[:, 0:1], 0.0), axis=0, keepdims=True)
    n_valid = jnp.clip(end_b - start[0:1], 0.0, bm)
    n_used = jnp.broadcast_to(pend[N_EXPERTS - 1:N_EXPERTS, 0:1] * (1.0 / bm), (1, nb))
    meta_ref[...] = jnp.concatenate([blk_e.astype(I32), n_valid.astype(I32), n_used.astype(I32),
                                     jnp.zeros((SUBLANES - 3, nb), I32)], axis=0)


def _plan(cnt, bm, n_blocks):
    assert bm & (bm - 1) == 0
    nb = -(-n_blocks // LANES) * LANES
    return pl.pallas_call(
        functools.partial(_plan_kernel, bm=bm),
        out_shape=[jax.ShapeDtypeStruct(cnt.shape, F32), jax.ShapeDtypeStruct((SUBLANES, nb), I32)],
        name="moe_plan",
    )(cnt)


def _layout_kernel(ps_ref, eidx_ref, rank_ref, gw_ref, *out_refs, chunks):
    *pos_refs, w_ref = out_refs
    tn = eidx_ref.shape[1]
    e_iota = lax.broadcasted_iota(I32, (N_EXPERTS, tn), 0)
    ps = ps_ref[...][:, 0:1]
    rows = []
    for k in range(TOP_K):
        hit = e_iota == eidx_ref[k:k + 1, :]
        base = jnp.sum(jnp.where(hit, ps, 0.0), axis=0, keepdims=True)
        rows.append(base.astype(I32) + rank_ref[k:k + 1, :])
    rows.append(jnp.zeros((SUBLANES - TOP_K, tn), I32))
    pos = jnp.concatenate(rows, axis=0)
    for pos_ref, c in zip(pos_refs, chunks):
        for q in range(tn // c):
            pos_ref[q] = pos[:, q * c:(q + 1) * c]
    gw = gw_ref[...]
    rep = jnp.concatenate([jnp.broadcast_to(gw[k:k + 1], (SC_LANES, tn)) for k in range(SUBLANES)], axis=0)
    w_ref[...] = rep.T


def _layouts(pstart, eidx, rank, gw, chunks):
    n = eidx.shape[1]
    tn = min(n, POS_TILE)
    slot = pl.BlockSpec((SUBLANES, tn), lambda i: (0, i))
    return pl.pallas_call(
        functools.partial(_layout_kernel, chunks=chunks),
        grid=(n // tn,),
        in_specs=[pl.BlockSpec(pstart.shape, lambda i: (0, 0)), slot, slot, slot],
        out_specs=[pl.BlockSpec((tn // c, SUBLANES, c), lambda i: (i, 0, 0)) for c in chunks]
        + [pl.BlockSpec((tn, SUBLANES * SC_LANES), lambda i: (i, 0))],
        out_shape=[jax.ShapeDtypeStruct((n // c, SUBLANES, c), I32) for c in chunks]
        + [jax.ShapeDtypeStruct((n, SUBLANES * SC_LANES), F32)],
        compiler_params=pltpu.CompilerParams(dimension_semantics=("arbitrary",), vmem_limit_bytes=VMEM_LIMIT),
        name="moe_layout",
    )(pstart, eidx, rank, gw)


def _sc_worker_id():
    return lax.axis_index("s") * SC_CORES + lax.axis_index("c")


def _sc_mesh():
    return plsc.VectorSubcoreMesh(core_axis_name="c", subcore_axis_name="s")


def _sc_dispatch(seg_a, seg_b, total_rows):
    (h_a, pos_a, c_a), (h_b, pos_b, c_b) = seg_a, seg_b
    width = h_a.shape[1]

    @functools.partial(
        pl.kernel, mesh=_sc_mesh(), out_type=jax.ShapeDtypeStruct((total_rows, width), I32),
        scratch_types=[pltpu.VMEM((2, SUBLANES, c_a), I32), pltpu.VMEM((2, c_a, width), I32),
                       pltpu.VMEM((2, SUBLANES, c_b), I32), pltpu.VMEM((2, c_b, width), I32),
                       pltpu.SemaphoreType.DMA((2,)), pltpu.SemaphoreType.DMA((2,))])
    def run(ha_hbm, pa_hbm, hb_hbm, pb_hbm, xs_hbm, idx_a, rows_a, idx_b, rows_b, in_sems, out_sems):
        wid = _sc_worker_id()

        def segment(h_hbm, p_hbm, idx_v, rows_v, c):
            nch = h_hbm.shape[0] // (SC_WORKERS * c)
            assert nch == 1 or nch % 2 == 0
            chunk0 = wid * nch

            def loads(ci, b):
                return [pltpu.make_async_copy(p_hbm.at[chunk0 + ci], idx_v.at[b], in_sems.at[b]),
                        pltpu.make_async_copy(h_hbm.at[pl.ds((chunk0 + ci) * c, c)], rows_v.at[b], in_sems.at[b])]

            def scatters(b):
                return [pltpu.make_async_copy(rows_v.at[b], xs_hbm.at[idx_v.at[b].at[k]], out_sems.at[b])
                        for k in range(TOP_K)]

            def start(copies):
                for cp in copies:
                    cp.start()

            def wait(copies):
                for cp in copies:
                    cp.wait()

            start(loads(0, 0))
            if nch == 1:
                wait(loads(0, 0))
                start(scatters(0))
                wait(scatters(0))
                return

            @pl.loop(0, nch, step=2)
            def _(ci):
                for b in range(2):
                    wait(loads(ci + b, b))
                    start(scatters(b))

                    @pl.when(ci + b + 1 < nch)
                    def _():
                        @pl.when(ci + b >= 1)
                        def _():
                            wait(scatters(1 - b))

                        start(loads(ci + b + 1, 1 - b))

            wait(scatters(0))
            wait(scatters(1))

        segment(ha_hbm, pa_hbm, idx_a, rows_a, c_a)
        segment(hb_hbm, pb_hbm, idx_b, rows_b, c_b)

    return run(h_a, pos_a, h_b, pos_b)


def _experts_kernel(sb_ref, nb_ref, cnt_ref, nu_ref, xs_hbm, wg_ref, wu_ref, wd_ref, y_hbm,
                    xbuf, ybuf, wgb, wub, wdb, in_sems, out_sems):
    e = pl.program_id(0)
    ring, bm, half = xbuf.shape
    n_used = nu_ref[0]

    def slot_of(g):
        return g & (ring - 1)

    def in_copy(g):
        s = slot_of(g)
        return pltpu.make_async_copy(xs_hbm.at[pl.ds(pl.multiple_of(g * bm, bm), bm), :], xbuf.at[s], in_sems.at[s])

    def out_copy(g):
        s = slot_of(g)
        return pltpu.make_async_copy(ybuf.at[s], y_hbm.at[pl.ds(pl.multiple_of(g * bm, bm), bm), :], out_sems.at[s])

    @pl.when(e == 0)
    def _():
        for g in range(ring - 1):
            @pl.when(g < n_used)
            def _():
                in_copy(g).start()

    wgb[...] = wg_ref[...].astype(BF16)
    wub[...] = wu_ref[...].astype(BF16)
    wdb[...] = wd_ref[...].astype(BF16)
    row = lax.broadcasted_iota(I32, (bm, half), 0)

    def block(b, carry):
        g = sb_ref[e] + b

        @pl.when(g + ring - 1 < n_used)
        def _():
            in_copy(g + ring - 1).start()

        in_copy(g).wait()

        @pl.when(g >= ring)
        def _():
            out_copy(g - ring).wait()

        s = slot_of(g)
        x_lo, x_hi = _unpack_bf16_pairs(jnp.where(row < cnt_ref[e] - b * bm, xbuf[s], 0))
        a = (jnp.dot(x_lo, wgb[:half], preferred_element_type=F32)
             + jnp.dot(x_hi, wgb[half:], preferred_element_type=F32))
        u = (jnp.dot(x_lo, wub[:half], preferred_element_type=F32)
             + jnp.dot(x_hi, wub[half:], preferred_element_type=F32))
        ybuf[s] = _pack_bf16_pairs(jnp.dot((_silu(a) * u).astype(BF16), wdb[...], preferred_element_type=F32))
        out_copy(g).start()
        return carry

    lax.fori_loop(0, nb_ref[e], block, 0)

    @pl.when(e == pl.num_programs(0) - 1)
    def _():
        for r in range(ring):
            @pl.when(n_used - 1 - r >= 0)
            def _():
                out_copy(n_used - 1 - r).wait()


def _experts(first_blk, n_blk, counts, n_used, xs, wg, wu, wd):
    rows = xs.shape[0]
    bm, ring = EXPERT_BLOCK, EXPERT_RING
    assert ring & (ring - 1) == 0
    w_blk = lambda e, *_: (e, 0, 0)
    grid_spec = pltpu.PrefetchScalarGridSpec(
        num_scalar_prefetch=4,
        grid=(N_EXPERTS,),
        in_specs=[pl.BlockSpec(memory_space=pl.ANY),
                  pl.BlockSpec((None, D_MODEL, D_EXPERT), w_blk),
                  pl.BlockSpec((None, D_MODEL, D_EXPERT), w_blk),
                  pl.BlockSpec((None, D_EXPERT, D_MODEL), w_blk)],
        out_specs=pl.BlockSpec(memory_space=pl.ANY),
        scratch_shapes=[pltpu.VMEM((ring, bm, D_MODEL // 2), I32), pltpu.VMEM((ring, bm, D_MODEL // 2), I32),
                        pltpu.VMEM((D_MODEL, D_EXPERT), BF16), pltpu.VMEM((D_MODEL, D_EXPERT), BF16),
                        pltpu.VMEM((D_EXPERT, D_MODEL), BF16),
                        pltpu.SemaphoreType.DMA((ring,)), pltpu.SemaphoreType.DMA((ring,))],
    )
    return pl.pallas_call(
        _experts_kernel,
        grid_spec=grid_spec,
        out_shape=jax.ShapeDtypeStruct((rows, D_MODEL // 2), I32),
        compiler_params=pltpu.CompilerParams(dimension_semantics=("arbitrary",), vmem_limit_bytes=VMEM_LIMIT),
        name="moe_experts",
    )(first_blk, n_blk, counts, n_used, xs, wg, wu, wd)


def _sc_combine(y, seg_a, seg_b):
    (pos_a, w_a), (pos_b, w_b) = seg_a, seg_b
    c = COMBINE_CHUNK
    half = y.shape[1]
    d = 2 * half
    n_a, n_b = w_a.shape[0], w_b.shape[0]
    params = dataclasses.replace(pltpu.CompilerParams(), needs_layout_passes=False)

    @functools.partial(
        pl.kernel, mesh=_sc_mesh(), compiler_params=params,
        out_type=[jax.ShapeDtypeStruct((n_a, d), F32), jax.ShapeDtypeStruct((n_b, d), F32)],
        scratch_types=[pltpu.VMEM(pos_a.shape[1:], I32), pltpu.VMEM(pos_b.shape[1:], I32),
                       pltpu.VMEM((2, c, SUBLANES * SC_LANES), F32), pltpu.VMEM((2, TOP_K, c, half), I32),
                       pltpu.VMEM((c, d), F32), pltpu.SemaphoreType.DMA((2,)), pltpu.SemaphoreType.DMA])
    def run(y_hbm, pa_hbm, wa_hbm, pb_hbm, wb_hbm, oa_hbm, ob_hbm, idx_a, idx_b, w_v, buf_v, out_v, sems, out_sem):
        wid = _sc_worker_id()

        def segment(p_hbm, w_hbm, o_hbm, idx_v):
            per_w = idx_v.shape[1]
            nch = per_w // c
            base = wid * per_w
            pltpu.sync_copy(p_hbm.at[wid], idx_v)

            def in_copies(ci, b):
                w_copy = pltpu.make_async_copy(w_hbm.at[pl.ds(base + ci * c, c)], w_v.at[b], sems.at[b])
                return [w_copy] + [
                    pltpu.make_async_copy(y_hbm.at[idx_v.at[k, pl.ds(ci * c, c)]], buf_v.at[b].at[k], sems.at[b])
                    for k in range(TOP_K)]

            def out_copy(ci):
                return pltpu.make_async_copy(out_v, o_hbm.at[pl.ds(base + ci * c, c)], out_sem)

            def reduce_rows(ci, b):
                @pl.when(ci >= 1)
                def _():
                    out_copy(ci - 1).wait()

                for t0 in range(0, c, COMBINE_TOKEN_GROUP):
                    group = range(t0, t0 + COMBINE_TOKEN_GROUP)
                    ws = [[w_v[b, t, pl.ds(k * SC_LANES, SC_LANES)] for k in range(TOP_K)] for t in group]

                    @plsc.parallel_loop(0, half // SC_LANES, unroll=2)
                    def _(j):
                        lanes = pl.ds(pl.multiple_of(j * SC_LANES, SC_LANES), SC_LANES)
                        hi_lanes = pl.ds(pl.multiple_of(half + j * SC_LANES, SC_LANES), SC_LANES)
                        for t, wt in zip(group, ws):
                            acc_lo = acc_hi = None
                            for k in range(TOP_K):
                                word = buf_v[b, k, t, lanes]
                                lo = plsc.bitcast(lax.shift_left(word, jnp.int32(16)), F32) * wt[k]
                                hi = plsc.bitcast(word & jnp.int32(-65536), F32) * wt[k]
                                acc_lo = lo if acc_lo is None else acc_lo + lo
                                acc_hi = hi if acc_hi is None else acc_hi + hi
                            out_v[t, lanes] = acc_lo
                            out_v[t, hi_lanes] = acc_hi

                out_copy(ci).start()

            for cp in in_copies(0, 0):
                cp.start()

            @pl.loop(0, nch, step=2)
            def _(ci):
                for b in range(2):
                    @pl.when(ci + b + 1 < nch)
                    def _():
                        for cp in in_copies(ci + b + 1, 1 - b):
                            cp.start()

                    for cp in in_copies(ci + b, b):
                        cp.wait()
                    reduce_rows(ci + b, b)

            out_copy(nch - 1).wait()

        segment(pa_hbm, wa_hbm, oa_hbm, idx_a)
        segment(pb_hbm, wb_hbm, ob_hbm, idx_b)

    return run(y, pos_a, w_a, pos_b, w_b)


def _shared_kernel(x1_ref, h2_ref, g2_ref, wsg_ref, wsu_ref, wsd_ref, o_ref):
    half = h2_ref.shape[1]
    x_lo, x_hi = _unpack_bf16_pairs(h2_ref[...])
    wsg, wsu = wsg_ref[...], wsu_ref[...]
    a = (jnp.dot(x_lo, wsg[:half], preferred_element_type=F32) + jnp.dot(x_hi, wsg[half:], preferred_element_type=F32))
    b = (jnp.dot(x_lo, wsu[:half], preferred_element_type=F32) + jnp.dot(x_hi, wsu[half:], preferred_element_type=F32))
    shared = jnp.dot((_silu(a) * b).astype(BF16), wsd_ref[...], preferred_element_type=F32)
    o_ref[...] = x1_ref[...] + g2_ref[...] * shared


def _shared(x1, h2, g2, wsg, wsu, wsd, tm, mod_spec):
    n = h2.shape[0]
    full = lambda arr: pl.BlockSpec(arr.shape, lambda i: (0,) * arr.ndim)
    tile = pl.BlockSpec((tm, D_MODEL), lambda i: (i, 0))
    return pl.pallas_call(
        _shared_kernel,
        grid=(n // tm,),
        in_specs=[tile, pl.BlockSpec((tm, D_MODEL // 2), lambda i: (i, 0)), mod_spec, full(wsg), full(wsu), full(wsd)],
        out_specs=tile,
        out_shape=jax.ShapeDtypeStruct((n, D_MODEL), F32),
        compiler_params=pltpu.CompilerParams(dimension_semantics=("arbitrary",), vmem_limit_bytes=VMEM_LIMIT),
        name="shared_expert",
    )(x1, h2, g2, wsg, wsu, wsd)


def _final_kernel(base_ref, routed_ref, g2_ref, o_ref):
    o_ref[...] = base_ref[...] + g2_ref[...] * routed_ref[...]


def _final(base, routed, g2, tm, mod_spec):
    n = base.shape[0]
    tile = pl.BlockSpec((tm, D_MODEL), lambda i: (i, 0))
    return pl.pallas_call(
        _final_kernel,
        grid=(n // tm,),
        in_specs=[tile, tile, mod_spec],
        out_specs=tile,
        out_shape=jax.ShapeDtypeStruct((n, D_MODEL), F32),
        compiler_params=pltpu.CompilerParams(dimension_semantics=("arbitrary",), vmem_limit_bytes=VMEM_LIMIT),
        name="ffn_residual",
    )(base, routed, g2)


def _block_diag(w):
    nb, bi, bj = w.shape
    return jnp.einsum('nij,nm->nimj', w, jnp.eye(nb, dtype=w.dtype)).reshape(nb * bi, nb * bj)


def kernel(x_prompt, x_sample, c_prompt, c_sample, cache_k_win, cache_v_win, state_conv, state_rnn, ada_w, ada_b, norm_mix, w_in, conv_w, conv_b, gate_a_w, gate_a_b, gate_x_w, gate_x_b, lru_lambda, q_norm, k_norm, attn_sinks, w_out, norm_ffn, router_w, router_bias, exp_w_gate, exp_w_up, exp_w_down, sh_w_gate, sh_w_up, sh_w_down):
    bp, tp, _ = x_prompt.shape
    bs, ts, _ = x_sample.shape
    win = cache_k_win.shape[2]
    n_p, n_s = bp * tp, bs * ts
    row = lambda v: v.reshape(1, -1)

    g_mix, g_ffn = row(norm_mix[0]), row(norm_ffn[0])
    win_bf = w_in[0].astype(BF16)
    q0 = 2 * D_RNN
    win_pair = jnp.concatenate([win_bf[:, :q0], _pair_heads(win_bf[:, q0:q0 + D_ATT], 1), win_bf[:, q0 + D_ATT:]],
                               axis=1)
    cw, cb = conv_w[0], row(conv_b[0])
    wg = jnp.concatenate([_block_diag(gate_a_w[0]), _block_diag(gate_x_w[0])], axis=1).astype(BF16)
    gb = row(jnp.concatenate([gate_a_b[0], gate_x_b[0]]))
    lam = row(lru_lambda[0])
    gq = row(q_norm[0])
    gk2 = row(jnp.tile(k_norm[0], N_KV_HEADS))
    sinks = attn_sinks[0]
    wo_top, wo_bot = w_out[0, :D_RNN].astype(BF16), w_out[0, D_RNN:].astype(BF16)
    wr_t = router_w[0].T
    rb = router_bias[0].reshape(N_EXPERTS, 1)
    wsg, wsu, wsd = sh_w_gate[0].astype(BF16), sh_w_up[0].astype(BF16), sh_w_down[0].astype(BF16)

    mod = _adaln(jnp.concatenate([c_prompt, c_sample], axis=0), ada_w[0], ada_b[0])
    chunks = [mod[:, i * D_MODEL:(i + 1) * D_MODEL] for i in range(6)]
    sh1p, sc1p, g1p, sh2p, sc2p, g2p = [c[:bp].reshape(bp, 1, D_MODEL) for c in chunks]
    sh1s, sc1s, g1s, sh2s, sc2s, g2s = [c[bp:] for c in chunks]

    conv0 = jnp.zeros((bp, SUBLANES, D_RNN), F32)
    h0 = jnp.zeros((bp, 1, D_RNN), F32)
    r_p, q_p, k_p, v_p, cs_p, hs_p = _front_prompt(x_prompt, sc1p, sh1p, g_mix, win_pair, cw, cb, wg, gb, lam, gk2,
                                                   conv0, h0)
    a_p = _attn_prompt(q_p, k_p, v_p, row(jnp.tile(q_norm[0], 2)), sinks)

    x_s_t = jnp.swapaxes(x_sample, 0, 1)
    r_s, q_s, k_s, v_s, cs_s, hs_s = _front_sample(x_s_t, sc1s, sh1s, g_mix, win_bf, cw, cb, wg, gb, lam, gk2,
                                                   jnp.swapaxes(state_conv[0], 0, 1), state_rnn[0])
    cache_k = cache_k_win[0].reshape(bs, win, D_KV)
    cache_v = cache_v_win[0].reshape(bs, win, D_KV)
    a_s = _attn_sample(q_s, k_s, v_s, cache_k, cache_v, gq, sinks)

    tiles_per_seq = tp // TM_POST
    mod_p = pl.BlockSpec((None, 1, D_MODEL), lambda i, *_: (i // tiles_per_seq, 0, 0))
    mod_s = pl.BlockSpec((bs, D_MODEL), lambda i, *_: (0, 0))
    cnt0 = jnp.zeros((N_EXPERTS, LANES), F32)
    x1_p, h2_p, eidx_p, gw_p, rank_p, cnt_p = _post(
        x_prompt.reshape(n_p, D_MODEL), r_p.reshape(n_p, D_RNN), a_p.reshape(n_p, D_ATT), g1p, sc2p, sh2p, g_ffn,
        wo_top, _pair_heads(wo_bot, 0), wr_t, rb, cnt0, TM_POST, mod_p)
    x1_s, h2_s, eidx_s, gw_s, rank_s, cnt_all = _post(
        x_s_t.reshape(n_s, D_MODEL), r_s.reshape(n_s, D_RNN), a_s.reshape(n_s, D_ATT), g1s, sc2s, sh2s, g_ffn,
        wo_top, wo_bot, wr_t, rb, cnt_p, bs, mod_s)

    bm = EXPERT_BLOCK
    n_blocks = -(-((n_p + n_s) * TOP_K) // bm) + N_EXPERTS
    pstart, meta = _plan(cnt_all, bm, n_blocks)
    counts = cnt_all[:, 0].astype(I32)
    first_blk = (pstart[:, 0] * (1.0 / bm)).astype(I32)
    n_blk = (counts + (bm - 1)) // bm
    n_used = meta[2, :1]

    def sc_layouts(eidx, rank, gw, n):
        per_w = n // SC_WORKERS
        c = min(DISPATCH_CHUNK, per_w)
        outs = _layouts(pstart, eidx, rank, gw, (c,) if c == per_w else (c, per_w))
        return (outs[0], c), (outs[-2], outs[-1])

    (dpos_p, c_p), comb_p = sc_layouts(eidx_p, rank_p, gw_p, n_p)
    (dpos_s, c_s), comb_s = sc_layouts(eidx_s, rank_s, gw_s, n_s)
    xs = _sc_dispatch((h2_p, dpos_p, c_p), (h2_s, dpos_s, c_s), n_blocks * bm)
    base_p = _shared(x1_p, h2_p, g2p, wsg, wsu, wsd, TM_POST, mod_p)
    base_s = _shared(x1_s, h2_s, g2s, wsg, wsu, wsd, bs, mod_s)
    y = _experts(first_blk, n_blk, counts, n_used, xs, exp_w_gate[0], exp_w_up[0], exp_w_down[0])
    routed_p, routed_s = _sc_combine(y, comb_p, comb_s)
    y_p = _final(base_p, routed_p, g2p, TM_POST, mod_p)
    y_s = _final(base_s, routed_s, g2s, bs, mod_s)

    y_prompt = y_p.reshape(bp, tp, D_MODEL)
    y_sample = jnp.swapaxes(y_s.reshape(ts, bs, D_MODEL), 0, 1)
    wk = min(WINDOW, tp)
    k_win_p = k_p[:, tp - wk:].reshape(1, bp, wk, N_KV_HEADS, HEAD_DIM)
    v_win_p = v_p[:, tp - wk:].reshape(1, bp, wk, N_KV_HEADS, HEAD_DIM)
    k_new = jnp.swapaxes(k_s, 0, 1)
    v_new = jnp.swapaxes(v_s, 0, 1)
    heads = lambda a: a.reshape(bs, ts, N_KV_HEADS, HEAD_DIM)
    k_win_s = jnp.concatenate([cache_k_win[0], heads(k_new)], axis=1)[None, :, ts:]
    v_win_s = jnp.concatenate([cache_v_win[0], heads(v_new)], axis=1)[None, :, ts:]
    return (y_prompt, y_sample, k_win_p, v_win_p, cs_p[None, :, SUBLANES - (CONV_W - 1):], hs_p.reshape(1, bp, D_RNN),
            k_win_s, v_win_s, jnp.swapaxes(cs_s, 0, 1)[None], hs_s[None])
```

```python
import dataclasses
import functools

import jax
import jax.numpy as jnp
from jax import lax
from jax.experimental import pallas as pl
from jax.experimental.pallas import tpu as pltpu
from jax.experimental.pallas import tpu_sc as plsc

F32 = jnp.float32
BF16 = jnp.bfloat16
I32 = jnp.int32

D_MODEL = 1024
D_RNN = 512
N_RNN_BLOCKS = 8
CONV_W = 4
LRU_C = 8.0
HEAD_DIM = 64
N_HEADS = 8
N_KV_HEADS = 2
GQA = N_HEADS // N_KV_HEADS
D_ATT = N_HEADS * HEAD_DIM
D_KV = N_KV_HEADS * HEAD_DIM
WINDOW = 128
N_EXPERTS = 64
TOP_K = 6
N_GROUPS = 8
GROUP_SIZE = N_EXPERTS // N_GROUPS
TOPK_GROUPS = 4
D_EXPERT = 256
D_SHARED = 256
ROUTE_SCALE = 2.5
EPS = 1e-6
NEG = -1e30
F32_TINY = 1.1754944e-38
D_IN = 2 * D_RNN + D_ATT + 2 * D_KV

SUBLANES = 8
LANES = 128
TM_PROMPT = 512
TM_POST = 512
ATT_BLOCK = WINDOW
ATT_STEP_BLOCKS = 8
EXPERT_BLOCK = 512
EXPERT_RING = 4
SAMPLE_CHUNK = 8
POS_TILE = 2048
VMEM_LIMIT = 48 * 1024 * 1024

SC_CORES = 2
SC_SUBCORES = 16
SC_WORKERS = SC_CORES * SC_SUBCORES
SC_LANES = 16
DISPATCH_CHUNK = 64
COMBINE_CHUNK = 8
COMBINE_TOKEN_GROUP = 4


def _sigmoid(x):
    return 0.5 * jnp.tanh(0.5 * x) + 0.5


def _silu(x):
    return x * _sigmoid(x)


def _gelu_tanh(x):
    c = 0.7978845608028654
    return x * (0.5 * (1.0 + jnp.tanh(c * (x + 0.044715 * (x * x * x)))))


def _log1p(x):
    u = 1.0 + x
    return jnp.where(u == 1.0, x, jnp.log(u) * x / jnp.where(u == 1.0, 1.0, u - 1.0))


def _neg_expm1_2x(x, exp_x):
    return -jnp.tanh(x) * (exp_x * exp_x + 1.0)


def _softplus(z):
    return jnp.maximum(z, 0.0) + _log1p(jnp.exp(-jnp.abs(z)))


def _div_pow2(x, d):
    assert d & (d - 1) == 0
    return lax.shift_right_logical(x, d.bit_length() - 1)


def _mod_pow2(x, d):
    assert d & (d - 1) == 0
    return x & (d - 1)


def _norm_mod(x, g, sc, sh):
    ms = jnp.mean(x * x, axis=-1, keepdims=True)
    return (x * lax.rsqrt(ms + EPS)) * g * (1.0 + sc) + sh


def _bdot(a, b):
    return jnp.dot(a.astype(BF16), b.astype(BF16), preferred_element_type=F32)


def _bdot_nt(a, b):
    return lax.dot_general(a.astype(BF16), b.astype(BF16), (((1,), (1,)), ((), ())),
                           preferred_element_type=F32)


def _pack_bf16_pairs(x):
    w = x.shape[1] // 2

    def bf16_bits(v):
        return pltpu.bitcast(v.astype(BF16).astype(F32), I32)

    return lax.shift_right_logical(bf16_bits(x[:, :w]), 16) | (bf16_bits(x[:, w:]) & jnp.int32(-65536))


def _unpack_bf16_pairs(p):
    lo = pltpu.bitcast(lax.shift_left(p, 16), F32)
    hi = pltpu.bitcast(p & jnp.int32(-65536), F32)
    return lo.astype(BF16), hi.astype(BF16)


def _knorm(k, gk2):
    lane = lax.broadcasted_iota(I32, k.shape, 1)
    first = lane < HEAD_DIM
    k2 = k * k
    s0 = jnp.sum(jnp.where(first, k2, 0.0), axis=-1, keepdims=True)
    s1 = jnp.sum(jnp.where(first, 0.0, k2), axis=-1, keepdims=True)
    ms = jnp.where(first, s0, s1) * (1.0 / HEAD_DIM)
    return k * lax.rsqrt(ms + EPS) * gk2


def _pair_heads(w, axis):
    shape = w.shape
    split = shape[:axis] + (N_KV_HEADS, GQA, HEAD_DIM) + shape[axis + 1:]
    return jnp.swapaxes(w.reshape(split), axis, axis + 1).reshape(shape)


def _lru_coeffs(u, wg, gb, lam):
    g = _bdot(u, wg) + gb
    r = _sigmoid(g[:, :D_RNN])
    i = _sigmoid(g[:, D_RNN:])
    log_a = (-LRU_C * r) * _softplus(-lam)
    a = jnp.exp(log_a)
    om = _neg_expm1_2x(log_a, a)
    b = (om * lax.rsqrt(jnp.maximum(om, F32_TINY))) * (i * u)
    return a, b


def _adaln_kernel(c_ref, w_ref, b_ref, o_ref):
    o_ref[...] = _bdot(_silu(c_ref[...]), w_ref[...]) + b_ref[...]


def _adaln(c_all, ada_w, ada_b):
    n = c_all.shape[0]
    return pl.pallas_call(
        _adaln_kernel,
        grid=(6,),
        in_specs=[pl.BlockSpec((n, D_MODEL), lambda j: (0, 0)),
                  pl.BlockSpec((D_MODEL, D_MODEL), lambda j: (0, j)),
                  pl.BlockSpec((1, D_MODEL), lambda j: (0, j))],
        out_specs=pl.BlockSpec((n, D_MODEL), lambda j: (0, j)),
        out_shape=jax.ShapeDtypeStruct((n, 6 * D_MODEL), F32),
        compiler_params=pltpu.CompilerParams(dimension_semantics=("arbitrary",), vmem_limit_bytes=VMEM_LIMIT),
        name="adaln",
    )(c_all, ada_w, ada_b.reshape(1, -1))


def _scan_rows(a, b, h_in):
    n, c = a.shape
    groups = n // SUBLANES
    a = a.reshape(groups, SUBLANES, c)
    b = b.reshape(groups, SUBLANES, c)
    sub = lax.broadcasted_iota(I32, a.shape, 1)
    s = 1
    while s < SUBLANES:
        m = sub >= s
        a_sh = jnp.where(m, pltpu.roll(a, s, 1), 1.0)
        b_sh = jnp.where(m, pltpu.roll(b, s, 1), 0.0)
        b = a * b_sh + b
        a = a * a_sh
        s *= 2
    carry = h_in
    hs = []
    for g in range(groups):
        hg = a[g] * carry + b[g]
        hs.append(hg)
        carry = hg[SUBLANES - 1:SUBLANES]
    return jnp.concatenate(hs, axis=0)


def _front_prompt_kernel(x_ref, sc_ref, sh_ref, g_ref, win_ref, cw_ref, cb_ref, wg_ref, gb_ref, lam_ref, gk_ref,
                         prev_ref, h0_ref, r_ref, q_ref, k_ref, v_ref, cs_ref, hs_ref, tail_ref, hc_ref):
    j = pl.program_id(1)
    tm = x_ref.shape[0]

    @pl.when(j == 0)
    def _():
        tail_ref[...] = prev_ref[...]
        hc_ref[...] = h0_ref[...]

    h = _norm_mod(x_ref[...], g_ref[...], sc_ref[...], sh_ref[...])
    proj = jnp.dot(h.astype(BF16), win_ref[...], preferred_element_type=F32)
    xr = proj[:, 0:D_RNN]
    yr = proj[:, D_RNN:2 * D_RNN]
    q_ref[...] = proj[:, 2 * D_RNN:2 * D_RNN + D_ATT]
    k_ref[...] = _knorm(proj[:, 2 * D_RNN + D_ATT:2 * D_RNN + D_ATT + D_KV], gk_ref[...])
    v_ref[...] = proj[:, 2 * D_RNN + D_ATT + D_KV:D_IN]

    tail = tail_ref[...]
    row8 = lax.broadcasted_iota(I32, tail.shape, 0)

    def shifted(s):
        rolled = pltpu.roll(xr, s, 0)
        top = jnp.where(row8 < s, pltpu.roll(tail, s, 0), rolled[0:SUBLANES])
        return jnp.concatenate([top, rolled[SUBLANES:]], axis=0)

    cw = cw_ref[...]
    u = cb_ref[...] + shifted(3) * cw[0:1]
    u = u + shifted(2) * cw[1:2]
    u = u + shifted(1) * cw[2:3]
    u = u + xr * cw[3:4]
    tail_ref[...] = xr[tm - SUBLANES:tm]

    a, b = _lru_coeffs(u, wg_ref[...], gb_ref[...], lam_ref[...])
    hs = _scan_rows(a, b, hc_ref[...])
    hc_ref[...] = hs[tm - 1:tm]
    r_ref[...] = hs * _gelu_tanh(yr)

    @pl.when(j == pl.num_programs(1) - 1)
    def _():
        cs_ref[...] = xr[tm - SUBLANES:tm]
        hs_ref[...] = hs[tm - 1:tm]


def _front_prompt(x, sc, sh, g, win, cw, cb, wg, gb, lam, gk, prev, h0):
    bsz, t, _ = x.shape
    tm = TM_PROMPT
    full = lambda a: pl.BlockSpec(a.shape, lambda b, j: (0,) * a.ndim)
    per_b = lambda a: pl.BlockSpec((None,) + a.shape[1:], lambda b, j: (b,) + (0,) * (a.ndim - 1))
    tile = lambda w: pl.BlockSpec((None, tm, w), lambda b, j: (b, j, 0))
    return pl.pallas_call(
        _front_prompt_kernel,
        grid=(bsz, t // tm),
        in_specs=[tile(D_MODEL), per_b(sc), per_b(sh), full(g), full(win), full(cw), full(cb), full(wg), full(gb),
                  full(lam), full(gk), per_b(prev), per_b(h0)],
        out_specs=[tile(D_RNN), tile(D_ATT), tile(D_KV), tile(D_KV),
                   pl.BlockSpec((None, SUBLANES, D_RNN), lambda b, j: (b, 0, 0)),
                   pl.BlockSpec((None, 1, D_RNN), lambda b, j: (b, 0, 0))],
        out_shape=[jax.ShapeDtypeStruct((bsz, t, D_RNN), F32), jax.ShapeDtypeStruct((bsz, t, D_ATT), F32),
                   jax.ShapeDtypeStruct((bsz, t, D_KV), F32), jax.ShapeDtypeStruct((bsz, t, D_KV), F32),
                   jax.ShapeDtypeStruct((bsz, SUBLANES, D_RNN), F32), jax.ShapeDtypeStruct((bsz, 1, D_RNN), F32)],
        scratch_shapes=[pltpu.VMEM((SUBLANES, D_RNN), F32), pltpu.VMEM((1, D_RNN), F32)],
        compiler_params=pltpu.CompilerParams(dimension_semantics=("arbitrary", "arbitrary"),
                                             vmem_limit_bytes=VMEM_LIMIT),
        name="front_prompt",
    )(x, sc, sh, g, win, cw, cb, wg, gb, lam, gk, prev, h0)


def _front_sample_kernel(x_ref, sc_ref, sh_ref, g_ref, win_ref, cw_ref, cb_ref, wg_ref, gb_ref, lam_ref, gk_ref,
                         prev_ref, h0_ref, r_ref, q_ref, k_ref, v_ref, cs_ref, hs_ref):
    t_len, bsz, _ = x_ref.shape
    x = x_ref[...]
    ms = jnp.mean(x * x, axis=-1, keepdims=True)
    h = (x * lax.rsqrt(ms + EPS)) * g_ref[...] * (1.0 + sc_ref[...]) + sh_ref[...]
    proj = jnp.dot(h.reshape(t_len * bsz, D_MODEL).astype(BF16), win_ref[...], preferred_element_type=F32)
    xr = proj[:, 0:D_RNN]
    yr = proj[:, D_RNN:2 * D_RNN]
    q_ref[...] = proj[:, 2 * D_RNN:2 * D_RNN + D_ATT].reshape(t_len, bsz, D_ATT)
    k_ref[...] = _knorm(proj[:, 2 * D_RNN + D_ATT:2 * D_RNN + D_ATT + D_KV], gk_ref[...]).reshape(t_len, bsz, D_KV)
    v_ref[...] = proj[:, 2 * D_RNN + D_ATT + D_KV:D_IN].reshape(t_len, bsz, D_KV)

    def at_time(t):
        if t >= 0:
            return xr[t * bsz:(t + 1) * bsz]
        return prev_ref[CONV_W - 1 + t]

    cw = cw_ref[...]
    us = []
    for t in range(t_len):
        u = cb_ref[...] + at_time(t - 3) * cw[0:1]
        u = u + at_time(t - 2) * cw[1:2]
        u = u + at_time(t - 1) * cw[2:3]
        u = u + at_time(t) * cw[3:4]
        us.append(u)
    a, b = _lru_coeffs(jnp.concatenate(us, axis=0), wg_ref[...], gb_ref[...], lam_ref[...])
    hcur = h0_ref[...]
    for t in range(t_len):
        hcur = a[t * bsz:(t + 1) * bsz] * hcur + b[t * bsz:(t + 1) * bsz]
        r_ref[t] = hcur * _gelu_tanh(yr[t * bsz:(t + 1) * bsz])
    hs_ref[...] = hcur
    for s in range(CONV_W - 1):
        cs_ref[s] = at_time(t_len - (CONV_W - 1) + s)


def _front_sample(x_t, sc, sh, g, win, cw, cb, wg, gb, lam, gk, prev_t, h0):
    t_len, bsz, _ = x_t.shape
    return pl.pallas_call(
        _front_sample_kernel,
        out_shape=[jax.ShapeDtypeStruct((t_len, bsz, D_RNN), F32), jax.ShapeDtypeStruct((t_len, bsz, D_ATT), F32),
                   jax.ShapeDtypeStruct((t_len, bsz, D_KV), F32), jax.ShapeDtypeStruct((t_len, bsz, D_KV), F32),
                   jax.ShapeDtypeStruct((CONV_W - 1, bsz, D_RNN), F32), jax.ShapeDtypeStruct((bsz, D_RNN), F32)],
        compiler_params=pltpu.CompilerParams(vmem_limit_bytes=VMEM_LIMIT),
        name="front_sample",
    )(x_t, sc, sh, g, win, cw, cb, wg, gb, lam, gk, prev_t, h0)


def _qnorm(q, gq):
    ms = jnp.mean(q * q, axis=-1, keepdims=True)
    return q * lax.rsqrt(ms + EPS) * gq


def _attn_prompt_kernel(sink_ref, q_ref, kp_ref, kc_ref, vp_ref, vc_ref, gq_ref, o_ref):
    j = pl.program_id(1)
    blk = kp_ref.shape[0]
    k_all = jnp.concatenate([kp_ref[...], kc_ref[...]], axis=0)
    v_all = jnp.concatenate([vp_ref[...], vc_ref[...]], axis=0)
    qi = lax.broadcasted_iota(I32, (blk, 2 * blk), 0)
    kj = lax.broadcasted_iota(I32, (blk, 2 * blk), 1)
    dist = blk + qi - kj
    window = (dist >= 0) & (dist <= WINDOW)
    distf = dist.astype(F32)
    slab = 2 * HEAD_DIM
    first_q = lax.broadcasted_iota(I32, (blk, slab), 1) < HEAD_DIM
    first_kv = lax.broadcasted_iota(I32, (2 * blk, slab), 1) < HEAD_DIM

    bias = [jnp.where(window, (-(2.0 ** -(h + 1))) * distf, NEG) for h in range(N_HEADS)]
    no_prev = (kj < blk) & (j == 0)
    gq = gq_ref[...] * (HEAD_DIM ** -0.5)

    def probs(s, h, first_block):
        b = jnp.where(no_prev, NEG, bias[h]) if first_block else bias[h]
        s = s + b
        sink = sink_ref[h]
        m = jnp.maximum(jnp.max(s, axis=-1, keepdims=True), sink)
        p = jnp.exp(s - m)
        return p, jnp.sum(p, axis=-1, keepdims=True) + jnp.exp(sink - m)

    for sub in range(q_ref.shape[0] // blk):
        first = sub == 0
        q = q_ref[sub * blk:(sub + 1) * blk, :]
        kk = k_all[sub * blk:(sub + 2) * blk]
        vv = v_all[sub * blk:(sub + 2) * blk]
        v_a = jnp.where(first_kv, vv, 0.0)
        v_b = jnp.where(first_kv, 0.0, vv)
        outs = []
        for i in range(GQA):
            q2 = _knorm(q[:, i * slab:(i + 1) * slab], gq)
            p_a, l_a = probs(_bdot_nt(jnp.where(first_q, q2, 0.0), kk), i, first)
            p_b, l_b = probs(_bdot_nt(jnp.where(first_q, 0.0, q2), kk), GQA + i, first)
            outs.append((_bdot(p_a, v_a) + _bdot(p_b, v_b)) / jnp.where(first_q, l_a, l_b))
        o_ref[sub * blk:(sub + 1) * blk, :] = jnp.concatenate(outs, axis=1)


def _attn_prompt(q, k, v, gq, sinks):
    bsz, t, _ = q.shape
    blk = ATT_BLOCK
    nsub = ATT_STEP_BLOCKS
    cur = lambda w: pl.BlockSpec((None, nsub * blk, w), lambda b, j: (b, j, 0))
    prv = lambda w: pl.BlockSpec((None, blk, w), lambda b, j: (b, jnp.maximum(nsub * j - 1, 0), 0))
    return pl.pallas_call(
        _attn_prompt_kernel,
        grid=(bsz, t // (nsub * blk)),
        in_specs=[pl.BlockSpec(memory_space=pltpu.SMEM), cur(D_ATT), prv(D_KV), cur(D_KV), prv(D_KV), cur(D_KV),
                  pl.BlockSpec(gq.shape, lambda b, j: (0, 0))],
        out_specs=cur(D_ATT),
        out_shape=jax.ShapeDtypeStruct((bsz, t, D_ATT), F32),
        compiler_params=pltpu.CompilerParams(dimension_semantics=("arbitrary", "arbitrary"),
                                             vmem_limit_bytes=VMEM_LIMIT),
        name="attn_prompt",
    )(sinks, q, k, k, v, v, gq)


def _attn_sample_kernel(sink_ref, q_ref, kn_ref, vn_ref, kc_ref, vc_ref, gq_ref, o_ref):
    t_len, cb, _ = q_ref.shape
    win = kc_ref.shape[1]
    rows = GQA * t_len * cb
    kc = kc_ref[...].reshape(cb * win, D_KV)
    vc = vc_ref[...].reshape(cb * win, D_KV)
    kn = kn_ref[...].reshape(t_len * cb, D_KV)
    vn = vn_ref[...].reshape(t_len * cb, D_KV)

    r_c = lax.broadcasted_iota(I32, (rows, cb * win), 0)
    c_c = lax.broadcasted_iota(I32, (rows, cb * win), 1)
    t_c = _div_pow2(_mod_pow2(r_c, t_len * cb), cb)
    valid_c = (_mod_pow2(r_c, cb) == _div_pow2(c_c, win)) & (_mod_pow2(c_c, win) >= t_c)
    dist_c = (win + t_c - _mod_pow2(c_c, win)).astype(F32)
    r_n = lax.broadcasted_iota(I32, (rows, t_len * cb), 0)
    c_n = lax.broadcasted_iota(I32, (rows, t_len * cb), 1)
    t_n = _div_pow2(_mod_pow2(r_n, t_len * cb), cb)
    valid_n = (_mod_pow2(r_n, cb) == _mod_pow2(c_n, cb)) & (_div_pow2(c_n, cb) <= t_n)
    dist_n = (t_n - _div_pow2(c_n, cb)).astype(F32)
    hl = _div_pow2(lax.broadcasted_iota(I32, (rows, 1), 0), t_len * cb)

    per_group = []
    for g in range(N_KV_HEADS):
        slabs = [q_ref[t][:, (g * GQA + i) * HEAD_DIM:(g * GQA + i + 1) * HEAD_DIM]
                 for i in range(GQA) for t in range(t_len)]
        qg = _qnorm(jnp.concatenate(slabs, axis=0), gq_ref[...])
        slope = jnp.zeros((rows, 1), F32)
        sink = jnp.zeros((rows, 1), F32)
        for i in range(GQA):
            slope = jnp.where(hl == i, 2.0 ** -(g * GQA + i + 1), slope)
            sink = jnp.where(hl == i, sink_ref[g * GQA + i], sink)
        lo, hi = g * HEAD_DIM, (g + 1) * HEAD_DIM
        s_c = _bdot_nt(qg, kc[:, lo:hi]) * (HEAD_DIM ** -0.5) - slope * dist_c
        s_n = _bdot_nt(qg, kn[:, lo:hi]) * (HEAD_DIM ** -0.5) - slope * dist_n
        s_c = jnp.where(valid_c, s_c, NEG)
        s_n = jnp.where(valid_n, s_n, NEG)
        m = jnp.maximum(jnp.maximum(jnp.max(s_c, axis=-1, keepdims=True), jnp.max(s_n, axis=-1, keepdims=True)), sink)
        p_c = jnp.exp(s_c - m)
        p_n = jnp.exp(s_n - m)
        l = jnp.sum(p_c, axis=-1, keepdims=True) + jnp.sum(p_n, axis=-1, keepdims=True) + jnp.exp(sink - m)
        per_group.append((_bdot(p_c, vc[:, lo:hi]) + _bdot(p_n, vn[:, lo:hi])) / l)
    for t in range(t_len):
        o_ref[t] = jnp.concatenate(
            [per_group[g][(i * t_len + t) * cb:(i * t_len + t + 1) * cb] for g in range(N_KV_HEADS) for i in range(GQA)],
            axis=1)


def _attn_sample(q_t, k_t, v_t, cache_k, cache_v, gq, sinks):
    t_len, bsz, _ = q_t.shape
    cb = SAMPLE_CHUNK
    win = cache_k.shape[1]
    new = lambda w: pl.BlockSpec((t_len, cb, w), lambda c: (0, c, 0))
    old = pl.BlockSpec((cb, win, D_KV), lambda c: (c, 0, 0))
    return pl.pallas_call(
        _attn_sample_kernel,
        grid=(bsz // cb,),
        in_specs=[pl.BlockSpec(memory_space=pltpu.SMEM), new(D_ATT), new(D_KV), new(D_KV), old, old,
                  pl.BlockSpec(gq.shape, lambda c: (0, 0))],
        out_specs=new(D_ATT),
        out_shape=jax.ShapeDtypeStruct((t_len, bsz, D_ATT), F32),
        compiler_params=pltpu.CompilerParams(dimension_semantics=("arbitrary",), vmem_limit_bytes=VMEM_LIMIT),
        name="attn_sample",
    )(sinks, q_t, k_t, v_t, cache_k, cache_v, gq)


def _route(s_t, sb_t):
    tm = s_t.shape[1]
    i8 = lax.broadcasted_iota(I32, (GROUP_SIZE, tm), 0)
    ninf = -jnp.inf
    sg = [sb_t[GROUP_SIZE * g:GROUP_SIZE * (g + 1)] for g in range(N_GROUPS)]
    gscore = []
    for g in range(N_GROUPS):
        m1 = jnp.max(sg[g], axis=0, keepdims=True)
        i1 = jnp.min(jnp.where(sg[g] == m1, i8, GROUP_SIZE), axis=0, keepdims=True)
        m2 = jnp.max(jnp.where(i8 == i1, ninf, sg[g]), axis=0, keepdims=True)
        gscore.append(m1 + m2)
    gs = jnp.concatenate(gscore, axis=0)
    gsel = jnp.zeros((N_GROUPS, tm), I32)
    for _ in range(TOPK_GROUPS):
        m = jnp.max(gs, axis=0, keepdims=True)
        idx = jnp.min(jnp.where(gs == m, i8, N_GROUPS), axis=0, keepdims=True)
        hit = i8 == idx
        gsel = jnp.where(hit, 1, gsel)
        gs = jnp.where(hit, ninf, gs)
    sm = [jnp.where(gsel[g:g + 1] > 0, sg[g], NEG) for g in range(N_GROUPS)]
    eid = [i8 + GROUP_SIZE * g for g in range(N_GROUPS)]
    sel = [jnp.zeros((GROUP_SIZE, tm), F32) for _ in range(N_GROUPS)]
    idxs, ws = [], []
    for _ in range(TOP_K):
        cm = functools.reduce(jnp.maximum, sm)
        m = jnp.max(cm, axis=0, keepdims=True)
        cand = functools.reduce(jnp.minimum, [jnp.where(sm[g] == m, eid[g], N_EXPERTS) for g in range(N_GROUPS)])
        idx = jnp.min(cand, axis=0, keepdims=True)
        wk = jnp.zeros((GROUP_SIZE, tm), F32)
        for g in range(N_GROUPS):
            hit = eid[g] == idx
            wk = wk + jnp.where(hit, s_t[GROUP_SIZE * g:GROUP_SIZE * (g + 1)], 0.0)
            sel[g] = jnp.where(hit, 1.0, sel[g])
            sm[g] = jnp.where(hit, ninf, sm[g])
        idxs.append(idx)
        ws.append(jnp.sum(wk, axis=0, keepdims=True))
    return idxs, ws, jnp.concatenate(sel, axis=0), eid


def _post_kernel(x_ref, r_ref, a_ref, g1_ref, sc2_ref, sh2_ref, nf_ref, wot_ref, wob_ref, wrt_ref, rb_ref, cin_ref,
                 x1_ref, h2_ref, eidx_ref, gw_ref, rank_ref, cnt_ref, carry_ref, before_ref):
    tm = x_ref.shape[0]

    @pl.when(pl.program_id(0) == 0)
    def _():
        carry_ref[...] = cin_ref[...]
        rr = lax.broadcasted_iota(I32, (tm, tm), 0)
        cc = lax.broadcasted_iota(I32, (tm, tm), 1)
        before_ref[...] = jnp.where(rr < cc, 1.0, 0.0).astype(BF16)

    mixed = (jnp.dot(r_ref[...].astype(BF16), wot_ref[...], preferred_element_type=F32)
             + jnp.dot(a_ref[...].astype(BF16), wob_ref[...], preferred_element_type=F32))
    x1 = x_ref[...] + g1_ref[...] * mixed
    x1_ref[...] = x1
    h2 = _norm_mod(x1, nf_ref[...], sc2_ref[...], sh2_ref[...])
    h2_ref[...] = _pack_bf16_pairs(h2)

    wr = wrt_ref[...]
    wr_hi = wr.astype(BF16)
    wr_lo = (wr - wr_hi.astype(F32)).astype(BF16)
    h_hi = h2.astype(BF16)
    h_lo = (h2 - h_hi.astype(F32)).astype(BF16)
    logits = _bdot_nt(wr_hi, h_hi) + _bdot_nt(wr_hi, h_lo) + _bdot_nt(wr_lo, h_hi)
    s_t = _sigmoid(logits)
    idxs, ws, sel, eid = _route(s_t, s_t + rb_ref[...])

    carry = carry_ref[...]
    tot = jnp.dot(sel.astype(BF16), before_ref[...], preferred_element_type=F32) + carry[:, 0:1]
    ranks = []
    for k in range(TOP_K):
        acc = jnp.zeros((GROUP_SIZE, tm), F32)
        for g in range(N_GROUPS):
            acc = acc + jnp.where(eid[g] == idxs[k], tot[GROUP_SIZE * g:GROUP_SIZE * (g + 1)], 0.0)
        ranks.append(jnp.sum(acc, axis=0, keepdims=True))
    carry = carry + jnp.sum(sel, axis=1, keepdims=True)
    carry_ref[...] = carry
    cnt_ref[...] = carry

    wsum = functools.reduce(lambda p, q: p + q, ws)
    pad_i = jnp.zeros((SUBLANES - TOP_K, tm), I32)
    pad_f = jnp.zeros((SUBLANES - TOP_K, tm), F32)
    eidx_ref[...] = jnp.concatenate(idxs + [pad_i], axis=0)
    rank_ref[...] = jnp.concatenate([r.astype(I32) for r in ranks] + [pad_i], axis=0)
    gw_ref[...] = jnp.concatenate([w / wsum * ROUTE_SCALE for w in ws] + [pad_f], axis=0)


def _post(x, r, a, g1, sc2, sh2, nf, wo_top, wo_bot, wr_t, rb, cnt_in, tm, mod_spec):
    n = x.shape[0]
    tile = lambda w: pl.BlockSpec((tm, w), lambda i: (i, 0))
    full = lambda arr: pl.BlockSpec(arr.shape, lambda i: (0,) * arr.ndim)
    slot = pl.BlockSpec((SUBLANES, tm), lambda i: (0, i))
    return pl.pallas_call(
        _post_kernel,
        grid=(n // tm,),
        in_specs=[tile(D_MODEL), tile(D_RNN), tile(D_ATT), mod_spec, mod_spec, mod_spec, full(nf), full(wo_top),
                  full(wo_bot), full(wr_t), full(rb), full(cnt_in)],
        out_specs=[tile(D_MODEL), tile(D_MODEL // 2), slot, slot, slot, full(cnt_in)],
        out_shape=[jax.ShapeDtypeStruct((n, D_MODEL), F32), jax.ShapeDtypeStruct((n, D_MODEL // 2), I32),
                   jax.ShapeDtypeStruct((SUBLANES, n), I32), jax.ShapeDtypeStruct((SUBLANES, n), F32),
                   jax.ShapeDtypeStruct((SUBLANES, n), I32), jax.ShapeDtypeStruct(cnt_in.shape, F32)],
        scratch_shapes=[pltpu.VMEM(cnt_in.shape, F32), pltpu.VMEM((tm, tm), BF16)],
        compiler_params=pltpu.CompilerParams(dimension_semantics=("arbitrary",), vmem_limit_bytes=VMEM_LIMIT),
        name="post_mix",
    )(x, r, a, g1, sc2, sh2, nf, wo_top, wo_bot, wr_t, rb, cnt_in)


def _plan_kernel(cnt_ref, ps_ref, used_ref, *, bm):
    cnt = cnt_ref[...]
    padded = jnp.ceil(cnt * (1.0 / bm)) * bm
    row = lax.broadcasted_iota(I32, cnt.shape, 0)
    pend = padded
    s = 1
    while s < N_EXPERTS:
        pend = pend + jnp.where(row >= s, pltpu.roll(pend, s, 0), 0.0)
        s *= 2
    ps_ref[...] = pend - padded
    used_ref[...] = jnp.broadcast_to(pend[N_EXPERTS - 1:N_EXPERTS] * (1.0 / bm), used_ref.shape).astype(I32)


def _plan(cnt, bm):
    assert bm & (bm - 1) == 0
    return pl.pallas_call(
        functools.partial(_plan_kernel, bm=bm),
        out_shape=[jax.ShapeDtypeStruct(cnt.shape, F32), jax.ShapeDtypeStruct((SUBLANES, LANES), I32)],
        name="moe_plan",
    )(cnt)


def _layout_kernel(ps_ref, eidx_ref, rank_ref, gw_ref, *out_refs, chunks):
    *pos_refs, w_ref = out_refs
    tn = eidx_ref.shape[1]
    e_iota = lax.broadcasted_iota(I32, (N_EXPERTS, tn), 0)
    ps = ps_ref[...][:, 0:1]
    rows = []
    for k in range(TOP_K):
        hit = e_iota == eidx_ref[k:k + 1, :]
        base = jnp.sum(jnp.where(hit, ps, 0.0), axis=0, keepdims=True)
        rows.append(base.astype(I32) + rank_ref[k:k + 1, :])
    rows.append(jnp.zeros((SUBLANES - TOP_K, tn), I32))
    pos = jnp.concatenate(rows, axis=0)
    for pos_ref, c in zip(pos_refs, chunks):
        for q in range(tn // c):
            pos_ref[q] = pos[:, q * c:(q + 1) * c]
    gw = gw_ref[...]
    rep = jnp.concatenate([jnp.broadcast_to(gw[k:k + 1], (SC_LANES, tn)) for k in range(SUBLANES)], axis=0)
    w_ref[...] = rep.T


def _layouts(pstart, eidx, rank, gw, chunks):
    n = eidx.shape[1]
    tn = min(n, POS_TILE)
    slot = pl.BlockSpec((SUBLANES, tn), lambda i: (0, i))
    return pl.pallas_call(
        functools.partial(_layout_kernel, chunks=chunks),
        grid=(n // tn,),
        in_specs=[pl.BlockSpec(pstart.shape, lambda i: (0, 0)), slot, slot, slot],
        out_specs=[pl.BlockSpec((tn // c, SUBLANES, c), lambda i: (i, 0, 0)) for c in chunks]
        + [pl.BlockSpec((tn, SUBLANES * SC_LANES), lambda i: (i, 0))],
        out_shape=[jax.ShapeDtypeStruct((n // c, SUBLANES, c), I32) for c in chunks]
        + [jax.ShapeDtypeStruct((n, SUBLANES * SC_LANES), F32)],
        compiler_params=pltpu.CompilerParams(dimension_semantics=("arbitrary",), vmem_limit_bytes=VMEM_LIMIT),
        name="moe_layout",
    )(pstart, eidx, rank, gw)


def _sc_worker_id():
    return lax.axis_index("s") * SC_CORES + lax.axis_index("c")


def _sc_mesh():
    return plsc.VectorSubcoreMesh(core_axis_name="c", subcore_axis_name="s")


def _sc_dispatch(seg_a, seg_b, total_rows):
    (h_a, pos_a, c_a), (h_b, pos_b, c_b) = seg_a, seg_b
    width = h_a.shape[1]

    @functools.partial(
        pl.kernel, mesh=_sc_mesh(), out_type=jax.ShapeDtypeStruct((total_rows, width), I32),
        scratch_types=[pltpu.VMEM((2, SUBLANES, c_a), I32), pltpu.VMEM((2, c_a, width), I32),
                       pltpu.VMEM((2, SUBLANES, c_b), I32), pltpu.VMEM((2, c_b, width), I32),
                       pltpu.SemaphoreType.DMA((2,)), pltpu.SemaphoreType.DMA((2,))])
    def run(ha_hbm, pa_hbm, hb_hbm, pb_hbm, xs_hbm, idx_a, rows_a, idx_b, rows_b, in_sems, out_sems):
        wid = _sc_worker_id()

        def segment(h_hbm, p_hbm, idx_v, rows_v, c):
            nch = h_hbm.shape[0] // (SC_WORKERS * c)
            assert nch == 1 or nch % 2 == 0
            chunk0 = wid * nch

            def loads(ci, b):
                return [pltpu.make_async_copy(p_hbm.at[chunk0 + ci], idx_v.at[b], in_sems.at[b]),
                        pltpu.make_async_copy(h_hbm.at[pl.ds((chunk0 + ci) * c, c)], rows_v.at[b], in_sems.at[b])]

            def scatters(b):
                return [pltpu.make_async_copy(rows_v.at[b], xs_hbm.at[idx_v.at[b].at[k]], out_sems.at[b])
                        for k in range(TOP_K)]

            def start(copies):
                for cp in copies:
                    cp.start()

            def wait(copies):
                for cp in copies:
                    cp.wait()

            start(loads(0, 0))
            if nch == 1:
                wait(loads(0, 0))
                start(scatters(0))
                wait(scatters(0))
                return

            @pl.loop(0, nch, step=2)
            def _(ci):
                for b in range(2):
                    wait(loads(ci + b, b))
                    start(scatters(b))

                    @pl.when(ci + b + 1 < nch)
                    def _():
                        @pl.when(ci + b >= 1)
                        def _():
                            wait(scatters(1 - b))

                        start(loads(ci + b + 1, 1 - b))

            wait(scatters(0))
            wait(scatters(1))

        segment(ha_hbm, pa_hbm, idx_a, rows_a, c_a)
        segment(hb_hbm, pb_hbm, idx_b, rows_b, c_b)

    return run(h_a, pos_a, h_b, pos_b)


def _experts_kernel(sb_ref, nb_ref, cnt_ref, nu_ref, xs_hbm, wg_ref, wu_ref, wd_ref, y_hbm,
                    xbuf, ybuf, wgb, wub, wdb, in_sems, out_sems):
    e = pl.program_id(0)
    ring, bm, half = xbuf.shape
    n_used = nu_ref[0]

    def slot_of(g):
        return g & (ring - 1)

    def in_copy(g):
        s = slot_of(g)
        return pltpu.make_async_copy(xs_hbm.at[pl.ds(pl.multiple_of(g * bm, bm), bm), :], xbuf.at[s], in_sems.at[s])

    def out_copy(g):
        s = slot_of(g)
        return pltpu.make_async_copy(ybuf.at[s], y_hbm.at[pl.ds(pl.multiple_of(g * bm, bm), bm), :], out_sems.at[s])

    @pl.when(e == 0)
    def _():
        for g in range(ring - 1):
            @pl.when(g < n_used)
            def _():
                in_copy(g).start()

    wgb[...] = wg_ref[...].astype(BF16)
    wub[...] = wu_ref[...].astype(BF16)
    wdb[...] = wd_ref[...].astype(BF16)
    row = lax.broadcasted_iota(I32, (bm, half), 0)

    def block(b, carry):
        g = sb_ref[e] + b

        @pl.when(g + ring - 1 < n_used)
        def _():
            in_copy(g + ring - 1).start()

        in_copy(g).wait()

        @pl.when(g >= ring)
        def _():
            out_copy(g - ring).wait()

        s = slot_of(g)
        x_lo, x_hi = _unpack_bf16_pairs(jnp.where(row < cnt_ref[e] - b * bm, xbuf[s], 0))
        a = (jnp.dot(x_lo, wgb[:half], preferred_element_type=F32)
             + jnp.dot(x_hi, wgb[half:], preferred_element_type=F32))
        u = (jnp.dot(x_lo, wub[:half], preferred_element_type=F32)
             + jnp.dot(x_hi, wub[half:], preferred_element_type=F32))
        ybuf[s] = _pack_bf16_pairs(jnp.dot((_silu(a) * u).astype(BF16), wdb[...], preferred_element_type=F32))
        out_copy(g).start()
        return carry

    lax.fori_loop(0, nb_ref[e], block, 0)

    @pl.when(e == pl.num_programs(0) - 1)
    def _():
        for r in range(ring):
            @pl.when(n_used - 1 - r >= 0)
            def _():
                out_copy(n_used - 1 - r).wait()


def _experts(first_blk, n_blk, counts, n_used, xs, wg, wu, wd):
    rows = xs.shape[0]
    bm, ring = EXPERT_BLOCK, EXPERT_RING
    assert ring & (ring - 1) == 0
    w_blk = lambda e, *_: (e, 0, 0)
    grid_spec = pltpu.PrefetchScalarGridSpec(
        num_scalar_prefetch=4,
        grid=(N_EXPERTS,),
        in_specs=[pl.BlockSpec(memory_space=pl.ANY),
                  pl.BlockSpec((None, D_MODEL, D_EXPERT), w_blk),
                  pl.BlockSpec((None, D_MODEL, D_EXPERT), w_blk),
                  pl.BlockSpec((None, D_EXPERT, D_MODEL), w_blk)],
        out_specs=pl.BlockSpec(memory_space=pl.ANY),
        scratch_shapes=[pltpu.VMEM((ring, bm, D_MODEL // 2), I32), pltpu.VMEM((ring, bm, D_MODEL // 2), I32),
                        pltpu.VMEM((D_MODEL, D_EXPERT), BF16), pltpu.VMEM((D_MODEL, D_EXPERT), BF16),
                        pltpu.VMEM((D_EXPERT, D_MODEL), BF16),
                        pltpu.SemaphoreType.DMA((ring,)), pltpu.SemaphoreType.DMA((ring,))],
    )
    return pl.pallas_call(
        _experts_kernel,
        grid_spec=grid_spec,
        out_shape=jax.ShapeDtypeStruct((rows, D_MODEL // 2), I32),
        compiler_params=pltpu.CompilerParams(dimension_semantics=("arbitrary",), vmem_limit_bytes=VMEM_LIMIT),
        name="moe_experts",
    )(first_blk, n_blk, counts, n_used, xs, wg, wu, wd)


def _sc_combine(y, seg_a, seg_b):
    (pos_a, w_a), (pos_b, w_b) = seg_a, seg_b
    c = COMBINE_CHUNK
    half = y.shape[1]
    d = 2 * half
    n_a, n_b = w_a.shape[0], w_b.shape[0]
    params = dataclasses.replace(pltpu.CompilerParams(), needs_layout_passes=False)

    @functools.partial(
        pl.kernel, mesh=_sc_mesh(), compiler_params=params,
        out_type=[jax.ShapeDtypeStruct((n_a, d), F32), jax.ShapeDtypeStruct((n_b, d), F32)],
        scratch_types=[pltpu.VMEM(pos_a.shape[1:], I32), pltpu.VMEM(pos_b.shape[1:], I32),
                       pltpu.VMEM((2, c, SUBLANES * SC_LANES), F32), pltpu.VMEM((2, TOP_K, c, half), I32),
                       pltpu.VMEM((c, d), F32), pltpu.SemaphoreType.DMA((2,)), pltpu.SemaphoreType.DMA])
    def run(y_hbm, pa_hbm, wa_hbm, pb_hbm, wb_hbm, oa_hbm, ob_hbm, idx_a, idx_b, w_v, buf_v, out_v, sems, out_sem):
        wid = _sc_worker_id()

        def segment(p_hbm, w_hbm, o_hbm, idx_v):
            per_w = idx_v.shape[1]
            nch = per_w // c
            base = wid * per_w
            pltpu.sync_copy(p_hbm.at[wid], idx_v)

            def in_copies(ci, b):
                w_copy = pltpu.make_async_copy(w_hbm.at[pl.ds(base + ci * c, c)], w_v.at[b], sems.at[b])
                return [w_copy] + [
                    pltpu.make_async_copy(y_hbm.at[idx_v.at[k, pl.ds(ci * c, c)]], buf_v.at[b].at[k], sems.at[b])
                    for k in range(TOP_K)]

            def out_copy(ci):
                return pltpu.make_async_copy(out_v, o_hbm.at[pl.ds(base + ci * c, c)], out_sem)

            def reduce_rows(ci, b):
                @pl.when(ci >= 1)
                def _():
                    out_copy(ci - 1).wait()

                for t0 in range(0, c, COMBINE_TOKEN_GROUP):
                    group = range(t0, t0 + COMBINE_TOKEN_GROUP)
                    ws = [[w_v[b, t, pl.ds(k * SC_LANES, SC_LANES)] for k in range(TOP_K)] for t in group]

                    @plsc.parallel_loop(0, half // SC_LANES, unroll=2)
                    def _(j):
                        lanes = pl.ds(pl.multiple_of(j * SC_LANES, SC_LANES), SC_LANES)
                        hi_lanes = pl.ds(pl.multiple_of(half + j * SC_LANES, SC_LANES), SC_LANES)
                        for t, wt in zip(group, ws):
                            acc_lo = acc_hi = None
                            for k in range(TOP_K):
                                word = buf_v[b, k, t, lanes]
                                lo = plsc.bitcast(lax.shift_left(word, jnp.int32(16)), F32) * wt[k]
                                hi = plsc.bitcast(word & jnp.int32(-65536), F32) * wt[k]
                                acc_lo = lo if acc_lo is None else acc_lo + lo
                                acc_hi = hi if acc_hi is None else acc_hi + hi
                            out_v[t, lanes] = acc_lo
                            out_v[t, hi_lanes] = acc_hi

                out_copy(ci).start()

            for cp in in_copies(0, 0):
                cp.start()

            @pl.loop(0, nch, step=2)
            def _(ci):
                for b in range(2):
                    @pl.when(ci + b + 1 < nch)
                    def _():
                        for cp in in_copies(ci + b + 1, 1 - b):
                            cp.start()

                    for cp in in_copies(ci + b, b):
                        cp.wait()
                    reduce_rows(ci + b, b)

            out_copy(nch - 1).wait()

        segment(pa_hbm, wa_hbm, oa_hbm, idx_a)
        segment(pb_hbm, wb_hbm, ob_hbm, idx_b)

    return run(y, pos_a, w_a, pos_b, w_b)


def _shared_kernel(x1_ref, h2_ref, g2_ref, wsg_ref, wsu_ref, wsd_ref, o_ref):
    half = h2_ref.shape[1]
    x_lo, x_hi = _unpack_bf16_pairs(h2_ref[...])
    wsg, wsu = wsg_ref[...], wsu_ref[...]
    a = (jnp.dot(x_lo, wsg[:half], preferred_element_type=F32) + jnp.dot(x_hi, wsg[half:], preferred_element_type=F32))
    b = (jnp.dot(x_lo, wsu[:half], preferred_element_type=F32) + jnp.dot(x_hi, wsu[half:], preferred_element_type=F32))
    shared = jnp.dot((_silu(a) * b).astype(BF16), wsd_ref[...], preferred_element_type=F32)
    o_ref[...] = x1_ref[...] + g2_ref[...] * shared


def _shared(x1, h2, g2, wsg, wsu, wsd, tm, mod_spec):
    n = h2.shape[0]
    full = lambda arr: pl.BlockSpec(arr.shape, lambda i: (0,) * arr.ndim)
    tile = pl.BlockSpec((tm, D_MODEL), lambda i: (i, 0))
    return pl.pallas_call(
        _shared_kernel,
        grid=(n // tm,),
        in_specs=[tile, pl.BlockSpec((tm, D_MODEL // 2), lambda i: (i, 0)), mod_spec, full(wsg), full(wsu), full(wsd)],
        out_specs=tile,
        out_shape=jax.ShapeDtypeStruct((n, D_MODEL), F32),
        compiler_params=pltpu.CompilerParams(dimension_semantics=("arbitrary",), vmem_limit_bytes=VMEM_LIMIT),
        name="shared_expert",
    )(x1, h2, g2, wsg, wsu, wsd)


def _final_kernel(base_ref, routed_ref, g2_ref, o_ref):
    o_ref[...] = base_ref[...] + g2_ref[...] * routed_ref[...]


def _final(base, routed, g2, tm, mod_spec):
    n = base.shape[0]
    tile = pl.BlockSpec((tm, D_MODEL), lambda i: (i, 0))
    return pl.pallas_call(
        _final_kernel,
        grid=(n // tm,),
        in_specs=[tile, tile, mod_spec],
        out_specs=tile,
        out_shape=jax.ShapeDtypeStruct((n, D_MODEL), F32),
        compiler_params=pltpu.CompilerParams(dimension_semantics=("arbitrary",), vmem_limit_bytes=VMEM_LIMIT),
        name="ffn_residual",
    )(base, routed, g2)


def _block_diag(w):
    nb, bi, bj = w.shape
    return jnp.einsum('nij,nm->nimj', w, jnp.eye(nb, dtype=w.dtype)).reshape(nb * bi, nb * bj)


def kernel(x_prompt, x_sample, c_prompt, c_sample, cache_k_win, cache_v_win, state_conv, state_rnn, ada_w, ada_b, norm_mix, w_in, conv_w, conv_b, gate_a_w, gate_a_b, gate_x_w, gate_x_b, lru_lambda, q_norm, k_norm, attn_sinks, w_out, norm_ffn, router_w, router_bias, exp_w_gate, exp_w_up, exp_w_down, sh_w_gate, sh_w_up, sh_w_down):
    bp, tp, _ = x_prompt.shape
    bs, ts, _ = x_sample.shape
    win = cache_k_win.shape[2]
    n_p, n_s = bp * tp, bs * ts
    row = lambda v: v.reshape(1, -1)

    g_mix, g_ffn = row(norm_mix[0]), row(norm_ffn[0])
    win_bf = w_in[0].astype(BF16)
    q0 = 2 * D_RNN
    win_pair = jnp.concatenate([win_bf[:, :q0], _pair_heads(win_bf[:, q0:q0 + D_ATT], 1), win_bf[:, q0 + D_ATT:]],
                               axis=1)
    cw, cb = conv_w[0], row(conv_b[0])
    wg = jnp.concatenate([_block_diag(gate_a_w[0]), _block_diag(gate_x_w[0])], axis=1).astype(BF16)
    gb = row(jnp.concatenate([gate_a_b[0], gate_x_b[0]]))
    lam = row(lru_lambda[0])
    gq = row(q_norm[0])
    gk2 = row(jnp.tile(k_norm[0], N_KV_HEADS))
    sinks = attn_sinks[0]
    wo_top, wo_bot = w_out[0, :D_RNN].astype(BF16), w_out[0, D_RNN:].astype(BF16)
    wr_t = router_w[0].T
    rb = router_bias[0].reshape(N_EXPERTS, 1)
    wsg, wsu, wsd = sh_w_gate[0].astype(BF16), sh_w_up[0].astype(BF16), sh_w_down[0].astype(BF16)

    mod = _adaln(jnp.concatenate([c_prompt, c_sample], axis=0), ada_w[0], ada_b[0])
    chunks = [mod[:, i * D_MODEL:(i + 1) * D_MODEL] for i in range(6)]
    sh1p, sc1p, g1p, sh2p, sc2p, g2p = [c[:bp].reshape(bp, 1, D_MODEL) for c in chunks]
    sh1s, sc1s, g1s, sh2s, sc2s, g2s = [c[bp:] for c in chunks]

    conv0 = jnp.zeros((bp, SUBLANES, D_RNN), F32)
    h0 = jnp.zeros((bp, 1, D_RNN), F32)
    r_p, q_p, k_p, v_p, cs_p, hs_p = _front_prompt(x_prompt, sc1p, sh1p, g_mix, win_pair, cw, cb, wg, gb, lam, gk2,
                                                   conv0, h0)
    a_p = _attn_prompt(q_p, k_p, v_p, row(jnp.tile(q_norm[0], 2)), sinks)

    x_s_t = jnp.swapaxes(x_sample, 0, 1)
    r_s, q_s, k_s, v_s, cs_s, hs_s = _front_sample(x_s_t, sc1s, sh1s, g_mix, win_bf, cw, cb, wg, gb, lam, gk2,
                                                   jnp.swapaxes(state_conv[0], 0, 1), state_rnn[0])
    cache_k = cache_k_win[0].reshape(bs, win, D_KV)
    cache_v = cache_v_win[0].reshape(bs, win, D_KV)
    a_s = _attn_sample(q_s, k_s, v_s, cache_k, cache_v, gq, sinks)

    tiles_per_seq = tp // TM_POST
    mod_p = pl.BlockSpec((None, 1, D_MODEL), lambda i, *_: (i // tiles_per_seq, 0, 0))
    mod_s = pl.BlockSpec((bs, D_MODEL), lambda i, *_: (0, 0))
    cnt0 = jnp.zeros((N_EXPERTS, LANES), F32)
    x1_p, h2_p, eidx_p, gw_p, rank_p, cnt_p = _post(
        x_prompt.reshape(n_p, D_MODEL), r_p.reshape(n_p, D_RNN), a_p.reshape(n_p, D_ATT), g1p, sc2p, sh2p, g_ffn,
        wo_top, _pair_heads(wo_bot, 0), wr_t, rb, cnt0, TM_POST, mod_p)
    x1_s, h2_s, eidx_s, gw_s, rank_s, cnt_all = _post(
        x_s_t.reshape(n_s, D_MODEL), r_s.reshape(n_s, D_RNN), a_s.reshape(n_s, D_ATT), g1s, sc2s, sh2s, g_ffn,
        wo_top, wo_bot, wr_t, rb, cnt_p, bs, mod_s)

    bm = EXPERT_BLOCK
    n_blocks = -(-((n_p + n_s) * TOP_K) // bm) + N_EXPERTS
    pstart, used = _plan(cnt_all, bm)
    counts = cnt_all[:, 0].astype(I32)
    first_blk = (pstart[:, 0] * (1.0 / bm)).astype(I32)
    n_blk = (counts + (bm - 1)) // bm
    n_used = used[0, :1]

    def sc_layouts(eidx, rank, gw, n):
        per_w = n // SC_WORKERS
        c = min(DISPATCH_CHUNK, per_w)
        outs = _layouts(pstart, eidx, rank, gw, (c,) if c == per_w else (c, per_w))
        return (outs[0], c), (outs[-2], outs[-1])

    (dpos_p, c_p), comb_p = sc_layouts(eidx_p, rank_p, gw_p, n_p)
    (dpos_s, c_s), comb_s = sc_layouts(eidx_s, rank_s, gw_s, n_s)
    xs = _sc_dispatch((h2_p, dpos_p, c_p), (h2_s, dpos_s, c_s), n_blocks * bm)
    base_p = _shared(x1_p, h2_p, g2p, wsg, wsu, wsd, TM_POST, mod_p)
    base_s = _shared(x1_s, h2_s, g2s, wsg, wsu, wsd, bs, mod_s)
    y = _experts(first_blk, n_blk, counts, n_used, xs, exp_w_gate[0], exp_w_up[0], exp_w_down[0])
    routed_p, routed_s = _sc_combine(y, comb_p, comb_s)
    y_p = _final(base_p, routed_p, g2p, TM_POST, mod_p)
    y_s = _final(base_s, routed_s, g2s, bs, mod_s)

    y_prompt = y_p.reshape(bp, tp, D_MODEL)
    y_sample = jnp.swapaxes(y_s.reshape(ts, bs, D_MODEL), 0, 1)
    wk = min(WINDOW, tp)
    k_win_p = k_p[:, tp - wk:].reshape(1, bp, wk, N_KV_HEADS, HEAD_DIM)
    v_win_p = v_p[:, tp - wk:].reshape(1, bp, wk, N_KV_HEADS, HEAD_DIM)
    k_new = jnp.swapaxes(k_s, 0, 1)
    v_new = jnp.swapaxes(v_s, 0, 1)
    heads = lambda a: a.reshape(bs, ts, N_KV_HEADS, HEAD_DIM)
    k_win_s = jnp.concatenate([cache_k_win[0], heads(k_new)], axis=1)[None, :, ts:]
    v_win_s = jnp.concatenate([cache_v_win[0], heads(v_new)], axis=1)[None, :, ts:]
    return (y_prompt, y_sample, k_win_p, v_win_p, cs_p[None, :, SUBLANES - (CONV_W - 1):], hs_p.reshape(1, bp, D_RNN),
            k_win_s, v_win_s, jnp.swapaxes(cs_s, 0, 1)[None], hs_s[None])
```

```python
import dataclasses
import functools

import jax
import jax.numpy as jnp
from jax import lax
from jax.experimental import pallas as pl
from jax.experimental.pallas import tpu as pltpu
from jax.experimental.pallas import tpu_sc as plsc

F32 = jnp.float32
BF16 = jnp.bfloat16
I32 = jnp.int32

D_MODEL = 1024
D_RNN = 512
N_RNN_BLOCKS = 8
CONV_W = 4
LRU_C = 8.0
HEAD_DIM = 64
N_HEADS = 8
N_KV_HEADS = 2
GQA = N_HEADS // N_KV_HEADS
D_ATT = N_HEADS * HEAD_DIM
D_KV = N_KV_HEADS * HEAD_DIM
WINDOW = 128
N_EXPERTS = 64
TOP_K = 6
N_GROUPS = 8
GROUP_SIZE = N_EXPERTS // N_GROUPS
TOPK_GROUPS = 4
D_EXPERT = 256
D_SHARED = 256
ROUTE_SCALE = 2.5
EPS = 1e-6
NEG = -1e30
F32_TINY = 1.1754944e-38
D_IN = 2 * D_RNN + D_ATT + 2 * D_KV

SUBLANES = 8
LANES = 128
TM_PROMPT = 512
TM_POST = 512
ATT_BLOCK = WINDOW
ATT_STEP_BLOCKS = 8
EXPERT_BLOCK = 512
EXPERT_RING = 4
SAMPLE_CHUNK = 8
POS_TILE = 2048
VMEM_LIMIT = 48 * 1024 * 1024

SC_CORES = 2
SC_SUBCORES = 16
SC_WORKERS = SC_CORES * SC_SUBCORES
SC_LANES = 16
DISPATCH_CHUNK = 64
COMBINE_CHUNK = 8
COMBINE_TOKEN_GROUP = 4


def _sigmoid(x):
    return 0.5 * jnp.tanh(0.5 * x) + 0.5


def _silu(x):
    return x * _sigmoid(x)


def _gelu_tanh(x):
    c = 0.7978845608028654
    return x * (0.5 * (1.0 + jnp.tanh(c * (x + 0.044715 * (x * x * x)))))


def _log1p(x):
    u = 1.0 + x
    return jnp.where(u == 1.0, x, jnp.log(u) * x / jnp.where(u == 1.0, 1.0, u - 1.0))


def _neg_expm1_2x(x, exp_x):
    return -jnp.tanh(x) * (exp_x * exp_x + 1.0)


def _softplus(z):
    return jnp.maximum(z, 0.0) + _log1p(jnp.exp(-jnp.abs(z)))


def _div_pow2(x, d):
    assert d & (d - 1) == 0
    return lax.shift_right_logical(x, d.bit_length() - 1)


def _mod_pow2(x, d):
    assert d & (d - 1) == 0
    return x & (d - 1)


def _norm_mod(x, g, sc, sh):
    ms = jnp.mean(x * x, axis=-1, keepdims=True)
    return (x * lax.rsqrt(ms + EPS)) * g * (1.0 + sc) + sh


def _bdot(a, b):
    return jnp.dot(a.astype(BF16), b.astype(BF16), preferred_element_type=F32)


def _bdot_nt(a, b):
    return lax.dot_general(a.astype(BF16), b.astype(BF16), (((1,), (1,)), ((), ())),
                           preferred_element_type=F32)


def _pack_bf16_pairs(x):
    w = x.shape[1] // 2

    def bf16_bits(v):
        return pltpu.bitcast(v.astype(BF16).astype(F32), I32)

    return lax.shift_right_logical(bf16_bits(x[:, :w]), 16) | (bf16_bits(x[:, w:]) & jnp.int32(-65536))


def _unpack_bf16_pairs(p):
    lo = pltpu.bitcast(lax.shift_left(p, 16), F32)
    hi = pltpu.bitcast(p & jnp.int32(-65536), F32)
    return lo.astype(BF16), hi.astype(BF16)


def _knorm(k, gk2):
    lane = lax.broadcasted_iota(I32, k.shape, 1)
    first = lane < HEAD_DIM
    k2 = k * k
    s0 = jnp.sum(jnp.where(first, k2, 0.0), axis=-1, keepdims=True)
    s1 = jnp.sum(jnp.where(first, 0.0, k2), axis=-1, keepdims=True)
    ms = jnp.where(first, s0, s1) * (1.0 / HEAD_DIM)
    return k * lax.rsqrt(ms + EPS) * gk2


def _pair_heads(w, axis):
    shape = w.shape
    split = shape[:axis] + (N_KV_HEADS, GQA, HEAD_DIM) + shape[axis + 1:]
    return jnp.swapaxes(w.reshape(split), axis, axis + 1).reshape(shape)


def _lru_coeffs(u, wg, gb, lam):
    g = _bdot(u, wg) + gb
    r = _sigmoid(g[:, :D_RNN])
    i = _sigmoid(g[:, D_RNN:])
    log_a = (-LRU_C * r) * _softplus(-lam)
    a = jnp.exp(log_a)
    om = _neg_expm1_2x(log_a, a)
    b = (om * lax.rsqrt(jnp.maximum(om, F32_TINY))) * (i * u)
    return a, b


def _adaln_kernel(c_ref, w_ref, b_ref, o_ref):
    o_ref[...] = _bdot(_silu(c_ref[...]), w_ref[...]) + b_ref[...]


def _adaln(c_all, ada_w, ada_b):
    n = c_all.shape[0]
    return pl.pallas_call(
        _adaln_kernel,
        grid=(6,),
        in_specs=[pl.BlockSpec((n, D_MODEL), lambda j: (0, 0)),
                  pl.BlockSpec((D_MODEL, D_MODEL), lambda j: (0, j)),
                  pl.BlockSpec((1, D_MODEL), lambda j: (0, j))],
        out_specs=pl.BlockSpec((n, D_MODEL), lambda j: (0, j)),
        out_shape=jax.ShapeDtypeStruct((n, 6 * D_MODEL), F32),
        compiler_params=pltpu.CompilerParams(dimension_semantics=("arbitrary",), vmem_limit_bytes=VMEM_LIMIT),
        name="adaln",
    )(c_all, ada_w, ada_b.reshape(1, -1))


def _scan_rows(a, b, h_in):
    n, c = a.shape
    groups = n // SUBLANES
    a = a.reshape(groups, SUBLANES, c)
    b = b.reshape(groups, SUBLANES, c)
    sub = lax.broadcasted_iota(I32, a.shape, 1)
    s = 1
    while s < SUBLANES:
        m = sub >= s
        a_sh = jnp.where(m, pltpu.roll(a, s, 1), 1.0)
        b_sh = jnp.where(m, pltpu.roll(b, s, 1), 0.0)
        b = a * b_sh + b
        a = a * a_sh
        s *= 2
    carry = h_in
    hs = []
    for g in range(groups):
        hg = a[g] * carry + b[g]
        hs.append(hg)
        carry = hg[SUBLANES - 1:SUBLANES]
    return jnp.concatenate(hs, axis=0)


def _front_prompt_kernel(x_ref, sc_ref, sh_ref, g_ref, win_ref, cw_ref, cb_ref, wg_ref, gb_ref, lam_ref, gk_ref,
                         prev_ref, h0_ref, r_ref, q_ref, k_ref, v_ref, cs_ref, hs_ref, tail_ref, hc_ref):
    j = pl.program_id(1)
    tm = x_ref.shape[0]

    @pl.when(j == 0)
    def _():
        tail_ref[...] = prev_ref[...]
        hc_ref[...] = h0_ref[...]

    h = _norm_mod(x_ref[...], g_ref[...], sc_ref[...], sh_ref[...])
    proj = jnp.dot(h.astype(BF16), win_ref[...], preferred_element_type=F32)
    xr = proj[:, 0:D_RNN]
    yr = proj[:, D_RNN:2 * D_RNN]
    q_ref[...] = proj[:, 2 * D_RNN:2 * D_RNN + D_ATT]
    k_ref[...] = _knorm(proj[:, 2 * D_RNN + D_ATT:2 * D_RNN + D_ATT + D_KV], gk_ref[...])
    v_ref[...] = proj[:, 2 * D_RNN + D_ATT + D_KV:D_IN]

    tail = tail_ref[...]
    row8 = lax.broadcasted_iota(I32, tail.shape, 0)

    def shifted(s):
        rolled = pltpu.roll(xr, s, 0)
        top = jnp.where(row8 < s, pltpu.roll(tail, s, 0), rolled[0:SUBLANES])
        return jnp.concatenate([top, rolled[SUBLANES:]], axis=0)

    cw = cw_ref[...]
    u = cb_ref[...] + shifted(3) * cw[0:1]
    u = u + shifted(2) * cw[1:2]
    u = u + shifted(1) * cw[2:3]
    u = u + xr * cw[3:4]
    tail_ref[...] = xr[tm - SUBLANES:tm]

    a, b = _lru_coeffs(u, wg_ref[...], gb_ref[...], lam_ref[...])
    hs = _scan_rows(a, b, hc_ref[...])
    hc_ref[...] = hs[tm - 1:tm]
    r_ref[...] = hs * _gelu_tanh(yr)

    @pl.when(j == pl.num_programs(1) - 1)
    def _():
        cs_ref[...] = xr[tm - SUBLANES:tm]
        hs_ref[...] = hs[tm - 1:tm]


def _front_prompt(x, sc, sh, g, win, cw, cb, wg, gb, lam, gk, prev, h0):
    bsz, t, _ = x.shape
    tm = TM_PROMPT
    full = lambda a: pl.BlockSpec(a.shape, lambda b, j: (0,) * a.ndim)
    per_b = lambda a: pl.BlockSpec((None,) + a.shape[1:], lambda b, j: (b,) + (0,) * (a.ndim - 1))
    tile = lambda w: pl.BlockSpec((None, tm, w), lambda b, j: (b, j, 0))
    return pl.pallas_call(
        _front_prompt_kernel,
        grid=(bsz, t // tm),
        in_specs=[tile(D_MODEL), per_b(sc), per_b(sh), full(g), full(win), full(cw), full(cb), full(wg), full(gb),
                  full(lam), full(gk), per_b(prev), per_b(h0)],
        out_specs=[tile(D_RNN), tile(D_ATT), tile(D_KV), tile(D_KV),
                   pl.BlockSpec((None, SUBLANES, D_RNN), lambda b, j: (b, 0, 0)),
                   pl.BlockSpec((None, 1, D_RNN), lambda b, j: (b, 0, 0))],
        out_shape=[jax.ShapeDtypeStruct((bsz, t, D_RNN), F32), jax.ShapeDtypeStruct((bsz, t, D_ATT), F32),
                   jax.ShapeDtypeStruct((bsz, t, D_KV), F32), jax.ShapeDtypeStruct((bsz, t, D_KV), F32),
                   jax.ShapeDtypeStruct((bsz, SUBLANES, D_RNN), F32), jax.ShapeDtypeStruct((bsz, 1, D_RNN), F32)],
        scratch_shapes=[pltpu.VMEM((SUBLANES, D_RNN), F32), pltpu.VMEM((1, D_RNN), F32)],
        compiler_params=pltpu.CompilerParams(dimension_semantics=("arbitrary", "arbitrary"),
                                             vmem_limit_bytes=VMEM_LIMIT),
        name="front_prompt",
    )(x, sc, sh, g, win, cw, cb, wg, gb, lam, gk, prev, h0)


def _front_sample_kernel(x_ref, sc_ref, sh_ref, g_ref, win_ref, cw_ref, cb_ref, wg_ref, gb_ref, lam_ref, gk_ref,
                         prev_ref, h0_ref, r_ref, q_ref, k_ref, v_ref, cs_ref, hs_ref):
    t_len, bsz, _ = x_ref.shape
    x = x_ref[...]
    ms = jnp.mean(x * x, axis=-1, keepdims=True)
    h = (x * lax.rsqrt(ms + EPS)) * g_ref[...] * (1.0 + sc_ref[...]) + sh_ref[...]
    proj = jnp.dot(h.reshape(t_len * bsz, D_MODEL).astype(BF16), win_ref[...], preferred_element_type=F32)
    xr = proj[:, 0:D_RNN]
    yr = proj[:, D_RNN:2 * D_RNN]
    q_ref[...] = proj[:, 2 * D_RNN:2 * D_RNN + D_ATT].reshape(t_len, bsz, D_ATT)
    k_ref[...] = _knorm(proj[:, 2 * D_RNN + D_ATT:2 * D_RNN + D_ATT + D_KV], gk_ref[...]).reshape(t_len, bsz, D_KV)
    v_ref[...] = proj[:, 2 * D_RNN + D_ATT + D_KV:D_IN].reshape(t_len, bsz, D_KV)

    def at_time(t):
        if t >= 0:
            return xr[t * bsz:(t + 1) * bsz]
        return prev_ref[CONV_W - 1 + t]

    cw = cw_ref[...]
    us = []
    for t in range(t_len):
        u = cb_ref[...] + at_time(t - 3) * cw[0:1]
        u = u + at_time(t - 2) * cw[1:2]
        u = u + at_time(t - 1) * cw[2:3]
        u = u + at_time(t) * cw[3:4]
        us.append(u)
    a, b = _lru_coeffs(jnp.concatenate(us, axis=0), wg_ref[...], gb_ref[...], lam_ref[...])
    hcur = h0_ref[...]
    for t in range(t_len):
        hcur = a[t * bsz:(t + 1) * bsz] * hcur + b[t * bsz:(t + 1) * bsz]
        r_ref[t] = hcur * _gelu_tanh(yr[t * bsz:(t + 1) * bsz])
    hs_ref[...] = hcur
    for s in range(CONV_W - 1):
        cs_ref[s] = at_time(t_len - (CONV_W - 1) + s)


def _front_sample(x_t, sc, sh, g, win, cw, cb, wg, gb, lam, gk, prev_t, h0):
    t_len, bsz, _ = x_t.shape
    return pl.pallas_call(
        _front_sample_kernel,
        out_shape=[jax.ShapeDtypeStruct((t_len, bsz, D_RNN), F32), jax.ShapeDtypeStruct((t_len, bsz, D_ATT), F32),
                   jax.ShapeDtypeStruct((t_len, bsz, D_KV), F32), jax.ShapeDtypeStruct((t_len, bsz, D_KV), F32),
                   jax.ShapeDtypeStruct((CONV_W - 1, bsz, D_RNN), F32), jax.ShapeDtypeStruct((bsz, D_RNN), F32)],
        compiler_params=pltpu.CompilerParams(vmem_limit_bytes=VMEM_LIMIT),
        name="front_sample",
    )(x_t, sc, sh, g, win, cw, cb, wg, gb, lam, gk, prev_t, h0)


def _qnorm(q, gq):
    ms = jnp.mean(q * q, axis=-1, keepdims=True)
    return q * lax.rsqrt(ms + EPS) * gq


def _attn_prompt_kernel(sink_ref, q_ref, kp_ref, kc_ref, vp_ref, vc_ref, gq_ref, o_ref):
    j = pl.program_id(1)
    blk = kp_ref.shape[0]
    k_all = jnp.concatenate([kp_ref[...], kc_ref[...]], axis=0)
    v_all = jnp.concatenate([vp_ref[...], vc_ref[...]], axis=0)
    qi = lax.broadcasted_iota(I32, (blk, 2 * blk), 0)
    kj = lax.broadcasted_iota(I32, (blk, 2 * blk), 1)
    dist = blk + qi - kj
    window = (dist >= 0) & (dist <= WINDOW)
    distf = dist.astype(F32)
    slab = 2 * HEAD_DIM
    first_q = lax.broadcasted_iota(I32, (blk, slab), 1) < HEAD_DIM
    first_kv = lax.broadcasted_iota(I32, (2 * blk, slab), 1) < HEAD_DIM

    bias = [jnp.where(window, (-(2.0 ** -(h + 1))) * distf, NEG) for h in range(N_HEADS)]
    no_prev = (kj < blk) & (j == 0)
    gq = gq_ref[...] * (HEAD_DIM ** -0.5)

    def probs(s, h, first_block):
        b = jnp.where(no_prev, NEG, bias[h]) if first_block else bias[h]
        s = s + b
        sink = sink_ref[h]
        m = jnp.maximum(jnp.max(s, axis=-1, keepdims=True), sink)
        p = jnp.exp(s - m)
        return p, jnp.sum(p, axis=-1, keepdims=True) + jnp.exp(sink - m)

    for sub in range(q_ref.shape[0] // blk):
        first = sub == 0
        q = q_ref[sub * blk:(sub + 1) * blk, :]
        kk = k_all[sub * blk:(sub + 2) * blk]
        vv = v_all[sub * blk:(sub + 2) * blk]
        v_a = jnp.where(first_kv, vv, 0.0)
        v_b = jnp.where(first_kv, 0.0, vv)
        outs = []
        for i in range(GQA):
            q2 = _knorm(q[:, i * slab:(i + 1) * slab], gq)
            p_a, l_a = probs(_bdot_nt(jnp.where(first_q, q2, 0.0), kk), i, first)
            p_b, l_b = probs(_bdot_nt(jnp.where(first_q, 0.0, q2), kk), GQA + i, first)
            outs.append((_bdot(p_a, v_a) + _bdot(p_b, v_b)) / jnp.where(first_q, l_a, l_b))
        o_ref[sub * blk:(sub + 1) * blk, :] = jnp.concatenate(outs, axis=1)


def _attn_prompt(q, k, v, gq, sinks):
    bsz, t, _ = q.shape
    blk = ATT_BLOCK
    nsub = ATT_STEP_BLOCKS
    cur = lambda w: pl.BlockSpec((None, nsub * blk, w), lambda b, j: (b, j, 0))
    prv = lambda w: pl.BlockSpec((None, blk, w), lambda b, j: (b, jnp.maximum(nsub * j - 1, 0), 0))
    return pl.pallas_call(
        _attn_prompt_kernel,
        grid=(bsz, t // (nsub * blk)),
        in_specs=[pl.BlockSpec(memory_space=pltpu.SMEM), cur(D_ATT), prv(D_KV), cur(D_KV), prv(D_KV), cur(D_KV),
                  pl.BlockSpec(gq.shape, lambda b, j: (0, 0))],
        out_specs=cur(D_ATT),
        out_shape=jax.ShapeDtypeStruct((bsz, t, D_ATT), F32),
        compiler_params=pltpu.CompilerParams(dimension_semantics=("arbitrary", "arbitrary"),
                                             vmem_limit_bytes=VMEM_LIMIT),
        name="attn_prompt",
    )(sinks, q, k, k, v, v, gq)


def _attn_sample_kernel(sink_ref, q_ref, kn_ref, vn_ref, kc_ref, vc_ref, gq_ref, o_ref):
    t_len, cb, _ = q_ref.shape
    win = kc_ref.shape[1]
    rows = GQA * t_len * cb
    kc = kc_ref[...].reshape(cb * win, D_KV)
    vc = vc_ref[...].reshape(cb * win, D_KV)
    kn = kn_ref[...].reshape(t_len * cb, D_KV)
    vn = vn_ref[...].reshape(t_len * cb, D_KV)

    r_c = lax.broadcasted_iota(I32, (rows, cb * win), 0)
    c_c = lax.broadcasted_iota(I32, (rows, cb * win), 1)
    t_c = _div_pow2(_mod_pow2(r_c, t_len * cb), cb)
    valid_c = (_mod_pow2(r_c, cb) == _div_pow2(c_c, win)) & (_mod_pow2(c_c, win) >= t_c)
    dist_c = (win + t_c - _mod_pow2(c_c, win)).astype(F32)
    r_n = lax.broadcasted_iota(I32, (rows, t_len * cb), 0)
    c_n = lax.broadcasted_iota(I32, (rows, t_len * cb), 1)
    t_n = _div_pow2(_mod_pow2(r_n, t_len * cb), cb)
    valid_n = (_mod_pow2(r_n, cb) == _mod_pow2(c_n, cb)) & (_div_pow2(c_n, cb) <= t_n)
    dist_n = (t_n - _div_pow2(c_n, cb)).astype(F32)
    hl = _div_pow2(lax.broadcasted_iota(I32, (rows, 1), 0), t_len * cb)

    per_group = []
    for g in range(N_KV_HEADS):
        slabs = [q_ref[t][:, (g * GQA + i) * HEAD_DIM:(g * GQA + i + 1) * HEAD_DIM]
                 for i in range(GQA) for t in range(t_len)]
        qg = _qnorm(jnp.concatenate(slabs, axis=0), gq_ref[...])
        slope = jnp.zeros((rows, 1), F32)
        sink = jnp.zeros((rows, 1), F32)
        for i in range(GQA):
            slope = jnp.where(hl == i, 2.0 ** -(g * GQA + i + 1), slope)
            sink = jnp.where(hl == i, sink_ref[g * GQA + i], sink)
        lo, hi = g * HEAD_DIM, (g + 1) * HEAD_DIM
        s_c = _bdot_nt(qg, kc[:, lo:hi]) * (HEAD_DIM ** -0.5) - slope * dist_c
        s_n = _bdot_nt(qg, kn[:, lo:hi]) * (HEAD_DIM ** -0.5) - slope * dist_n
        s_c = jnp.where(valid_c, s_c, NEG)
        s_n = jnp.where(valid_n, s_n, NEG)
        m = jnp.maximum(jnp.maximum(jnp.max(s_c, axis=-1, keepdims=True), jnp.max(s_n, axis=-1, keepdims=True)), sink)
        p_c = jnp.exp(s_c - m)
        p_n = jnp.exp(s_n - m)
        l = jnp.sum(p_c, axis=-1, keepdims=True) + jnp.sum(p_n, axis=-1, keepdims=True) + jnp.exp(sink - m)
        per_group.append((_bdot(p_c, vc[:, lo:hi]) + _bdot(p_n, vn[:, lo:hi])) / l)
    for t in range(t_len):
        o_ref[t] = jnp.concatenate(
            [per_group[g][(i * t_len + t) * cb:(i * t_len + t + 1) * cb] for g in range(N_KV_HEADS) for i in range(GQA)],
            axis=1)


def _attn_sample(q_t, k_t, v_t, cache_k, cache_v, gq, sinks):
    t_len, bsz, _ = q_t.shape
    cb = SAMPLE_CHUNK
    win = cache_k.shape[1]
    new = lambda w: pl.BlockSpec((t_len, cb, w), lambda c: (0, c, 0))
    old = pl.BlockSpec((cb, win, D_KV), lambda c: (c, 0, 0))
    return pl.pallas_call(
        _attn_sample_kernel,
        grid=(bsz // cb,),
        in_specs=[pl.BlockSpec(memory_space=pltpu.SMEM), new(D_ATT), new(D_KV), new(D_KV), old, old,
                  pl.BlockSpec(gq.shape, lambda c: (0, 0))],
        out_specs=new(D_ATT),
        out_shape=jax.ShapeDtypeStruct((t_len, bsz, D_ATT), F32),
        compiler_params=pltpu.CompilerParams(dimension_semantics=("arbitrary",), vmem_limit_bytes=VMEM_LIMIT),
        name="attn_sample",
    )(sinks, q_t, k_t, v_t, cache_k, cache_v, gq)


def _route(s_t, sb_t):
    tm = s_t.shape[1]
    i8 = lax.broadcasted_iota(I32, (GROUP_SIZE, tm), 0)
    ninf = -jnp.inf
    sg = [sb_t[GROUP_SIZE * g:GROUP_SIZE * (g + 1)] for g in range(N_GROUPS)]
    gscore = []
    for g in range(N_GROUPS):
        m1 = jnp.max(sg[g], axis=0, keepdims=True)
        i1 = jnp.min(jnp.where(sg[g] == m1, i8, GROUP_SIZE), axis=0, keepdims=True)
        m2 = jnp.max(jnp.where(i8 == i1, ninf, sg[g]), axis=0, keepdims=True)
        gscore.append(m1 + m2)
    gs = jnp.concatenate(gscore, axis=0)
    gsel = jnp.zeros((N_GROUPS, tm), I32)
    for _ in range(TOPK_GROUPS):
        m = jnp.max(gs, axis=0, keepdims=True)
        idx = jnp.min(jnp.where(gs == m, i8, N_GROUPS), axis=0, keepdims=True)
        hit = i8 == idx
        gsel = jnp.where(hit, 1, gsel)
        gs = jnp.where(hit, ninf, gs)
    sm = [jnp.where(gsel[g:g + 1] > 0, sg[g], NEG) for g in range(N_GROUPS)]
    eid = [i8 + GROUP_SIZE * g for g in range(N_GROUPS)]
    sel = [jnp.zeros((GROUP_SIZE, tm), F32) for _ in range(N_GROUPS)]
    idxs, ws = [], []
    for _ in range(TOP_K):
        cm = functools.reduce(jnp.maximum, sm)
        m = jnp.max(cm, axis=0, keepdims=True)
        cand = functools.reduce(jnp.minimum, [jnp.where(sm[g] == m, eid[g], N_EXPERTS) for g in range(N_GROUPS)])
        idx = jnp.min(cand, axis=0, keepdims=True)
        wk = jnp.zeros((GROUP_SIZE, tm), F32)
        for g in range(N_GROUPS):
            hit = eid[g] == idx
            wk = wk + jnp.where(hit, s_t[GROUP_SIZE * g:GROUP_SIZE * (g + 1)], 0.0)
            sel[g] = jnp.where(hit, 1.0, sel[g])
            sm[g] = jnp.where(hit, ninf, sm[g])
        idxs.append(idx)
        ws.append(jnp.sum(wk, axis=0, keepdims=True))
    return idxs, ws, jnp.concatenate(sel, axis=0), eid


def _post_kernel(x_ref, r_ref, a_ref, g1_ref, sc2_ref, sh2_ref, nf_ref, wot_ref, wob_ref, wrt_ref, rb_ref, cin_ref,
                 x1_ref, h2_ref, eidx_ref, gw_ref, rank_ref, cnt_ref, carry_ref, before_ref):
    tm = x_ref.shape[0]

    @pl.when(pl.program_id(0) == 0)
    def _():
        carry_ref[...] = cin_ref[...]
        rr = lax.broadcasted_iota(I32, (tm, tm), 0)
        cc = lax.broadcasted_iota(I32, (tm, tm), 1)
        before_ref[...] = jnp.where(rr < cc, 1.0, 0.0).astype(BF16)

    mixed = (jnp.dot(r_ref[...].astype(BF16), wot_ref[...], preferred_element_type=F32)
             + jnp.dot(a_ref[...].astype(BF16), wob_ref[...], preferred_element_type=F32))
    x1 = x_ref[...] + g1_ref[...] * mixed
    x1_ref[...] = x1
    h2 = _norm_mod(x1, nf_ref[...], sc2_ref[...], sh2_ref[...])
    h2_ref[...] = _pack_bf16_pairs(h2)

    wr = wrt_ref[...]
    wr_hi = wr.astype(BF16)
    wr_lo = (wr - wr_hi.astype(F32)).astype(BF16)
    h_hi = h2.astype(BF16)
    h_lo = (h2 - h_hi.astype(F32)).astype(BF16)
    logits = _bdot_nt(wr_hi, h_hi) + _bdot_nt(wr_hi, h_lo) + _bdot_nt(wr_lo, h_hi)
    s_t = _sigmoid(logits)
    idxs, ws, sel, eid = _route(s_t, s_t + rb_ref[...])

    carry = carry_ref[...]
    tot = jnp.dot(sel.astype(BF16), before_ref[...], preferred_element_type=F32) + carry[:, 0:1]
    ranks = []
    for k in range(TOP_K):
        acc = jnp.zeros((GROUP_SIZE, tm), F32)
        for g in range(N_GROUPS):
            acc = acc + jnp.where(eid[g] == idxs[k], tot[GROUP_SIZE * g:GROUP_SIZE * (g + 1)], 0.0)
        ranks.append(jnp.sum(acc, axis=0, keepdims=True))
    carry = carry + jnp.sum(sel, axis=1, keepdims=True)
    carry_ref[...] = carry
    cnt_ref[...] = carry

    wsum = functools.reduce(lambda p, q: p + q, ws)
    pad_i = jnp.zeros((SUBLANES - TOP_K, tm), I32)
    pad_f = jnp.zeros((SUBLANES - TOP_K, tm), F32)
    eidx_ref[...] = jnp.concatenate(idxs + [pad_i], axis=0)
    rank_ref[...] = jnp.concatenate([r.astype(I32) for r in ranks] + [pad_i], axis=0)
    gw_ref[...] = jnp.concatenate([w / wsum * ROUTE_SCALE for w in ws] + [pad_f], axis=0)


def _post(x, r, a, g1, sc2, sh2, nf, wo_top, wo_bot, wr_t, rb, cnt_in, tm, mod_spec):
    n = x.shape[0]
    tile = lambda w: pl.BlockSpec((tm, w), lambda i: (i, 0))
    full = lambda arr: pl.BlockSpec(arr.shape, lambda i: (0,) * arr.ndim)
    slot = pl.BlockSpec((SUBLANES, tm), lambda i: (0, i))
    return pl.pallas_call(
        _post_kernel,
        grid=(n // tm,),
        in_specs=[tile(D_MODEL), tile(D_RNN), tile(D_ATT), mod_spec, mod_spec, mod_spec, full(nf), full(wo_top),
                  full(wo_bot), full(wr_t), full(rb), full(cnt_in)],
        out_specs=[tile(D_MODEL), tile(D_MODEL // 2), slot, slot, slot, full(cnt_in)],
        out_shape=[jax.ShapeDtypeStruct((n, D_MODEL), F32), jax.ShapeDtypeStruct((n, D_MODEL // 2), I32),
                   jax.ShapeDtypeStruct((SUBLANES, n), I32), jax.ShapeDtypeStruct((SUBLANES, n), F32),
                   jax.ShapeDtypeStruct((SUBLANES, n), I32), jax.ShapeDtypeStruct(cnt_in.shape, F32)],
        scratch_shapes=[pltpu.VMEM(cnt_in.shape, F32), pltpu.VMEM((tm, tm), BF16)],
        compiler_params=pltpu.CompilerParams(dimension_semantics=("arbitrary",), vmem_limit_bytes=VMEM_LIMIT),
        name="post_mix",
    )(x, r, a, g1, sc2, sh2, nf, wo_top, wo_bot, wr_t, rb, cnt_in)


def _plan_kernel(cnt_ref, ps_ref, used_ref, *, bm):
    cnt = cnt_ref[...]
    padded = jnp.ceil(cnt * (1.0 / bm)) * bm
    row = lax.broadcasted_iota(I32, cnt.shape, 0)
    pend = padded
    s = 1
    while s < N_EXPERTS:
        pend = pend + jnp.where(row >= s, pltpu.roll(pend, s, 0), 0.0)
        s *= 2
    ps_ref[...] = pend - padded
    used_ref[...] = jnp.broadcast_to(pend[N_EXPERTS - 1:N_EXPERTS] * (1.0 / bm), used_ref.shape).astype(I32)


def _plan(cnt, bm):
    assert bm & (bm - 1) == 0
    return pl.pallas_call(
        functools.partial(_plan_kernel, bm=bm),
        out_shape=[jax.ShapeDtypeStruct(cnt.shape, F32), jax.ShapeDtypeStruct((SUBLANES, LANES), I32)],
        name="moe_plan",
    )(cnt)


def _layout_kernel(ps_ref, eidx_ref, rank_ref, gw_ref, *out_refs, chunks):
    *pos_refs, w_ref = out_refs
    tn = eidx_ref.shape[1]
    e_iota = lax.broadcasted_iota(I32, (N_EXPERTS, tn), 0)
    ps = ps_ref[...][:, 0:1]
    rows = []
    for k in range(TOP_K):
        hit = e_iota == eidx_ref[k:k + 1, :]
        base = jnp.sum(jnp.where(hit, ps, 0.0), axis=0, keepdims=True)
        rows.append(base.astype(I32) + rank_ref[k:k + 1, :])
    rows.append(jnp.zeros((SUBLANES - TOP_K, tn), I32))
    pos = jnp.concatenate(rows, axis=0)
    for pos_ref, c in zip(pos_refs, chunks):
        for q in range(tn // c):
            pos_ref[q] = pos[:, q * c:(q + 1) * c]
    gw = gw_ref[...]
    rep = jnp.concatenate([jnp.broadcast_to(gw[k:k + 1], (SC_LANES, tn)) for k in range(SUBLANES)], axis=0)
    w_ref[...] = rep.T


def _layouts(pstart, eidx, rank, gw, chunks):
    n = eidx.shape[1]
    tn = min(n, POS_TILE)
    slot = pl.BlockSpec((SUBLANES, tn), lambda i: (0, i))
    return pl.pallas_call(
        functools.partial(_layout_kernel, chunks=chunks),
        grid=(n // tn,),
        in_specs=[pl.BlockSpec(pstart.shape, lambda i: (0, 0)), slot, slot, slot],
        out_specs=[pl.BlockSpec((tn // c, SUBLANES, c), lambda i: (i, 0, 0)) for c in chunks]
        + [pl.BlockSpec((tn, SUBLANES * SC_LANES), lambda i: (i, 0))],
        out_shape=[jax.ShapeDtypeStruct((n // c, SUBLANES, c), I32) for c in chunks]
        + [jax.ShapeDtypeStruct((n, SUBLANES * SC_LANES), F32)],
        compiler_params=pltpu.CompilerParams(dimension_semantics=("arbitrary",), vmem_limit_bytes=VMEM_LIMIT),
        name="moe_layout",
    )(pstart, eidx, rank, gw)


def _sc_worker_id():
    return lax.axis_index("s") * SC_CORES + lax.axis_index("c")


def _sc_mesh():
    return plsc.VectorSubcoreMesh(core_axis_name="c", subcore_axis_name="s")


def _sc_dispatch(seg_a, seg_b, total_rows):
    (h_a, pos_a, c_a), (h_b, pos_b, c_b) = seg_a, seg_b
    width = h_a.shape[1]

    @functools.partial(
        pl.kernel, mesh=_sc_mesh(), out_type=jax.ShapeDtypeStruct((total_rows, width), I32),
        scratch_types=[pltpu.VMEM((2, SUBLANES, c_a), I32), pltpu.VMEM((2, c_a, width), I32),
                       pltpu.VMEM((2, SUBLANES, c_b), I32), pltpu.VMEM((2, c_b, width), I32),
                       pltpu.SemaphoreType.DMA((2,)), pltpu.SemaphoreType.DMA((2,))])
    def run(ha_hbm, pa_hbm, hb_hbm, pb_hbm, xs_hbm, idx_a, rows_a, idx_b, rows_b, in_sems, out_sems):
        wid = _sc_worker_id()

        def segment(h_hbm, p_hbm, idx_v, rows_v, c):
            nch = h_hbm.shape[0] // (SC_WORKERS * c)
            assert nch == 1 or nch % 2 == 0
            chunk0 = wid * nch

            def loads(ci, b):
                return [pltpu.make_async_copy(p_hbm.at[chunk0 + ci], idx_v.at[b], in_sems.at[b]),
                        pltpu.make_async_copy(h_hbm.at[pl.ds((chunk0 + ci) * c, c)], rows_v.at[b], in_sems.at[b])]

            def scatters(b):
                return [pltpu.make_async_copy(rows_v.at[b], xs_hbm.at[idx_v.at[b].at[k]], out_sems.at[b])
                        for k in range(TOP_K)]

            def start(copies):
                for cp in copies:
                    cp.start()

            def wait(copies):
                for cp in copies:
                    cp.wait()

            start(loads(0, 0))
            if nch == 1:
                wait(loads(0, 0))
                start(scatters(0))
                wait(scatters(0))
                return

            @pl.loop(0, nch, step=2)
            def _(ci):
                for b in range(2):
                    wait(loads(ci + b, b))
                    start(scatters(b))

                    @pl.when(ci + b + 1 < nch)
                    def _():
                        @pl.when(ci + b >= 1)
                        def _():
                            wait(scatters(1 - b))

                        start(loads(ci + b + 1, 1 - b))

            wait(scatters(0))
            wait(scatters(1))

        segment(ha_hbm, pa_hbm, idx_a, rows_a, c_a)
        segment(hb_hbm, pb_hbm, idx_b, rows_b, c_b)

    return run(h_a, pos_a, h_b, pos_b)


def _experts_kernel(sb_ref, nb_ref, cnt_ref, nu_ref, xs_hbm, wg_ref, wu_ref, wd_ref, y_hbm,
                    xbuf, ybuf, wgb, wub, wdb, in_sems, out_sems):
    e = pl.program_id(0)
    ring, bm, half = xbuf.shape
    n_used = nu_ref[0]

    def slot_of(g):
        return g & (ring - 1)

    def in_copy(g):
        s = slot_of(g)
        return pltpu.make_async_copy(xs_hbm.at[pl.ds(pl.multiple_of(g * bm, bm), bm), :], xbuf.at[s], in_sems.at[s])

    def out_copy(g):
        s = slot_of(g)
        return pltpu.make_async_copy(ybuf.at[s], y_hbm.at[pl.ds(pl.multiple_of(g * bm, bm), bm), :], out_sems.at[s])

    @pl.when(e == 0)
    def _():
        for g in range(ring - 1):
            @pl.when(g < n_used)
            def _():
                in_copy(g).start()

    wgb[...] = wg_ref[...].astype(BF16)
    wub[...] = wu_ref[...].astype(BF16)
    wdb[...] = wd_ref[...].astype(BF16)
    row = lax.broadcasted_iota(I32, (bm, half), 0)

    def block(b, carry):
        g = sb_ref[e] + b

        @pl.when(g + ring - 1 < n_used)
        def _():
            in_copy(g + ring - 1).start()

        in_copy(g).wait()

        @pl.when(g >= ring)
        def _():
            out_copy(g - ring).wait()

        s = slot_of(g)
        x_lo, x_hi = _unpack_bf16_pairs(jnp.where(row < cnt_ref[e] - b * bm, xbuf[s], 0))
        a = (jnp.dot(x_lo, wgb[:half], preferred_element_type=F32)
             + jnp.dot(x_hi, wgb[half:], preferred_element_type=F32))
        u = (jnp.dot(x_lo, wub[:half], preferred_element_type=F32)
             + jnp.dot(x_hi, wub[half:], preferred_element_type=F32))
        ybuf[s] = _pack_bf16_pairs(jnp.dot((_silu(a) * u).astype(BF16), wdb[...], preferred_element_type=F32))
        out_copy(g).start()
        return carry

    lax.fori_loop(0, nb_ref[e], block, 0)

    @pl.when(e == pl.num_programs(0) - 1)
    def _():
        for r in range(ring):
            @pl.when(n_used - 1 - r >= 0)
            def _():
                out_copy(n_used - 1 - r).wait()


def _experts(first_blk, n_blk, counts, n_used, xs, wg, wu, wd):
    rows = xs.shape[0]
    bm, ring = EXPERT_BLOCK, EXPERT_RING
    assert ring & (ring - 1) == 0
    w_blk = lambda e, *_: (e, 0, 0)
    grid_spec = pltpu.PrefetchScalarGridSpec(
        num_scalar_prefetch=4,
        grid=(N_EXPERTS,),
        in_specs=[pl.BlockSpec(memory_space=pl.ANY),
                  pl.BlockSpec((None, D_MODEL, D_EXPERT), w_blk),
                  pl.BlockSpec((None, D_MODEL, D_EXPERT), w_blk),
                  pl.BlockSpec((None, D_EXPERT, D_MODEL), w_blk)],
        out_specs=pl.BlockSpec(memory_space=pl.ANY),
        scratch_shapes=[pltpu.VMEM((ring, bm, D_MODEL // 2), I32), pltpu.VMEM((ring, bm, D_MODEL // 2), I32),
                        pltpu.VMEM((D_MODEL, D_EXPERT), BF16), pltpu.VMEM((D_MODEL, D_EXPERT), BF16),
                        pltpu.VMEM((D_EXPERT, D_MODEL), BF16),
                        pltpu.SemaphoreType.DMA((ring,)), pltpu.SemaphoreType.DMA((ring,))],
    )
    return pl.pallas_call(
        _experts_kernel,
        grid_spec=grid_spec,
        out_shape=jax.ShapeDtypeStruct((rows, D_MODEL // 2), I32),
        compiler_params=pltpu.CompilerParams(dimension_semantics=("arbitrary",), vmem_limit_bytes=VMEM_LIMIT),
        name="moe_experts",
    )(first_blk, n_blk, counts, n_used, xs, wg, wu, wd)


def _sc_combine(y, segs):
    c = COMBINE_CHUNK
    half = y.shape[1]
    d = 2 * half
    ns = len(segs)
    params = dataclasses.replace(pltpu.CompilerParams(), needs_layout_passes=False)

    @functools.partial(
        pl.kernel, mesh=_sc_mesh(), compiler_params=params,
        out_type=[jax.ShapeDtypeStruct((w.shape[0], d), F32) for _, w in segs],
        scratch_types=[pltpu.VMEM(pos.shape[1:], I32) for pos, _ in segs]
        + [pltpu.VMEM((2, c, SUBLANES * SC_LANES), F32), pltpu.VMEM((2, TOP_K, c, half), I32),
           pltpu.VMEM((c, d), F32), pltpu.SemaphoreType.DMA((2,)), pltpu.SemaphoreType.DMA])
    def run(y_hbm, *refs):
        ins, outs, idxs = refs[:2 * ns], refs[2 * ns:3 * ns], refs[3 * ns:4 * ns]
        w_v, buf_v, out_v, sems, out_sem = refs[4 * ns:]
        wid = _sc_worker_id()

        def segment(p_hbm, w_hbm, o_hbm, idx_v):
            per_w = idx_v.shape[1]
            nch = per_w // c
            base = wid * per_w
            pltpu.sync_copy(p_hbm.at[wid], idx_v)

            def in_copies(ci, b):
                w_copy = pltpu.make_async_copy(w_hbm.at[pl.ds(base + ci * c, c)], w_v.at[b], sems.at[b])
                return [w_copy] + [
                    pltpu.make_async_copy(y_hbm.at[idx_v.at[k, pl.ds(ci * c, c)]], buf_v.at[b].at[k], sems.at[b])
                    for k in range(TOP_K)]

            def out_copy(ci):
                return pltpu.make_async_copy(out_v, o_hbm.at[pl.ds(base + ci * c, c)], out_sem)

            def reduce_rows(ci, b):
                @pl.when(ci >= 1)
                def _():
                    out_copy(ci - 1).wait()

                for t0 in range(0, c, COMBINE_TOKEN_GROUP):
                    group = range(t0, t0 + COMBINE_TOKEN_GROUP)
                    ws = [[w_v[b, t, pl.ds(k * SC_LANES, SC_LANES)] for k in range(TOP_K)] for t in group]

                    @plsc.parallel_loop(0, half // SC_LANES, unroll=2)
                    def _(j):
                        lanes = pl.ds(pl.multiple_of(j * SC_LANES, SC_LANES), SC_LANES)
                        hi_lanes = pl.ds(pl.multiple_of(half + j * SC_LANES, SC_LANES), SC_LANES)
                        for t, wt in zip(group, ws):
                            acc_lo = acc_hi = None
                            for k in range(TOP_K):
                                word = buf_v[b, k, t, lanes]
                                lo = plsc.bitcast(lax.shift_left(word, jnp.int32(16)), F32) * wt[k]
                                hi = plsc.bitcast(word & jnp.int32(-65536), F32) * wt[k]
                                acc_lo = lo if acc_lo is None else acc_lo + lo
                                acc_hi = hi if acc_hi is None else acc_hi + hi
                            out_v[t, lanes] = acc_lo
                            out_v[t, hi_lanes] = acc_hi

                out_copy(ci).start()

            for cp in in_copies(0, 0):
                cp.start()

            @pl.loop(0, nch, step=2)
            def _(ci):
                for b in range(2):
                    @pl.when(ci + b + 1 < nch)
                    def _():
                        for cp in in_copies(ci + b + 1, 1 - b):
                            cp.start()

                    for cp in in_copies(ci + b, b):
                        cp.wait()
                    reduce_rows(ci + b, b)

            out_copy(nch - 1).wait()

        for i in range(ns):
            segment(ins[2 * i], ins[2 * i + 1], outs[i], idxs[i])

    res = run(y, *[a for seg in segs for a in seg])
    return list(res) if isinstance(res, (list, tuple)) else [res]


def _shared_kernel(x1_ref, h2_ref, g2_ref, wsg_ref, wsu_ref, wsd_ref, o_ref):
    half = h2_ref.shape[1]
    x_lo, x_hi = _unpack_bf16_pairs(h2_ref[...])
    wsg, wsu = wsg_ref[...], wsu_ref[...]
    a = (jnp.dot(x_lo, wsg[:half], preferred_element_type=F32) + jnp.dot(x_hi, wsg[half:], preferred_element_type=F32))
    b = (jnp.dot(x_lo, wsu[:half], preferred_element_type=F32) + jnp.dot(x_hi, wsu[half:], preferred_element_type=F32))
    shared = jnp.dot((_silu(a) * b).astype(BF16), wsd_ref[...], preferred_element_type=F32)
    o_ref[...] = x1_ref[...] + g2_ref[...] * shared


def _shared(x1, h2, g2, wsg, wsu, wsd, tm, mod_spec):
    n = h2.shape[0]
    full = lambda arr: pl.BlockSpec(arr.shape, lambda i: (0,) * arr.ndim)
    tile = pl.BlockSpec((tm, D_MODEL), lambda i: (i, 0))
    return pl.pallas_call(
        _shared_kernel,
        grid=(n // tm,),
        in_specs=[tile, pl.BlockSpec((tm, D_MODEL // 2), lambda i: (i, 0)), mod_spec, full(wsg), full(wsu), full(wsd)],
        out_specs=tile,
        out_shape=jax.ShapeDtypeStruct((n, D_MODEL), F32),
        compiler_params=pltpu.CompilerParams(dimension_semantics=("arbitrary",), vmem_limit_bytes=VMEM_LIMIT),
        name="shared_expert",
    )(x1, h2, g2, wsg, wsu, wsd)


def _final_kernel(base_ref, routed_ref, g2_ref, *rest):
    o_ref = rest[-1]
    o_ref[...] = base_ref[...] + g2_ref[...] * routed_ref[...]


def _final(base, routed, g2, tm, mod_spec, tile0=0, out=None):
    n = base.shape[0]
    here = pl.BlockSpec((tm, D_MODEL), lambda i: (i + tile0, 0))
    in_specs = [here, pl.BlockSpec((tm, D_MODEL), lambda i: (i, 0)), mod_spec]
    args = [base, routed, g2]
    if out is not None:
        in_specs.append(pl.BlockSpec(memory_space=pl.ANY))
        args.append(out)
    return pl.pallas_call(
        _final_kernel,
        grid=(routed.shape[0] // tm,),
        in_specs=in_specs,
        out_specs=here,
        out_shape=jax.ShapeDtypeStruct((n, D_MODEL), F32),
        input_output_aliases={} if out is None else {3: 0},
        compiler_params=pltpu.CompilerParams(dimension_semantics=("arbitrary",), vmem_limit_bytes=VMEM_LIMIT),
        name="ffn_residual",
    )(*args)


def _block_diag(w):
    nb, bi, bj = w.shape
    return jnp.einsum('nij,nm->nimj', w, jnp.eye(nb, dtype=w.dtype)).reshape(nb * bi, nb * bj)


def kernel(x_prompt, x_sample, c_prompt, c_sample, cache_k_win, cache_v_win, state_conv, state_rnn, ada_w, ada_b, norm_mix, w_in, conv_w, conv_b, gate_a_w, gate_a_b, gate_x_w, gate_x_b, lru_lambda, q_norm, k_norm, attn_sinks, w_out, norm_ffn, router_w, router_bias, exp_w_gate, exp_w_up, exp_w_down, sh_w_gate, sh_w_up, sh_w_down):
    bp, tp, _ = x_prompt.shape
    bs, ts, _ = x_sample.shape
    win = cache_k_win.shape[2]
    n_p, n_s = bp * tp, bs * ts
    row = lambda v: v.reshape(1, -1)

    g_mix, g_ffn = row(norm_mix[0]), row(norm_ffn[0])
    win_bf = w_in[0].astype(BF16)
    q0 = 2 * D_RNN
    win_pair = jnp.concatenate([win_bf[:, :q0], _pair_heads(win_bf[:, q0:q0 + D_ATT], 1), win_bf[:, q0 + D_ATT:]],
                               axis=1)
    cw, cb = conv_w[0], row(conv_b[0])
    wg = jnp.concatenate([_block_diag(gate_a_w[0]), _block_diag(gate_x_w[0])], axis=1).astype(BF16)
    gb = row(jnp.concatenate([gate_a_b[0], gate_x_b[0]]))
    lam = row(lru_lambda[0])
    gq = row(q_norm[0])
    gk2 = row(jnp.tile(k_norm[0], N_KV_HEADS))
    sinks = attn_sinks[0]
    wo_top, wo_bot = w_out[0, :D_RNN].astype(BF16), w_out[0, D_RNN:].astype(BF16)
    wr_t = router_w[0].T
    rb = router_bias[0].reshape(N_EXPERTS, 1)
    wsg, wsu, wsd = sh_w_gate[0].astype(BF16), sh_w_up[0].astype(BF16), sh_w_down[0].astype(BF16)

    mod = _adaln(jnp.concatenate([c_prompt, c_sample], axis=0), ada_w[0], ada_b[0])
    chunks = [mod[:, i * D_MODEL:(i + 1) * D_MODEL] for i in range(6)]
    sh1p, sc1p, g1p, sh2p, sc2p, g2p = [c[:bp].reshape(bp, 1, D_MODEL) for c in chunks]
    sh1s, sc1s, g1s, sh2s, sc2s, g2s = [c[bp:] for c in chunks]

    conv0 = jnp.zeros((bp, SUBLANES, D_RNN), F32)
    h0 = jnp.zeros((bp, 1, D_RNN), F32)
    r_p, q_p, k_p, v_p, cs_p, hs_p = _front_prompt(x_prompt, sc1p, sh1p, g_mix, win_pair, cw, cb, wg, gb, lam, gk2,
                                                   conv0, h0)
    a_p = _attn_prompt(q_p, k_p, v_p, row(jnp.tile(q_norm[0], 2)), sinks)

    x_s_t = jnp.swapaxes(x_sample, 0, 1)
    r_s, q_s, k_s, v_s, cs_s, hs_s = _front_sample(x_s_t, sc1s, sh1s, g_mix, win_bf, cw, cb, wg, gb, lam, gk2,
                                                   jnp.swapaxes(state_conv[0], 0, 1), state_rnn[0])
    cache_k = cache_k_win[0].reshape(bs, win, D_KV)
    cache_v = cache_v_win[0].reshape(bs, win, D_KV)
    a_s = _attn_sample(q_s, k_s, v_s, cache_k, cache_v, gq, sinks)

    tiles_per_seq = tp // TM_POST
    mod_p = pl.BlockSpec((None, 1, D_MODEL), lambda i, *_: (i // tiles_per_seq, 0, 0))
    mod_s = pl.BlockSpec((bs, D_MODEL), lambda i, *_: (0, 0))
    cnt0 = jnp.zeros((N_EXPERTS, LANES), F32)
    x1_p, h2_p, eidx_p, gw_p, rank_p, cnt_p = _post(
        x_prompt.reshape(n_p, D_MODEL), r_p.reshape(n_p, D_RNN), a_p.reshape(n_p, D_ATT), g1p, sc2p, sh2p, g_ffn,
        wo_top, _pair_heads(wo_bot, 0), wr_t, rb, cnt0, TM_POST, mod_p)
    x1_s, h2_s, eidx_s, gw_s, rank_s, cnt_all = _post(
        x_s_t.reshape(n_s, D_MODEL), r_s.reshape(n_s, D_RNN), a_s.reshape(n_s, D_ATT), g1s, sc2s, sh2s, g_ffn,
        wo_top, wo_bot, wr_t, rb, cnt_p, bs, mod_s)

    bm = EXPERT_BLOCK
    n_blocks = -(-((n_p + n_s) * TOP_K) // bm) + N_EXPERTS
    pstart, used = _plan(cnt_all, bm)
    counts = cnt_all[:, 0].astype(I32)
    first_blk = (pstart[:, 0] * (1.0 / bm)).astype(I32)
    n_blk = (counts + (bm - 1)) // bm
    n_used = used[0, :1]

    def sc_layouts(eidx, rank, gw, n, parts):
        per_w = n // (SC_WORKERS * parts)
        c = min(DISPATCH_CHUNK, per_w)
        outs = _layouts(pstart, eidx, rank, gw, (c,) if c == per_w else (c, per_w))
        return (outs[0], c), (outs[-2], outs[-1])

    (dpos_p, c_p), (cpos_p, cw_p) = sc_layouts(eidx_p, rank_p, gw_p, n_p, 2)
    (dpos_s, c_s), comb_s = sc_layouts(eidx_s, rank_s, gw_s, n_s, 1)
    xs = _sc_dispatch((h2_p, dpos_p, c_p), (h2_s, dpos_s, c_s), n_blocks * bm)
    base_p = _shared(x1_p, h2_p, g2p, wsg, wsu, wsd, TM_POST, mod_p)
    base_s = _shared(x1_s, h2_s, g2s, wsg, wsu, wsd, bs, mod_s)
    y = _experts(first_blk, n_blk, counts, n_used, xs, exp_w_gate[0], exp_w_up[0], exp_w_down[0])
    n_h = n_p // 2
    routed_a, routed_s = _sc_combine(y, [(cpos_p[:SC_WORKERS], cw_p[:n_h]), comb_s])
    routed_b, = _sc_combine(y, [(cpos_p[SC_WORKERS:], cw_p[n_h:])])
    tiles_h = n_h // TM_POST
    mod_p2 = pl.BlockSpec((None, 1, D_MODEL), lambda i, *_: ((i + tiles_h) // tiles_per_seq, 0, 0))
    y_p = _final(base_p, routed_a, g2p, TM_POST, mod_p)
    y_p = _final(base_p, routed_b, g2p, TM_POST, mod_p2, tile0=tiles_h, out=y_p)
    y_s = _final(base_s, routed_s, g2s, bs, mod_s)

    y_prompt = y_p.reshape(bp, tp, D_MODEL)
    y_sample = jnp.swapaxes(y_s.reshape(ts, bs, D_MODEL), 0, 1)
    wk = min(WINDOW, tp)
    k_win_p = k_p[:, tp - wk:].reshape(1, bp, wk, N_KV_HEADS, HEAD_DIM)
    v_win_p = v_p[:, tp - wk:].reshape(1, bp, wk, N_KV_HEADS, HEAD_DIM)
    k_new = jnp.swapaxes(k_s, 0, 1)
    v_new = jnp.swapaxes(v_s, 0, 1)
    heads = lambda a: a.reshape(bs, ts, N_KV_HEADS, HEAD_DIM)
    k_win_s = jnp.concatenate([cache_k_win[0], heads(k_new)], axis=1)[None, :, ts:]
    v_win_s = jnp.concatenate([cache_v_win[0], heads(v_new)], axis=1)[None, :, ts:]
    return (y_prompt, y_sample, k_win_p, v_win_p, cs_p[None, :, SUBLANES - (CONV_W - 1):], hs_p.reshape(1, bp, D_RNN),
            k_win_s, v_win_s, jnp.swapaxes(cs_s, 0, 1)[None], hs_s[None])
```

```python
import dataclasses
import functools

import jax
import jax.numpy as jnp
from jax import lax
from jax.experimental import pallas as pl
from jax.experimental.pallas import tpu as pltpu
from jax.experimental.pallas import tpu_sc as plsc

F32 = jnp.float32
BF16 = jnp.bfloat16
I32 = jnp.int32

D_MODEL = 1024
D_RNN = 512
N_RNN_BLOCKS = 8
CONV_W = 4
LRU_C = 8.0
HEAD_DIM = 64
N_HEADS = 8
N_KV_HEADS = 2
GQA = N_HEADS // N_KV_HEADS
D_ATT = N_HEADS * HEAD_DIM
D_KV = N_KV_HEADS * HEAD_DIM
WINDOW = 128
N_EXPERTS = 64
TOP_K = 6
N_GROUPS = 8
GROUP_SIZE = N_EXPERTS // N_GROUPS
TOPK_GROUPS = 4
D_EXPERT = 256
D_SHARED = 256
ROUTE_SCALE = 2.5
EPS = 1e-6
NEG = -1e30
F32_TINY = 1.1754944e-38
D_IN = 2 * D_RNN + D_ATT + 2 * D_KV

SUBLANES = 8
LANES = 128
TM_PROMPT = 512
TM_POST = 512
ATT_BLOCK = WINDOW
ATT_STEP_BLOCKS = 8
EXPERT_BLOCK = 512
EXPERT_RING = 4
SAMPLE_CHUNK = 8
POS_TILE = 2048
VMEM_LIMIT = 48 * 1024 * 1024

SC_CORES = 2
SC_SUBCORES = 16
SC_WORKERS = SC_CORES * SC_SUBCORES
SC_LANES = 16
DISPATCH_CHUNK = 64
COMBINE_CHUNK = 8
COMBINE_TOKEN_GROUP = 4


def _sigmoid(x):
    return 0.5 * jnp.tanh(0.5 * x) + 0.5


def _silu(x):
    return x * _sigmoid(x)


def _gelu_tanh(x):
    c = 0.7978845608028654
    return x * (0.5 * (1.0 + jnp.tanh(c * (x + 0.044715 * (x * x * x)))))


def _log1p(x):
    u = 1.0 + x
    return jnp.where(u == 1.0, x, jnp.log(u) * x / jnp.where(u == 1.0, 1.0, u - 1.0))


def _neg_expm1_2x(x, exp_x):
    return -jnp.tanh(x) * (exp_x * exp_x + 1.0)


def _softplus(z):
    return jnp.maximum(z, 0.0) + _log1p(jnp.exp(-jnp.abs(z)))


def _div_pow2(x, d):
    assert d & (d - 1) == 0
    return lax.shift_right_logical(x, d.bit_length() - 1)


def _mod_pow2(x, d):
    assert d & (d - 1) == 0
    return x & (d - 1)


def _norm_mod(x, g, sc, sh):
    ms = jnp.mean(x * x, axis=-1, keepdims=True)
    return (x * lax.rsqrt(ms + EPS)) * g * (1.0 + sc) + sh


def _bdot(a, b):
    return jnp.dot(a.astype(BF16), b.astype(BF16), preferred_element_type=F32)


def _bdot_nt(a, b):
    return lax.dot_general(a.astype(BF16), b.astype(BF16), (((1,), (1,)), ((), ())),
                           preferred_element_type=F32)


def _pack_bf16_pairs(x):
    w = x.shape[1] // 2

    def bf16_bits(v):
        return pltpu.bitcast(v.astype(BF16).astype(F32), I32)

    return lax.shift_right_logical(bf16_bits(x[:, :w]), 16) | (bf16_bits(x[:, w:]) & jnp.int32(-65536))


def _unpack_bf16_pairs(p):
    lo = pltpu.bitcast(lax.shift_left(p, 16), F32)
    hi = pltpu.bitcast(p & jnp.int32(-65536), F32)
    return lo.astype(BF16), hi.astype(BF16)


def _knorm(k, gk2):
    lane = lax.broadcasted_iota(I32, k.shape, 1)
    first = lane < HEAD_DIM
    k2 = k * k
    s0 = jnp.sum(jnp.where(first, k2, 0.0), axis=-1, keepdims=True)
    s1 = jnp.sum(jnp.where(first, 0.0, k2), axis=-1, keepdims=True)
    ms = jnp.where(first, s0, s1) * (1.0 / HEAD_DIM)
    return k * lax.rsqrt(ms + EPS) * gk2


def _pair_heads(w, axis):
    shape = w.shape
    split = shape[:axis] + (N_KV_HEADS, GQA, HEAD_DIM) + shape[axis + 1:]
    return jnp.swapaxes(w.reshape(split), axis, axis + 1).reshape(shape)


def _lru_coeffs(u, wg, gb, lam):
    g = _bdot(u, wg) + gb
    r = _sigmoid(g[:, :D_RNN])
    i = _sigmoid(g[:, D_RNN:])
    log_a = (-LRU_C * r) * _softplus(-lam)
    a = jnp.exp(log_a)
    om = _neg_expm1_2x(log_a, a)
    b = (om * lax.rsqrt(jnp.maximum(om, F32_TINY))) * (i * u)
    return a, b


def _adaln_kernel(c_ref, w_ref, b_ref, o_ref):
    o_ref[...] = _bdot(_silu(c_ref[...]), w_ref[...]) + b_ref[...]


def _adaln(c_all, ada_w, ada_b):
    n = c_all.shape[0]
    return pl.pallas_call(
        _adaln_kernel,
        grid=(6,),
        in_specs=[pl.BlockSpec((n, D_MODEL), lambda j: (0, 0)),
                  pl.BlockSpec((D_MODEL, D_MODEL), lambda j: (0, j)),
                  pl.BlockSpec((1, D_MODEL), lambda j: (0, j))],
        out_specs=pl.BlockSpec((n, D_MODEL), lambda j: (0, j)),
        out_shape=jax.ShapeDtypeStruct((n, 6 * D_MODEL), F32),
        compiler_params=pltpu.CompilerParams(dimension_semantics=("arbitrary",), vmem_limit_bytes=VMEM_LIMIT),
        name="adaln",
    )(c_all, ada_w, ada_b.reshape(1, -1))


def _scan_rows(a, b, h_in):
    n, c = a.shape
    groups = n // SUBLANES
    a = a.reshape(groups, SUBLANES, c)
    b = b.reshape(groups, SUBLANES, c)
    sub = lax.broadcasted_iota(I32, a.shape, 1)
    s = 1
    while s < SUBLANES:
        m = sub >= s
        a_sh = jnp.where(m, pltpu.roll(a, s, 1), 1.0)
        b_sh = jnp.where(m, pltpu.roll(b, s, 1), 0.0)
        b = a * b_sh + b
        a = a * a_sh
        s *= 2
    carry = h_in
    hs = []
    for g in range(groups):
        hg = a[g] * carry + b[g]
        hs.append(hg)
        carry = hg[SUBLANES - 1:SUBLANES]
    return jnp.concatenate(hs, axis=0)


def _front_prompt_kernel(x_ref, sc_ref, sh_ref, g_ref, win_ref, cw_ref, cb_ref, wg_ref, gb_ref, lam_ref, gk_ref,
                         prev_ref, h0_ref, r_ref, q_ref, k_ref, v_ref, cs_ref, hs_ref, tail_ref, hc_ref):
    j = pl.program_id(1)
    tm = x_ref.shape[0]

    @pl.when(j == 0)
    def _():
        tail_ref[...] = prev_ref[...]
        hc_ref[...] = h0_ref[...]

    h = _norm_mod(x_ref[...], g_ref[...], sc_ref[...], sh_ref[...])
    proj = jnp.dot(h.astype(BF16), win_ref[...], preferred_element_type=F32)
    xr = proj[:, 0:D_RNN]
    yr = proj[:, D_RNN:2 * D_RNN]
    q_ref[...] = proj[:, 2 * D_RNN:2 * D_RNN + D_ATT]
    k_ref[...] = _knorm(proj[:, 2 * D_RNN + D_ATT:2 * D_RNN + D_ATT + D_KV], gk_ref[...])
    v_ref[...] = proj[:, 2 * D_RNN + D_ATT + D_KV:D_IN]

    tail = tail_ref[...]
    row8 = lax.broadcasted_iota(I32, tail.shape, 0)

    def shifted(s):
        rolled = pltpu.roll(xr, s, 0)
        top = jnp.where(row8 < s, pltpu.roll(tail, s, 0), rolled[0:SUBLANES])
        return jnp.concatenate([top, rolled[SUBLANES:]], axis=0)

    cw = cw_ref[...]
    u = cb_ref[...] + shifted(3) * cw[0:1]
    u = u + shifted(2) * cw[1:2]
    u = u + shifted(1) * cw[2:3]
    u = u + xr * cw[3:4]
    tail_ref[...] = xr[tm - SUBLANES:tm]

    a, b = _lru_coeffs(u, wg_ref[...], gb_ref[...], lam_ref[...])
    hs = _scan_rows(a, b, hc_ref[...])
    hc_ref[...] = hs[tm - 1:tm]
    r_ref[...] = hs * _gelu_tanh(yr)

    @pl.when(j == pl.num_programs(1) - 1)
    def _():
        cs_ref[...] = xr[tm - SUBLANES:tm]
        hs_ref[...] = hs[tm - 1:tm]


def _front_prompt(x, sc, sh, g, win, cw, cb, wg, gb, lam, gk, prev, h0):
    bsz, t, _ = x.shape
    tm = TM_PROMPT
    full = lambda a: pl.BlockSpec(a.shape, lambda b, j: (0,) * a.ndim)
    per_b = lambda a: pl.BlockSpec((None,) + a.shape[1:], lambda b, j: (b,) + (0,) * (a.ndim - 1))
    tile = lambda w: pl.BlockSpec((None, tm, w), lambda b, j: (b, j, 0))
    return pl.pallas_call(
        _front_prompt_kernel,
        grid=(bsz, t // tm),
        in_specs=[tile(D_MODEL), per_b(sc), per_b(sh), full(g), full(win), full(cw), full(cb), full(wg), full(gb),
                  full(lam), full(gk), per_b(prev), per_b(h0)],
        out_specs=[tile(D_RNN), tile(D_ATT), tile(D_KV), tile(D_KV),
                   pl.BlockSpec((None, SUBLANES, D_RNN), lambda b, j: (b, 0, 0)),
                   pl.BlockSpec((None, 1, D_RNN), lambda b, j: (b, 0, 0))],
        out_shape=[jax.ShapeDtypeStruct((bsz, t, D_RNN), F32), jax.ShapeDtypeStruct((bsz, t, D_ATT), F32),
                   jax.ShapeDtypeStruct((bsz, t, D_KV), F32), jax.ShapeDtypeStruct((bsz, t, D_KV), F32),
                   jax.ShapeDtypeStruct((bsz, SUBLANES, D_RNN), F32), jax.ShapeDtypeStruct((bsz, 1, D_RNN), F32)],
        scratch_shapes=[pltpu.VMEM((SUBLANES, D_RNN), F32), pltpu.VMEM((1, D_RNN), F32)],
        compiler_params=pltpu.CompilerParams(dimension_semantics=("arbitrary", "arbitrary"),
                                             vmem_limit_bytes=VMEM_LIMIT),
        name="front_prompt",
    )(x, sc, sh, g, win, cw, cb, wg, gb, lam, gk, prev, h0)


def _front_sample_kernel(x_ref, sc_ref, sh_ref, g_ref, win_ref, cw_ref, cb_ref, wg_ref, gb_ref, lam_ref, gk_ref,
                         prev_ref, h0_ref, r_ref, q_ref, k_ref, v_ref, cs_ref, hs_ref):
    t_len, bsz, _ = x_ref.shape
    x = x_ref[...]
    ms = jnp.mean(x * x, axis=-1, keepdims=True)
    h = (x * lax.rsqrt(ms + EPS)) * g_ref[...] * (1.0 + sc_ref[...]) + sh_ref[...]
    proj = jnp.dot(h.reshape(t_len * bsz, D_MODEL).astype(BF16), win_ref[...], preferred_element_type=F32)
    xr = proj[:, 0:D_RNN]
    yr = proj[:, D_RNN:2 * D_RNN]
    q_ref[...] = proj[:, 2 * D_RNN:2 * D_RNN + D_ATT].reshape(t_len, bsz, D_ATT)
    k_ref[...] = _knorm(proj[:, 2 * D_RNN + D_ATT:2 * D_RNN + D_ATT + D_KV], gk_ref[...]).reshape(t_len, bsz, D_KV)
    v_ref[...] = proj[:, 2 * D_RNN + D_ATT + D_KV:D_IN].reshape(t_len, bsz, D_KV)

    def at_time(t):
        if t >= 0:
            return xr[t * bsz:(t + 1) * bsz]
        return prev_ref[CONV_W - 1 + t]

    cw = cw_ref[...]
    us = []
    for t in range(t_len):
        u = cb_ref[...] + at_time(t - 3) * cw[0:1]
        u = u + at_time(t - 2) * cw[1:2]
        u = u + at_time(t - 1) * cw[2:3]
        u = u + at_time(t) * cw[3:4]
        us.append(u)
    a, b = _lru_coeffs(jnp.concatenate(us, axis=0), wg_ref[...], gb_ref[...], lam_ref[...])
    hcur = h0_ref[...]
    for t in range(t_len):
        hcur = a[t * bsz:(t + 1) * bsz] * hcur + b[t * bsz:(t + 1) * bsz]
        r_ref[t] = hcur * _gelu_tanh(yr[t * bsz:(t + 1) * bsz])
    hs_ref[...] = hcur
    for s in range(CONV_W - 1):
        cs_ref[s] = at_time(t_len - (CONV_W - 1) + s)


def _front_sample(x_t, sc, sh, g, win, cw, cb, wg, gb, lam, gk, prev_t, h0):
    t_len, bsz, _ = x_t.shape
    return pl.pallas_call(
        _front_sample_kernel,
        out_shape=[jax.ShapeDtypeStruct((t_len, bsz, D_RNN), F32), jax.ShapeDtypeStruct((t_len, bsz, D_ATT), F32),
                   jax.ShapeDtypeStruct((t_len, bsz, D_KV), F32), jax.ShapeDtypeStruct((t_len, bsz, D_KV), F32),
                   jax.ShapeDtypeStruct((CONV_W - 1, bsz, D_RNN), F32), jax.ShapeDtypeStruct((bsz, D_RNN), F32)],
        compiler_params=pltpu.CompilerParams(vmem_limit_bytes=VMEM_LIMIT),
        name="front_sample",
    )(x_t, sc, sh, g, win, cw, cb, wg, gb, lam, gk, prev_t, h0)


def _qnorm(q, gq):
    ms = jnp.mean(q * q, axis=-1, keepdims=True)
    return q * lax.rsqrt(ms + EPS) * gq


def _attn_prompt_kernel(sink_ref, q_ref, kp_ref, kc_ref, vp_ref, vc_ref, gq_ref, o_ref):
    j = pl.program_id(1)
    blk = kp_ref.shape[0]
    k_all = jnp.concatenate([kp_ref[...], kc_ref[...]], axis=0)
    v_all = jnp.concatenate([vp_ref[...], vc_ref[...]], axis=0)
    qi = lax.broadcasted_iota(I32, (blk, 2 * blk), 0)
    kj = lax.broadcasted_iota(I32, (blk, 2 * blk), 1)
    dist = blk + qi - kj
    window = (dist >= 0) & (dist <= WINDOW)
    distf = dist.astype(F32)
    slab = 2 * HEAD_DIM
    first_q = lax.broadcasted_iota(I32, (blk, slab), 1) < HEAD_DIM
    first_kv = lax.broadcasted_iota(I32, (2 * blk, slab), 1) < HEAD_DIM

    bias = [jnp.where(window, (-(2.0 ** -(h + 1))) * distf, NEG) for h in range(N_HEADS)]
    no_prev = (kj < blk) & (j == 0)
    gq = gq_ref[...] * (HEAD_DIM ** -0.5)

    def probs(s, h, first_block):
        b = jnp.where(no_prev, NEG, bias[h]) if first_block else bias[h]
        s = s + b
        sink = sink_ref[h]
        m = jnp.maximum(jnp.max(s, axis=-1, keepdims=True), sink)
        p = jnp.exp(s - m)
        return p, jnp.sum(p, axis=-1, keepdims=True) + jnp.exp(sink - m)

    for sub in range(q_ref.shape[0] // blk):
        first = sub == 0
        q = q_ref[sub * blk:(sub + 1) * blk, :]
        kk = k_all[sub * blk:(sub + 2) * blk]
        vv = v_all[sub * blk:(sub + 2) * blk]
        v_a = jnp.where(first_kv, vv, 0.0)
        v_b = jnp.where(first_kv, 0.0, vv)
        outs = []
        for i in range(GQA):
            q2 = _knorm(q[:, i * slab:(i + 1) * slab], gq)
            p_a, l_a = probs(_bdot_nt(jnp.where(first_q, q2, 0.0), kk), i, first)
            p_b, l_b = probs(_bdot_nt(jnp.where(first_q, 0.0, q2), kk), GQA + i, first)
            outs.append((_bdot(p_a, v_a) + _bdot(p_b, v_b)) / jnp.where(first_q, l_a, l_b))
        o_ref[sub * blk:(sub + 1) * blk, :] = jnp.concatenate(outs, axis=1)


def _attn_prompt(q, k, v, gq, sinks):
    bsz, t, _ = q.shape
    blk = ATT_BLOCK
    nsub = ATT_STEP_BLOCKS
    cur = lambda w: pl.BlockSpec((None, nsub * blk, w), lambda b, j: (b, j, 0))
    prv = lambda w: pl.BlockSpec((None, blk, w), lambda b, j: (b, jnp.maximum(nsub * j - 1, 0), 0))
    return pl.pallas_call(
        _attn_prompt_kernel,
        grid=(bsz, t // (nsub * blk)),
        in_specs=[pl.BlockSpec(memory_space=pltpu.SMEM), cur(D_ATT), prv(D_KV), cur(D_KV), prv(D_KV), cur(D_KV),
                  pl.BlockSpec(gq.shape, lambda b, j: (0, 0))],
        out_specs=cur(D_ATT),
        out_shape=jax.ShapeDtypeStruct((bsz, t, D_ATT), F32),
        compiler_params=pltpu.CompilerParams(dimension_semantics=("arbitrary", "arbitrary"),
                                             vmem_limit_bytes=VMEM_LIMIT),
        name="attn_prompt",
    )(sinks, q, k, k, v, v, gq)


def _attn_sample_kernel(sink_ref, q_ref, kn_ref, vn_ref, kc_ref, vc_ref, gq_ref, o_ref):
    t_len, cb, _ = q_ref.shape
    win = kc_ref.shape[1]
    rows = GQA * t_len * cb
    kc = kc_ref[...].reshape(cb * win, D_KV)
    vc = vc_ref[...].reshape(cb * win, D_KV)
    kn = kn_ref[...].reshape(t_len * cb, D_KV)
    vn = vn_ref[...].reshape(t_len * cb, D_KV)

    r_c = lax.broadcasted_iota(I32, (rows, cb * win), 0)
    c_c = lax.broadcasted_iota(I32, (rows, cb * win), 1)
    t_c = _div_pow2(_mod_pow2(r_c, t_len * cb), cb)
    valid_c = (_mod_pow2(r_c, cb) == _div_pow2(c_c, win)) & (_mod_pow2(c_c, win) >= t_c)
    dist_c = (win + t_c - _mod_pow2(c_c, win)).astype(F32)
    r_n = lax.broadcasted_iota(I32, (rows, t_len * cb), 0)
    c_n = lax.broadcasted_iota(I32, (rows, t_len * cb), 1)
    t_n = _div_pow2(_mod_pow2(r_n, t_len * cb), cb)
    valid_n = (_mod_pow2(r_n, cb) == _mod_pow2(c_n, cb)) & (_div_pow2(c_n, cb) <= t_n)
    dist_n = (t_n - _div_pow2(c_n, cb)).astype(F32)
    hl = _div_pow2(lax.broadcasted_iota(I32, (rows, 1), 0), t_len * cb)

    per_group = []
    for g in range(N_KV_HEADS):
        slabs = [q_ref[t][:, (g * GQA + i) * HEAD_DIM:(g * GQA + i + 1) * HEAD_DIM]
                 for i in range(GQA) for t in range(t_len)]
        qg = _qnorm(jnp.concatenate(slabs, axis=0), gq_ref[...])
        slope = jnp.zeros((rows, 1), F32)
        sink = jnp.zeros((rows, 1), F32)
        for i in range(GQA):
            slope = jnp.where(hl == i, 2.0 ** -(g * GQA + i + 1), slope)
            sink = jnp.where(hl == i, sink_ref[g * GQA + i], sink)
        lo, hi = g * HEAD_DIM, (g + 1) * HEAD_DIM
        s_c = _bdot_nt(qg, kc[:, lo:hi]) * (HEAD_DIM ** -0.5) - slope * dist_c
        s_n = _bdot_nt(qg, kn[:, lo:hi]) * (HEAD_DIM ** -0.5) - slope * dist_n
        s_c = jnp.where(valid_c, s_c, NEG)
        s_n = jnp.where(valid_n, s_n, NEG)
        m = jnp.maximum(jnp.maximum(jnp.max(s_c, axis=-1, keepdims=True), jnp.max(s_n, axis=-1, keepdims=True)), sink)
        p_c = jnp.exp(s_c - m)
        p_n = jnp.exp(s_n - m)
        l = jnp.sum(p_c, axis=-1, keepdims=True) + jnp.sum(p_n, axis=-1, keepdims=True) + jnp.exp(sink - m)
        per_group.append((_bdot(p_c, vc[:, lo:hi]) + _bdot(p_n, vn[:, lo:hi])) / l)
    for t in range(t_len):
        o_ref[t] = jnp.concatenate(
            [per_group[g][(i * t_len + t) * cb:(i * t_len + t + 1) * cb] for g in range(N_KV_HEADS) for i in range(GQA)],
            axis=1)


def _attn_sample(q_t, k_t, v_t, cache_k, cache_v, gq, sinks):
    t_len, bsz, _ = q_t.shape
    cb = SAMPLE_CHUNK
    win = cache_k.shape[1]
    new = lambda w: pl.BlockSpec((t_len, cb, w), lambda c: (0, c, 0))
    old = pl.BlockSpec((cb, win, D_KV), lambda c: (c, 0, 0))
    return pl.pallas_call(
        _attn_sample_kernel,
        grid=(bsz // cb,),
        in_specs=[pl.BlockSpec(memory_space=pltpu.SMEM), new(D_ATT), new(D_KV), new(D_KV), old, old,
                  pl.BlockSpec(gq.shape, lambda c: (0, 0))],
        out_specs=new(D_ATT),
        out_shape=jax.ShapeDtypeStruct((t_len, bsz, D_ATT), F32),
        compiler_params=pltpu.CompilerParams(dimension_semantics=("arbitrary",), vmem_limit_bytes=VMEM_LIMIT),
        name="attn_sample",
    )(sinks, q_t, k_t, v_t, cache_k, cache_v, gq)


def _route(s_t, sb_t):
    tm = s_t.shape[1]
    i8 = lax.broadcasted_iota(I32, (GROUP_SIZE, tm), 0)
    ninf = -jnp.inf
    sg = [sb_t[GROUP_SIZE * g:GROUP_SIZE * (g + 1)] for g in range(N_GROUPS)]
    gscore = []
    for g in range(N_GROUPS):
        m1 = jnp.max(sg[g], axis=0, keepdims=True)
        i1 = jnp.min(jnp.where(sg[g] == m1, i8, GROUP_SIZE), axis=0, keepdims=True)
        m2 = jnp.max(jnp.where(i8 == i1, ninf, sg[g]), axis=0, keepdims=True)
        gscore.append(m1 + m2)
    gs = jnp.concatenate(gscore, axis=0)
    gsel = jnp.zeros((N_GROUPS, tm), I32)
    for _ in range(TOPK_GROUPS):
        m = jnp.max(gs, axis=0, keepdims=True)
        idx = jnp.min(jnp.where(gs == m, i8, N_GROUPS), axis=0, keepdims=True)
        hit = i8 == idx
        gsel = jnp.where(hit, 1, gsel)
        gs = jnp.where(hit, ninf, gs)
    sm = [jnp.where(gsel[g:g + 1] > 0, sg[g], NEG) for g in range(N_GROUPS)]
    eid = [i8 + GROUP_SIZE * g for g in range(N_GROUPS)]
    sel = [jnp.zeros((GROUP_SIZE, tm), F32) for _ in range(N_GROUPS)]
    idxs, ws = [], []
    for _ in range(TOP_K):
        cm = functools.reduce(jnp.maximum, sm)
        m = jnp.max(cm, axis=0, keepdims=True)
        cand = functools.reduce(jnp.minimum, [jnp.where(sm[g] == m, eid[g], N_EXPERTS) for g in range(N_GROUPS)])
        idx = jnp.min(cand, axis=0, keepdims=True)
        wk = jnp.zeros((GROUP_SIZE, tm), F32)
        for g in range(N_GROUPS):
            hit = eid[g] == idx
            wk = wk + jnp.where(hit, s_t[GROUP_SIZE * g:GROUP_SIZE * (g + 1)], 0.0)
            sel[g] = jnp.where(hit, 1.0, sel[g])
            sm[g] = jnp.where(hit, ninf, sm[g])
        idxs.append(idx)
        ws.append(jnp.sum(wk, axis=0, keepdims=True))
    return idxs, ws, jnp.concatenate(sel, axis=0), eid


def _post_kernel(x_ref, r_ref, a_ref, g1_ref, sc2_ref, sh2_ref, nf_ref, wot_ref, wob_ref, wrt_ref, rb_ref, cin_ref,
                 x1_ref, h2_ref, eidx_ref, gw_ref, rank_ref, cnt_ref, carry_ref, before_ref):
    tm = x_ref.shape[0]

    @pl.when(pl.program_id(0) == 0)
    def _():
        carry_ref[...] = cin_ref[...]
        rr = lax.broadcasted_iota(I32, (tm, tm), 0)
        cc = lax.broadcasted_iota(I32, (tm, tm), 1)
        before_ref[...] = jnp.where(rr < cc, 1.0, 0.0).astype(BF16)

    mixed = (jnp.dot(r_ref[...].astype(BF16), wot_ref[...], preferred_element_type=F32)
             + jnp.dot(a_ref[...].astype(BF16), wob_ref[...], preferred_element_type=F32))
    x1 = x_ref[...] + g1_ref[...] * mixed
    x1_ref[...] = x1
    h2 = _norm_mod(x1, nf_ref[...], sc2_ref[...], sh2_ref[...])
    h2_ref[...] = _pack_bf16_pairs(h2)

    wr = wrt_ref[...]
    wr_hi = wr.astype(BF16)
    wr_lo = (wr - wr_hi.astype(F32)).astype(BF16)
    h_hi = h2.astype(BF16)
    h_lo = (h2 - h_hi.astype(F32)).astype(BF16)
    logits = _bdot_nt(wr_hi, h_hi) + _bdot_nt(wr_hi, h_lo) + _bdot_nt(wr_lo, h_hi)
    s_t = _sigmoid(logits)
    idxs, ws, sel, eid = _route(s_t, s_t + rb_ref[...])

    carry = carry_ref[...]
    tot = jnp.dot(sel.astype(BF16), before_ref[...], preferred_element_type=F32) + carry[:, 0:1]
    ranks = []
    for k in range(TOP_K):
        acc = jnp.zeros((GROUP_SIZE, tm), F32)
        for g in range(N_GROUPS):
            acc = acc + jnp.where(eid[g] == idxs[k], tot[GROUP_SIZE * g:GROUP_SIZE * (g + 1)], 0.0)
        ranks.append(jnp.sum(acc, axis=0, keepdims=True))
    carry = carry + jnp.sum(sel, axis=1, keepdims=True)
    carry_ref[...] = carry
    cnt_ref[...] = carry

    wsum = functools.reduce(lambda p, q: p + q, ws)
    pad_i = jnp.zeros((SUBLANES - TOP_K, tm), I32)
    pad_f = jnp.zeros((SUBLANES - TOP_K, tm), F32)
    eidx_ref[...] = jnp.concatenate(idxs + [pad_i], axis=0)
    rank_ref[...] = jnp.concatenate([r.astype(I32) for r in ranks] + [pad_i], axis=0)
    gw_ref[...] = jnp.concatenate([w / wsum * ROUTE_SCALE for w in ws] + [pad_f], axis=0)


def _post(x, r, a, g1, sc2, sh2, nf, wo_top, wo_bot, wr_t, rb, cnt_in, tm, mod_spec):
    n = x.shape[0]
    tile = lambda w: pl.BlockSpec((tm, w), lambda i: (i, 0))
    full = lambda arr: pl.BlockSpec(arr.shape, lambda i: (0,) * arr.ndim)
    slot = pl.BlockSpec((SUBLANES, tm), lambda i: (0, i))
    return pl.pallas_call(
        _post_kernel,
        grid=(n // tm,),
        in_specs=[tile(D_MODEL), tile(D_RNN), tile(D_ATT), mod_spec, mod_spec, mod_spec, full(nf), full(wo_top),
                  full(wo_bot), full(wr_t), full(rb), full(cnt_in)],
        out_specs=[tile(D_MODEL), tile(D_MODEL // 2), slot, slot, slot, full(cnt_in)],
        out_shape=[jax.ShapeDtypeStruct((n, D_MODEL), F32), jax.ShapeDtypeStruct((n, D_MODEL // 2), I32),
                   jax.ShapeDtypeStruct((SUBLANES, n), I32), jax.ShapeDtypeStruct((SUBLANES, n), F32),
                   jax.ShapeDtypeStruct((SUBLANES, n), I32), jax.ShapeDtypeStruct(cnt_in.shape, F32)],
        scratch_shapes=[pltpu.VMEM(cnt_in.shape, F32), pltpu.VMEM((tm, tm), BF16)],
        compiler_params=pltpu.CompilerParams(dimension_semantics=("arbitrary",), vmem_limit_bytes=VMEM_LIMIT),
        name="post_mix",
    )(x, r, a, g1, sc2, sh2, nf, wo_top, wo_bot, wr_t, rb, cnt_in)


def _plan_kernel(cnt_ref, ps_ref, used_ref, *, bm):
    cnt = cnt_ref[...]
    padded = jnp.ceil(cnt * (1.0 / bm)) * bm
    row = lax.broadcasted_iota(I32, cnt.shape, 0)
    pend = padded
    s = 1
    while s < N_EXPERTS:
        pend = pend + jnp.where(row >= s, pltpu.roll(pend, s, 0), 0.0)
        s *= 2
    ps_ref[...] = pend - padded
    used_ref[...] = jnp.broadcast_to(pend[N_EXPERTS - 1:N_EXPERTS] * (1.0 / bm), used_ref.shape).astype(I32)


def _plan(cnt, bm):
    assert bm & (bm - 1) == 0
    return pl.pallas_call(
        functools.partial(_plan_kernel, bm=bm),
        out_shape=[jax.ShapeDtypeStruct(cnt.shape, F32), jax.ShapeDtypeStruct((SUBLANES, LANES), I32)],
        name="moe_plan",
    )(cnt)


def _layout_kernel(ps_ref, eidx_ref, rank_ref, gw_ref, *out_refs, chunks):
    *pos_refs, w_ref = out_refs
    tn = eidx_ref.shape[1]
    e_iota = lax.broadcasted_iota(I32, (N_EXPERTS, tn), 0)
    ps = ps_ref[...][:, 0:1]
    rows = []
    for k in range(TOP_K):
        hit = e_iota == eidx_ref[k:k + 1, :]
        base = jnp.sum(jnp.where(hit, ps, 0.0), axis=0, keepdims=True)
        rows.append(base.astype(I32) + rank_ref[k:k + 1, :])
    rows.append(jnp.zeros((SUBLANES - TOP_K, tn), I32))
    pos = jnp.concatenate(rows, axis=0)
    for pos_ref, c in zip(pos_refs, chunks):
        for q in range(tn // c):
            pos_ref[q] = pos[:, q * c:(q + 1) * c]
    gw = gw_ref[...]
    rep = jnp.concatenate([jnp.broadcast_to(gw[k:k + 1], (SC_LANES, tn)) for k in range(SUBLANES)], axis=0)
    w_ref[...] = rep.T


def _layouts(pstart, eidx, rank, gw, chunks):
    n = eidx.shape[1]
    tn = min(n, POS_TILE)
    slot = pl.BlockSpec((SUBLANES, tn), lambda i: (0, i))
    return pl.pallas_call(
        functools.partial(_layout_kernel, chunks=chunks),
        grid=(n // tn,),
        in_specs=[pl.BlockSpec(pstart.shape, lambda i: (0, 0)), slot, slot, slot],
        out_specs=[pl.BlockSpec((tn // c, SUBLANES, c), lambda i: (i, 0, 0)) for c in chunks]
        + [pl.BlockSpec((tn, SUBLANES * SC_LANES), lambda i: (i, 0))],
        out_shape=[jax.ShapeDtypeStruct((n // c, SUBLANES, c), I32) for c in chunks]
        + [jax.ShapeDtypeStruct((n, SUBLANES * SC_LANES), F32)],
        compiler_params=pltpu.CompilerParams(dimension_semantics=("arbitrary",), vmem_limit_bytes=VMEM_LIMIT),
        name="moe_layout",
    )(pstart, eidx, rank, gw)


def _sc_worker_id():
    return lax.axis_index("s") * SC_CORES + lax.axis_index("c")


def _sc_mesh():
    return plsc.VectorSubcoreMesh(core_axis_name="c", subcore_axis_name="s")


def _sc_dispatch(seg_a, seg_b, total_rows):
    (h_a, pos_a, c_a), (h_b, pos_b, c_b) = seg_a, seg_b
    width = h_a.shape[1]

    @functools.partial(
        pl.kernel, mesh=_sc_mesh(), out_type=jax.ShapeDtypeStruct((total_rows, width), I32),
        scratch_types=[pltpu.VMEM((2, SUBLANES, c_a), I32), pltpu.VMEM((2, c_a, width), I32),
                       pltpu.VMEM((2, SUBLANES, c_b), I32), pltpu.VMEM((2, c_b, width), I32),
                       pltpu.SemaphoreType.DMA((2,)), pltpu.SemaphoreType.DMA((2,))])
    def run(ha_hbm, pa_hbm, hb_hbm, pb_hbm, xs_hbm, idx_a, rows_a, idx_b, rows_b, in_sems, out_sems):
        wid = _sc_worker_id()

        def segment(h_hbm, p_hbm, idx_v, rows_v, c):
            nch = h_hbm.shape[0] // (SC_WORKERS * c)
            assert nch == 1 or nch % 2 == 0
            chunk0 = wid * nch

            def loads(ci, b):
                return [pltpu.make_async_copy(p_hbm.at[chunk0 + ci], idx_v.at[b], in_sems.at[b]),
                        pltpu.make_async_copy(h_hbm.at[pl.ds((chunk0 + ci) * c, c)], rows_v.at[b], in_sems.at[b])]

            def scatters(b):
                return [pltpu.make_async_copy(rows_v.at[b], xs_hbm.at[idx_v.at[b].at[k]], out_sems.at[b])
                        for k in range(TOP_K)]

            def start(copies):
                for cp in copies:
                    cp.start()

            def wait(copies):
                for cp in copies:
                    cp.wait()

            start(loads(0, 0))
            if nch == 1:
                wait(loads(0, 0))
                start(scatters(0))
                wait(scatters(0))
                return

            @pl.loop(0, nch, step=2)
            def _(ci):
                for b in range(2):
                    wait(loads(ci + b, b))
                    start(scatters(b))

                    @pl.when(ci + b + 1 < nch)
                    def _():
                        @pl.when(ci + b >= 1)
                        def _():
                            wait(scatters(1 - b))

                        start(loads(ci + b + 1, 1 - b))

            wait(scatters(0))
            wait(scatters(1))

        segment(ha_hbm, pa_hbm, idx_a, rows_a, c_a)
        segment(hb_hbm, pb_hbm, idx_b, rows_b, c_b)

    return run(h_a, pos_a, h_b, pos_b)


def _experts_kernel(sb_ref, nb_ref, cnt_ref, nu_ref, xs_hbm, wg_ref, wu_ref, wd_ref, y_hbm,
                    xbuf, ybuf, wgb, wub, wdb, in_sems, out_sems):
    e = pl.program_id(0)
    ring, bm, half = xbuf.shape
    n_used = nu_ref[0]

    def slot_of(g):
        return g & (ring - 1)

    def in_copy(g):
        s = slot_of(g)
        return pltpu.make_async_copy(xs_hbm.at[pl.ds(pl.multiple_of(g * bm, bm), bm), :], xbuf.at[s], in_sems.at[s])

    def out_copy(g):
        s = slot_of(g)
        return pltpu.make_async_copy(ybuf.at[s], y_hbm.at[pl.ds(pl.multiple_of(g * bm, bm), bm), :], out_sems.at[s])

    @pl.when(e == 0)
    def _():
        for g in range(ring):
            @pl.when(g < n_used)
            def _():
                in_copy(g).start()

    wgb[...] = wg_ref[...].astype(BF16)
    wub[...] = wu_ref[...].astype(BF16)
    wdb[...] = wd_ref[...].astype(BF16)

    def blocks(b, k):
        g = sb_ref[e] + b
        for i in range(k):
            in_copy(g + i).wait()

            @pl.when(g + i >= ring)
            def _():
                out_copy(g + i - ring).wait()

        s = pl.multiple_of(slot_of(g), k)
        row = lax.broadcasted_iota(I32, (k * bm, half), 0)
        packed = xbuf[pl.ds(s, k)].reshape(k * bm, half)
        x_lo, x_hi = _unpack_bf16_pairs(jnp.where(row < cnt_ref[e] - b * bm, packed, 0))
        a = (jnp.dot(x_lo, wgb[:half], preferred_element_type=F32)
             + jnp.dot(x_hi, wgb[half:], preferred_element_type=F32))
        u = (jnp.dot(x_lo, wub[:half], preferred_element_type=F32)
             + jnp.dot(x_hi, wub[half:], preferred_element_type=F32))
        y = _pack_bf16_pairs(jnp.dot((_silu(a) * u).astype(BF16), wdb[...], preferred_element_type=F32))
        ybuf[pl.ds(s, k)] = y.reshape(k, bm, half)
        for i in range(k):
            out_copy(g + i).start()

            @pl.when(g + i + ring < n_used)
            def _():
                in_copy(g + i + ring).start()

    def single(b):
        def body(_, carry):
            blocks(b, 1)
            return carry
        return body

    n_blk = nb_ref[e]
    n_first = jnp.minimum(sb_ref[e] & 1, n_blk)
    n_pairs = lax.shift_right_logical(n_blk - n_first, 1)
    n_last = (n_blk - n_first) & 1
    lax.fori_loop(0, n_first, single(0), 0)

    def pair(i, carry):
        blocks(n_first + 2 * i, 2)
        return carry

    lax.fori_loop(0, n_pairs, pair, 0)
    lax.fori_loop(0, n_last, single(n_first + 2 * n_pairs), 0)

    @pl.when(e == pl.num_programs(0) - 1)
    def _():
        for r in range(ring):
            @pl.when(n_used - 1 - r >= 0)
            def _():
                out_copy(n_used - 1 - r).wait()


def _experts(first_blk, n_blk, counts, n_used, xs, wg, wu, wd):
    rows = xs.shape[0]
    bm, ring = EXPERT_BLOCK, EXPERT_RING
    assert ring & (ring - 1) == 0
    w_blk = lambda e, *_: (e, 0, 0)
    grid_spec = pltpu.PrefetchScalarGridSpec(
        num_scalar_prefetch=4,
        grid=(N_EXPERTS,),
        in_specs=[pl.BlockSpec(memory_space=pl.ANY),
                  pl.BlockSpec((None, D_MODEL, D_EXPERT), w_blk),
                  pl.BlockSpec((None, D_MODEL, D_EXPERT), w_blk),
                  pl.BlockSpec((None, D_EXPERT, D_MODEL), w_blk)],
        out_specs=pl.BlockSpec(memory_space=pl.ANY),
        scratch_shapes=[pltpu.VMEM((ring, bm, D_MODEL // 2), I32), pltpu.VMEM((ring, bm, D_MODEL // 2), I32),
                        pltpu.VMEM((D_MODEL, D_EXPERT), BF16), pltpu.VMEM((D_MODEL, D_EXPERT), BF16),
                        pltpu.VMEM((D_EXPERT, D_MODEL), BF16),
                        pltpu.SemaphoreType.DMA((ring,)), pltpu.SemaphoreType.DMA((ring,))],
    )
    return pl.pallas_call(
        _experts_kernel,
        grid_spec=grid_spec,
        out_shape=jax.ShapeDtypeStruct((rows, D_MODEL // 2), I32),
        compiler_params=pltpu.CompilerParams(dimension_semantics=("arbitrary",), vmem_limit_bytes=VMEM_LIMIT),
        name="moe_experts",
    )(first_blk, n_blk, counts, n_used, xs, wg, wu, wd)


def _sc_combine(y, segs):
    c = COMBINE_CHUNK
    half = y.shape[1]
    d = 2 * half
    ns = len(segs)
    params = dataclasses.replace(pltpu.CompilerParams(), needs_layout_passes=False)

    @functools.partial(
        pl.kernel, mesh=_sc_mesh(), compiler_params=params,
        out_type=[jax.ShapeDtypeStruct((w.shape[0], d), F32) for _, w in segs],
        scratch_types=[pltpu.VMEM(pos.shape[1:], I32) for pos, _ in segs]
        + [pltpu.VMEM((2, c, SUBLANES * SC_LANES), F32), pltpu.VMEM((2, TOP_K, c, half), I32),
           pltpu.VMEM((c, d), F32), pltpu.SemaphoreType.DMA((2,)), pltpu.SemaphoreType.DMA])
    def run(y_hbm, *refs):
        ins, outs, idxs = refs[:2 * ns], refs[2 * ns:3 * ns], refs[3 * ns:4 * ns]
        w_v, buf_v, out_v, sems, out_sem = refs[4 * ns:]
        wid = _sc_worker_id()

        def segment(p_hbm, w_hbm, o_hbm, idx_v):
            per_w = idx_v.shape[1]
            nch = per_w // c
            base = wid * per_w
            pltpu.sync_copy(p_hbm.at[wid], idx_v)

            def in_copies(ci, b):
                w_copy = pltpu.make_async_copy(w_hbm.at[pl.ds(base + ci * c, c)], w_v.at[b], sems.at[b])
                return [w_copy] + [
                    pltpu.make_async_copy(y_hbm.at[idx_v.at[k, pl.ds(ci * c, c)]], buf_v.at[b].at[k], sems.at[b])
                    for k in range(TOP_K)]

            def out_copy(ci):
                return pltpu.make_async_copy(out_v, o_hbm.at[pl.ds(base + ci * c, c)], out_sem)

            def reduce_rows(ci, b):
                @pl.when(ci >= 1)
                def _():
                    out_copy(ci - 1).wait()

                for t0 in range(0, c, COMBINE_TOKEN_GROUP):
                    group = range(t0, t0 + COMBINE_TOKEN_GROUP)
                    ws = [[w_v[b, t, pl.ds(k * SC_LANES, SC_LANES)] for k in range(TOP_K)] for t in group]

                    @plsc.parallel_loop(0, half // SC_LANES, unroll=2)
                    def _(j):
                        lanes = pl.ds(pl.multiple_of(j * SC_LANES, SC_LANES), SC_LANES)
                        hi_lanes = pl.ds(pl.multiple_of(half + j * SC_LANES, SC_LANES), SC_LANES)
                        for t, wt in zip(group, ws):
                            acc_lo = acc_hi = None
                            for k in range(TOP_K):
                                word = buf_v[b, k, t, lanes]
                                lo = plsc.bitcast(lax.shift_left(word, jnp.int32(16)), F32) * wt[k]
                                hi = plsc.bitcast(word & jnp.int32(-65536), F32) * wt[k]
                                acc_lo = lo if acc_lo is None else acc_lo + lo
                                acc_hi = hi if acc_hi is None else acc_hi + hi
                            out_v[t, lanes] = acc_lo
                            out_v[t, hi_lanes] = acc_hi

                out_copy(ci).start()

            for cp in in_copies(0, 0):
                cp.start()

            @pl.loop(0, nch, step=2)
            def _(ci):
                for b in range(2):
                    @pl.when(ci + b + 1 < nch)
                    def _():
                        for cp in in_copies(ci + b + 1, 1 - b):
                            cp.start()

                    for cp in in_copies(ci + b, b):
                        cp.wait()
                    reduce_rows(ci + b, b)

            out_copy(nch - 1).wait()

        for i in range(ns):
            segment(ins[2 * i], ins[2 * i + 1], outs[i], idxs[i])

    res = run(y, *[a for seg in segs for a in seg])
    return list(res) if isinstance(res, (list, tuple)) else [res]


def _shared_kernel(x1_ref, h2_ref, g2_ref, wsg_ref, wsu_ref, wsd_ref, o_ref):
    half = h2_ref.shape[1]
    x_lo, x_hi = _unpack_bf16_pairs(h2_ref[...])
    wsg, wsu = wsg_ref[...], wsu_ref[...]
    a = (jnp.dot(x_lo, wsg[:half], preferred_element_type=F32) + jnp.dot(x_hi, wsg[half:], preferred_element_type=F32))
    b = (jnp.dot(x_lo, wsu[:half], preferred_element_type=F32) + jnp.dot(x_hi, wsu[half:], preferred_element_type=F32))
    shared = jnp.dot((_silu(a) * b).astype(BF16), wsd_ref[...], preferred_element_type=F32)
    o_ref[...] = x1_ref[...] + g2_ref[...] * shared


def _shared(x1, h2, g2, wsg, wsu, wsd, tm, mod_spec):
    n = h2.shape[0]
    full = lambda arr: pl.BlockSpec(arr.shape, lambda i: (0,) * arr.ndim)
    tile = pl.BlockSpec((tm, D_MODEL), lambda i: (i, 0))
    return pl.pallas_call(
        _shared_kernel,
        grid=(n // tm,),
        in_specs=[tile, pl.BlockSpec((tm, D_MODEL // 2), lambda i: (i, 0)), mod_spec, full(wsg), full(wsu), full(wsd)],
        out_specs=tile,
        out_shape=jax.ShapeDtypeStruct((n, D_MODEL), F32),
        compiler_params=pltpu.CompilerParams(dimension_semantics=("arbitrary",), vmem_limit_bytes=VMEM_LIMIT),
        name="shared_expert",
    )(x1, h2, g2, wsg, wsu, wsd)


def _final_kernel(base_ref, routed_ref, g2_ref, *rest):
    o_ref = rest[-1]
    o_ref[...] = base_ref[...] + g2_ref[...] * routed_ref[...]


def _final(base, routed, g2, tm, mod_spec, tile0=0, out=None):
    n = base.shape[0]
    here = pl.BlockSpec((tm, D_MODEL), lambda i: (i + tile0, 0))
    in_specs = [here, pl.BlockSpec((tm, D_MODEL), lambda i: (i, 0)), mod_spec]
    args = [base, routed, g2]
    if out is not None:
        in_specs.append(pl.BlockSpec(memory_space=pl.ANY))
        args.append(out)
    return pl.pallas_call(
        _final_kernel,
        grid=(routed.shape[0] // tm,),
        in_specs=in_specs,
        out_specs=here,
        out_shape=jax.ShapeDtypeStruct((n, D_MODEL), F32),
        input_output_aliases={} if out is None else {3: 0},
        compiler_params=pltpu.CompilerParams(dimension_semantics=("arbitrary",), vmem_limit_bytes=VMEM_LIMIT),
        name="ffn_residual",
    )(*args)


def _block_diag(w):
    nb, bi, bj = w.shape
    return jnp.einsum('nij,nm->nimj', w, jnp.eye(nb, dtype=w.dtype)).reshape(nb * bi, nb * bj)


def kernel(x_prompt, x_sample, c_prompt, c_sample, cache_k_win, cache_v_win, state_conv, state_rnn, ada_w, ada_b, norm_mix, w_in, conv_w, conv_b, gate_a_w, gate_a_b, gate_x_w, gate_x_b, lru_lambda, q_norm, k_norm, attn_sinks, w_out, norm_ffn, router_w, router_bias, exp_w_gate, exp_w_up, exp_w_down, sh_w_gate, sh_w_up, sh_w_down):
    bp, tp, _ = x_prompt.shape
    bs, ts, _ = x_sample.shape
    win = cache_k_win.shape[2]
    n_p, n_s = bp * tp, bs * ts
    row = lambda v: v.reshape(1, -1)

    g_mix, g_ffn = row(norm_mix[0]), row(norm_ffn[0])
    win_bf = w_in[0].astype(BF16)
    q0 = 2 * D_RNN
    win_pair = jnp.concatenate([win_bf[:, :q0], _pair_heads(win_bf[:, q0:q0 + D_ATT], 1), win_bf[:, q0 + D_ATT:]],
                               axis=1)
    cw, cb = conv_w[0], row(conv_b[0])
    wg = jnp.concatenate([_block_diag(gate_a_w[0]), _block_diag(gate_x_w[0])], axis=1).astype(BF16)
    gb = row(jnp.concatenate([gate_a_b[0], gate_x_b[0]]))
    lam = row(lru_lambda[0])
    gq = row(q_norm[0])
    gk2 = row(jnp.tile(k_norm[0], N_KV_HEADS))
    sinks = attn_sinks[0]
    wo_top, wo_bot = w_out[0, :D_RNN].astype(BF16), w_out[0, D_RNN:].astype(BF16)
    wr_t = router_w[0].T
    rb = router_bias[0].reshape(N_EXPERTS, 1)
    wsg, wsu, wsd = sh_w_gate[0].astype(BF16), sh_w_up[0].astype(BF16), sh_w_down[0].astype(BF16)

    mod = _adaln(jnp.concatenate([c_prompt, c_sample], axis=0), ada_w[0], ada_b[0])
    chunks = [mod[:, i * D_MODEL:(i + 1) * D_MODEL] for i in range(6)]
    sh1p, sc1p, g1p, sh2p, sc2p, g2p = [c[:bp].reshape(bp, 1, D_MODEL) for c in chunks]
    sh1s, sc1s, g1s, sh2s, sc2s, g2s = [c[bp:] for c in chunks]

    conv0 = jnp.zeros((bp, SUBLANES, D_RNN), F32)
    h0 = jnp.zeros((bp, 1, D_RNN), F32)
    r_p, q_p, k_p, v_p, cs_p, hs_p = _front_prompt(x_prompt, sc1p, sh1p, g_mix, win_pair, cw, cb, wg, gb, lam, gk2,
                                                   conv0, h0)
    a_p = _attn_prompt(q_p, k_p, v_p, row(jnp.tile(q_norm[0], 2)), sinks)

    x_s_t = jnp.swapaxes(x_sample, 0, 1)
    r_s, q_s, k_s, v_s, cs_s, hs_s = _front_sample(x_s_t, sc1s, sh1s, g_mix, win_bf, cw, cb, wg, gb, lam, gk2,
                                                   jnp.swapaxes(state_conv[0], 0, 1), state_rnn[0])
    cache_k = cache_k_win[0].reshape(bs, win, D_KV)
    cache_v = cache_v_win[0].reshape(bs, win, D_KV)
    a_s = _attn_sample(q_s, k_s, v_s, cache_k, cache_v, gq, sinks)

    tiles_per_seq = tp // TM_POST
    mod_p = pl.BlockSpec((None, 1, D_MODEL), lambda i, *_: (i // tiles_per_seq, 0, 0))
    mod_s = pl.BlockSpec((bs, D_MODEL), lambda i, *_: (0, 0))
    cnt0 = jnp.zeros((N_EXPERTS, LANES), F32)
    x1_p, h2_p, eidx_p, gw_p, rank_p, cnt_p = _post(
        x_prompt.reshape(n_p, D_MODEL), r_p.reshape(n_p, D_RNN), a_p.reshape(n_p, D_ATT), g1p, sc2p, sh2p, g_ffn,
        wo_top, _pair_heads(wo_bot, 0), wr_t, rb, cnt0, TM_POST, mod_p)
    x1_s, h2_s, eidx_s, gw_s, rank_s, cnt_all = _post(
        x_s_t.reshape(n_s, D_MODEL), r_s.reshape(n_s, D_RNN), a_s.reshape(n_s, D_ATT), g1s, sc2s, sh2s, g_ffn,
        wo_top, wo_bot, wr_t, rb, cnt_p, bs, mod_s)

    bm = EXPERT_BLOCK
    n_blocks = -(-((n_p + n_s) * TOP_K) // bm) + N_EXPERTS
    pstart, used = _plan(cnt_all, bm)
    counts = cnt_all[:, 0].astype(I32)
    first_blk = (pstart[:, 0] * (1.0 / bm)).astype(I32)
    n_blk = (counts + (bm - 1)) // bm
    n_used = used[0, :1]

    def sc_layouts(eidx, rank, gw, n, parts):
        per_w = n // (SC_WORKERS * parts)
        c = min(DISPATCH_CHUNK, per_w)
        outs = _layouts(pstart, eidx, rank, gw, (c,) if c == per_w else (c, per_w))
        return (outs[0], c), (outs[-2], outs[-1])

    (dpos_p, c_p), (cpos_p, cw_p) = sc_layouts(eidx_p, rank_p, gw_p, n_p, 2)
    (dpos_s, c_s), comb_s = sc_layouts(eidx_s, rank_s, gw_s, n_s, 1)
    xs = _sc_dispatch((h2_p, dpos_p, c_p), (h2_s, dpos_s, c_s), n_blocks * bm)
    base_p = _shared(x1_p, h2_p, g2p, wsg, wsu, wsd, TM_POST, mod_p)
    base_s = _shared(x1_s, h2_s, g2s, wsg, wsu, wsd, bs, mod_s)
    y = _experts(first_blk, n_blk, counts, n_used, xs, exp_w_gate[0], exp_w_up[0], exp_w_down[0])
    n_h = n_p // 2
    routed_a, routed_s = _sc_combine(y, [(cpos_p[:SC_WORKERS], cw_p[:n_h]), comb_s])
    routed_b, = _sc_combine(y, [(cpos_p[SC_WORKERS:], cw_p[n_h:])])
    tiles_h = n_h // TM_POST
    mod_p2 = pl.BlockSpec((None, 1, D_MODEL), lambda i, *_: ((i + tiles_h) // tiles_per_seq, 0, 0))
    y_p = _final(base_p, routed_a, g2p, TM_POST, mod_p)
    y_p = _final(base_p, routed_b, g2p, TM_POST, mod_p2, tile0=tiles_h, out=y_p)
    y_s = _final(base_s, routed_s, g2s, bs, mod_s)

    y_prompt = y_p.reshape(bp, tp, D_MODEL)
    y_sample = jnp.swapaxes(y_s.reshape(ts, bs, D_MODEL), 0, 1)
    wk = min(WINDOW, tp)
    k_win_p = k_p[:, tp - wk:].reshape(1, bp, wk, N_KV_HEADS, HEAD_DIM)
    v_win_p = v_p[:, tp - wk:].reshape(1, bp, wk, N_KV_HEADS, HEAD_DIM)
    k_new = jnp.swapaxes(k_s, 0, 1)
    v_new = jnp.swapaxes(v_s, 0, 1)
    heads = lambda a: a.reshape(bs, ts, N_KV_HEADS, HEAD_DIM)
    k_win_s = jnp.concatenate([cache_k_win[0], heads(k_new)], axis=1)[None, :, ts:]
    v_win_s = jnp.concatenate([cache_v_win[0], heads(v_new)], axis=1)[None, :, ts:]
    return (y_prompt, y_sample, k_win_p, v_win_p, cs_p[None, :, SUBLANES - (CONV_W - 1):], hs_p.reshape(1, bp, D_RNN),
            k_win_s, v_win_s, jnp.swapaxes(cs_s, 0, 1)[None], hs_s[None])
```

```python
import dataclasses
import functools

import jax
import jax.numpy as jnp
from jax import lax
from jax.experimental import pallas as pl
from jax.experimental.pallas import tpu as pltpu
from jax.experimental.pallas import tpu_sc as plsc

F32 = jnp.float32
BF16 = jnp.bfloat16
I32 = jnp.int32

D_MODEL = 1024
D_RNN = 512
N_RNN_BLOCKS = 8
CONV_W = 4
LRU_C = 8.0
HEAD_DIM = 64
N_HEADS = 8
N_KV_HEADS = 2
GQA = N_HEADS // N_KV_HEADS
D_ATT = N_HEADS * HEAD_DIM
D_KV = N_KV_HEADS * HEAD_DIM
WINDOW = 128
N_EXPERTS = 64
TOP_K = 6
N_GROUPS = 8
GROUP_SIZE = N_EXPERTS // N_GROUPS
TOPK_GROUPS = 4
D_EXPERT = 256
D_SHARED = 256
ROUTE_SCALE = 2.5
EPS = 1e-6
NEG = -1e30
F32_TINY = 1.1754944e-38
D_IN = 2 * D_RNN + D_ATT + 2 * D_KV

SUBLANES = 8
LANES = 128
TM_PROMPT = 512
TM_POST = 512
ATT_BLOCK = WINDOW
ATT_STEP_BLOCKS = 8
EXPERT_BLOCK = 512
EXPERT_RING = 4
SAMPLE_CHUNK = 8
POS_TILE = 2048
VMEM_LIMIT = 48 * 1024 * 1024

SC_CORES = 2
SC_SUBCORES = 16
SC_WORKERS = SC_CORES * SC_SUBCORES
SC_LANES = 16
DISPATCH_CHUNK = 64
COMBINE_CHUNK = 8
COMBINE_TOKEN_GROUP = 4


def _sigmoid(x):
    return 0.5 * jnp.tanh(0.5 * x) + 0.5


def _silu(x):
    return x * _sigmoid(x)


def _gelu_tanh(x):
    c = 0.7978845608028654
    return x * (0.5 * (1.0 + jnp.tanh(c * (x + 0.044715 * (x * x * x)))))


def _log1p(x):
    u = 1.0 + x
    return jnp.where(u == 1.0, x, jnp.log(u) * x / jnp.where(u == 1.0, 1.0, u - 1.0))


def _neg_expm1_2x(x, exp_x):
    return -jnp.tanh(x) * (exp_x * exp_x + 1.0)


def _softplus(z):
    return jnp.maximum(z, 0.0) + _log1p(jnp.exp(-jnp.abs(z)))


def _div_pow2(x, d):
    assert d & (d - 1) == 0
    return lax.shift_right_logical(x, d.bit_length() - 1)


def _mod_pow2(x, d):
    assert d & (d - 1) == 0
    return x & (d - 1)


def _norm_mod(x, g, sc, sh):
    ms = jnp.mean(x * x, axis=-1, keepdims=True)
    return (x * lax.rsqrt(ms + EPS)) * g * (1.0 + sc) + sh


def _bdot(a, b):
    return jnp.dot(a.astype(BF16), b.astype(BF16), preferred_element_type=F32)


def _bdot_nt(a, b):
    return lax.dot_general(a.astype(BF16), b.astype(BF16), (((1,), (1,)), ((), ())),
                           preferred_element_type=F32)


def _pack_bf16_pairs(x):
    w = x.shape[1] // 2

    def bf16_bits(v):
        return pltpu.bitcast(v.astype(BF16).astype(F32), I32)

    return lax.shift_right_logical(bf16_bits(x[:, :w]), 16) | (bf16_bits(x[:, w:]) & jnp.int32(-65536))


def _unpack_bf16_pairs(p):
    lo = pltpu.bitcast(lax.shift_left(p, 16), F32)
    hi = pltpu.bitcast(p & jnp.int32(-65536), F32)
    return lo.astype(BF16), hi.astype(BF16)


def _knorm(k, gk2):
    lane = lax.broadcasted_iota(I32, k.shape, 1)
    first = lane < HEAD_DIM
    k2 = k * k
    s0 = jnp.sum(jnp.where(first, k2, 0.0), axis=-1, keepdims=True)
    s1 = jnp.sum(jnp.where(first, 0.0, k2), axis=-1, keepdims=True)
    ms = jnp.where(first, s0, s1) * (1.0 / HEAD_DIM)
    return k * lax.rsqrt(ms + EPS) * gk2


def _pair_heads(w, axis):
    shape = w.shape
    split = shape[:axis] + (N_KV_HEADS, GQA, HEAD_DIM) + shape[axis + 1:]
    return jnp.swapaxes(w.reshape(split), axis, axis + 1).reshape(shape)


def _lru_coeffs(u, wg, gb, lam):
    g = _bdot(u, wg) + gb
    r = _sigmoid(g[:, :D_RNN])
    i = _sigmoid(g[:, D_RNN:])
    log_a = (-LRU_C * r) * _softplus(-lam)
    a = jnp.exp(log_a)
    om = _neg_expm1_2x(log_a, a)
    b = (om * lax.rsqrt(jnp.maximum(om, F32_TINY))) * (i * u)
    return a, b


def _adaln_kernel(c_ref, w_ref, b_ref, o_ref):
    o_ref[...] = _bdot(_silu(c_ref[...]), w_ref[...]) + b_ref[...]


def _adaln(c_all, ada_w, ada_b):
    n = c_all.shape[0]
    return pl.pallas_call(
        _adaln_kernel,
        grid=(6,),
        in_specs=[pl.BlockSpec((n, D_MODEL), lambda j: (0, 0)),
                  pl.BlockSpec((D_MODEL, D_MODEL), lambda j: (0, j)),
                  pl.BlockSpec((1, D_MODEL), lambda j: (0, j))],
        out_specs=pl.BlockSpec((n, D_MODEL), lambda j: (0, j)),
        out_shape=jax.ShapeDtypeStruct((n, 6 * D_MODEL), F32),
        compiler_params=pltpu.CompilerParams(dimension_semantics=("arbitrary",), vmem_limit_bytes=VMEM_LIMIT),
        name="adaln",
    )(c_all, ada_w, ada_b.reshape(1, -1))


def _scan_rows(a, b, h_in):
    n, c = a.shape
    groups = n // SUBLANES
    a = a.reshape(groups, SUBLANES, c)
    b = b.reshape(groups, SUBLANES, c)
    sub = lax.broadcasted_iota(I32, a.shape, 1)
    s = 1
    while s < SUBLANES:
        m = sub >= s
        a_sh = jnp.where(m, pltpu.roll(a, s, 1), 1.0)
        b_sh = jnp.where(m, pltpu.roll(b, s, 1), 0.0)
        b = a * b_sh + b
        a = a * a_sh
        s *= 2
    carry = h_in
    hs = []
    for g in range(groups):
        hg = a[g] * carry + b[g]
        hs.append(hg)
        carry = hg[SUBLANES - 1:SUBLANES]
    return jnp.concatenate(hs, axis=0)


def _front_prompt_kernel(x_ref, sc_ref, sh_ref, g_ref, win_ref, cw_ref, cb_ref, wg_ref, gb_ref, lam_ref, gk_ref,
                         prev_ref, h0_ref, r_ref, q_ref, k_ref, v_ref, cs_ref, hs_ref, tail_ref, hc_ref):
    j = pl.program_id(1)
    tm = x_ref.shape[0]

    @pl.when(j == 0)
    def _():
        tail_ref[...] = prev_ref[...]
        hc_ref[...] = h0_ref[...]

    h = _norm_mod(x_ref[...], g_ref[...], sc_ref[...], sh_ref[...])
    proj = jnp.dot(h.astype(BF16), win_ref[...], preferred_element_type=F32)
    xr = proj[:, 0:D_RNN]
    yr = proj[:, D_RNN:2 * D_RNN]
    q_ref[...] = proj[:, 2 * D_RNN:2 * D_RNN + D_ATT]
    k_ref[...] = _knorm(proj[:, 2 * D_RNN + D_ATT:2 * D_RNN + D_ATT + D_KV], gk_ref[...])
    v_ref[...] = proj[:, 2 * D_RNN + D_ATT + D_KV:D_IN]

    tail = tail_ref[...]
    row8 = lax.broadcasted_iota(I32, tail.shape, 0)

    def shifted(s):
        rolled = pltpu.roll(xr, s, 0)
        top = jnp.where(row8 < s, pltpu.roll(tail, s, 0), rolled[0:SUBLANES])
        return jnp.concatenate([top, rolled[SUBLANES:]], axis=0)

    cw = cw_ref[...]
    u = cb_ref[...] + shifted(3) * cw[0:1]
    u = u + shifted(2) * cw[1:2]
    u = u + shifted(1) * cw[2:3]
    u = u + xr * cw[3:4]
    tail_ref[...] = xr[tm - SUBLANES:tm]

    a, b = _lru_coeffs(u, wg_ref[...], gb_ref[...], lam_ref[...])
    hs = _scan_rows(a, b, hc_ref[...])
    hc_ref[...] = hs[tm - 1:tm]
    r_ref[...] = hs * _gelu_tanh(yr)

    @pl.when(j == pl.num_programs(1) - 1)
    def _():
        cs_ref[...] = xr[tm - SUBLANES:tm]
        hs_ref[...] = hs[tm - 1:tm]


def _front_prompt(x, sc, sh, g, win, cw, cb, wg, gb, lam, gk, prev, h0):
    bsz, t, _ = x.shape
    tm = TM_PROMPT
    full = lambda a: pl.BlockSpec(a.shape, lambda b, j: (0,) * a.ndim)
    per_b = lambda a: pl.BlockSpec((None,) + a.shape[1:], lambda b, j: (b,) + (0,) * (a.ndim - 1))
    tile = lambda w: pl.BlockSpec((None, tm, w), lambda b, j: (b, j, 0))
    return pl.pallas_call(
        _front_prompt_kernel,
        grid=(bsz, t // tm),
        in_specs=[tile(D_MODEL), per_b(sc), per_b(sh), full(g), full(win), full(cw), full(cb), full(wg), full(gb),
                  full(lam), full(gk), per_b(prev), per_b(h0)],
        out_specs=[tile(D_RNN), tile(D_ATT), tile(D_KV), tile(D_KV),
                   pl.BlockSpec((None, SUBLANES, D_RNN), lambda b, j: (b, 0, 0)),
                   pl.BlockSpec((None, 1, D_RNN), lambda b, j: (b, 0, 0))],
        out_shape=[jax.ShapeDtypeStruct((bsz, t, D_RNN), F32), jax.ShapeDtypeStruct((bsz, t, D_ATT), F32),
                   jax.ShapeDtypeStruct((bsz, t, D_KV), F32), jax.ShapeDtypeStruct((bsz, t, D_KV), F32),
                   jax.ShapeDtypeStruct((bsz, SUBLANES, D_RNN), F32), jax.ShapeDtypeStruct((bsz, 1, D_RNN), F32)],
        scratch_shapes=[pltpu.VMEM((SUBLANES, D_RNN), F32), pltpu.VMEM((1, D_RNN), F32)],
        compiler_params=pltpu.CompilerParams(dimension_semantics=("arbitrary", "arbitrary"),
                                             vmem_limit_bytes=VMEM_LIMIT),
        name="front_prompt",
    )(x, sc, sh, g, win, cw, cb, wg, gb, lam, gk, prev, h0)


def _front_sample_kernel(x_ref, sc_ref, sh_ref, g_ref, win_ref, cw_ref, cb_ref, wg_ref, gb_ref, lam_ref, gk_ref,
                         prev_ref, h0_ref, r_ref, q_ref, k_ref, v_ref, cs_ref, hs_ref):
    t_len, bsz, _ = x_ref.shape
    x = x_ref[...]
    ms = jnp.mean(x * x, axis=-1, keepdims=True)
    h = (x * lax.rsqrt(ms + EPS)) * g_ref[...] * (1.0 + sc_ref[...]) + sh_ref[...]
    proj = jnp.dot(h.reshape(t_len * bsz, D_MODEL).astype(BF16), win_ref[...], preferred_element_type=F32)
    xr = proj[:, 0:D_RNN]
    yr = proj[:, D_RNN:2 * D_RNN]
    q_ref[...] = proj[:, 2 * D_RNN:2 * D_RNN + D_ATT].reshape(t_len, bsz, D_ATT)
    k_ref[...] = _knorm(proj[:, 2 * D_RNN + D_ATT:2 * D_RNN + D_ATT + D_KV], gk_ref[...]).reshape(t_len, bsz, D_KV)
    v_ref[...] = proj[:, 2 * D_RNN + D_ATT + D_KV:D_IN].reshape(t_len, bsz, D_KV)

    def at_time(t):
        if t >= 0:
            return xr[t * bsz:(t + 1) * bsz]
        return prev_ref[CONV_W - 1 + t]

    cw = cw_ref[...]
    us = []
    for t in range(t_len):
        u = cb_ref[...] + at_time(t - 3) * cw[0:1]
        u = u + at_time(t - 2) * cw[1:2]
        u = u + at_time(t - 1) * cw[2:3]
        u = u + at_time(t) * cw[3:4]
        us.append(u)
    a, b = _lru_coeffs(jnp.concatenate(us, axis=0), wg_ref[...], gb_ref[...], lam_ref[...])
    hcur = h0_ref[...]
    for t in range(t_len):
        hcur = a[t * bsz:(t + 1) * bsz] * hcur + b[t * bsz:(t + 1) * bsz]
        r_ref[t] = hcur * _gelu_tanh(yr[t * bsz:(t + 1) * bsz])
    hs_ref[...] = hcur
    for s in range(CONV_W - 1):
        cs_ref[s] = at_time(t_len - (CONV_W - 1) + s)


def _front_sample(x_t, sc, sh, g, win, cw, cb, wg, gb, lam, gk, prev_t, h0):
    t_len, bsz, _ = x_t.shape
    return pl.pallas_call(
        _front_sample_kernel,
        out_shape=[jax.ShapeDtypeStruct((t_len, bsz, D_RNN), F32), jax.ShapeDtypeStruct((t_len, bsz, D_ATT), F32),
                   jax.ShapeDtypeStruct((t_len, bsz, D_KV), F32), jax.ShapeDtypeStruct((t_len, bsz, D_KV), F32),
                   jax.ShapeDtypeStruct((CONV_W - 1, bsz, D_RNN), F32), jax.ShapeDtypeStruct((bsz, D_RNN), F32)],
        compiler_params=pltpu.CompilerParams(vmem_limit_bytes=VMEM_LIMIT),
        name="front_sample",
    )(x_t, sc, sh, g, win, cw, cb, wg, gb, lam, gk, prev_t, h0)


def _qnorm(q, gq):
    ms = jnp.mean(q * q, axis=-1, keepdims=True)
    return q * lax.rsqrt(ms + EPS) * gq


def _attn_prompt_kernel(sink_ref, q_ref, kp_ref, kc_ref, vp_ref, vc_ref, gq_ref, o_ref):
    j = pl.program_id(1)
    blk = kp_ref.shape[0]
    k_all = jnp.concatenate([kp_ref[...], kc_ref[...]], axis=0)
    v_all = jnp.concatenate([vp_ref[...], vc_ref[...]], axis=0)
    qi = lax.broadcasted_iota(I32, (blk, 2 * blk), 0)
    kj = lax.broadcasted_iota(I32, (blk, 2 * blk), 1)
    dist = blk + qi - kj
    window = (dist >= 0) & (dist <= WINDOW)
    distf = dist.astype(F32)
    slab = 2 * HEAD_DIM
    first_q = lax.broadcasted_iota(I32, (blk, slab), 1) < HEAD_DIM
    first_kv = lax.broadcasted_iota(I32, (2 * blk, slab), 1) < HEAD_DIM

    bias = [jnp.where(window, (-(2.0 ** -(h + 1))) * distf, NEG) for h in range(N_HEADS)]
    no_prev = (kj < blk) & (j == 0)
    gq = gq_ref[...] * (HEAD_DIM ** -0.5)

    def probs(s, h, first_block):
        b = jnp.where(no_prev, NEG, bias[h]) if first_block else bias[h]
        s = s + b
        sink = sink_ref[h]
        m = jnp.maximum(jnp.max(s, axis=-1, keepdims=True), sink)
        p = jnp.exp(s - m)
        return p, jnp.sum(p, axis=-1, keepdims=True) + jnp.exp(sink - m)

    for sub in range(q_ref.shape[0] // blk):
        first = sub == 0
        q = q_ref[sub * blk:(sub + 1) * blk, :]
        kk = k_all[sub * blk:(sub + 2) * blk]
        vv = v_all[sub * blk:(sub + 2) * blk]
        v_a = jnp.where(first_kv, vv, 0.0)
        v_b = jnp.where(first_kv, 0.0, vv)
        outs = []
        for i in range(GQA):
            q2 = _knorm(q[:, i * slab:(i + 1) * slab], gq)
            p_a, l_a = probs(_bdot_nt(jnp.where(first_q, q2, 0.0), kk), i, first)
            p_b, l_b = probs(_bdot_nt(jnp.where(first_q, 0.0, q2), kk), GQA + i, first)
            outs.append((_bdot(p_a, v_a) + _bdot(p_b, v_b)) / jnp.where(first_q, l_a, l_b))
        o_ref[sub * blk:(sub + 1) * blk, :] = jnp.concatenate(outs, axis=1)


def _attn_prompt(q, k, v, gq, sinks):
    bsz, t, _ = q.shape
    blk = ATT_BLOCK
    nsub = ATT_STEP_BLOCKS
    cur = lambda w: pl.BlockSpec((None, nsub * blk, w), lambda b, j: (b, j, 0))
    prv = lambda w: pl.BlockSpec((None, blk, w), lambda b, j: (b, jnp.maximum(nsub * j - 1, 0), 0))
    return pl.pallas_call(
        _attn_prompt_kernel,
        grid=(bsz, t // (nsub * blk)),
        in_specs=[pl.BlockSpec(memory_space=pltpu.SMEM), cur(D_ATT), prv(D_KV), cur(D_KV), prv(D_KV), cur(D_KV),
                  pl.BlockSpec(gq.shape, lambda b, j: (0, 0))],
        out_specs=cur(D_ATT),
        out_shape=jax.ShapeDtypeStruct((bsz, t, D_ATT), F32),
        compiler_params=pltpu.CompilerParams(dimension_semantics=("arbitrary", "arbitrary"),
                                             vmem_limit_bytes=VMEM_LIMIT),
        name="attn_prompt",
    )(sinks, q, k, k, v, v, gq)


def _attn_sample_kernel(sink_ref, q_ref, kn_ref, vn_ref, kc_ref, vc_ref, gq_ref, o_ref):
    t_len, cb, _ = q_ref.shape
    win = kc_ref.shape[1]
    rows = GQA * t_len * cb
    kc = kc_ref[...].reshape(cb * win, D_KV)
    vc = vc_ref[...].reshape(cb * win, D_KV)
    kn = kn_ref[...].reshape(t_len * cb, D_KV)
    vn = vn_ref[...].reshape(t_len * cb, D_KV)

    r_c = lax.broadcasted_iota(I32, (rows, cb * win), 0)
    c_c = lax.broadcasted_iota(I32, (rows, cb * win), 1)
    t_c = _div_pow2(_mod_pow2(r_c, t_len * cb), cb)
    valid_c = (_mod_pow2(r_c, cb) == _div_pow2(c_c, win)) & (_mod_pow2(c_c, win) >= t_c)
    dist_c = (win + t_c - _mod_pow2(c_c, win)).astype(F32)
    r_n = lax.broadcasted_iota(I32, (rows, t_len * cb), 0)
    c_n = lax.broadcasted_iota(I32, (rows, t_len * cb), 1)
    t_n = _div_pow2(_mod_pow2(r_n, t_len * cb), cb)
    valid_n = (_mod_pow2(r_n, cb) == _mod_pow2(c_n, cb)) & (_div_pow2(c_n, cb) <= t_n)
    dist_n = (t_n - _div_pow2(c_n, cb)).astype(F32)
    hl = _div_pow2(lax.broadcasted_iota(I32, (rows, 1), 0), t_len * cb)

    per_group = []
    for g in range(N_KV_HEADS):
        slabs = [q_ref[t][:, (g * GQA + i) * HEAD_DIM:(g * GQA + i + 1) * HEAD_DIM]
                 for i in range(GQA) for t in range(t_len)]
        qg = _qnorm(jnp.concatenate(slabs, axis=0), gq_ref[...])
        slope = jnp.zeros((rows, 1), F32)
        sink = jnp.zeros((rows, 1), F32)
        for i in range(GQA):
            slope = jnp.where(hl == i, 2.0 ** -(g * GQA + i + 1), slope)
            sink = jnp.where(hl == i, sink_ref[g * GQA + i], sink)
        lo, hi = g * HEAD_DIM, (g + 1) * HEAD_DIM
        s_c = _bdot_nt(qg, kc[:, lo:hi]) * (HEAD_DIM ** -0.5) - slope * dist_c
        s_n = _bdot_nt(qg, kn[:, lo:hi]) * (HEAD_DIM ** -0.5) - slope * dist_n
        s_c = jnp.where(valid_c, s_c, NEG)
        s_n = jnp.where(valid_n, s_n, NEG)
        m = jnp.maximum(jnp.maximum(jnp.max(s_c, axis=-1, keepdims=True), jnp.max(s_n, axis=-1, keepdims=True)), sink)
        p_c = jnp.exp(s_c - m)
        p_n = jnp.exp(s_n - m)
        l = jnp.sum(p_c, axis=-1, keepdims=True) + jnp.sum(p_n, axis=-1, keepdims=True) + jnp.exp(sink - m)
        per_group.append((_bdot(p_c, vc[:, lo:hi]) + _bdot(p_n, vn[:, lo:hi])) / l)
    for t in range(t_len):
        o_ref[t] = jnp.concatenate(
            [per_group[g][(i * t_len + t) * cb:(i * t_len + t + 1) * cb] for g in range(N_KV_HEADS) for i in range(GQA)],
            axis=1)


def _attn_sample(q_t, k_t, v_t, cache_k, cache_v, gq, sinks):
    t_len, bsz, _ = q_t.shape
    cb = SAMPLE_CHUNK
    win = cache_k.shape[1]
    new = lambda w: pl.BlockSpec((t_len, cb, w), lambda c: (0, c, 0))
    old = pl.BlockSpec((cb, win, D_KV), lambda c: (c, 0, 0))
    return pl.pallas_call(
        _attn_sample_kernel,
        grid=(bsz // cb,),
        in_specs=[pl.BlockSpec(memory_space=pltpu.SMEM), new(D_ATT), new(D_KV), new(D_KV), old, old,
                  pl.BlockSpec(gq.shape, lambda c: (0, 0))],
        out_specs=new(D_ATT),
        out_shape=jax.ShapeDtypeStruct((t_len, bsz, D_ATT), F32),
        compiler_params=pltpu.CompilerParams(dimension_semantics=("arbitrary",), vmem_limit_bytes=VMEM_LIMIT),
        name="attn_sample",
    )(sinks, q_t, k_t, v_t, cache_k, cache_v, gq)


def _route(s_t, sb_t):
    tm = s_t.shape[1]
    i8 = lax.broadcasted_iota(I32, (GROUP_SIZE, tm), 0)
    ninf = -jnp.inf
    sg = [sb_t[GROUP_SIZE * g:GROUP_SIZE * (g + 1)] for g in range(N_GROUPS)]
    gscore = []
    for g in range(N_GROUPS):
        m1 = jnp.max(sg[g], axis=0, keepdims=True)
        i1 = jnp.min(jnp.where(sg[g] == m1, i8, GROUP_SIZE), axis=0, keepdims=True)
        m2 = jnp.max(jnp.where(i8 == i1, ninf, sg[g]), axis=0, keepdims=True)
        gscore.append(m1 + m2)
    gs = jnp.concatenate(gscore, axis=0)
    gsel = jnp.zeros((N_GROUPS, tm), I32)
    for _ in range(TOPK_GROUPS):
        m = jnp.max(gs, axis=0, keepdims=True)
        idx = jnp.min(jnp.where(gs == m, i8, N_GROUPS), axis=0, keepdims=True)
        hit = i8 == idx
        gsel = jnp.where(hit, 1, gsel)
        gs = jnp.where(hit, ninf, gs)
    sm = [jnp.where(gsel[g:g + 1] > 0, sg[g], NEG) for g in range(N_GROUPS)]
    eid = [i8 + GROUP_SIZE * g for g in range(N_GROUPS)]
    sel = [jnp.zeros((GROUP_SIZE, tm), F32) for _ in range(N_GROUPS)]
    idxs, ws = [], []
    for _ in range(TOP_K):
        cm = functools.reduce(jnp.maximum, sm)
        m = jnp.max(cm, axis=0, keepdims=True)
        cand = functools.reduce(jnp.minimum, [jnp.where(sm[g] == m, eid[g], N_EXPERTS) for g in range(N_GROUPS)])
        idx = jnp.min(cand, axis=0, keepdims=True)
        wk = jnp.zeros((GROUP_SIZE, tm), F32)
        for g in range(N_GROUPS):
            hit = eid[g] == idx
            wk = wk + jnp.where(hit, s_t[GROUP_SIZE * g:GROUP_SIZE * (g + 1)], 0.0)
            sel[g] = jnp.where(hit, 1.0, sel[g])
            sm[g] = jnp.where(hit, ninf, sm[g])
        idxs.append(idx)
        ws.append(jnp.sum(wk, axis=0, keepdims=True))
    return idxs, ws, jnp.concatenate(sel, axis=0), eid


def _post_kernel(x_ref, r_ref, a_ref, g1_ref, sc2_ref, sh2_ref, nf_ref, wot_ref, wob_ref, wrt_ref, rb_ref, cin_ref,
                 x1_ref, h2_ref, eidx_ref, gw_ref, rank_ref, cnt_ref, carry_ref, before_ref):
    tm = x_ref.shape[0]

    @pl.when(pl.program_id(0) == 0)
    def _():
        carry_ref[...] = cin_ref[...]
        rr = lax.broadcasted_iota(I32, (tm, tm), 0)
        cc = lax.broadcasted_iota(I32, (tm, tm), 1)
        before_ref[...] = jnp.where(rr < cc, 1.0, 0.0).astype(BF16)

    mixed = (jnp.dot(r_ref[...].astype(BF16), wot_ref[...], preferred_element_type=F32)
             + jnp.dot(a_ref[...].astype(BF16), wob_ref[...], preferred_element_type=F32))
    x1 = x_ref[...] + g1_ref[...] * mixed
    x1_ref[...] = x1
    h2 = _norm_mod(x1, nf_ref[...], sc2_ref[...], sh2_ref[...])
    h2_ref[...] = _pack_bf16_pairs(h2)

    wr = wrt_ref[...]
    wr_hi = wr.astype(BF16)
    wr_lo = (wr - wr_hi.astype(F32)).astype(BF16)
    h_hi = h2.astype(BF16)
    h_lo = (h2 - h_hi.astype(F32)).astype(BF16)
    logits = _bdot_nt(wr_hi, h_hi) + _bdot_nt(wr_hi, h_lo) + _bdot_nt(wr_lo, h_hi)
    s_t = _sigmoid(logits)
    idxs, ws, sel, eid = _route(s_t, s_t + rb_ref[...])

    carry = carry_ref[...]
    tot = jnp.dot(sel.astype(BF16), before_ref[...], preferred_element_type=F32) + carry[:, 0:1]
    ranks = []
    for k in range(TOP_K):
        acc = jnp.zeros((GROUP_SIZE, tm), F32)
        for g in range(N_GROUPS):
            acc = acc + jnp.where(eid[g] == idxs[k], tot[GROUP_SIZE * g:GROUP_SIZE * (g + 1)], 0.0)
        ranks.append(jnp.sum(acc, axis=0, keepdims=True))
    carry = carry + jnp.sum(sel, axis=1, keepdims=True)
    carry_ref[...] = carry
    cnt_ref[...] = carry

    wsum = functools.reduce(lambda p, q: p + q, ws)
    pad_i = jnp.zeros((SUBLANES - TOP_K, tm), I32)
    pad_f = jnp.zeros((SUBLANES - TOP_K, tm), F32)
    eidx_ref[...] = jnp.concatenate(idxs + [pad_i], axis=0)
    rank_ref[...] = jnp.concatenate([r.astype(I32) for r in ranks] + [pad_i], axis=0)
    gw_ref[...] = jnp.concatenate([w / wsum * ROUTE_SCALE for w in ws] + [pad_f], axis=0)


def _post(x, r, a, g1, sc2, sh2, nf, wo_top, wo_bot, wr_t, rb, cnt_in, tm, mod_spec):
    n = x.shape[0]
    tile = lambda w: pl.BlockSpec((tm, w), lambda i: (i, 0))
    full = lambda arr: pl.BlockSpec(arr.shape, lambda i: (0,) * arr.ndim)
    slot = pl.BlockSpec((SUBLANES, tm), lambda i: (0, i))
    return pl.pallas_call(
        _post_kernel,
        grid=(n // tm,),
        in_specs=[tile(D_MODEL), tile(D_RNN), tile(D_ATT), mod_spec, mod_spec, mod_spec, full(nf), full(wo_top),
                  full(wo_bot), full(wr_t), full(rb), full(cnt_in)],
        out_specs=[tile(D_MODEL), tile(D_MODEL // 2), slot, slot, slot, full(cnt_in)],
        out_shape=[jax.ShapeDtypeStruct((n, D_MODEL), F32), jax.ShapeDtypeStruct((n, D_MODEL // 2), I32),
                   jax.ShapeDtypeStruct((SUBLANES, n), I32), jax.ShapeDtypeStruct((SUBLANES, n), F32),
                   jax.ShapeDtypeStruct((SUBLANES, n), I32), jax.ShapeDtypeStruct(cnt_in.shape, F32)],
        scratch_shapes=[pltpu.VMEM(cnt_in.shape, F32), pltpu.VMEM((tm, tm), BF16)],
        compiler_params=pltpu.CompilerParams(dimension_semantics=("arbitrary",), vmem_limit_bytes=VMEM_LIMIT),
        name="post_mix",
    )(x, r, a, g1, sc2, sh2, nf, wo_top, wo_bot, wr_t, rb, cnt_in)


def _plan_kernel(cnt_ref, ps_ref, used_ref, *, bm):
    cnt = cnt_ref[...]
    padded = jnp.ceil(cnt * (1.0 / bm)) * bm
    row = lax.broadcasted_iota(I32, cnt.shape, 0)
    pend = padded
    s = 1
    while s < N_EXPERTS:
        pend = pend + jnp.where(row >= s, pltpu.roll(pend, s, 0), 0.0)
        s *= 2
    ps_ref[...] = pend - padded
    used_ref[...] = jnp.broadcast_to(pend[N_EXPERTS - 1:N_EXPERTS] * (1.0 / bm), used_ref.shape).astype(I32)


def _plan(cnt, bm):
    assert bm & (bm - 1) == 0
    return pl.pallas_call(
        functools.partial(_plan_kernel, bm=bm),
        out_shape=[jax.ShapeDtypeStruct(cnt.shape, F32), jax.ShapeDtypeStruct((SUBLANES, LANES), I32)],
        name="moe_plan",
    )(cnt)


def _layout_kernel(ps_ref, eidx_ref, rank_ref, gw_ref, *out_refs, chunks):
    *pos_refs, w_ref = out_refs
    tn = eidx_ref.shape[1]
    e_iota = lax.broadcasted_iota(I32, (N_EXPERTS, tn), 0)
    ps = ps_ref[...][:, 0:1]
    rows = []
    for k in range(TOP_K):
        hit = e_iota == eidx_ref[k:k + 1, :]
        base = jnp.sum(jnp.where(hit, ps, 0.0), axis=0, keepdims=True)
        rows.append(base.astype(I32) + rank_ref[k:k + 1, :])
    rows.append(jnp.zeros((SUBLANES - TOP_K, tn), I32))
    pos = jnp.concatenate(rows, axis=0)
    for pos_ref, c in zip(pos_refs, chunks):
        for q in range(tn // c):
            pos_ref[q] = pos[:, q * c:(q + 1) * c]
    gw = gw_ref[...]
    rep = jnp.concatenate([jnp.broadcast_to(gw[k:k + 1], (SC_LANES, tn)) for k in range(SUBLANES)], axis=0)
    w_ref[...] = rep.T


def _layouts(pstart, eidx, rank, gw, chunks):
    n = eidx.shape[1]
    tn = min(n, POS_TILE)
    slot = pl.BlockSpec((SUBLANES, tn), lambda i: (0, i))
    return pl.pallas_call(
        functools.partial(_layout_kernel, chunks=chunks),
        grid=(n // tn,),
        in_specs=[pl.BlockSpec(pstart.shape, lambda i: (0, 0)), slot, slot, slot],
        out_specs=[pl.BlockSpec((tn // c, SUBLANES, c), lambda i: (i, 0, 0)) for c in chunks]
        + [pl.BlockSpec((tn, SUBLANES * SC_LANES), lambda i: (i, 0))],
        out_shape=[jax.ShapeDtypeStruct((n // c, SUBLANES, c), I32) for c in chunks]
        + [jax.ShapeDtypeStruct((n, SUBLANES * SC_LANES), F32)],
        compiler_params=pltpu.CompilerParams(dimension_semantics=("arbitrary",), vmem_limit_bytes=VMEM_LIMIT),
        name="moe_layout",
    )(pstart, eidx, rank, gw)


def _sc_worker_id():
    return lax.axis_index("s") * SC_CORES + lax.axis_index("c")


def _sc_mesh():
    return plsc.VectorSubcoreMesh(core_axis_name="c", subcore_axis_name="s")


def _sc_dispatch(seg_a, seg_b, total_rows):
    (h_a, pos_a, c_a), (h_b, pos_b, c_b) = seg_a, seg_b
    width = h_a.shape[1]

    @functools.partial(
        pl.kernel, mesh=_sc_mesh(), out_type=jax.ShapeDtypeStruct((total_rows, width), I32),
        scratch_types=[pltpu.VMEM((2, SUBLANES, c_a), I32), pltpu.VMEM((2, c_a, width), I32),
                       pltpu.VMEM((2, SUBLANES, c_b), I32), pltpu.VMEM((2, c_b, width), I32),
                       pltpu.SemaphoreType.DMA((2,)), pltpu.SemaphoreType.DMA((2,))])
    def run(ha_hbm, pa_hbm, hb_hbm, pb_hbm, xs_hbm, idx_a, rows_a, idx_b, rows_b, in_sems, out_sems):
        wid = _sc_worker_id()

        def segment(h_hbm, p_hbm, idx_v, rows_v, c):
            nch = h_hbm.shape[0] // (SC_WORKERS * c)
            assert nch == 1 or nch % 2 == 0
            chunk0 = wid * nch

            def loads(ci, b):
                return [pltpu.make_async_copy(p_hbm.at[chunk0 + ci], idx_v.at[b], in_sems.at[b]),
                        pltpu.make_async_copy(h_hbm.at[pl.ds((chunk0 + ci) * c, c)], rows_v.at[b], in_sems.at[b])]

            def scatters(b):
                return [pltpu.make_async_copy(rows_v.at[b], xs_hbm.at[idx_v.at[b].at[k]], out_sems.at[b])
                        for k in range(TOP_K)]

            def start(copies):
                for cp in copies:
                    cp.start()

            def wait(copies):
                for cp in copies:
                    cp.wait()

            start(loads(0, 0))
            if nch == 1:
                wait(loads(0, 0))
                start(scatters(0))
                wait(scatters(0))
                return

            @pl.loop(0, nch, step=2)
            def _(ci):
                for b in range(2):
                    wait(loads(ci + b, b))
                    start(scatters(b))

                    @pl.when(ci + b + 1 < nch)
                    def _():
                        @pl.when(ci + b >= 1)
                        def _():
                            wait(scatters(1 - b))

                        start(loads(ci + b + 1, 1 - b))

            wait(scatters(0))
            wait(scatters(1))

        segment(ha_hbm, pa_hbm, idx_a, rows_a, c_a)
        segment(hb_hbm, pb_hbm, idx_b, rows_b, c_b)

    return run(h_a, pos_a, h_b, pos_b)


def _experts_kernel(sb_ref, nb_ref, cnt_ref, nu_ref, xs_hbm, wg_ref, wu_ref, wd_ref, y_hbm,
                    xbuf, ybuf, wgb, wub, wdb, in_sems, out_sems):
    e = pl.program_id(0)
    ring, bm, half = xbuf.shape
    n_used = nu_ref[0]

    def slot_of(g):
        return g & (ring - 1)

    def in_copy(g):
        s = slot_of(g)
        return pltpu.make_async_copy(xs_hbm.at[pl.ds(pl.multiple_of(g * bm, bm), bm), :], xbuf.at[s], in_sems.at[s])

    def out_copy(g):
        s = slot_of(g)
        return pltpu.make_async_copy(ybuf.at[s], y_hbm.at[pl.ds(pl.multiple_of(g * bm, bm), bm), :], out_sems.at[s])

    @pl.when(e == 0)
    def _():
        for g in range(ring):
            @pl.when(g < n_used)
            def _():
                in_copy(g).start()

    wgb[...] = wg_ref[...].astype(BF16)
    wub[...] = wu_ref[...].astype(BF16)
    wdb[...] = wd_ref[...].astype(BF16)

    def blocks(b, k):
        g = sb_ref[e] + b
        for i in range(k):
            in_copy(g + i).wait()

            @pl.when(g + i >= ring)
            def _():
                out_copy(g + i - ring).wait()

        s = pl.multiple_of(slot_of(g), k)
        row = lax.broadcasted_iota(I32, (k * bm, half), 0)
        packed = xbuf[pl.ds(s, k)].reshape(k * bm, half)
        x_lo, x_hi = _unpack_bf16_pairs(jnp.where(row < cnt_ref[e] - b * bm, packed, 0))
        a = (jnp.dot(x_lo, wgb[:half], preferred_element_type=F32)
             + jnp.dot(x_hi, wgb[half:], preferred_element_type=F32))
        u = (jnp.dot(x_lo, wub[:half], preferred_element_type=F32)
             + jnp.dot(x_hi, wub[half:], preferred_element_type=F32))
        y = _pack_bf16_pairs(jnp.dot((_silu(a) * u).astype(BF16), wdb[...], preferred_element_type=F32))
        ybuf[pl.ds(s, k)] = y.reshape(k, bm, half)
        for i in range(k):
            out_copy(g + i).start(priority=1)

            @pl.when(g + i + ring < n_used)
            def _():
                in_copy(g + i + ring).start()

    def single(b):
        def body(_, carry):
            blocks(b, 1)
            return carry
        return body

    n_blk = nb_ref[e]
    n_first = jnp.minimum(sb_ref[e] & 1, n_blk)
    n_pairs = lax.shift_right_logical(n_blk - n_first, 1)
    n_last = (n_blk - n_first) & 1
    lax.fori_loop(0, n_first, single(0), 0)

    def pair(i, carry):
        blocks(n_first + 2 * i, 2)
        return carry

    lax.fori_loop(0, n_pairs, pair, 0)
    lax.fori_loop(0, n_last, single(n_first + 2 * n_pairs), 0)

    @pl.when(e == pl.num_programs(0) - 1)
    def _():
        for r in range(ring):
            @pl.when(n_used - 1 - r >= 0)
            def _():
                out_copy(n_used - 1 - r).wait()


def _experts(first_blk, n_blk, counts, n_used, xs, wg, wu, wd):
    rows = xs.shape[0]
    bm, ring = EXPERT_BLOCK, EXPERT_RING
    assert ring & (ring - 1) == 0
    w_blk = lambda e, *_: (e, 0, 0)
    grid_spec = pltpu.PrefetchScalarGridSpec(
        num_scalar_prefetch=4,
        grid=(N_EXPERTS,),
        in_specs=[pl.BlockSpec(memory_space=pl.ANY),
                  pl.BlockSpec((None, D_MODEL, D_EXPERT), w_blk),
                  pl.BlockSpec((None, D_MODEL, D_EXPERT), w_blk),
                  pl.BlockSpec((None, D_EXPERT, D_MODEL), w_blk)],
        out_specs=pl.BlockSpec(memory_space=pl.ANY),
        scratch_shapes=[pltpu.VMEM((ring, bm, D_MODEL // 2), I32), pltpu.VMEM((ring, bm, D_MODEL // 2), I32),
                        pltpu.VMEM((D_MODEL, D_EXPERT), BF16), pltpu.VMEM((D_MODEL, D_EXPERT), BF16),
                        pltpu.VMEM((D_EXPERT, D_MODEL), BF16),
                        pltpu.SemaphoreType.DMA((ring,)), pltpu.SemaphoreType.DMA((ring,))],
    )
    return pl.pallas_call(
        _experts_kernel,
        grid_spec=grid_spec,
        out_shape=jax.ShapeDtypeStruct((rows, D_MODEL // 2), I32),
        compiler_params=pltpu.CompilerParams(dimension_semantics=("arbitrary",), vmem_limit_bytes=VMEM_LIMIT),
        name="moe_experts",
    )(first_blk, n_blk, counts, n_used, xs, wg, wu, wd)


def _sc_combine(y, segs):
    c = COMBINE_CHUNK
    half = y.shape[1]
    d = 2 * half
    ns = len(segs)
    params = dataclasses.replace(pltpu.CompilerParams(), needs_layout_passes=False)

    @functools.partial(
        pl.kernel, mesh=_sc_mesh(), compiler_params=params,
        out_type=[jax.ShapeDtypeStruct((w.shape[0], d), F32) for _, w in segs],
        scratch_types=[pltpu.VMEM(pos.shape[1:], I32) for pos, _ in segs]
        + [pltpu.VMEM((2, c, SUBLANES * SC_LANES), F32), pltpu.VMEM((2, TOP_K, c, half), I32),
           pltpu.VMEM((c, d), F32), pltpu.SemaphoreType.DMA((2,)), pltpu.SemaphoreType.DMA])
    def run(y_hbm, *refs):
        ins, outs, idxs = refs[:2 * ns], refs[2 * ns:3 * ns], refs[3 * ns:4 * ns]
        w_v, buf_v, out_v, sems, out_sem = refs[4 * ns:]
        wid = _sc_worker_id()

        def segment(p_hbm, w_hbm, o_hbm, idx_v):
            per_w = idx_v.shape[1]
            nch = per_w // c
            base = wid * per_w
            pltpu.sync_copy(p_hbm.at[wid], idx_v)

            def in_copies(ci, b):
                w_copy = pltpu.make_async_copy(w_hbm.at[pl.ds(base + ci * c, c)], w_v.at[b], sems.at[b])
                return [w_copy] + [
                    pltpu.make_async_copy(y_hbm.at[idx_v.at[k, pl.ds(ci * c, c)]], buf_v.at[b].at[k], sems.at[b])
                    for k in range(TOP_K)]

            def out_copy(ci):
                return pltpu.make_async_copy(out_v, o_hbm.at[pl.ds(base + ci * c, c)], out_sem)

            def reduce_rows(ci, b):
                @pl.when(ci >= 1)
                def _():
                    out_copy(ci - 1).wait()

                for t0 in range(0, c, COMBINE_TOKEN_GROUP):
                    group = range(t0, t0 + COMBINE_TOKEN_GROUP)
                    ws = [[w_v[b, t, pl.ds(k * SC_LANES, SC_LANES)] for k in range(TOP_K)] for t in group]

                    @plsc.parallel_loop(0, half // SC_LANES, unroll=2)
                    def _(j):
                        lanes = pl.ds(pl.multiple_of(j * SC_LANES, SC_LANES), SC_LANES)
                        hi_lanes = pl.ds(pl.multiple_of(half + j * SC_LANES, SC_LANES), SC_LANES)
                        for t, wt in zip(group, ws):
                            acc_lo = acc_hi = None
                            for k in range(TOP_K):
                                word = buf_v[b, k, t, lanes]
                                lo = plsc.bitcast(lax.shift_left(word, jnp.int32(16)), F32) * wt[k]
                                hi = plsc.bitcast(word & jnp.int32(-65536), F32) * wt[k]
                                acc_lo = lo if acc_lo is None else acc_lo + lo
                                acc_hi = hi if acc_hi is None else acc_hi + hi
                            out_v[t, lanes] = acc_lo
                            out_v[t, hi_lanes] = acc_hi

                out_copy(ci).start()

            for cp in in_copies(0, 0):
                cp.start()

            @pl.loop(0, nch, step=2)
            def _(ci):
                for b in range(2):
                    @pl.when(ci + b + 1 < nch)
                    def _():
                        for cp in in_copies(ci + b + 1, 1 - b):
                            cp.start()

                    for cp in in_copies(ci + b, b):
                        cp.wait()
                    reduce_rows(ci + b, b)

            out_copy(nch - 1).wait()

        for i in range(ns):
            segment(ins[2 * i], ins[2 * i + 1], outs[i], idxs[i])

    res = run(y, *[a for seg in segs for a in seg])
    return list(res) if isinstance(res, (list, tuple)) else [res]


def _shared_kernel(x1_ref, h2_ref, g2_ref, wsg_ref, wsu_ref, wsd_ref, o_ref):
    half = h2_ref.shape[1]
    x_lo, x_hi = _unpack_bf16_pairs(h2_ref[...])
    wsg, wsu = wsg_ref[...], wsu_ref[...]
    a = (jnp.dot(x_lo, wsg[:half], preferred_element_type=F32) + jnp.dot(x_hi, wsg[half:], preferred_element_type=F32))
    b = (jnp.dot(x_lo, wsu[:half], preferred_element_type=F32) + jnp.dot(x_hi, wsu[half:], preferred_element_type=F32))
    shared = jnp.dot((_silu(a) * b).astype(BF16), wsd_ref[...], preferred_element_type=F32)
    o_ref[...] = x1_ref[...] + g2_ref[...] * shared


def _shared(x1, h2, g2, wsg, wsu, wsd, tm, mod_spec):
    n = h2.shape[0]
    full = lambda arr: pl.BlockSpec(arr.shape, lambda i: (0,) * arr.ndim)
    tile = pl.BlockSpec((tm, D_MODEL), lambda i: (i, 0))
    return pl.pallas_call(
        _shared_kernel,
        grid=(n // tm,),
        in_specs=[tile, pl.BlockSpec((tm, D_MODEL // 2), lambda i: (i, 0)), mod_spec, full(wsg), full(wsu), full(wsd)],
        out_specs=tile,
        out_shape=jax.ShapeDtypeStruct((n, D_MODEL), F32),
        compiler_params=pltpu.CompilerParams(dimension_semantics=("arbitrary",), vmem_limit_bytes=VMEM_LIMIT),
        name="shared_expert",
    )(x1, h2, g2, wsg, wsu, wsd)


def _final_kernel(base_ref, routed_ref, g2_ref, *rest):
    o_ref = rest[-1]
    o_ref[...] = base_ref[...] + g2_ref[...] * routed_ref[...]


def _final(base, routed, g2, tm, mod_spec, tile0=0, out=None):
    n = base.shape[0]
    here = pl.BlockSpec((tm, D_MODEL), lambda i: (i + tile0, 0))
    in_specs = [here, pl.BlockSpec((tm, D_MODEL), lambda i: (i, 0)), mod_spec]
    args = [base, routed, g2]
    if out is not None:
        in_specs.append(pl.BlockSpec(memory_space=pl.ANY))
        args.append(out)
    return pl.pallas_call(
        _final_kernel,
        grid=(routed.shape[0] // tm,),
        in_specs=in_specs,
        out_specs=here,
        out_shape=jax.ShapeDtypeStruct((n, D_MODEL), F32),
        input_output_aliases={} if out is None else {3: 0},
        compiler_params=pltpu.CompilerParams(dimension_semantics=("arbitrary",), vmem_limit_bytes=VMEM_LIMIT),
        name="ffn_residual",
    )(*args)


def _block_diag(w):
    nb, bi, bj = w.shape
    return jnp.einsum('nij,nm->nimj', w, jnp.eye(nb, dtype=w.dtype)).reshape(nb * bi, nb * bj)


def kernel(x_prompt, x_sample, c_prompt, c_sample, cache_k_win, cache_v_win, state_conv, state_rnn, ada_w, ada_b, norm_mix, w_in, conv_w, conv_b, gate_a_w, gate_a_b, gate_x_w, gate_x_b, lru_lambda, q_norm, k_norm, attn_sinks, w_out, norm_ffn, router_w, router_bias, exp_w_gate, exp_w_up, exp_w_down, sh_w_gate, sh_w_up, sh_w_down):
    bp, tp, _ = x_prompt.shape
    bs, ts, _ = x_sample.shape
    win = cache_k_win.shape[2]
    n_p, n_s = bp * tp, bs * ts
    row = lambda v: v.reshape(1, -1)

    g_mix, g_ffn = row(norm_mix[0]), row(norm_ffn[0])
    win_bf = w_in[0].astype(BF16)
    q0 = 2 * D_RNN
    win_pair = jnp.concatenate([win_bf[:, :q0], _pair_heads(win_bf[:, q0:q0 + D_ATT], 1), win_bf[:, q0 + D_ATT:]],
                               axis=1)
    cw, cb = conv_w[0], row(conv_b[0])
    wg = jnp.concatenate([_block_diag(gate_a_w[0]), _block_diag(gate_x_w[0])], axis=1).astype(BF16)
    gb = row(jnp.concatenate([gate_a_b[0], gate_x_b[0]]))
    lam = row(lru_lambda[0])
    gq = row(q_norm[0])
    gk2 = row(jnp.tile(k_norm[0], N_KV_HEADS))
    sinks = attn_sinks[0]
    wo_top, wo_bot = w_out[0, :D_RNN].astype(BF16), w_out[0, D_RNN:].astype(BF16)
    wr_t = router_w[0].T
    rb = router_bias[0].reshape(N_EXPERTS, 1)
    wsg, wsu, wsd = sh_w_gate[0].astype(BF16), sh_w_up[0].astype(BF16), sh_w_down[0].astype(BF16)

    mod = _adaln(jnp.concatenate([c_prompt, c_sample], axis=0), ada_w[0], ada_b[0])
    chunks = [mod[:, i * D_MODEL:(i + 1) * D_MODEL] for i in range(6)]
    sh1p, sc1p, g1p, sh2p, sc2p, g2p = [c[:bp].reshape(bp, 1, D_MODEL) for c in chunks]
    sh1s, sc1s, g1s, sh2s, sc2s, g2s = [c[bp:] for c in chunks]

    conv0 = jnp.zeros((bp, SUBLANES, D_RNN), F32)
    h0 = jnp.zeros((bp, 1, D_RNN), F32)
    r_p, q_p, k_p, v_p, cs_p, hs_p = _front_prompt(x_prompt, sc1p, sh1p, g_mix, win_pair, cw, cb, wg, gb, lam, gk2,
                                                   conv0, h0)
    a_p = _attn_prompt(q_p, k_p, v_p, row(jnp.tile(q_norm[0], 2)), sinks)

    x_s_t = jnp.swapaxes(x_sample, 0, 1)
    r_s, q_s, k_s, v_s, cs_s, hs_s = _front_sample(x_s_t, sc1s, sh1s, g_mix, win_bf, cw, cb, wg, gb, lam, gk2,
                                                   jnp.swapaxes(state_conv[0], 0, 1), state_rnn[0])
    cache_k = cache_k_win[0].reshape(bs, win, D_KV)
    cache_v = cache_v_win[0].reshape(bs, win, D_KV)
    a_s = _attn_sample(q_s, k_s, v_s, cache_k, cache_v, gq, sinks)

    tiles_per_seq = tp // TM_POST
    mod_p = pl.BlockSpec((None, 1, D_MODEL), lambda i, *_: (i // tiles_per_seq, 0, 0))
    mod_s = pl.BlockSpec((bs, D_MODEL), lambda i, *_: (0, 0))
    cnt0 = jnp.zeros((N_EXPERTS, LANES), F32)
    x1_p, h2_p, eidx_p, gw_p, rank_p, cnt_p = _post(
        x_prompt.reshape(n_p, D_MODEL), r_p.reshape(n_p, D_RNN), a_p.reshape(n_p, D_ATT), g1p, sc2p, sh2p, g_ffn,
        wo_top, _pair_heads(wo_bot, 0), wr_t, rb, cnt0, TM_POST, mod_p)
    x1_s, h2_s, eidx_s, gw_s, rank_s, cnt_all = _post(
        x_s_t.reshape(n_s, D_MODEL), r_s.reshape(n_s, D_RNN), a_s.reshape(n_s, D_ATT), g1s, sc2s, sh2s, g_ffn,
        wo_top, wo_bot, wr_t, rb, cnt_p, bs, mod_s)

    bm = EXPERT_BLOCK
    n_blocks = -(-((n_p + n_s) * TOP_K) // bm) + N_EXPERTS
    pstart, used = _plan(cnt_all, bm)
    counts = cnt_all[:, 0].astype(I32)
    first_blk = (pstart[:, 0] * (1.0 / bm)).astype(I32)
    n_blk = (counts + (bm - 1)) // bm
    n_used = used[0, :1]

    def sc_layouts(eidx, rank, gw, n, parts):
        per_w = n // (SC_WORKERS * parts)
        c = min(DISPATCH_CHUNK, per_w)
        outs = _layouts(pstart, eidx, rank, gw, (c,) if c == per_w else (c, per_w))
        return (outs[0], c), (outs[-2], outs[-1])

    (dpos_p, c_p), (cpos_p, cw_p) = sc_layouts(eidx_p, rank_p, gw_p, n_p, 2)
    (dpos_s, c_s), comb_s = sc_layouts(eidx_s, rank_s, gw_s, n_s, 1)
    xs = _sc_dispatch((h2_p, dpos_p, c_p), (h2_s, dpos_s, c_s), n_blocks * bm)
    base_p = _shared(x1_p, h2_p, g2p, wsg, wsu, wsd, TM_POST, mod_p)
    base_s = _shared(x1_s, h2_s, g2s, wsg, wsu, wsd, bs, mod_s)
    y = _experts(first_blk, n_blk, counts, n_used, xs, exp_w_gate[0], exp_w_up[0], exp_w_down[0])
    n_h = n_p // 2
    routed_a, routed_s = _sc_combine(y, [(cpos_p[:SC_WORKERS], cw_p[:n_h]), comb_s])
    routed_b, = _sc_combine(y, [(cpos_p[SC_WORKERS:], cw_p[n_h:])])
    tiles_h = n_h // TM_POST
    mod_p2 = pl.BlockSpec((None, 1, D_MODEL), lambda i, *_: ((i + tiles_h) // tiles_per_seq, 0, 0))
    y_p = _final(base_p, routed_a, g2p, TM_POST, mod_p)
    y_p = _final(base_p, routed_b, g2p, TM_POST, mod_p2, tile0=tiles_h, out=y_p)
    y_s = _final(base_s, routed_s, g2s, bs, mod_s)

    y_prompt = y_p.reshape(bp, tp, D_MODEL)
    y_sample = jnp.swapaxes(y_s.reshape(ts, bs, D_MODEL), 0, 1)
    wk = min(WINDOW, tp)
    k_win_p = k_p[:, tp - wk:].reshape(1, bp, wk, N_KV_HEADS, HEAD_DIM)
    v_win_p = v_p[:, tp - wk:].reshape(1, bp, wk, N_KV_HEADS, HEAD_DIM)
    k_new = jnp.swapaxes(k_s, 0, 1)
    v_new = jnp.swapaxes(v_s, 0, 1)
    heads = lambda a: a.reshape(bs, ts, N_KV_HEADS, HEAD_DIM)
    k_win_s = jnp.concatenate([cache_k_win[0], heads(k_new)], axis=1)[None, :, ts:]
    v_win_s = jnp.concatenate([cache_v_win[0], heads(v_new)], axis=1)[None, :, ts:]
    return (y_prompt, y_sample, k_win_p, v_win_p, cs_p[None, :, SUBLANES - (CONV_W - 1):], hs_p.reshape(1, bp, D_RNN),
            k_win_s, v_win_s, jnp.swapaxes(cs_s, 0, 1)[None], hs_s[None])
```
